```python
import jax, jax.numpy as jnp
from jax import lax
import numpy as np

D_MODEL = 1024
BATCH = 8
SEQ = 2048
DEPTH = 2

HEAD_DIM = D_MODEL // 8
DILATED_CONFIGS = ((128, 1), (512, 4), (2048, 16))
N_GROUPS = len(DILATED_CONFIGS)
N_SLOT_HEADS = 4
N_ATTN_HEADS = N_GROUPS * N_SLOT_HEADS
QKV_WIDTH = 3 * N_ATTN_HEADS * HEAD_DIM
ATTN_WIDTH = N_SLOT_HEADS * HEAD_DIM
N_MEM = 256
N_MEM_HEADS = 4
MEM_WIDTH = N_MEM_HEADS * HEAD_DIM
N_FOURIER_GROUPS = 4
FOURIER_GROUP_DIM = HEAD_DIM
FOURIER_WIDTH = N_FOURIER_GROUPS * FOURIER_GROUP_DIM
MIX_WIDTH = ATTN_WIDTH + MEM_WIDTH
D_FF = ((8 * D_MODEL // 3 + 255) // 256) * 256
N_EXPERTS = 8
TOP_K = 2
D_FF_EXPERT = 7 * D_MODEL // 2
ALPHA = (2.0 * DEPTH) ** 0.25
BETA = (8.0 * DEPTH) ** -0.25
LN_EPS = 1e-5
NEG_INF = -1e30
N_EVEN = (DEPTH + 1) // 2
N_ODD = DEPTH // 2

kernel_name = "hybrid_dilated_fourier_memory_moe_encoder"


def _layer_norm(x, g, b=None):
    xf = x.astype(jnp.float32)
    xc = xf - jnp.mean(xf, axis=-1, keepdims=True)
    var = jnp.mean(xc * xc, axis=-1, keepdims=True)
    y = xc * lax.rsqrt(var + LN_EPS) * g
    if b is not None:
        y = y + b
    return y.astype(x.dtype)


def _alibi_slopes():
    exps = np.arange(1, N_ATTN_HEADS + 1, dtype=np.float32) * np.float32(8.0 / N_ATTN_HEADS)
    return jnp.asarray(np.exp2(-exps).astype(np.float32))


def _dilated_group_attention(q, k, v, dilation, n_side, slopes):
    B, H, S, hd = q.shape
    L = S // dilation
    nb = -(-L // n_side)
    Lp = nb * n_side

    def classes(t):
        return t.reshape(B, H, L, dilation, hd).transpose(0, 1, 3, 2, 4)

    pad4 = ((0, 0), (0, 0), (0, 0))
    qb = jnp.pad(classes(q), pad4 + ((0, Lp - L), (0, 0))).reshape(B, H, dilation, nb, n_side, hd)

    def windows(t):
        tp = jnp.pad(classes(t), pad4 + ((n_side, Lp - L + n_side), (0, 0)))
        tp = tp.reshape(B, H, dilation, nb + 2, n_side, hd)
        return jnp.concatenate([tp[:, :, :, :-2], tp[:, :, :, 1:-1], tp[:, :, :, 2:]], axis=4)

    kw, vw = windows(k), windows(v)
    qi = jnp.arange(n_side)
    kj = jnp.arange(3 * n_side)
    blk = jnp.arange(nb)
    delta = kj[None, :] - n_side - qi[:, None]
    key_idx = blk[:, None] * n_side - n_side + kj[None, :]
    valid = (jnp.abs(delta) <= n_side)[None] & ((key_idx >= 0) & (key_idx < L))[:, None, :]
    dist = (jnp.abs(delta) * dilation).astype(jnp.float32)
    bias = -slopes.astype(jnp.float32)[:, None, None] * dist

    s = jnp.einsum('bhrnqd,bhrnkd->bhrnqk', qb, kw).astype(jnp.float32) * (hd ** -0.5)
    s = s + bias[None, :, None, None]
    s = jnp.where(valid[None, None, None], s, NEG_INF)
    m = jnp.max(s, axis=-1, keepdims=True)
    p = jnp.exp(s - m)
    l = jnp.sum(p, axis=-1, keepdims=True)
    o = jnp.einsum('bhrnqk,bhrnkd->bhrnqd', p, vw.astype(jnp.float32)) / l
    lse = (m + jnp.log(l))[..., 0]
    o = o.reshape(B, H, dilation, Lp, hd)[:, :, :, :L].transpose(0, 1, 3, 2, 4).reshape(B, H, S, hd)
    lse = lse.reshape(B, H, dilation, Lp)[..., :L].transpose(0, 1, 3, 2).reshape(B, H, S)
    return o, lse


def _memory_attention(q_flat, mem, w_mem_kv):
    B, S, _ = q_flat.shape
    n_mem = mem.shape[1]
    q = q_flat.reshape(B, S, N_MEM_HEADS, HEAD_DIM)
    kv = (mem @ w_mem_kv).reshape(B, n_mem, 2, N_MEM_HEADS, HEAD_DIM)
    s = jnp.einsum('bshd,bmhd->bhsm', q, kv[:, :, 0]).astype(jnp.float32) * (HEAD_DIM ** -0.5)
    p = jax.nn.softmax(s, axis=-1)
    o = jnp.einsum('bhsm,bmhd->bshd', p, kv[:, :, 1].astype(jnp.float32))
    return o.reshape(B, S, MEM_WIDTH).astype(q_flat.dtype)


def _dilated_mixer(x, mem, w_in, w_mem_kv, w_out):
    B, S, _ = x.shape
    proj = x @ w_in
    qkv = proj[..., :QKV_WIDTH].reshape(B, S, 3, N_GROUPS, N_SLOT_HEADS, HEAD_DIM)
    qkv = qkv.transpose(2, 3, 0, 4, 1, 5)
    q_mem = proj[..., QKV_WIDTH:]
    slopes = _alibi_slopes().reshape(N_GROUPS, N_SLOT_HEADS)
    outs, lses = [], []
    for g, (window, dilation) in enumerate(DILATED_CONFIGS):
        o, lse = _dilated_group_attention(qkv[0, g], qkv[1, g], qkv[2, g], dilation,
                                          window // (2 * dilation), slopes[g])
        outs.append(o)
        lses.append(lse)
    weights = jax.nn.softmax(jnp.stack(lses), axis=0)
    attn = jnp.einsum('gbhs,gbhsd->bshd', weights, jnp.stack(outs))
    attn = attn.reshape(B, S, ATTN_WIDTH).astype(x.dtype)
    mix = jnp.concatenate([attn, _memory_attention(q_mem, mem, w_mem_kv)], axis=-1)
    return mix @ w_out


def _fourier_mixer(x, mem, w_in, fourier_g, w_mem_kv, w_out):
    B, S, _ = x.shape
    proj = x @ w_in
    u = proj[..., :FOURIER_WIDTH].reshape(B, S, N_FOURIER_GROUPS, FOURIER_GROUP_DIM)
    u = _layer_norm(u, fourier_g)
    f = jnp.fft.fft2(u.astype(jnp.float32), axes=(1, 3), norm='ortho').real
    f = f.reshape(B, S, FOURIER_WIDTH).astype(x.dtype)
    q_mem = proj[..., FOURIER_WIDTH:]
    mix = jnp.concatenate([f, _memory_attention(q_mem, mem, w_mem_kv)], axis=-1)
    return mix @ w_out


def _swiglu(h, w_gate, w_up, w_down):
    return (jax.nn.silu(h @ w_gate) * (h @ w_up)) @ w_down


def _moe_swiglu(h, router_w, router_b, w_gate, w_up, w_down):
    B, S, D = h.shape
    t = h.reshape(B * S, D)
    logits = (t @ router_w).astype(jnp.float32) + router_b.astype(jnp.float32)
    top_logits, top_idx = lax.top_k(logits, TOP_K)
    gates = jax.nn.softmax(top_logits, axis=-1)
    combine = jnp.einsum('tk,tke->te', gates,
                         jax.nn.one_hot(top_idx, N_EXPERTS, dtype=jnp.float32)).astype(h.dtype)
    y = jnp.zeros_like(t)
    for e in range(N_EXPERTS):
        y = y + combine[:, e:e + 1] * _swiglu(t, w_gate[e], w_up[e], w_down[e])
    return y.reshape(B, S, D)


def _normal(key, shape, fan_in, scale=1.0):
    return jax.random.normal(key, shape, jnp.float32) * (scale * fan_in ** -0.5)


def _gain(key, shape):
    return 1.0 + 0.05 * jax.random.normal(key, shape, jnp.float32)


def _bias(key, shape, scale=0.02):
    return scale * jax.random.normal(key, shape, jnp.float32)


def setup_inputs(seed: int = 0) -> dict:
    key = jax.random.key(seed)
    ks = iter(jax.random.split(key, 32))
    D = D_MODEL
    x = jax.random.normal(next(ks), (BATCH, SEQ, D), jnp.float32)
    mem = jax.random.normal(next(ks), (BATCH, N_MEM, D), jnp.float32)
    a_w_in = jnp.concatenate([
        _normal(next(ks), (N_EVEN, D, 2 * N_ATTN_HEADS * HEAD_DIM), D),
        _normal(next(ks), (N_EVEN, D, N_ATTN_HEADS * HEAD_DIM), D, BETA),
        _normal(next(ks), (N_EVEN, D, MEM_WIDTH), D),
    ], axis=-1)
    a_w_mem_kv = jnp.concatenate([
        _normal(next(ks), (N_EVEN, D, MEM_WIDTH), D),
        _normal(next(ks), (N_EVEN, D, MEM_WIDTH), D, BETA),
    ], axis=-1)
    a_w_out = _normal(next(ks), (N_EVEN, MIX_WIDTH, D), MIX_WIDTH, BETA)
    a_ln1_g = _gain(next(ks), (N_EVEN, D))
    a_ln1_b = _bias(next(ks), (N_EVEN, D))
    a_ffn_gate = _normal(next(ks), (N_EVEN, D, D_FF), D)
    a_ffn_up = _normal(next(ks), (N_EVEN, D, D_FF), D)
    a_ffn_down = _normal(next(ks), (N_EVEN, D_FF, D), D_FF, BETA)
    a_ln2_g = _gain(next(ks), (N_EVEN, D))
    a_ln2_b = _bias(next(ks), (N_EVEN, D))
    b_w_in = _normal(next(ks), (N_ODD, D, FOURIER_WIDTH + MEM_WIDTH), D)
    b_fourier_g = _gain(next(ks), (N_ODD, N_FOURIER_GROUPS, FOURIER_GROUP_DIM))
    b_w_mem_kv = jnp.concatenate([
        _normal(next(ks), (N_ODD, D, MEM_WIDTH), D),
        _normal(next(ks), (N_ODD, D, MEM_WIDTH), D, BETA),
    ], axis=-1)
    b_w_out = _normal(next(ks), (N_ODD, MIX_WIDTH, D), MIX_WIDTH, BETA)
    b_ln1_g = _gain(next(ks), (N_ODD, D))
    b_ln1_b = _bias(next(ks), (N_ODD, D))
    b_router_w = _normal(next(ks), (N_ODD, D, N_EXPERTS), D)
    b_router_b = _bias(next(ks), (N_ODD, N_EXPERTS), 0.01)
    b_moe_gate = _normal(next(ks), (N_ODD, N_EXPERTS, D, D_FF_EXPERT), D)
    b_moe_up = _normal(next(ks), (N_ODD, N_EXPERTS, D, D_FF_EXPERT), D)
    b_moe_down = _normal(next(ks), (N_ODD, N_EXPERTS, D_FF_EXPERT, D), D_FF_EXPERT, BETA)
    b_ln2_g = _gain(next(ks), (N_ODD, D))
    b_ln2_b = _bias(next(ks), (N_ODD, D))
    return {
        "x": x, "mem": mem,
        "a_w_in": a_w_in, "a_w_mem_kv": a_w_mem_kv, "a_w_out": a_w_out,
        "a_ln1_g": a_ln1_g, "a_ln1_b": a_ln1_b,
        "a_ffn_gate": a_ffn_gate, "a_ffn_up": a_ffn_up, "a_ffn_down": a_ffn_down,
        "a_ln2_g": a_ln2_g, "a_ln2_b": a_ln2_b,
        "b_w_in": b_w_in, "b_fourier_g": b_fourier_g, "b_w_mem_kv": b_w_mem_kv, "b_w_out": b_w_out,
        "b_ln1_g": b_ln1_g, "b_ln1_b": b_ln1_b,
        "b_router_w": b_router_w, "b_router_b": b_router_b,
        "b_moe_gate": b_moe_gate, "b_moe_up": b_moe_up, "b_moe_down": b_moe_down,
        "b_ln2_g": b_ln2_g, "b_ln2_b": b_ln2_b,
    }


def reference(x, mem, a_w_in, a_w_mem_kv, a_w_out, a_ln1_g, a_ln1_b, a_ffn_gate, a_ffn_up,
              a_ffn_down, a_ln2_g, a_ln2_b, b_w_in, b_fourier_g, b_w_mem_kv, b_w_out, b_ln1_g,
              b_ln1_b, b_router_w, b_router_b, b_moe_gate, b_moe_up, b_moe_down, b_ln2_g, b_ln2_b):
    for i in range(DEPTH):
        j = i // 2
        if i % 2 == 0:
            y = _dilated_mixer(x, mem, a_w_in[j], a_w_mem_kv[j], a_w_out[j])
            x = _layer_norm(ALPHA * x + y, a_ln1_g[j], a_ln1_b[j])
            y = _swiglu(x, a_ffn_gate[j], a_ffn_up[j], a_ffn_down[j])
            x = _layer_norm(ALPHA * x + y, a_ln2_g[j], a_ln2_b[j])
        else:
            y = _fourier_mixer(x, mem, b_w_in[j], b_fourier_g[j], b_w_mem_kv[j], b_w_out[j])
            x = _layer_norm(ALPHA * x + y, b_ln1_g[j], b_ln1_b[j])
            y = _moe_swiglu(x, b_router_w[j], b_router_b[j], b_moe_gate[j], b_moe_up[j], b_moe_down[j])
            x = _layer_norm(ALPHA * x + y, b_ln2_g[j], b_ln2_b[j])
    return x
```

```python
import functools

import numpy as np
import jax
import jax.numpy as jnp
from jax import lax
from jax.experimental import pallas as pl
from jax.experimental.pallas import tpu as pltpu

F32 = jnp.float32
BF16 = jnp.bfloat16
I32 = jnp.int32

LANES = 128
HEAD_DIM = 128
HEADS_PER_BLOCK = 4
DILATIONS = (1, 4, 16)
N_SIDE = 64
N_ATTN_HEADS = 12
N_EXPERTS = 8
TOP_K = 2
ALPHA = (2.0 * 2) ** 0.25
LN_EPS = 1e-5
NEG_INF = -1e30
VMEM_LIMIT_BYTES = 56 * 1024 * 1024

ROUTE_E, ROUTE_RANK, ROUTE_GATE = 0, 2, 4


def _params(*sem):
    return pltpu.CompilerParams(dimension_semantics=sem, vmem_limit_bytes=VMEM_LIMIT_BYTES)


def _layer_norm(z, g, b=None):
    mu = jnp.mean(z, axis=-1, keepdims=True)
    zc = z - mu
    var = jnp.mean(zc * zc, axis=-1, keepdims=True)
    y = zc * lax.rsqrt(var + LN_EPS) * g
    return y if b is None else y + b


def _proj_kernel(*refs, dil, row_chunk, group_ln):
    if group_ln:
        x_ref, w_ref, g_ref, o_ref = refs[:4]
        scratch = refs[4:]
    else:
        x_ref, w_ref, o_ref = refs[:3]
        g_ref = None
        scratch = refs[3:]
    s_len = x_ref.shape[0]
    w = w_ref[...].astype(BF16)
    for rc in range(s_len // row_chunk):
        rows = pl.ds(rc * row_chunk, row_chunk)
        r = jnp.dot(x_ref[rows, :], w, preferred_element_type=F32)
        for hh in range(HEADS_PER_BLOCK):
            rh = r[:, hh * HEAD_DIM:(hh + 1) * HEAD_DIM]
            if group_ln:
                rh = _layer_norm(rh, g_ref[hh:hh + 1, :])
            if dil == 1:
                o_ref[hh, 0, rows, :] = rh.astype(o_ref.dtype)
            else:
                scratch[0][hh, rows, :] = rh
    if dil > 1:
        acc_ref = scratch[0]
        cls_len = s_len // dil
        for hh in range(HEADS_PER_BLOCK):
            for r_ in range(dil):
                o_ref[hh, r_, :, :] = acc_ref.at[hh][pl.ds(r_, cls_len, stride=dil), :].astype(o_ref.dtype)


def _proj_heads(x, w, *, col_block0, col_block_stride, n_blocks, dil=1, gain=None):
    b_sz, s_len, d_model = x.shape
    blk_cols = HEADS_PER_BLOCK * HEAD_DIM
    row_chunk = min(512, s_len)
    in_specs = [
        pl.BlockSpec((None, s_len, d_model), lambda wb, b: (b, 0, 0)),
        pl.BlockSpec((d_model, blk_cols), lambda wb, b: (0, col_block0 + wb * col_block_stride)),
    ]
    args = [x, w]
    if gain is not None:
        in_specs.append(pl.BlockSpec((HEADS_PER_BLOCK, HEAD_DIM), lambda wb, b: (0, 0)))
        args.append(gain)
    scratch = []
    if dil > 1:
        scratch.append(pltpu.VMEM((HEADS_PER_BLOCK, s_len, HEAD_DIM), F32))
    return pl.pallas_call(
        functools.partial(_proj_kernel, dil=dil, row_chunk=row_chunk, group_ln=gain is not None),
        out_shape=jax.ShapeDtypeStruct((b_sz, n_blocks * HEADS_PER_BLOCK, dil, s_len // dil, HEAD_DIM), BF16),
        grid=(n_blocks, b_sz),
        in_specs=in_specs,
        out_specs=pl.BlockSpec((None, HEADS_PER_BLOCK, dil, s_len // dil, HEAD_DIM),
                               lambda wb, b: (b, wb, 0, 0, 0)),
        scratch_shapes=scratch,
        compiler_params=_params("arbitrary", "arbitrary"),
        name=f"proj_heads_d{dil}",
    )(*args)


def _dil_attn_kernel(slopes_ref, q0, k0, v0, q1, k1, v1, q2, k2, v2, o_ref, o_scr, l_scr):
    s_len = o_ref.shape[0]
    h = pl.program_id(1)
    scale = HEAD_DIM ** -0.5
    groups = ((q0, k0, v0), (q1, k1, v1), (q2, k2, v2))
    for g, (qr, kr, vr) in enumerate(groups):
        dil = DILATIONS[g]
        cls_len = s_len // dil
        qb = min(128, cls_len)
        kw = min(qb + 2 * N_SIDE, cls_len)
        nb = cls_len // qb
        neg_slope = jnp.full((1, kw), slopes_ref[g * HEADS_PER_BLOCK + h], F32) * (-float(dil))
        base_delta = (lax.broadcasted_iota(I32, (qb, kw), 1) - lax.broadcasted_iota(I32, (qb, kw), 0))
        nb_shift = nb.bit_length() - 1

        def body(idx, carry, qr=qr, kr=kr, vr=vr, dil=dil, cls_len=cls_len, qb=qb, kw=kw, nb=nb,
                 neg_slope=neg_slope, base_delta=base_delta, g=g, nb_shift=nb_shift):
            r = lax.shift_right_logical(idx, nb_shift)
            i = lax.bitwise_and(idx, nb - 1)
            qs = pl.multiple_of(i * qb, qb)
            ks = pl.multiple_of(jnp.clip(qs - N_SIDE, 0, cls_len - kw), N_SIDE)
            q = qr[r, pl.ds(qs, qb), :]
            k = kr[r, pl.ds(ks, kw), :]
            v = vr[r, pl.ds(ks, kw), :]
            dist = jnp.abs(base_delta + (ks - qs))
            bias = jnp.where(dist <= N_SIDE, dist.astype(F32) * neg_slope, NEG_INF)
            s = lax.dot_general(q, k, (((1,), (1,)), ((), ())), preferred_element_type=F32) * scale + bias
            m = jnp.max(s, axis=-1, keepdims=True)
            p = jnp.exp(s - m)
            l = jnp.sum(p, axis=-1, keepdims=True)
            o = jnp.dot(p.astype(BF16), v, preferred_element_type=F32) / l
            lse = jnp.broadcast_to(m + jnp.log(l), (qb, LANES))
            row0 = qs * dil + r
            if dil == 1:
                o_scr[g, pl.ds(row0, qb), :] = o
                l_scr[g, pl.ds(row0, qb), :] = lse
            else:
                o_scr.at[g][pl.ds(row0, qb, stride=dil), :] = o
                l_scr.at[g][pl.ds(row0, qb, stride=dil), :] = lse
            return carry

        lax.fori_loop(0, dil * nb, body, 0, unroll=2)

    chunk = 256
    for c in range(s_len // chunk):
        rows = pl.ds(c * chunk, chunk)
        l0, l1, l2 = l_scr[0, rows, :], l_scr[1, rows, :], l_scr[2, rows, :]
        m = jnp.maximum(jnp.maximum(l0, l1), l2)
        e0, e1, e2 = jnp.exp(l0 - m), jnp.exp(l1 - m), jnp.exp(l2 - m)
        num = e0 * o_scr[0, rows, :] + e1 * o_scr[1, rows, :] + e2 * o_scr[2, rows, :]
        o_ref[rows, :] = (num / (e0 + e1 + e2)).astype(o_ref.dtype)


def _dilated_attention(slopes, qkv):
    b_sz = qkv[0].shape[0]
    s_len = qkv[0].shape[2] * qkv[0].shape[3]
    in_specs = [pl.BlockSpec(memory_space=pltpu.SMEM)]
    args = [slopes]
    for g, arr in enumerate(qkv):
        dil = DILATIONS[g]
        for which in range(3):
            in_specs.append(pl.BlockSpec((None, None, dil, s_len // dil, HEAD_DIM),
                                         lambda b, h, which=which: (b, which * HEADS_PER_BLOCK + h, 0, 0, 0)))
            args.append(arr)
    return pl.pallas_call(
        _dil_attn_kernel,
        out_shape=jax.ShapeDtypeStruct((b_sz, s_len, HEADS_PER_BLOCK * HEAD_DIM), BF16),
        grid=(b_sz, HEADS_PER_BLOCK),
        in_specs=in_specs,
        out_specs=pl.BlockSpec((None, s_len, HEAD_DIM), lambda b, h: (b, 0, h)),
        scratch_shapes=[pltpu.VMEM((3, s_len, HEAD_DIM), F32), pltpu.VMEM((3, s_len, LANES), F32)],
        compiler_params=_params("arbitrary", "arbitrary"),
        name="dilated_attention",
    )(*args)


def _mem_attn_kernel(q_ref, k_ref, v_ref, o_ref):
    s_len = q_ref.shape[0]
    scale = HEAD_DIM ** -0.5
    k = k_ref[...]
    v = v_ref[...]
    chunk = 256
    for c in range(s_len // chunk):
        rows = pl.ds(c * chunk, chunk)
        s = lax.dot_general(q_ref[rows, :], k, (((1,), (1,)), ((), ())), preferred_element_type=F32) * scale
        m = jnp.max(s, axis=-1, keepdims=True)
        p = jnp.exp(s - m)
        l = jnp.sum(p, axis=-1, keepdims=True)
        o = jnp.dot(p.astype(BF16), v, preferred_element_type=F32) / l
        o_ref[rows, :] = o.astype(o_ref.dtype)


def _memory_attention(q_heads, q_head0, kv_heads):
    b_sz, _, _, s_len, _ = q_heads.shape
    n_mem = kv_heads.shape[3]
    return pl.pallas_call(
        _mem_attn_kernel,
        out_shape=jax.ShapeDtypeStruct((b_sz, s_len, HEADS_PER_BLOCK * HEAD_DIM), BF16),
        grid=(b_sz, HEADS_PER_BLOCK),
        in_specs=[
            pl.BlockSpec((None, None, None, s_len, HEAD_DIM), lambda b, h: (b, q_head0 + h, 0, 0, 0)),
            pl.BlockSpec((None, None, None, n_mem, HEAD_DIM), lambda b, h: (b, h, 0, 0, 0)),
            pl.BlockSpec((None, None, None, n_mem, HEAD_DIM), lambda b, h: (b, HEADS_PER_BLOCK + h, 0, 0, 0)),
        ],
        out_specs=pl.BlockSpec((None, s_len, HEAD_DIM), lambda b, h: (b, 0, h)),
        compiler_params=_params("arbitrary", "arbitrary"),
        name="memory_attention",
    )(q_heads, kv_heads, kv_heads)


def _out_ln_kernel(a1_ref, a2_ref, w_ref, res_ref, g_ref, b_ref, o32_ref, o16_ref, w_scr):
    @pl.when(pl.program_id(0) == 0)
    def _():
        w_scr[...] = w_ref[...].astype(BF16)

    k1 = a1_ref.shape[1]
    y = jnp.dot(a1_ref[...], w_scr[0:k1, :], preferred_element_type=F32)
    y = y + jnp.dot(a2_ref[...], w_scr[k1:, :], preferred_element_type=F32)
    z = _layer_norm(ALPHA * res_ref[...] + y, g_ref[...], b_ref[...])
    o32_ref[...] = z
    o16_ref[...] = z.astype(BF16)


def _out_proj_ln(a1, a2, w, res, gain, bias, *, tm=512):
    t_len, d_model = res.shape
    k1, k2 = a1.shape[1], a2.shape[1]
    return pl.pallas_call(
        _out_ln_kernel,
        out_shape=(jax.ShapeDtypeStruct((t_len, d_model), F32), jax.ShapeDtypeStruct((t_len, d_model), BF16)),
        grid=(t_len // tm,),
        in_specs=[
            pl.BlockSpec((tm, k1), lambda i: (i, 0)),
            pl.BlockSpec((tm, k2), lambda i: (i, 0)),
            pl.BlockSpec((k1 + k2, d_model), lambda i: (0, 0)),
            pl.BlockSpec((tm, d_model), lambda i: (i, 0)),
            pl.BlockSpec((1, d_model), lambda i: (0, 0)),
            pl.BlockSpec((1, d_model), lambda i: (0, 0)),
        ],
        out_specs=(pl.BlockSpec((tm, d_model), lambda i: (i, 0)), pl.BlockSpec((tm, d_model), lambda i: (i, 0))),
        scratch_shapes=[pltpu.VMEM((k1 + k2, d_model), BF16)],
        compiler_params=_params("arbitrary"),
        name="out_proj_ln",
    )(a1, a2, w, res, gain.reshape(1, -1), bias.reshape(1, -1))


def _swiglu_chunk(x, wg, wu, wd):
    hg = jnp.dot(x, wg.astype(BF16), preferred_element_type=F32)
    hu = jnp.dot(x, wu.astype(BF16), preferred_element_type=F32)
    h = hg * (1.0 / (1.0 + jnp.exp(-hg))) * hu
    return jnp.dot(h.astype(BF16), wd.astype(BF16), preferred_element_type=F32)


def _ffn_kernel(x_ref, res_ref, wg_ref, wu_ref, wd_ref, g_ref, b_ref, o32_ref, o16_ref, acc_ref):
    f = pl.program_id(1)
    part = _swiglu_chunk(x_ref[...], wg_ref[...], wu_ref[...], wd_ref[...])

    @pl.when(f == 0)
    def _():
        acc_ref[...] = part

    @pl.when(f > 0)
    def _():
        acc_ref[...] += part

    @pl.when(f == pl.num_programs(1) - 1)
    def _():
        z = _layer_norm(ALPHA * res_ref[...] + acc_ref[...], g_ref[...], b_ref[...])
        o32_ref[...] = z
        o16_ref[...] = z.astype(BF16)


def _dense_ffn_ln(x16, res, wg, wu, wd, gain, bias, *, tm=1024, tf=256):
    t_len, d_model = res.shape
    d_ff = wg.shape[1]
    return pl.pallas_call(
        _ffn_kernel,
        out_shape=(jax.ShapeDtypeStruct((t_len, d_model), F32), jax.ShapeDtypeStruct((t_len, d_model), BF16)),
        grid=(t_len // tm, d_ff // tf),
        in_specs=[
            pl.BlockSpec((tm, d_model), lambda i, f: (i, 0)),
            pl.BlockSpec((tm, d_model), lambda i, f: (i, 0)),
            pl.BlockSpec((d_model, tf), lambda i, f: (0, f)),
            pl.BlockSpec((d_model, tf), lambda i, f: (0, f)),
            pl.BlockSpec((tf, d_model), lambda i, f: (f, 0)),
            pl.BlockSpec((1, d_model), lambda i, f: (0, 0)),
            pl.BlockSpec((1, d_model), lambda i, f: (0, 0)),
        ],
        out_specs=(pl.BlockSpec((tm, d_model), lambda i, f: (i, 0)),
                   pl.BlockSpec((tm, d_model), lambda i, f: (i, 0))),
        scratch_shapes=[pltpu.VMEM((tm, d_model), F32)],
        compiler_params=_params("arbitrary", "arbitrary"),
        name="dense_ffn_ln",
    )(x16, res, wg, wu, wd, gain.reshape(1, -1), bias.reshape(1, -1))


def _dft_constants(s_len):
    n = np.arange(s_len, dtype=np.int64)
    ang_s = 2.0 * np.pi * ((n[:, None] * n[None, :]) % s_len) / s_len
    c = np.arange(HEAD_DIM, dtype=np.int64)
    ang_c = 2.0 * np.pi * ((c[:, None] * c[None, :]) % HEAD_DIM) / HEAD_DIM
    norm = 1.0 / np.sqrt(float(s_len * HEAD_DIM))
    cs = np.concatenate([np.cos(ang_s), -np.sin(ang_s)], axis=1) * norm
    return (jnp.asarray(cs, dtype=BF16), jnp.asarray(np.cos(ang_c), dtype=BF16),
            jnp.asarray(np.sin(ang_c), dtype=BF16))


def _fourier_kernel(u_ref, cs_ref, cc_ref, sc_ref, o_ref, ab_scr):
    n_grp, s_len, _ = u_ref.shape
    cc = cc_ref[...]
    sc = sc_ref[...]
    for g in range(n_grp):
        u = u_ref[g]
        cols = pl.ds(g * HEAD_DIM, HEAD_DIM)
        ab_scr[0:s_len, cols] = jnp.dot(u, cc, preferred_element_type=F32).astype(BF16)
        ab_scr[s_len:, cols] = jnp.dot(u, sc, preferred_element_type=F32).astype(BF16)
    chunk = 512
    for c in range(s_len // chunk):
        rows = pl.ds(c * chunk, chunk)
        o_ref[rows, :] = jnp.dot(cs_ref[rows, :], ab_scr[...], preferred_element_type=F32).astype(o_ref.dtype)


def _fourier_mix(u_heads):
    b_sz, n_grp, _, s_len, _ = u_heads.shape
    cs, cc, sc = _dft_constants(s_len)
    const = dict(pipeline_mode=pl.Buffered(1))
    return pl.pallas_call(
        _fourier_kernel,
        out_shape=jax.ShapeDtypeStruct((b_sz, s_len, n_grp * HEAD_DIM), BF16),
        grid=(b_sz,),
        in_specs=[
            pl.BlockSpec((None, n_grp, None, s_len, HEAD_DIM), lambda b: (b, 0, 0, 0, 0)),
            pl.BlockSpec((s_len, 2 * s_len), lambda b: (0, 0), **const),
            pl.BlockSpec((HEAD_DIM, HEAD_DIM), lambda b: (0, 0), **const),
            pl.BlockSpec((HEAD_DIM, HEAD_DIM), lambda b: (0, 0), **const),
        ],
        out_specs=pl.BlockSpec((None, s_len, n_grp * HEAD_DIM), lambda b: (b, 0, 0)),
        scratch_shapes=[pltpu.VMEM((2 * s_len, n_grp * HEAD_DIM), BF16)],
        compiler_params=_params("arbitrary"),
        name="fourier_mix",
    )(u_heads, cs, cc, sc)


def _split_bf16(a):
    hi = a.astype(BF16)
    lo = (a - hi.astype(F32)).astype(BF16)
    return hi, lo


def _router_kernel(x_ref, w_ref, b_ref, route_ref, cnt_ref, carry_ref):
    tq = x_ref.shape[0]

    @pl.when(pl.program_id(0) == 0)
    def _():
        carry_ref[...] = jnp.zeros_like(carry_ref)

    x_hi, x_lo = _split_bf16(x_ref[...])
    w_hi, w_lo = _split_bf16(w_ref[...])
    logits = (jnp.dot(x_hi, w_hi, preferred_element_type=F32)
              + jnp.dot(x_lo, w_hi, preferred_element_type=F32)
              + jnp.dot(x_hi, w_lo, preferred_element_type=F32)) + b_ref[...]
    lane = lax.broadcasted_iota(I32, (tq, LANES), 1).astype(F32)
    logits = jnp.where(lane < N_EXPERTS, logits, NEG_INF)

    def top1(vals):
        m = jnp.max(vals, axis=-1, keepdims=True)
        idx = jnp.min(jnp.where(vals == m, lane, float(LANES)), axis=-1, keepdims=True)
        return m, idx

    m1, i1 = top1(logits)
    oh1 = lane == i1
    m2, i2 = top1(jnp.where(oh1, NEG_INF, logits))
    oh2 = lane == i2
    e2 = jnp.exp(m2 - m1)
    gate1 = 1.0 / (1.0 + e2)
    gate2 = e2 / (1.0 + e2)

    oh1f = oh1.astype(F32)
    oh2f = oh2.astype(F32)
    both = oh1f + oh2f
    tri = (lax.broadcasted_iota(I32, (tq, tq), 0) > lax.broadcasted_iota(I32, (tq, tq), 1)).astype(BF16)
    before = jnp.dot(tri, both.astype(BF16), preferred_element_type=F32) + carry_ref[0:1, :]
    rank1 = jnp.sum(oh1f * before, axis=-1, keepdims=True)
    rank2 = jnp.sum(oh2f * before, axis=-1, keepdims=True)
    carry_ref[...] = carry_ref[...] + jnp.sum(both, axis=0, keepdims=True)

    rec = jnp.zeros((tq, LANES), F32)
    for lane_idx, val in ((ROUTE_E, i1), (ROUTE_E + 1, i2),
                          (ROUTE_RANK, rank1), (ROUTE_RANK + 1, rank2),
                          (ROUTE_GATE, gate1), (ROUTE_GATE + 1, gate2)):
        rec = jnp.where(lane == lane_idx, val, rec)
    route_ref[...] = rec
    cnt_ref[...] = carry_ref[...]


def _router(x, w, b, *, tq=512):
    t_len, d_model = x.shape
    w_pad = jnp.zeros((d_model, LANES), F32).at[:, :N_EXPERTS].set(w)
    b_pad = jnp.zeros((1, LANES), F32).at[0, :N_EXPERTS].set(b)
    return pl.pallas_call(
        _router_kernel,
        out_shape=(jax.ShapeDtypeStruct((t_len, LANES), F32), jax.ShapeDtypeStruct((8, LANES), F32)),
        grid=(t_len // tq,),
        in_specs=[
            pl.BlockSpec((tq, d_model), lambda i: (i, 0)),
            pl.BlockSpec((d_model, LANES), lambda i: (0, 0)),
            pl.BlockSpec((1, LANES), lambda i: (0, 0)),
        ],
        out_specs=(pl.BlockSpec((tq, LANES), lambda i: (i, 0)), pl.BlockSpec((8, LANES), lambda i: (0, 0))),
        scratch_shapes=[pltpu.VMEM((8, LANES), F32)],
        compiler_params=_params("arbitrary"),
        name="router",
    )(x, w_pad, b_pad)


def _dispatch_kernel(pos_ref, x_hbm, init_hbm, xs_hbm, sem):
    del init_hbm
    tq = pos_ref.shape[0] // TOP_K
    base = pl.program_id(0) * tq

    def copy(t, k):
        return pltpu.make_async_copy(x_hbm.at[pl.ds(base + t, 1)],
                                     xs_hbm.at[pl.ds(pos_ref[TOP_K * t + k], 1)], sem)

    def start(t, carry):
        for k in range(TOP_K):
            copy(t, k).start()
        return carry

    def wait(t, carry):
        for k in range(TOP_K):
            copy(t, k).wait()
        return carry

    lax.fori_loop(0, tq, start, 0)
    lax.fori_loop(0, tq, wait, 0)


def _dispatch(x, pos_flat, n_sorted_rows, *, tq=512):
    t_len, d_model = x.shape
    init = jnp.zeros((n_sorted_rows, d_model), x.dtype)
    return pl.pallas_call(
        _dispatch_kernel,
        out_shape=jax.ShapeDtypeStruct((n_sorted_rows, d_model), x.dtype),
        grid=(t_len // tq,),
        in_specs=[
            pl.BlockSpec((TOP_K * tq,), lambda i: (i,), memory_space=pltpu.SMEM),
            pl.BlockSpec(memory_space=pl.ANY),
            pl.BlockSpec(memory_space=pl.ANY),
        ],
        out_specs=pl.BlockSpec(memory_space=pl.ANY),
        scratch_shapes=[pltpu.SemaphoreType.DMA],
        input_output_aliases={2: 0},
        compiler_params=_params("arbitrary"),
        name="moe_dispatch",
    )(pos_flat, x, init)


def _expert_kernel(tile_e_ref, nt_ref, x_ref, wg_ref, wu_ref, wd_ref, o_ref, x16_scr, acc_ref):
    del tile_e_ref
    i = pl.program_id(0)
    c = pl.program_id(1)

    @pl.when(i < nt_ref[0])
    def _():
        @pl.when(c == 0)
        def _():
            x16_scr[...] = x_ref[...].astype(BF16)

        part = _swiglu_chunk(x16_scr[...], wg_ref[...], wu_ref[...], wd_ref[...])

        @pl.when(c == 0)
        def _():
            acc_ref[...] = part

        @pl.when(c > 0)
        def _():
            acc_ref[...] += part

        @pl.when(c == pl.num_programs(1) - 1)
        def _():
            o_ref[...] = acc_ref[...]

    @pl.when(jnp.logical_and(i >= nt_ref[0], c == pl.num_programs(1) - 1))
    def _():
        o_ref[...] = jnp.zeros_like(o_ref)


def _expert_ffn(xs, tile_e, n_tiles, wg, wu, wd, *, tm, tf=512):
    n_rows, d_model = xs.shape
    d_ff = wg.shape[2]
    n_chunks = d_ff // tf
    last = n_chunks - 1

    def row_map(i, c, te, nt):
        return (jnp.minimum(i, nt[0] - 1), 0)

    def chunk_of(i, c, nt):
        return jnp.where(i < nt[0], c, last)

    grid_spec = pltpu.PrefetchScalarGridSpec(
        num_scalar_prefetch=2,
        grid=(n_rows // tm, n_chunks),
        in_specs=[
            pl.BlockSpec((tm, d_model), row_map),
            pl.BlockSpec((None, d_model, tf), lambda i, c, te, nt: (te[i], 0, chunk_of(i, c, nt))),
            pl.BlockSpec((None, d_model, tf), lambda i, c, te, nt: (te[i], 0, chunk_of(i, c, nt))),
            pl.BlockSpec((None, tf, d_model), lambda i, c, te, nt: (te[i], chunk_of(i, c, nt), 0)),
        ],
        out_specs=pl.BlockSpec((tm, d_model), lambda i, c, te, nt: (i, 0)),
        scratch_shapes=[pltpu.VMEM((tm, d_model), BF16), pltpu.VMEM((tm, d_model), F32)],
    )
    return pl.pallas_call(
        _expert_kernel,
        out_shape=jax.ShapeDtypeStruct((n_rows, d_model), F32),
        grid_spec=grid_spec,
        compiler_params=_params("arbitrary", "arbitrary"),
        name="expert_ffn",
    )(tile_e, n_tiles, xs, wg, wu, wd)


def _combine_kernel(pos_ref, res_ref, route_ref, g_ref, b_ref, ys_hbm, o_ref, buf_ref, sem):
    tq = pos_ref.shape[0] // TOP_K

    def copy(t, k):
        return pltpu.make_async_copy(ys_hbm.at[pl.ds(pos_ref[TOP_K * t + k], 1)],
                                     buf_ref.at[k, pl.ds(t, 1)], sem)

    def start(t, carry):
        for k in range(TOP_K):
            copy(t, k).start()
        return carry

    def wait(t, carry):
        for k in range(TOP_K):
            copy(t, k).wait()
        return carry

    lax.fori_loop(0, tq, start, 0)
    lax.fori_loop(0, tq, wait, 0)

    route = route_ref[...]
    y = (route[:, ROUTE_GATE:ROUTE_GATE + 1] * buf_ref[0]
         + route[:, ROUTE_GATE + 1:ROUTE_GATE + 2] * buf_ref[1])
    o_ref[...] = _layer_norm(ALPHA * res_ref[...] + y, g_ref[...], b_ref[...])


def _combine_ln(ys, pos_flat, res, route, gain, bias, *, tq=256):
    t_len, d_model = res.shape
    return pl.pallas_call(
        _combine_kernel,
        out_shape=jax.ShapeDtypeStruct((t_len, d_model), F32),
        grid=(t_len // tq,),
        in_specs=[
            pl.BlockSpec((TOP_K * tq,), lambda i: (i,), memory_space=pltpu.SMEM),
            pl.BlockSpec((tq, d_model), lambda i: (i, 0)),
            pl.BlockSpec((tq, LANES), lambda i: (i, 0)),
            pl.BlockSpec((1, d_model), lambda i: (0, 0)),
            pl.BlockSpec((1, d_model), lambda i: (0, 0)),
            pl.BlockSpec(memory_space=pl.ANY),
        ],
        out_specs=pl.BlockSpec((tq, d_model), lambda i: (i, 0)),
        scratch_shapes=[pltpu.VMEM((TOP_K, tq, d_model), F32), pltpu.SemaphoreType.DMA],
        compiler_params=_params("arbitrary"),
        name="moe_combine_ln",
    )(pos_flat, res, route, gain.reshape(1, -1), bias.reshape(1, -1), ys)


def _moe_ffn_ln(x, router_w, router_b, wg, wu, wd, gain, bias, *, tm=1024):
    t_len, _ = x.shape
    route, counts = _router(x, router_w, router_b)
    experts = route[:, ROUTE_E:ROUTE_E + TOP_K].astype(I32)
    ranks = route[:, ROUTE_RANK:ROUTE_RANK + TOP_K].astype(I32)
    counts = counts[0, :N_EXPERTS].astype(I32)
    tiles_per_expert = (counts + tm - 1) // tm
    tile_end = jnp.cumsum(tiles_per_expert)
    row_start = (tile_end - tiles_per_expert) * tm
    pos_flat = (row_start[experts] + ranks).reshape(-1)
    n_tiles_max = (TOP_K * t_len) // tm + N_EXPERTS
    n_tiles = tile_end[-1:]
    tile_ids = jnp.minimum(jnp.arange(n_tiles_max, dtype=I32), n_tiles[0] - 1)
    tile_e = jnp.minimum(jnp.searchsorted(tile_end, tile_ids, side="right"), N_EXPERTS - 1).astype(I32)

    xs = _dispatch(x, pos_flat, n_tiles_max * tm)
    ys = _expert_ffn(xs, tile_e, n_tiles.astype(I32), wg, wu, wd, tm=tm)
    return _combine_ln(ys, pos_flat, x, route, gain, bias)


def _alibi_slopes():
    exps = np.arange(1, N_ATTN_HEADS + 1, dtype=np.float32) * np.float32(8.0 / N_ATTN_HEADS)
    return jnp.asarray(np.exp2(-exps).astype(np.float32))


def kernel(x, mem, a_w_in, a_w_mem_kv, a_w_out, a_ln1_g, a_ln1_b, a_ffn_gate, a_ffn_up, a_ffn_down, a_ln2_g, a_ln2_b, b_w_in, b_fourier_g, b_w_mem_kv, b_w_out, b_ln1_g, b_ln1_b, b_router_w, b_router_b, b_moe_gate, b_moe_up, b_moe_down, b_ln2_g, b_ln2_b):
    b_sz, s_len, d_model = x.shape
    t_len = b_sz * s_len
    x16 = x.astype(BF16)
    mem16 = mem.astype(BF16)
    x32 = x.reshape(t_len, d_model)

    n_qkv_blocks = 3 * len(DILATIONS)
    qkv = [_proj_heads(x16, a_w_in[0], col_block0=g, col_block_stride=len(DILATIONS), n_blocks=3, dil=dil)
           for g, dil in enumerate(DILATIONS)]
    q_mem = _proj_heads(x16, a_w_in[0], col_block0=n_qkv_blocks, col_block_stride=1, n_blocks=1)
    mem_kv = _proj_heads(mem16, a_w_mem_kv[0], col_block0=0, col_block_stride=1, n_blocks=2)
    attn = _dilated_attention(_alibi_slopes(), qkv).reshape(t_len, -1)
    memo = _memory_attention(q_mem, 0, mem_kv).reshape(t_len, -1)
    x32, x16 = _out_proj_ln(attn, memo, a_w_out[0], x32, a_ln1_g[0], a_ln1_b[0])
    x32, x16 = _dense_ffn_ln(x16, x32, a_ffn_gate[0], a_ffn_up[0], a_ffn_down[0], a_ln2_g[0], a_ln2_b[0])

    x16_b = x16.reshape(b_sz, s_len, d_model)
    u = _proj_heads(x16_b, b_w_in[0], col_block0=0, col_block_stride=1, n_blocks=1, gain=b_fourier_g[0])
    q_mem = _proj_heads(x16_b, b_w_in[0], col_block0=1, col_block_stride=1, n_blocks=1)
    mem_kv = _proj_heads(mem16, b_w_mem_kv[0], col_block0=0, col_block_stride=1, n_blocks=2)
    four = _fourier_mix(u).reshape(t_len, -1)
    memo = _memory_attention(q_mem, 0, mem_kv).reshape(t_len, -1)
    x32, x16 = _out_proj_ln(four, memo, b_w_out[0], x32, b_ln1_g[0], b_ln1_b[0])
    out = _moe_ffn_ln(x32, b_router_w[0], b_router_b[0], b_moe_gate[0], b_moe_up[0], b_moe_down[0],
                      b_ln2_g[0], b_ln2_b[0])
    return out.reshape(b_sz, s_len, d_model)
```

```python
import functools

import numpy as np
import jax
import jax.numpy as jnp
from jax import lax
from jax.experimental import pallas as pl
from jax.experimental.pallas import tpu as pltpu

F32 = jnp.float32
BF16 = jnp.bfloat16
I32 = jnp.int32

LANES = 128
HEAD_DIM = 128
HEADS_PER_BLOCK = 4
DILATIONS = (1, 4, 16)
N_SIDE = 64
ATTN_Q_BLOCK = 64
ATTN_K_WINDOW = 256
ATTN_BATCH = 8
N_ATTN_HEADS = 12
N_EXPERTS = 8
TOP_K = 2
ALPHA = (2.0 * 2) ** 0.25
LN_EPS = 1e-5
NEG_INF = -1e30
VMEM_LIMIT_BYTES = 56 * 1024 * 1024

ROUTE_E, ROUTE_RANK, ROUTE_GATE = 0, 2, 4


def _params(*sem):
    return pltpu.CompilerParams(dimension_semantics=sem, vmem_limit_bytes=VMEM_LIMIT_BYTES)


def _layer_norm(z, g, b=None):
    mu = jnp.mean(z, axis=-1, keepdims=True)
    zc = z - mu
    var = jnp.mean(zc * zc, axis=-1, keepdims=True)
    y = zc * lax.rsqrt(var + LN_EPS) * g
    return y if b is None else y + b


def _proj_kernel(*refs, dil, row_chunk, group_ln):
    if group_ln:
        x_ref, w_ref, g_ref, o_ref = refs[:4]
        scratch = refs[4:]
    else:
        x_ref, w_ref, o_ref = refs[:3]
        g_ref = None
        scratch = refs[3:]
    s_len = x_ref.shape[0]
    w = w_ref[...].astype(BF16)
    for rc in range(s_len // row_chunk):
        rows = pl.ds(rc * row_chunk, row_chunk)
        r = jnp.dot(x_ref[rows, :], w, preferred_element_type=F32)
        for hh in range(HEADS_PER_BLOCK):
            rh = r[:, hh * HEAD_DIM:(hh + 1) * HEAD_DIM]
            if group_ln:
                rh = _layer_norm(rh, g_ref[hh:hh + 1, :])
            if dil == 1:
                o_ref[hh, 0, rows, :] = rh.astype(o_ref.dtype)
            else:
                scratch[0][hh, rows, :] = rh
    if dil > 1:
        acc_ref = scratch[0]
        cls_len = s_len // dil
        for hh in range(HEADS_PER_BLOCK):
            for r_ in range(dil):
                o_ref[hh, r_, :, :] = acc_ref.at[hh][pl.ds(r_, cls_len, stride=dil), :].astype(o_ref.dtype)


def _proj_heads(x, w, *, col_block0, col_block_stride, n_blocks, dil=1, gain=None):
    b_sz, s_len, d_model = x.shape
    blk_cols = HEADS_PER_BLOCK * HEAD_DIM
    row_chunk = min(512, s_len)
    in_specs = [
        pl.BlockSpec((None, s_len, d_model), lambda wb, b: (b, 0, 0)),
        pl.BlockSpec((d_model, blk_cols), lambda wb, b: (0, col_block0 + wb * col_block_stride)),
    ]
    args = [x, w]
    if gain is not None:
        in_specs.append(pl.BlockSpec((HEADS_PER_BLOCK, HEAD_DIM), lambda wb, b: (0, 0)))
        args.append(gain)
    scratch = []
    if dil > 1:
        scratch.append(pltpu.VMEM((HEADS_PER_BLOCK, s_len, HEAD_DIM), F32))
    return pl.pallas_call(
        functools.partial(_proj_kernel, dil=dil, row_chunk=row_chunk, group_ln=gain is not None),
        out_shape=jax.ShapeDtypeStruct((b_sz, n_blocks * HEADS_PER_BLOCK, dil, s_len // dil, HEAD_DIM), BF16),
        grid=(n_blocks, b_sz),
        in_specs=in_specs,
        out_specs=pl.BlockSpec((None, HEADS_PER_BLOCK, dil, s_len // dil, HEAD_DIM),
                               lambda wb, b: (b, wb, 0, 0, 0)),
        scratch_shapes=scratch,
        compiler_params=_params("arbitrary", "arbitrary"),
        name=f"proj_heads_d{dil}",
    )(*args)


def _dil_attn_kernel(slopes_ref, q0, k0, v0, q1, k1, v1, q2, k2, v2, o_ref, o_scr, l_scr, bias_scr):
    s_len = o_ref.shape[0]
    h = pl.program_id(1)
    scale = HEAD_DIM ** -0.5
    qb = ATTN_Q_BLOCK
    groups = ((q0, k0, v0), (q1, k1, v1), (q2, k2, v2))
    for g, (qr, kr, vr) in enumerate(groups):
        dil = DILATIONS[g]
        cls_len = s_len // dil
        kw = min(ATTN_K_WINDOW, cls_len)
        nb = cls_len // qb
        nb_shift = nb.bit_length() - 1

        neg_slope = jnp.full((1, kw), slopes_ref[g * HEADS_PER_BLOCK + h], F32) * (-float(dil))
        base_delta = (lax.broadcasted_iota(I32, (qb, kw), 1) - lax.broadcasted_iota(I32, (qb, kw), 0))
        for case in range(kw // N_SIDE):
            dist = jnp.abs(base_delta - case * N_SIDE)
            bias_scr[g, case, :, 0:kw] = jnp.where(dist <= N_SIDE, dist.astype(F32) * neg_slope, NEG_INF)

        def body(it, carry, qr=qr, kr=kr, vr=vr, dil=dil, cls_len=cls_len, kw=kw, nb=nb, g=g,
                 nb_shift=nb_shift):
            blocks = []
            for j in range(ATTN_BATCH):
                idx = it * ATTN_BATCH + j
                r = lax.shift_right_logical(idx, nb_shift)
                i = lax.bitwise_and(idx, nb - 1)
                qs = pl.multiple_of(i * qb, qb)
                ks = pl.multiple_of(jnp.clip(qs - N_SIDE, 0, cls_len - kw), N_SIDE)
                blocks.append((r, qs, ks))
            scores = []
            for r, qs, ks in blocks:
                q = qr[r, pl.ds(qs, qb), :]
                k = kr[r, pl.ds(ks, kw), :]
                scores.append(lax.dot_general(q, k, (((1,), (1,)), ((), ())), preferred_element_type=F32))
            probs = []
            for (r, qs, ks), s in zip(blocks, scores):
                bias = bias_scr[g, lax.shift_right_logical(qs - ks, N_SIDE.bit_length() - 1), :, 0:kw]
                s = s * scale + bias
                m = jnp.max(s, axis=-1, keepdims=True)
                p = jnp.exp(s - m)
                l = jnp.sum(p, axis=-1, keepdims=True)
                probs.append((p.astype(BF16), m, l))
            for (r, qs, ks), (p, m, l) in zip(blocks, probs):
                v = vr[r, pl.ds(ks, kw), :]
                o = jnp.dot(p, v, preferred_element_type=F32) / l
                lse = jnp.broadcast_to(m + jnp.log(l), (qb, LANES))
                row0 = qs * dil + r
                if dil == 1:
                    o_scr[g, pl.ds(row0, qb), :] = o
                    l_scr[g, pl.ds(row0, qb), :] = lse
                else:
                    o_scr.at[g][pl.ds(row0, qb, stride=dil), :] = o
                    l_scr.at[g][pl.ds(row0, qb, stride=dil), :] = lse
            return carry

        lax.fori_loop(0, dil * nb // ATTN_BATCH, body, 0)

    chunk = 256
    for c in range(s_len // chunk):
        rows = pl.ds(c * chunk, chunk)
        l0, l1, l2 = l_scr[0, rows, :], l_scr[1, rows, :], l_scr[2, rows, :]
        m = jnp.maximum(jnp.maximum(l0, l1), l2)
        e0, e1, e2 = jnp.exp(l0 - m), jnp.exp(l1 - m), jnp.exp(l2 - m)
        num = e0 * o_scr[0, rows, :] + e1 * o_scr[1, rows, :] + e2 * o_scr[2, rows, :]
        o_ref[rows, :] = (num / (e0 + e1 + e2)).astype(o_ref.dtype)


def _dilated_attention(slopes, qkv):
    b_sz = qkv[0].shape[0]
    s_len = qkv[0].shape[2] * qkv[0].shape[3]
    in_specs = [pl.BlockSpec(memory_space=pltpu.SMEM)]
    args = [slopes]
    for g, arr in enumerate(qkv):
        dil = DILATIONS[g]
        for which in range(3):
            in_specs.append(pl.BlockSpec((None, None, dil, s_len // dil, HEAD_DIM),
                                         lambda b, h, which=which: (b, which * HEADS_PER_BLOCK + h, 0, 0, 0)))
            args.append(arr)
    return pl.pallas_call(
        _dil_attn_kernel,
        out_shape=jax.ShapeDtypeStruct((b_sz, s_len, HEADS_PER_BLOCK * HEAD_DIM), BF16),
        grid=(b_sz, HEADS_PER_BLOCK),
        in_specs=in_specs,
        out_specs=pl.BlockSpec((None, s_len, HEAD_DIM), lambda b, h: (b, 0, h)),
        scratch_shapes=[pltpu.VMEM((3, s_len, HEAD_DIM), F32), pltpu.VMEM((3, s_len, LANES), F32),
                        pltpu.VMEM((3, ATTN_K_WINDOW // N_SIDE, ATTN_Q_BLOCK, ATTN_K_WINDOW), F32)],
        compiler_params=_params("arbitrary", "arbitrary"),
        name="dilated_attention",
    )(*args)


def _mem_attn_kernel(q_ref, k_ref, v_ref, o_ref):
    s_len = q_ref.shape[0]
    scale = HEAD_DIM ** -0.5
    k = k_ref[...]
    v = v_ref[...]
    chunk = 256
    for c in range(s_len // chunk):
        rows = pl.ds(c * chunk, chunk)
        s = lax.dot_general(q_ref[rows, :], k, (((1,), (1,)), ((), ())), preferred_element_type=F32) * scale
        m = jnp.max(s, axis=-1, keepdims=True)
        p = jnp.exp(s - m)
        l = jnp.sum(p, axis=-1, keepdims=True)
        o = jnp.dot(p.astype(BF16), v, preferred_element_type=F32) / l
        o_ref[rows, :] = o.astype(o_ref.dtype)


def _memory_attention(q_heads, q_head0, kv_heads):
    b_sz, _, _, s_len, _ = q_heads.shape
    n_mem = kv_heads.shape[3]
    return pl.pallas_call(
        _mem_attn_kernel,
        out_shape=jax.ShapeDtypeStruct((b_sz, s_len, HEADS_PER_BLOCK * HEAD_DIM), BF16),
        grid=(b_sz, HEADS_PER_BLOCK),
        in_specs=[
            pl.BlockSpec((None, None, None, s_len, HEAD_DIM), lambda b, h: (b, q_head0 + h, 0, 0, 0)),
            pl.BlockSpec((None, None, None, n_mem, HEAD_DIM), lambda b, h: (b, h, 0, 0, 0)),
            pl.BlockSpec((None, None, None, n_mem, HEAD_DIM), lambda b, h: (b, HEADS_PER_BLOCK + h, 0, 0, 0)),
        ],
        out_specs=pl.BlockSpec((None, s_len, HEAD_DIM), lambda b, h: (b, 0, h)),
        compiler_params=_params("arbitrary", "arbitrary"),
        name="memory_attention",
    )(q_heads, kv_heads, kv_heads)


def _out_ln_kernel(a1_ref, a2_ref, w_ref, res_ref, g_ref, b_ref, o32_ref, o16_ref, w_scr):
    @pl.when(pl.program_id(0) == 0)
    def _():
        w_scr[...] = w_ref[...].astype(BF16)

    k1 = a1_ref.shape[1]
    y = jnp.dot(a1_ref[...], w_scr[0:k1, :], preferred_element_type=F32)
    y = y + jnp.dot(a2_ref[...], w_scr[k1:, :], preferred_element_type=F32)
    z = _layer_norm(ALPHA * res_ref[...] + y, g_ref[...], b_ref[...])
    o32_ref[...] = z
    o16_ref[...] = z.astype(BF16)


def _out_proj_ln(a1, a2, w, res, gain, bias, *, tm=512):
    t_len, d_model = res.shape
    k1, k2 = a1.shape[1], a2.shape[1]
    return pl.pallas_call(
        _out_ln_kernel,
        out_shape=(jax.ShapeDtypeStruct((t_len, d_model), F32), jax.ShapeDtypeStruct((t_len, d_model), BF16)),
        grid=(t_len // tm,),
        in_specs=[
            pl.BlockSpec((tm, k1), lambda i: (i, 0)),
            pl.BlockSpec((tm, k2), lambda i: (i, 0)),
            pl.BlockSpec((k1 + k2, d_model), lambda i: (0, 0)),
            pl.BlockSpec((tm, d_model), lambda i: (i, 0)),
            pl.BlockSpec((1, d_model), lambda i: (0, 0)),
            pl.BlockSpec((1, d_model), lambda i: (0, 0)),
        ],
        out_specs=(pl.BlockSpec((tm, d_model), lambda i: (i, 0)), pl.BlockSpec((tm, d_model), lambda i: (i, 0))),
        scratch_shapes=[pltpu.VMEM((k1 + k2, d_model), BF16)],
        compiler_params=_params("arbitrary"),
        name="out_proj_ln",
    )(a1, a2, w, res, gain.reshape(1, -1), bias.reshape(1, -1))


def _swiglu_chunk(x, wg, wu, wd):
    hg = jnp.dot(x, wg.astype(BF16), preferred_element_type=F32)
    hu = jnp.dot(x, wu.astype(BF16), preferred_element_type=F32)
    h = hg * (1.0 / (1.0 + jnp.exp(-hg))) * hu
    return jnp.dot(h.astype(BF16), wd.astype(BF16), preferred_element_type=F32)


def _ffn_kernel(x_ref, res_ref, wg_ref, wu_ref, wd_ref, g_ref, b_ref, o32_ref, o16_ref, acc_ref):
    f = pl.program_id(1)
    part = _swiglu_chunk(x_ref[...], wg_ref[...], wu_ref[...], wd_ref[...])

    @pl.when(f == 0)
    def _():
        acc_ref[...] = part

    @pl.when(f > 0)
    def _():
        acc_ref[...] += part

    @pl.when(f == pl.num_programs(1) - 1)
    def _():
        z = _layer_norm(ALPHA * res_ref[...] + acc_ref[...], g_ref[...], b_ref[...])
        o32_ref[...] = z
        o16_ref[...] = z.astype(BF16)


def _dense_ffn_ln(x16, res, wg, wu, wd, gain, bias, *, tm=1024, tf=256):
    t_len, d_model = res.shape
    d_ff = wg.shape[1]
    return pl.pallas_call(
        _ffn_kernel,
        out_shape=(jax.ShapeDtypeStruct((t_len, d_model), F32), jax.ShapeDtypeStruct((t_len, d_model), BF16)),
        grid=(t_len // tm, d_ff // tf),
        in_specs=[
            pl.BlockSpec((tm, d_model), lambda i, f: (i, 0)),
            pl.BlockSpec((tm, d_model), lambda i, f: (i, 0)),
            pl.BlockSpec((d_model, tf), lambda i, f: (0, f)),
            pl.BlockSpec((d_model, tf), lambda i, f: (0, f)),
            pl.BlockSpec((tf, d_model), lambda i, f: (f, 0)),
            pl.BlockSpec((1, d_model), lambda i, f: (0, 0)),
            pl.BlockSpec((1, d_model), lambda i, f: (0, 0)),
        ],
        out_specs=(pl.BlockSpec((tm, d_model), lambda i, f: (i, 0)),
                   pl.BlockSpec((tm, d_model), lambda i, f: (i, 0))),
        scratch_shapes=[pltpu.VMEM((tm, d_model), F32)],
        compiler_params=_params("arbitrary", "arbitrary"),
        name="dense_ffn_ln",
    )(x16, res, wg, wu, wd, gain.reshape(1, -1), bias.reshape(1, -1))


def _dft_constants(s_len):
    n = np.arange(s_len, dtype=np.int64)
    ang_s = 2.0 * np.pi * ((n[:, None] * n[None, :]) % s_len) / s_len
    c = np.arange(HEAD_DIM, dtype=np.int64)
    ang_c = 2.0 * np.pi * ((c[:, None] * c[None, :]) % HEAD_DIM) / HEAD_DIM
    norm = 1.0 / np.sqrt(float(s_len * HEAD_DIM))
    cs = np.concatenate([np.cos(ang_s), -np.sin(ang_s)], axis=1) * norm
    return tuple(jnp.asarray(a, dtype=F32).astype(BF16) for a in (cs, np.cos(ang_c), np.sin(ang_c)))


def _fourier_kernel(u_ref, cs_ref, cc_ref, sc_ref, o_ref, ab_scr):
    n_grp, s_len, _ = u_ref.shape
    cc = cc_ref[...]
    sc = sc_ref[...]
    for g in range(n_grp):
        u = u_ref[g]
        cols = pl.ds(g * HEAD_DIM, HEAD_DIM)
        ab_scr[0:s_len, cols] = jnp.dot(u, cc, preferred_element_type=F32).astype(BF16)
        ab_scr[s_len:, cols] = jnp.dot(u, sc, preferred_element_type=F32).astype(BF16)
    chunk = 512
    for c in range(s_len // chunk):
        rows = pl.ds(c * chunk, chunk)
        o_ref[rows, :] = jnp.dot(cs_ref[rows, :], ab_scr[...], preferred_element_type=F32).astype(o_ref.dtype)


def _fourier_mix(u_heads):
    b_sz, n_grp, _, s_len, _ = u_heads.shape
    cs, cc, sc = _dft_constants(s_len)
    const = dict(pipeline_mode=pl.Buffered(1))
    return pl.pallas_call(
        _fourier_kernel,
        out_shape=jax.ShapeDtypeStruct((b_sz, s_len, n_grp * HEAD_DIM), BF16),
        grid=(b_sz,),
        in_specs=[
            pl.BlockSpec((None, n_grp, None, s_len, HEAD_DIM), lambda b: (b, 0, 0, 0, 0)),
            pl.BlockSpec((s_len, 2 * s_len), lambda b: (0, 0), **const),
            pl.BlockSpec((HEAD_DIM, HEAD_DIM), lambda b: (0, 0), **const),
            pl.BlockSpec((HEAD_DIM, HEAD_DIM), lambda b: (0, 0), **const),
        ],
        out_specs=pl.BlockSpec((None, s_len, n_grp * HEAD_DIM), lambda b: (b, 0, 0)),
        scratch_shapes=[pltpu.VMEM((2 * s_len, n_grp * HEAD_DIM), BF16)],
        compiler_params=_params("arbitrary"),
        name="fourier_mix",
    )(u_heads, cs, cc, sc)


def _split_bf16(a):
    hi = a.astype(BF16)
    lo = (a - hi.astype(F32)).astype(BF16)
    return hi, lo


def _router_kernel(x_ref, w_ref, b_ref, route_ref, cnt_ref, carry_ref):
    tq = x_ref.shape[0]

    @pl.when(pl.program_id(0) == 0)
    def _():
        carry_ref[...] = jnp.zeros_like(carry_ref)

    x_hi, x_lo = _split_bf16(x_ref[...])
    w_hi, w_lo = _split_bf16(w_ref[...])
    logits = (jnp.dot(x_hi, w_hi, preferred_element_type=F32)
              + jnp.dot(x_lo, w_hi, preferred_element_type=F32)
              + jnp.dot(x_hi, w_lo, preferred_element_type=F32)) + b_ref[...]
    lane = lax.broadcasted_iota(I32, (tq, LANES), 1).astype(F32)
    logits = jnp.where(lane < N_EXPERTS, logits, NEG_INF)

    def top1(vals):
        m = jnp.max(vals, axis=-1, keepdims=True)
        idx = jnp.min(jnp.where(vals == m, lane, float(LANES)), axis=-1, keepdims=True)
        return m, idx

    m1, i1 = top1(logits)
    oh1 = lane == i1
    m2, i2 = top1(jnp.where(oh1, NEG_INF, logits))
    oh2 = lane == i2
    e2 = jnp.exp(m2 - m1)
    gate1 = 1.0 / (1.0 + e2)
    gate2 = e2 / (1.0 + e2)

    oh1f = oh1.astype(F32)
    oh2f = oh2.astype(F32)
    both = oh1f + oh2f
    tri = (lax.broadcasted_iota(I32, (tq, tq), 0) > lax.broadcasted_iota(I32, (tq, tq), 1)).astype(BF16)
    before = jnp.dot(tri, both.astype(BF16), preferred_element_type=F32) + carry_ref[0:1, :]
    rank1 = jnp.sum(oh1f * before, axis=-1, keepdims=True)
    rank2 = jnp.sum(oh2f * before, axis=-1, keepdims=True)
    carry_ref[...] = carry_ref[...] + jnp.sum(both, axis=0, keepdims=True)

    rec = jnp.zeros((tq, LANES), F32)
    for lane_idx, val in ((ROUTE_E, i1), (ROUTE_E + 1, i2),
                          (ROUTE_RANK, rank1), (ROUTE_RANK + 1, rank2),
                          (ROUTE_GATE, gate1), (ROUTE_GATE + 1, gate2)):
        rec = jnp.where(lane == lane_idx, val, rec)
    route_ref[...] = rec
    cnt_ref[...] = carry_ref[...]


def _router(x, w, b, *, tq=512):
    t_len, d_model = x.shape
    w_pad = jnp.zeros((d_model, LANES), F32).at[:, :N_EXPERTS].set(w)
    b_pad = jnp.zeros((1, LANES), F32).at[0, :N_EXPERTS].set(b)
    return pl.pallas_call(
        _router_kernel,
        out_shape=(jax.ShapeDtypeStruct((t_len, LANES), F32), jax.ShapeDtypeStruct((8, LANES), F32)),
        grid=(t_len // tq,),
        in_specs=[
            pl.BlockSpec((tq, d_model), lambda i: (i, 0)),
            pl.BlockSpec((d_model, LANES), lambda i: (0, 0)),
            pl.BlockSpec((1, LANES), lambda i: (0, 0)),
        ],
        out_specs=(pl.BlockSpec((tq, LANES), lambda i: (i, 0)), pl.BlockSpec((8, LANES), lambda i: (0, 0))),
        scratch_shapes=[pltpu.VMEM((8, LANES), F32)],
        compiler_params=_params("arbitrary"),
        name="router",
    )(x, w_pad, b_pad)


def _dispatch_kernel(pos_ref, x_ref, init_hbm, xs_hbm, sem):
    del init_hbm
    tq = x_ref.shape[0]

    def start(t, carry):
        for k in range(TOP_K):
            pltpu.make_async_copy(x_ref.at[pl.ds(t, 1)],
                                  xs_hbm.at[pl.ds(pos_ref[TOP_K * t + k], 1)], sem).start()
        return carry

    lax.fori_loop(0, tq, start, 0, unroll=8)
    for k in range(TOP_K):
        pltpu.make_async_copy(x_ref, xs_hbm.at[pl.ds(0, tq)], sem).wait()


def _dispatch(x, pos_flat, n_sorted_rows, *, tq=512):
    t_len, d_model = x.shape
    init = jnp.zeros((n_sorted_rows, d_model), x.dtype)
    return pl.pallas_call(
        _dispatch_kernel,
        out_shape=jax.ShapeDtypeStruct((n_sorted_rows, d_model), x.dtype),
        grid=(t_len // tq,),
        in_specs=[
            pl.BlockSpec((TOP_K * tq,), lambda i: (i,), memory_space=pltpu.SMEM),
            pl.BlockSpec((tq, d_model), lambda i: (i, 0)),
            pl.BlockSpec(memory_space=pl.ANY),
        ],
        out_specs=pl.BlockSpec(memory_space=pl.ANY),
        scratch_shapes=[pltpu.SemaphoreType.DMA],
        input_output_aliases={2: 0},
        compiler_params=_params("arbitrary"),
        name="moe_dispatch",
    )(pos_flat, x, init)


def _expert_kernel(tile_e_ref, nt_ref, x_ref, wg_ref, wu_ref, wd_ref, o_ref, x16_scr, acc_ref):
    del tile_e_ref
    i = pl.program_id(0)
    c = pl.program_id(1)

    @pl.when(i < nt_ref[0])
    def _():
        @pl.when(c == 0)
        def _():
            x16_scr[...] = x_ref[...].astype(BF16)

        part = _swiglu_chunk(x16_scr[...], wg_ref[...], wu_ref[...], wd_ref[...])

        @pl.when(c == 0)
        def _():
            acc_ref[...] = part

        @pl.when(c > 0)
        def _():
            acc_ref[...] += part

        @pl.when(c == pl.num_programs(1) - 1)
        def _():
            o_ref[...] = acc_ref[...]

    @pl.when(jnp.logical_and(i >= nt_ref[0], c == pl.num_programs(1) - 1))
    def _():
        o_ref[...] = jnp.zeros_like(o_ref)


def _expert_ffn(xs, tile_e, n_tiles, wg, wu, wd, *, tm, tf=512):
    n_rows, d_model = xs.shape
    d_ff = wg.shape[2]
    n_chunks = d_ff // tf
    last = n_chunks - 1

    def row_map(i, c, te, nt):
        return (jnp.minimum(i, nt[0] - 1), 0)

    def chunk_of(i, c, nt):
        return jnp.where(i < nt[0], c, last)

    grid_spec = pltpu.PrefetchScalarGridSpec(
        num_scalar_prefetch=2,
        grid=(n_rows // tm, n_chunks),
        in_specs=[
            pl.BlockSpec((tm, d_model), row_map),
            pl.BlockSpec((None, d_model, tf), lambda i, c, te, nt: (te[i], 0, chunk_of(i, c, nt))),
            pl.BlockSpec((None, d_model, tf), lambda i, c, te, nt: (te[i], 0, chunk_of(i, c, nt))),
            pl.BlockSpec((None, tf, d_model), lambda i, c, te, nt: (te[i], chunk_of(i, c, nt), 0)),
        ],
        out_specs=pl.BlockSpec((tm, d_model), lambda i, c, te, nt: (i, 0)),
        scratch_shapes=[pltpu.VMEM((tm, d_model), BF16), pltpu.VMEM((tm, d_model), F32)],
    )
    return pl.pallas_call(
        _expert_kernel,
        out_shape=jax.ShapeDtypeStruct((n_rows, d_model), F32),
        grid_spec=grid_spec,
        compiler_params=_params("arbitrary", "arbitrary"),
        name="expert_ffn",
    )(tile_e, n_tiles, xs, wg, wu, wd)


def _combine_kernel(pos_ref, res_ref, route_ref, g_ref, b_ref, ys_hbm, o_ref, buf_ref, sem):
    tq = res_ref.shape[0]

    def start(t, carry):
        for k in range(TOP_K):
            pltpu.make_async_copy(ys_hbm.at[pl.ds(pos_ref[TOP_K * t + k], 1)],
                                  buf_ref.at[k, pl.ds(t, 1)], sem).start()
        return carry

    lax.fori_loop(0, tq, start, 0, unroll=8)
    for k in range(TOP_K):
        pltpu.make_async_copy(ys_hbm.at[pl.ds(0, tq)], buf_ref.at[k], sem).wait()

    route = route_ref[...]
    y = (route[:, ROUTE_GATE:ROUTE_GATE + 1] * buf_ref[0]
         + route[:, ROUTE_GATE + 1:ROUTE_GATE + 2] * buf_ref[1])
    o_ref[...] = _layer_norm(ALPHA * res_ref[...] + y, g_ref[...], b_ref[...])


def _combine_ln(ys, pos_flat, res, route, gain, bias, *, tq=256):
    t_len, d_model = res.shape
    return pl.pallas_call(
        _combine_kernel,
        out_shape=jax.ShapeDtypeStruct((t_len, d_model), F32),
        grid=(t_len // tq,),
        in_specs=[
            pl.BlockSpec((TOP_K * tq,), lambda i: (i,), memory_space=pltpu.SMEM),
            pl.BlockSpec((tq, d_model), lambda i: (i, 0)),
            pl.BlockSpec((tq, LANES), lambda i: (i, 0)),
            pl.BlockSpec((1, d_model), lambda i: (0, 0)),
            pl.BlockSpec((1, d_model), lambda i: (0, 0)),
            pl.BlockSpec(memory_space=pl.ANY),
        ],
        out_specs=pl.BlockSpec((tq, d_model), lambda i: (i, 0)),
        scratch_shapes=[pltpu.VMEM((TOP_K, tq, d_model), F32), pltpu.SemaphoreType.DMA],
        compiler_params=_params("arbitrary"),
        name="moe_combine_ln",
    )(pos_flat, res, route, gain.reshape(1, -1), bias.reshape(1, -1), ys)


def _moe_ffn_ln(x, router_w, router_b, wg, wu, wd, gain, bias, *, tm=1024):
    t_len, _ = x.shape
    route, counts = _router(x, router_w, router_b)
    experts = route[:, ROUTE_E:ROUTE_E + TOP_K].astype(I32)
    ranks = route[:, ROUTE_RANK:ROUTE_RANK + TOP_K].astype(I32)
    counts = counts[0, :N_EXPERTS].astype(I32)
    tiles_per_expert = (counts + tm - 1) // tm
    tile_end = jnp.cumsum(tiles_per_expert)
    row_start = (tile_end - tiles_per_expert) * tm
    pos_flat = (row_start[experts] + ranks).reshape(-1)
    n_tiles_max = (TOP_K * t_len) // tm + N_EXPERTS
    n_tiles = tile_end[-1:]
    tile_ids = jnp.minimum(jnp.arange(n_tiles_max, dtype=I32), n_tiles[0] - 1)
    tile_e = jnp.sum((tile_ids[:, None] >= tile_end[None, :-1]).astype(I32), axis=1)

    xs = _dispatch(x, pos_flat, n_tiles_max * tm)
    ys = _expert_ffn(xs, tile_e, n_tiles.astype(I32), wg, wu, wd, tm=tm)
    return _combine_ln(ys, pos_flat, x, route, gain, bias)


def _alibi_slopes():
    exps = np.arange(1, N_ATTN_HEADS + 1, dtype=np.float32) * np.float32(8.0 / N_ATTN_HEADS)
    return jnp.asarray(np.exp2(-exps).astype(np.float32))


def kernel(x, mem, a_w_in, a_w_mem_kv, a_w_out, a_ln1_g, a_ln1_b, a_ffn_gate, a_ffn_up, a_ffn_down, a_ln2_g, a_ln2_b, b_w_in, b_fourier_g, b_w_mem_kv, b_w_out, b_ln1_g, b_ln1_b, b_router_w, b_router_b, b_moe_gate, b_moe_up, b_moe_down, b_ln2_g, b_ln2_b):
    b_sz, s_len, d_model = x.shape
    t_len = b_sz * s_len
    x16 = x.astype(BF16)
    mem16 = mem.astype(BF16)
    x32 = x.reshape(t_len, d_model)

    n_qkv_blocks = 3 * len(DILATIONS)
    qkv = [_proj_heads(x16, a_w_in[0], col_block0=g, col_block_stride=len(DILATIONS), n_blocks=3, dil=dil)
           for g, dil in enumerate(DILATIONS)]
    q_mem = _proj_heads(x16, a_w_in[0], col_block0=n_qkv_blocks, col_block_stride=1, n_blocks=1)
    mem_kv = _proj_heads(mem16, a_w_mem_kv[0], col_block0=0, col_block_stride=1, n_blocks=2)
    attn = _dilated_attention(_alibi_slopes(), qkv).reshape(t_len, -1)
    memo = _memory_attention(q_mem, 0, mem_kv).reshape(t_len, -1)
    x32, x16 = _out_proj_ln(attn, memo, a_w_out[0], x32, a_ln1_g[0], a_ln1_b[0])
    x32, x16 = _dense_ffn_ln(x16, x32, a_ffn_gate[0], a_ffn_up[0], a_ffn_down[0], a_ln2_g[0], a_ln2_b[0])

    x16_b = x16.reshape(b_sz, s_len, d_model)
    u = _proj_heads(x16_b, b_w_in[0], col_block0=0, col_block_stride=1, n_blocks=1, gain=b_fourier_g[0])
    q_mem = _proj_heads(x16_b, b_w_in[0], col_block0=1, col_block_stride=1, n_blocks=1)
    mem_kv = _proj_heads(mem16, b_w_mem_kv[0], col_block0=0, col_block_stride=1, n_blocks=2)
    four = _fourier_mix(u).reshape(t_len, -1)
    memo = _memory_attention(q_mem, 0, mem_kv).reshape(t_len, -1)
    x32, x16 = _out_proj_ln(four, memo, b_w_out[0], x32, b_ln1_g[0], b_ln1_b[0])
    out = _moe_ffn_ln(x32, b_router_w[0], b_router_b[0], b_moe_gate[0], b_moe_up[0], b_moe_down[0],
                      b_ln2_g[0], b_ln2_b[0])
    return out.reshape(b_sz, s_len, d_model)
```

```python
import functools

import numpy as np
import jax
import jax.numpy as jnp
from jax import lax
from jax.experimental import pallas as pl
from jax.experimental.pallas import tpu as pltpu

F32 = jnp.float32
BF16 = jnp.bfloat16
I32 = jnp.int32

LANES = 128
HEAD_DIM = 128
HEADS_PER_BLOCK = 4
DILATIONS = (1, 4, 16)
N_SIDE = 64
ATTN_Q_BLOCK = 64
ATTN_K_WINDOW = 256
ATTN_BATCH = 8
N_ATTN_HEADS = 12
N_EXPERTS = 8
TOP_K = 2
ALPHA = (2.0 * 2) ** 0.25
LN_EPS = 1e-5
NEG_INF = -1e30
VMEM_LIMIT_BYTES = 56 * 1024 * 1024

ROUTE_E, ROUTE_RANK, ROUTE_GATE = 0, 2, 4


def _params(*sem):
    return pltpu.CompilerParams(dimension_semantics=sem, vmem_limit_bytes=VMEM_LIMIT_BYTES)


def _layer_norm(z, g, b=None):
    mu = jnp.mean(z, axis=-1, keepdims=True)
    zc = z - mu
    var = jnp.mean(zc * zc, axis=-1, keepdims=True)
    y = zc * lax.rsqrt(var + LN_EPS) * g
    return y if b is None else y + b


def _proj_kernel(*refs, dil, row_chunk, group_ln):
    if group_ln:
        x_ref, w_ref, g_ref, o_ref = refs[:4]
        scratch = refs[4:]
    else:
        x_ref, w_ref, o_ref = refs[:3]
        g_ref = None
        scratch = refs[3:]
    s_len = x_ref.shape[0]
    w = w_ref[...]
    for rc in range(s_len // row_chunk):
        rows = pl.ds(rc * row_chunk, row_chunk)
        r = jnp.dot(x_ref[rows, :], w, preferred_element_type=F32)
        for hh in range(HEADS_PER_BLOCK):
            rh = r[:, hh * HEAD_DIM:(hh + 1) * HEAD_DIM]
            if group_ln:
                rh = _layer_norm(rh, g_ref[hh:hh + 1, :])
            if dil == 1:
                o_ref[hh, 0, rows, :] = rh.astype(o_ref.dtype)
            else:
                scratch[0][hh, rows, :] = rh
    if dil > 1:
        acc_ref = scratch[0]
        cls_len = s_len // dil
        for hh in range(HEADS_PER_BLOCK):
            for r_ in range(dil):
                o_ref[hh, r_, :, :] = acc_ref.at[hh][pl.ds(r_, cls_len, stride=dil), :].astype(o_ref.dtype)


def _proj_heads(x, w, *, col_block0, col_block_stride, n_blocks, dil=1, gain=None):
    b_sz, s_len, d_model = x.shape
    blk_cols = HEADS_PER_BLOCK * HEAD_DIM
    row_chunk = min(512, s_len)
    in_specs = [
        pl.BlockSpec((None, s_len, d_model), lambda wb, b: (b, 0, 0)),
        pl.BlockSpec((d_model, blk_cols), lambda wb, b: (0, col_block0 + wb * col_block_stride)),
    ]
    args = [x, w]
    if gain is not None:
        in_specs.append(pl.BlockSpec((HEADS_PER_BLOCK, HEAD_DIM), lambda wb, b: (0, 0)))
        args.append(gain)
    scratch = []
    if dil > 1:
        scratch.append(pltpu.VMEM((HEADS_PER_BLOCK, s_len, HEAD_DIM), F32))
    return pl.pallas_call(
        functools.partial(_proj_kernel, dil=dil, row_chunk=row_chunk, group_ln=gain is not None),
        out_shape=jax.ShapeDtypeStruct((b_sz, n_blocks * HEADS_PER_BLOCK, dil, s_len // dil, HEAD_DIM), BF16),
        grid=(n_blocks, b_sz),
        in_specs=in_specs,
        out_specs=pl.BlockSpec((None, HEADS_PER_BLOCK, dil, s_len // dil, HEAD_DIM),
                               lambda wb, b: (b, wb, 0, 0, 0)),
        scratch_shapes=scratch,
        compiler_params=_params("arbitrary", "arbitrary"),
        name=f"proj_heads_d{dil}",
    )(*args)


def _dil_attn_kernel(slopes_ref, q0, k0, v0, q1, k1, v1, q2, k2, v2, o_ref, o_scr, l_scr, bias_scr):
    s_len = o_ref.shape[0]
    h = pl.program_id(1)
    scale = HEAD_DIM ** -0.5
    qb = ATTN_Q_BLOCK
    groups = ((q0, k0, v0), (q1, k1, v1), (q2, k2, v2))
    for g, (qr, kr, vr) in enumerate(groups):
        dil = DILATIONS[g]
        cls_len = s_len // dil
        kw = min(ATTN_K_WINDOW, cls_len)
        nb = cls_len // qb
        nb_shift = nb.bit_length() - 1

        neg_slope = jnp.full((1, kw), slopes_ref[g * HEADS_PER_BLOCK + h], F32) * (-float(dil))
        base_delta = (lax.broadcasted_iota(I32, (qb, kw), 1) - lax.broadcasted_iota(I32, (qb, kw), 0))
        for case in range(kw // N_SIDE):
            dist = jnp.abs(base_delta - case * N_SIDE)
            bias_scr[g, case, :, 0:kw] = jnp.where(dist <= N_SIDE, dist.astype(F32) * neg_slope, NEG_INF)

        def body(it, carry, qr=qr, kr=kr, vr=vr, dil=dil, cls_len=cls_len, kw=kw, nb=nb, g=g,
                 nb_shift=nb_shift):
            blocks = []
            for j in range(ATTN_BATCH):
                idx = it * ATTN_BATCH + j
                r = lax.shift_right_logical(idx, nb_shift)
                i = lax.bitwise_and(idx, nb - 1)
                qs = pl.multiple_of(i * qb, qb)
                ks = pl.multiple_of(jnp.clip(qs - N_SIDE, 0, cls_len - kw), N_SIDE)
                blocks.append((r, qs, ks))
            scores = []
            for r, qs, ks in blocks:
                q = qr[r, pl.ds(qs, qb), :]
                k = kr[r, pl.ds(ks, kw), :]
                scores.append(lax.dot_general(q, k, (((1,), (1,)), ((), ())), preferred_element_type=F32))
            probs = []
            for (r, qs, ks), s in zip(blocks, scores):
                bias = bias_scr[g, lax.shift_right_logical(qs - ks, N_SIDE.bit_length() - 1), :, 0:kw]
                s = s * scale + bias
                m = jnp.max(s, axis=-1, keepdims=True)
                p = jnp.exp(s - m)
                l = jnp.sum(p, axis=-1, keepdims=True)
                probs.append((p.astype(BF16), m, l))
            for (r, qs, ks), (p, m, l) in zip(blocks, probs):
                v = vr[r, pl.ds(ks, kw), :]
                o = jnp.dot(p, v, preferred_element_type=F32) / l
                lse = jnp.broadcast_to(m + jnp.log(l), (qb, LANES))
                row0 = qs * dil + r
                if dil == 1:
                    o_scr[g, pl.ds(row0, qb), :] = o
                    l_scr[g, pl.ds(row0, qb), :] = lse
                else:
                    o_scr.at[g][pl.ds(row0, qb, stride=dil), :] = o
                    l_scr.at[g][pl.ds(row0, qb, stride=dil), :] = lse
            return carry

        lax.fori_loop(0, dil * nb // ATTN_BATCH, body, 0)

    chunk = 256
    for c in range(s_len // chunk):
        rows = pl.ds(c * chunk, chunk)
        l0, l1, l2 = l_scr[0, rows, :], l_scr[1, rows, :], l_scr[2, rows, :]
        m = jnp.maximum(jnp.maximum(l0, l1), l2)
        e0, e1, e2 = jnp.exp(l0 - m), jnp.exp(l1 - m), jnp.exp(l2 - m)
        num = e0 * o_scr[0, rows, :] + e1 * o_scr[1, rows, :] + e2 * o_scr[2, rows, :]
        o_ref[rows, :] = (num / (e0 + e1 + e2)).astype(o_ref.dtype)


def _dilated_attention(slopes, qkv):
    b_sz = qkv[0].shape[0]
    s_len = qkv[0].shape[2] * qkv[0].shape[3]
    in_specs = [pl.BlockSpec(memory_space=pltpu.SMEM)]
    args = [slopes]
    for g, arr in enumerate(qkv):
        dil = DILATIONS[g]
        for which in range(3):
            in_specs.append(pl.BlockSpec((None, None, dil, s_len // dil, HEAD_DIM),
                                         lambda b, h, which=which: (b, which * HEADS_PER_BLOCK + h, 0, 0, 0)))
            args.append(arr)
    return pl.pallas_call(
        _dil_attn_kernel,
        out_shape=jax.ShapeDtypeStruct((b_sz, s_len, HEADS_PER_BLOCK * HEAD_DIM), BF16),
        grid=(b_sz, HEADS_PER_BLOCK),
        in_specs=in_specs,
        out_specs=pl.BlockSpec((None, s_len, HEAD_DIM), lambda b, h: (b, 0, h)),
        scratch_shapes=[pltpu.VMEM((3, s_len, HEAD_DIM), F32), pltpu.VMEM((3, s_len, LANES), F32),
                        pltpu.VMEM((3, ATTN_K_WINDOW // N_SIDE, ATTN_Q_BLOCK, ATTN_K_WINDOW), F32)],
        compiler_params=_params("arbitrary", "arbitrary"),
        name="dilated_attention",
    )(*args)


def _mem_attn_kernel(q_ref, k_ref, v_ref, o_ref):
    s_len = q_ref.shape[0]
    scale = HEAD_DIM ** -0.5
    k = k_ref[...]
    v = v_ref[...]
    chunk = 256
    for c in range(s_len // chunk):
        rows = pl.ds(c * chunk, chunk)
        s = lax.dot_general(q_ref[rows, :], k, (((1,), (1,)), ((), ())), preferred_element_type=F32) * scale
        m = jnp.max(s, axis=-1, keepdims=True)
        p = jnp.exp(s - m)
        l = jnp.sum(p, axis=-1, keepdims=True)
        o = jnp.dot(p.astype(BF16), v, preferred_element_type=F32) / l
        o_ref[rows, :] = o.astype(o_ref.dtype)


def _memory_attention(q_heads, q_head0, kv_heads):
    b_sz, _, _, s_len, _ = q_heads.shape
    n_mem = kv_heads.shape[3]
    return pl.pallas_call(
        _mem_attn_kernel,
        out_shape=jax.ShapeDtypeStruct((b_sz, s_len, HEADS_PER_BLOCK * HEAD_DIM), BF16),
        grid=(b_sz, HEADS_PER_BLOCK),
        in_specs=[
            pl.BlockSpec((None, None, None, s_len, HEAD_DIM), lambda b, h: (b, q_head0 + h, 0, 0, 0)),
            pl.BlockSpec((None, None, None, n_mem, HEAD_DIM), lambda b, h: (b, h, 0, 0, 0)),
            pl.BlockSpec((None, None, None, n_mem, HEAD_DIM), lambda b, h: (b, HEADS_PER_BLOCK + h, 0, 0, 0)),
        ],
        out_specs=pl.BlockSpec((None, s_len, HEAD_DIM), lambda b, h: (b, 0, h)),
        compiler_params=_params("arbitrary", "arbitrary"),
        name="memory_attention",
    )(q_heads, kv_heads, kv_heads)


def _out_ln_kernel(a1_ref, a2_ref, w_ref, res_ref, g_ref, b_ref, o32_ref, o16_ref):
    k1 = a1_ref.shape[1]
    y = jnp.dot(a1_ref[...], w_ref[0:k1, :], preferred_element_type=F32)
    y = y + jnp.dot(a2_ref[...], w_ref[k1:, :], preferred_element_type=F32)
    z = _layer_norm(ALPHA * res_ref[...] + y, g_ref[...], b_ref[...])
    o32_ref[...] = z
    o16_ref[...] = z.astype(BF16)


def _out_proj_ln(a1, a2, w, res, gain, bias, *, tm=512):
    t_len, d_model = res.shape
    k1, k2 = a1.shape[1], a2.shape[1]
    return pl.pallas_call(
        _out_ln_kernel,
        out_shape=(jax.ShapeDtypeStruct((t_len, d_model), F32), jax.ShapeDtypeStruct((t_len, d_model), BF16)),
        grid=(t_len // tm,),
        in_specs=[
            pl.BlockSpec((tm, k1), lambda i: (i, 0)),
            pl.BlockSpec((tm, k2), lambda i: (i, 0)),
            pl.BlockSpec((k1 + k2, d_model), lambda i: (0, 0), pipeline_mode=pl.Buffered(1)),
            pl.BlockSpec((tm, d_model), lambda i: (i, 0)),
            pl.BlockSpec((1, d_model), lambda i: (0, 0)),
            pl.BlockSpec((1, d_model), lambda i: (0, 0)),
        ],
        out_specs=(pl.BlockSpec((tm, d_model), lambda i: (i, 0)), pl.BlockSpec((tm, d_model), lambda i: (i, 0))),
        compiler_params=_params("arbitrary"),
        name="out_proj_ln",
    )(a1, a2, w, res, gain.reshape(1, -1), bias.reshape(1, -1))


def _swiglu_chunk(x, wg, wu, wd):
    hg = jnp.dot(x, wg, preferred_element_type=F32)
    hu = jnp.dot(x, wu, preferred_element_type=F32)
    h = hg * (1.0 / (1.0 + jnp.exp(-hg))) * hu
    return jnp.dot(h.astype(BF16), wd, preferred_element_type=F32)


def _ffn_kernel(x_ref, res_ref, wg_ref, wu_ref, wd_ref, g_ref, b_ref, o32_ref, o16_ref, *, tf):
    x = x_ref[...]
    acc = None
    for f in range(wg_ref.shape[1] // tf):
        cols = pl.ds(f * tf, tf)
        part = _swiglu_chunk(x, wg_ref[:, cols], wu_ref[:, cols], wd_ref[cols, :])
        acc = part if acc is None else acc + part
    z = _layer_norm(ALPHA * res_ref[...] + acc, g_ref[...], b_ref[...])
    o32_ref[...] = z
    o16_ref[...] = z.astype(BF16)


def _dense_ffn_ln(x16, res, wg, wu, wd, gain, bias, *, tm=512, tf=256):
    t_len, d_model = res.shape
    d_ff = wg.shape[1]
    resident = dict(pipeline_mode=pl.Buffered(1))
    return pl.pallas_call(
        functools.partial(_ffn_kernel, tf=tf),
        out_shape=(jax.ShapeDtypeStruct((t_len, d_model), F32), jax.ShapeDtypeStruct((t_len, d_model), BF16)),
        grid=(t_len // tm,),
        in_specs=[
            pl.BlockSpec((tm, d_model), lambda i: (i, 0)),
            pl.BlockSpec((tm, d_model), lambda i: (i, 0)),
            pl.BlockSpec((d_model, d_ff), lambda i: (0, 0), **resident),
            pl.BlockSpec((d_model, d_ff), lambda i: (0, 0), **resident),
            pl.BlockSpec((d_ff, d_model), lambda i: (0, 0), **resident),
            pl.BlockSpec((1, d_model), lambda i: (0, 0)),
            pl.BlockSpec((1, d_model), lambda i: (0, 0)),
        ],
        out_specs=(pl.BlockSpec((tm, d_model), lambda i: (i, 0)),
                   pl.BlockSpec((tm, d_model), lambda i: (i, 0))),
        compiler_params=_params("arbitrary"),
        name="dense_ffn_ln",
    )(x16, res, wg, wu, wd, gain.reshape(1, -1), bias.reshape(1, -1))


def _dft_constants(s_len):
    n = np.arange(s_len, dtype=np.int64)
    ang_s = 2.0 * np.pi * ((n[:, None] * n[None, :]) % s_len) / s_len
    c = np.arange(HEAD_DIM, dtype=np.int64)
    ang_c = 2.0 * np.pi * ((c[:, None] * c[None, :]) % HEAD_DIM) / HEAD_DIM
    norm = 1.0 / np.sqrt(float(s_len * HEAD_DIM))
    cs = np.concatenate([np.cos(ang_s), -np.sin(ang_s)], axis=1) * norm
    return tuple(jnp.asarray(a, dtype=F32).astype(BF16) for a in (cs, np.cos(ang_c), np.sin(ang_c)))


def _fourier_kernel(u_ref, cs_ref, cc_ref, sc_ref, o_ref, ab_scr):
    n_grp, s_len, _ = u_ref.shape
    cc = cc_ref[...]
    sc = sc_ref[...]
    for g in range(n_grp):
        u = u_ref[g]
        cols = pl.ds(g * HEAD_DIM, HEAD_DIM)
        ab_scr[0:s_len, cols] = jnp.dot(u, cc, preferred_element_type=F32).astype(BF16)
        ab_scr[s_len:, cols] = jnp.dot(u, sc, preferred_element_type=F32).astype(BF16)
    chunk = 512
    for c in range(s_len // chunk):
        rows = pl.ds(c * chunk, chunk)
        o_ref[rows, :] = jnp.dot(cs_ref[rows, :], ab_scr[...], preferred_element_type=F32).astype(o_ref.dtype)


def _fourier_mix(u_heads):
    b_sz, n_grp, _, s_len, _ = u_heads.shape
    cs, cc, sc = _dft_constants(s_len)
    const = dict(pipeline_mode=pl.Buffered(1))
    return pl.pallas_call(
        _fourier_kernel,
        out_shape=jax.ShapeDtypeStruct((b_sz, s_len, n_grp * HEAD_DIM), BF16),
        grid=(b_sz,),
        in_specs=[
            pl.BlockSpec((None, n_grp, None, s_len, HEAD_DIM), lambda b: (b, 0, 0, 0, 0)),
            pl.BlockSpec((s_len, 2 * s_len), lambda b: (0, 0), **const),
            pl.BlockSpec((HEAD_DIM, HEAD_DIM), lambda b: (0, 0), **const),
            pl.BlockSpec((HEAD_DIM, HEAD_DIM), lambda b: (0, 0), **const),
        ],
        out_specs=pl.BlockSpec((None, s_len, n_grp * HEAD_DIM), lambda b: (b, 0, 0)),
        scratch_shapes=[pltpu.VMEM((2 * s_len, n_grp * HEAD_DIM), BF16)],
        compiler_params=_params("arbitrary"),
        name="fourier_mix",
    )(u_heads, cs, cc, sc)


def _split_bf16(a):
    hi = a.astype(BF16)
    lo = (a - hi.astype(F32)).astype(BF16)
    return hi, lo


def _router_kernel(x_ref, w_ref, b_ref, route_ref, cnt_ref, carry_ref):
    tq = x_ref.shape[0]

    @pl.when(pl.program_id(0) == 0)
    def _():
        carry_ref[...] = jnp.zeros_like(carry_ref)

    x_hi, x_lo = _split_bf16(x_ref[...])
    w_hi, w_lo = _split_bf16(w_ref[...])
    logits = (jnp.dot(x_hi, w_hi, preferred_element_type=F32)
              + jnp.dot(x_lo, w_hi, preferred_element_type=F32)
              + jnp.dot(x_hi, w_lo, preferred_element_type=F32)) + b_ref[...]
    lane = lax.broadcasted_iota(I32, (tq, LANES), 1).astype(F32)
    logits = jnp.where(lane < N_EXPERTS, logits, NEG_INF)

    def top1(vals):
        m = jnp.max(vals, axis=-1, keepdims=True)
        idx = jnp.min(jnp.where(vals == m, lane, float(LANES)), axis=-1, keepdims=True)
        return m, idx

    m1, i1 = top1(logits)
    oh1 = lane == i1
    m2, i2 = top1(jnp.where(oh1, NEG_INF, logits))
    oh2 = lane == i2
    e2 = jnp.exp(m2 - m1)
    gate1 = 1.0 / (1.0 + e2)
    gate2 = e2 / (1.0 + e2)

    oh1f = oh1.astype(F32)
    oh2f = oh2.astype(F32)
    both = oh1f + oh2f
    tri = (lax.broadcasted_iota(I32, (tq, tq), 0) > lax.broadcasted_iota(I32, (tq, tq), 1)).astype(BF16)
    before = jnp.dot(tri, both.astype(BF16), preferred_element_type=F32) + carry_ref[0:1, :]
    rank1 = jnp.sum(oh1f * before, axis=-1, keepdims=True)
    rank2 = jnp.sum(oh2f * before, axis=-1, keepdims=True)
    carry_ref[...] = carry_ref[...] + jnp.sum(both, axis=0, keepdims=True)

    rec = jnp.zeros((tq, LANES), F32)
    for lane_idx, val in ((ROUTE_E, i1), (ROUTE_E + 1, i2),
                          (ROUTE_RANK, rank1), (ROUTE_RANK + 1, rank2),
                          (ROUTE_GATE, gate1), (ROUTE_GATE + 1, gate2)):
        rec = jnp.where(lane == lane_idx, val, rec)
    route_ref[...] = rec
    cnt_ref[...] = carry_ref[...]


def _router(x, w, b, *, tq=512):
    t_len, d_model = x.shape
    w_pad = jnp.zeros((d_model, LANES), F32).at[:, :N_EXPERTS].set(w)
    b_pad = jnp.zeros((1, LANES), F32).at[0, :N_EXPERTS].set(b)
    return pl.pallas_call(
        _router_kernel,
        out_shape=(jax.ShapeDtypeStruct((t_len, LANES), F32), jax.ShapeDtypeStruct((8, LANES), F32)),
        grid=(t_len // tq,),
        in_specs=[
            pl.BlockSpec((tq, d_model), lambda i: (i, 0)),
            pl.BlockSpec((d_model, LANES), lambda i: (0, 0)),
            pl.BlockSpec((1, LANES), lambda i: (0, 0)),
        ],
        out_specs=(pl.BlockSpec((tq, LANES), lambda i: (i, 0)), pl.BlockSpec((8, LANES), lambda i: (0, 0))),
        scratch_shapes=[pltpu.VMEM((8, LANES), F32)],
        compiler_params=_params("arbitrary"),
        name="router",
    )(x, w_pad, b_pad)


def _dispatch_kernel(pos_ref, x_ref, init_hbm, xs_hbm, sem):
    del init_hbm
    tq = x_ref.shape[0]

    def start(t, carry):
        for k in range(TOP_K):
            pltpu.make_async_copy(x_ref.at[pl.ds(t, 1)],
                                  xs_hbm.at[pl.ds(pos_ref[TOP_K * t + k], 1)], sem).start()
        return carry

    lax.fori_loop(0, tq, start, 0, unroll=8)
    for k in range(TOP_K):
        pltpu.make_async_copy(x_ref, xs_hbm.at[pl.ds(0, tq)], sem).wait()


def _dispatch(x, pos_flat, n_sorted_rows, *, tq=512):
    t_len, d_model = x.shape
    init = jnp.zeros((n_sorted_rows, d_model), x.dtype)
    return pl.pallas_call(
        _dispatch_kernel,
        out_shape=jax.ShapeDtypeStruct((n_sorted_rows, d_model), x.dtype),
        grid=(t_len // tq,),
        in_specs=[
            pl.BlockSpec((TOP_K * tq,), lambda i: (i,), memory_space=pltpu.SMEM),
            pl.BlockSpec((tq, d_model), lambda i: (i, 0)),
            pl.BlockSpec(memory_space=pl.ANY),
        ],
        out_specs=pl.BlockSpec(memory_space=pl.ANY),
        scratch_shapes=[pltpu.SemaphoreType.DMA],
        input_output_aliases={2: 0},
        compiler_params=_params("arbitrary"),
        name="moe_dispatch",
    )(pos_flat, x, init)


def _expert_kernel(tile_e_ref, nt_ref, x_ref, wg_ref, wu_ref, wd_ref, o_ref, x16_scr):
    del tile_e_ref
    i = pl.program_id(0)
    c = pl.program_id(1)

    @pl.when(c == 0)
    def _():
        o_ref[...] = jnp.zeros_like(o_ref)
        x16_scr[...] = x_ref[...].astype(BF16)

    @pl.when(i < nt_ref[0])
    def _():
        o_ref[...] += _swiglu_chunk(x16_scr[...], wg_ref[...].astype(BF16), wu_ref[...].astype(BF16),
                                    wd_ref[...].astype(BF16))


def _expert_ffn(xs, tile_e, n_tiles, wg, wu, wd, *, tm, tf=512):
    n_rows, d_model = xs.shape
    d_ff = wg.shape[2]
    n_chunks = d_ff // tf
    last = n_chunks - 1

    def row_map(i, c, te, nt):
        return (jnp.minimum(i, nt[0] - 1), 0)

    def chunk_of(i, c, nt):
        return jnp.where(i < nt[0], c, last)

    grid_spec = pltpu.PrefetchScalarGridSpec(
        num_scalar_prefetch=2,
        grid=(n_rows // tm, n_chunks),
        in_specs=[
            pl.BlockSpec((tm, d_model), row_map),
            pl.BlockSpec((None, d_model, tf), lambda i, c, te, nt: (te[i], 0, chunk_of(i, c, nt))),
            pl.BlockSpec((None, d_model, tf), lambda i, c, te, nt: (te[i], 0, chunk_of(i, c, nt))),
            pl.BlockSpec((None, tf, d_model), lambda i, c, te, nt: (te[i], chunk_of(i, c, nt), 0)),
        ],
        out_specs=pl.BlockSpec((tm, d_model), lambda i, c, te, nt: (i, 0)),
        scratch_shapes=[pltpu.VMEM((tm, d_model), BF16)],
    )
    return pl.pallas_call(
        _expert_kernel,
        out_shape=jax.ShapeDtypeStruct((n_rows, d_model), F32),
        grid_spec=grid_spec,
        compiler_params=_params("arbitrary", "arbitrary"),
        name="expert_ffn",
    )(tile_e, n_tiles, xs, wg, wu, wd)


def _combine_kernel(pos_ref, res_ref, route_ref, g_ref, b_ref, ys_hbm, o_ref, buf_ref, sem):
    tq = res_ref.shape[0]

    def start(t, carry):
        for k in range(TOP_K):
            pltpu.make_async_copy(ys_hbm.at[pl.ds(pos_ref[TOP_K * t + k], 1)],
                                  buf_ref.at[k, pl.ds(t, 1)], sem).start()
        return carry

    lax.fori_loop(0, tq, start, 0, unroll=8)
    for k in range(TOP_K):
        pltpu.make_async_copy(ys_hbm.at[pl.ds(0, tq)], buf_ref.at[k], sem).wait()

    route = route_ref[...]
    y = (route[:, ROUTE_GATE:ROUTE_GATE + 1] * buf_ref[0]
         + route[:, ROUTE_GATE + 1:ROUTE_GATE + 2] * buf_ref[1])
    o_ref[...] = _layer_norm(ALPHA * res_ref[...] + y, g_ref[...], b_ref[...])


def _combine_ln(ys, pos_flat, res, route, gain, bias, *, tq=256):
    t_len, d_model = res.shape
    return pl.pallas_call(
        _combine_kernel,
        out_shape=jax.ShapeDtypeStruct((t_len, d_model), F32),
        grid=(t_len // tq,),
        in_specs=[
            pl.BlockSpec((TOP_K * tq,), lambda i: (i,), memory_space=pltpu.SMEM),
            pl.BlockSpec((tq, d_model), lambda i: (i, 0)),
            pl.BlockSpec((tq, LANES), lambda i: (i, 0)),
            pl.BlockSpec((1, d_model), lambda i: (0, 0)),
            pl.BlockSpec((1, d_model), lambda i: (0, 0)),
            pl.BlockSpec(memory_space=pl.ANY),
        ],
        out_specs=pl.BlockSpec((tq, d_model), lambda i: (i, 0)),
        scratch_shapes=[pltpu.VMEM((TOP_K, tq, d_model), F32), pltpu.SemaphoreType.DMA],
        compiler_params=_params("arbitrary"),
        name="moe_combine_ln",
    )(pos_flat, res, route, gain.reshape(1, -1), bias.reshape(1, -1), ys)


def _moe_ffn_ln(x, router_w, router_b, wg, wu, wd, gain, bias, *, tm=1024):
    t_len, _ = x.shape
    route, counts = _router(x, router_w, router_b)
    experts = route[:, ROUTE_E:ROUTE_E + TOP_K].astype(I32)
    ranks = route[:, ROUTE_RANK:ROUTE_RANK + TOP_K].astype(I32)
    counts = counts[0, :N_EXPERTS].astype(I32)
    tiles_per_expert = (counts + tm - 1) // tm
    tile_end = jnp.cumsum(tiles_per_expert)
    row_start = (tile_end - tiles_per_expert) * tm
    pos_flat = (row_start[experts] + ranks).reshape(-1)
    n_tiles_max = (TOP_K * t_len) // tm + N_EXPERTS
    n_tiles = tile_end[-1:]
    tile_ids = jnp.minimum(jnp.arange(n_tiles_max, dtype=I32), n_tiles[0] - 1)
    tile_e = jnp.sum((tile_ids[:, None] >= tile_end[None, :-1]).astype(I32), axis=1)

    xs = _dispatch(x, pos_flat, n_tiles_max * tm)
    ys = _expert_ffn(xs, tile_e, n_tiles.astype(I32), wg, wu, wd, tm=tm)
    return _combine_ln(ys, pos_flat, x, route, gain, bias)


def _alibi_slopes():
    exps = np.arange(1, N_ATTN_HEADS + 1, dtype=np.float32) * np.float32(8.0 / N_ATTN_HEADS)
    return jnp.asarray(np.exp2(-exps).astype(np.float32))


def kernel(x, mem, a_w_in, a_w_mem_kv, a_w_out, a_ln1_g, a_ln1_b, a_ffn_gate, a_ffn_up, a_ffn_down, a_ln2_g, a_ln2_b, b_w_in, b_fourier_g, b_w_mem_kv, b_w_out, b_ln1_g, b_ln1_b, b_router_w, b_router_b, b_moe_gate, b_moe_up, b_moe_down, b_ln2_g, b_ln2_b):
    b_sz, s_len, d_model = x.shape
    t_len = b_sz * s_len
    x16 = x.astype(BF16)
    mem16 = mem.astype(BF16)
    x32 = x.reshape(t_len, d_model)
    a_w_in, a_w_mem_kv, a_w_out, a_ffn_gate, a_ffn_up, a_ffn_down, b_w_in, b_w_mem_kv, b_w_out = (
        w[0].astype(BF16) for w in (a_w_in, a_w_mem_kv, a_w_out, a_ffn_gate, a_ffn_up, a_ffn_down,
                                    b_w_in, b_w_mem_kv, b_w_out))

    n_qkv_blocks = 3 * len(DILATIONS)
    qkv = [_proj_heads(x16, a_w_in, col_block0=g, col_block_stride=len(DILATIONS), n_blocks=3, dil=dil)
           for g, dil in enumerate(DILATIONS)]
    q_mem = _proj_heads(x16, a_w_in, col_block0=n_qkv_blocks, col_block_stride=1, n_blocks=1)
    mem_kv = _proj_heads(mem16, a_w_mem_kv, col_block0=0, col_block_stride=1, n_blocks=2)
    attn = _dilated_attention(_alibi_slopes(), qkv).reshape(t_len, -1)
    memo = _memory_attention(q_mem, 0, mem_kv).reshape(t_len, -1)
    x32, x16 = _out_proj_ln(attn, memo, a_w_out, x32, a_ln1_g[0], a_ln1_b[0])
    x32, x16 = _dense_ffn_ln(x16, x32, a_ffn_gate, a_ffn_up, a_ffn_down, a_ln2_g[0], a_ln2_b[0])

    x16_b = x16.reshape(b_sz, s_len, d_model)
    u = _proj_heads(x16_b, b_w_in, col_block0=0, col_block_stride=1, n_blocks=1, gain=b_fourier_g[0])
    q_mem = _proj_heads(x16_b, b_w_in, col_block0=1, col_block_stride=1, n_blocks=1)
    mem_kv = _proj_heads(mem16, b_w_mem_kv, col_block0=0, col_block_stride=1, n_blocks=2)
    four = _fourier_mix(u).reshape(t_len, -1)
    memo = _memory_attention(q_mem, 0, mem_kv).reshape(t_len, -1)
    x32, x16 = _out_proj_ln(four, memo, b_w_out, x32, b_ln1_g[0], b_ln1_b[0])
    out = _moe_ffn_ln(x32, b_router_w[0], b_router_b[0], b_moe_gate[0], b_moe_up[0], b_moe_down[0],
                      b_ln2_g[0], b_ln2_b[0])
    return out.reshape(b_sz, s_len, d_model)
```

```python
import functools

import numpy as np
import jax
import jax.numpy as jnp
from jax import lax
from jax.experimental import pallas as pl
from jax.experimental.pallas import tpu as pltpu

F32 = jnp.float32
BF16 = jnp.bfloat16
I32 = jnp.int32

LANES = 128
HEAD_DIM = 128
HEADS_PER_BLOCK = 4
DILATIONS = (1, 4, 16)
N_SIDE = 64
ATTN_Q_BLOCK = 64
ATTN_K_WINDOW = 256
ATTN_BATCH = 16
N_ATTN_HEADS = 12
N_EXPERTS = 8
TOP_K = 2
ALPHA = (2.0 * 2) ** 0.25
LN_EPS = 1e-5
NEG_INF = -1e30
VMEM_LIMIT_BYTES = 56 * 1024 * 1024

ROUTE_E, ROUTE_RANK, ROUTE_GATE = 0, 2, 4


def _params(*sem):
    return pltpu.CompilerParams(dimension_semantics=sem, vmem_limit_bytes=VMEM_LIMIT_BYTES)


def _layer_norm(z, g, b=None):
    mu = jnp.mean(z, axis=-1, keepdims=True)
    zc = z - mu
    var = jnp.mean(zc * zc, axis=-1, keepdims=True)
    y = zc * lax.rsqrt(var + LN_EPS) * g
    return y if b is None else y + b


def _proj_kernel(*refs, dil, row_chunk, group_ln):
    if group_ln:
        x_ref, w_ref, g_ref, o_ref = refs[:4]
        scratch = refs[4:]
    else:
        x_ref, w_ref, o_ref = refs[:3]
        g_ref = None
        scratch = refs[3:]
    s_len = x_ref.shape[0]
    w = w_ref[...]
    for rc in range(s_len // row_chunk):
        rows = pl.ds(rc * row_chunk, row_chunk)
        r = jnp.dot(x_ref[rows, :], w, preferred_element_type=F32)
        for hh in range(HEADS_PER_BLOCK):
            rh = r[:, hh * HEAD_DIM:(hh + 1) * HEAD_DIM]
            if group_ln:
                rh = _layer_norm(rh, g_ref[hh:hh + 1, :])
            if dil == 1:
                o_ref[hh, 0, rows, :] = rh.astype(o_ref.dtype)
            else:
                scratch[0][hh, rows, :] = rh
    if dil > 1:
        acc_ref = scratch[0]
        cls_len = s_len // dil
        for hh in range(HEADS_PER_BLOCK):
            for r_ in range(dil):
                o_ref[hh, r_, :, :] = acc_ref.at[hh][pl.ds(r_, cls_len, stride=dil), :].astype(o_ref.dtype)


def _proj_heads(x, w, *, col_block0, col_block_stride, n_blocks, dil=1, gain=None):
    b_sz, s_len, d_model = x.shape
    blk_cols = HEADS_PER_BLOCK * HEAD_DIM
    row_chunk = min(512, s_len)
    in_specs = [
        pl.BlockSpec((None, s_len, d_model), lambda b, wb: (b, 0, 0)),
        pl.BlockSpec((d_model, blk_cols), lambda b, wb: (0, col_block0 + wb * col_block_stride)),
    ]
    args = [x, w]
    if gain is not None:
        in_specs.append(pl.BlockSpec((HEADS_PER_BLOCK, HEAD_DIM), lambda b, wb: (0, 0)))
        args.append(gain)
    scratch = []
    if dil > 1:
        scratch.append(pltpu.VMEM((HEADS_PER_BLOCK, s_len, HEAD_DIM), F32))
    return pl.pallas_call(
        functools.partial(_proj_kernel, dil=dil, row_chunk=row_chunk, group_ln=gain is not None),
        out_shape=jax.ShapeDtypeStruct((b_sz, n_blocks * HEADS_PER_BLOCK, dil, s_len // dil, HEAD_DIM), BF16),
        grid=(b_sz, n_blocks),
        in_specs=in_specs,
        out_specs=pl.BlockSpec((None, HEADS_PER_BLOCK, dil, s_len // dil, HEAD_DIM),
                               lambda b, wb: (b, wb, 0, 0, 0)),
        scratch_shapes=scratch,
        compiler_params=_params("arbitrary", "arbitrary"),
        name=f"proj_heads_d{dil}",
    )(*args)


def _dil_attn_kernel(slopes_ref, q0, k0, v0, q1, k1, v1, q2, k2, v2, o_ref, o_scr, l_scr, bias_scr):
    s_len = o_ref.shape[0]
    h = pl.program_id(1)
    scale = HEAD_DIM ** -0.5
    qb = ATTN_Q_BLOCK
    groups = ((q0, k0, v0), (q1, k1, v1), (q2, k2, v2))
    for g, (qr, kr, vr) in enumerate(groups):
        dil = DILATIONS[g]
        cls_len = s_len // dil
        kw = min(ATTN_K_WINDOW, cls_len)
        nb = cls_len // qb
        nb_shift = nb.bit_length() - 1

        neg_slope = jnp.full((1, kw), slopes_ref[g * HEADS_PER_BLOCK + h], F32) * (-float(dil))
        base_delta = (lax.broadcasted_iota(I32, (qb, kw), 1) - lax.broadcasted_iota(I32, (qb, kw), 0))
        for case in range(kw // N_SIDE):
            dist = jnp.abs(base_delta - case * N_SIDE)
            bias_scr[g, case, :, 0:kw] = jnp.where(dist <= N_SIDE, dist.astype(F32) * neg_slope, NEG_INF)

        def body(it, carry, qr=qr, kr=kr, vr=vr, dil=dil, cls_len=cls_len, kw=kw, nb=nb, g=g,
                 nb_shift=nb_shift):
            blocks = []
            for j in range(ATTN_BATCH):
                idx = it * ATTN_BATCH + j
                r = lax.shift_right_logical(idx, nb_shift)
                i = lax.bitwise_and(idx, nb - 1)
                qs = pl.multiple_of(i * qb, qb)
                ks = pl.multiple_of(jnp.clip(qs - N_SIDE, 0, cls_len - kw), N_SIDE)
                blocks.append((r, qs, ks))
            scores = []
            for r, qs, ks in blocks:
                q = qr[r, pl.ds(qs, qb), :]
                k = kr[r, pl.ds(ks, kw), :]
                scores.append(lax.dot_general(q, k, (((1,), (1,)), ((), ())), preferred_element_type=F32))
            probs = []
            for (r, qs, ks), s in zip(blocks, scores):
                bias = bias_scr[g, lax.shift_right_logical(qs - ks, N_SIDE.bit_length() - 1), :, 0:kw]
                s = s * scale + bias
                m = jnp.max(s, axis=-1, keepdims=True)
                p = jnp.exp(s - m)
                l = jnp.sum(p, axis=-1, keepdims=True)
                probs.append((p.astype(BF16), m, l))
            for (r, qs, ks), (p, m, l) in zip(blocks, probs):
                v = vr[r, pl.ds(ks, kw), :]
                o = jnp.dot(p, v, preferred_element_type=F32) / l
                lse = jnp.broadcast_to(m + jnp.log(l), (qb, LANES))
                row0 = qs * dil + r
                if dil == 1:
                    o_scr[g, pl.ds(row0, qb), :] = o
                    l_scr[g, pl.ds(row0, qb), :] = lse
                else:
                    o_scr.at[g][pl.ds(row0, qb, stride=dil), :] = o
                    l_scr.at[g][pl.ds(row0, qb, stride=dil), :] = lse
            return carry

        lax.fori_loop(0, dil * nb // ATTN_BATCH, body, 0)

    chunk = 256
    for c in range(s_len // chunk):
        rows = pl.ds(c * chunk, chunk)
        l0, l1, l2 = l_scr[0, rows, :], l_scr[1, rows, :], l_scr[2, rows, :]
        m = jnp.maximum(jnp.maximum(l0, l1), l2)
        e0, e1, e2 = jnp.exp(l0 - m), jnp.exp(l1 - m), jnp.exp(l2 - m)
        num = e0 * o_scr[0, rows, :] + e1 * o_scr[1, rows, :] + e2 * o_scr[2, rows, :]
        o_ref[rows, :] = (num / (e0 + e1 + e2)).astype(o_ref.dtype)


def _dilated_attention(slopes, qkv):
    b_sz = qkv[0].shape[0]
    s_len = qkv[0].shape[2] * qkv[0].shape[3]
    in_specs = [pl.BlockSpec(memory_space=pltpu.SMEM)]
    args = [slopes]
    for g, arr in enumerate(qkv):
        dil = DILATIONS[g]
        for which in range(3):
            in_specs.append(pl.BlockSpec((None, None, dil, s_len // dil, HEAD_DIM),
                                         lambda b, h, which=which: (b, which * HEADS_PER_BLOCK + h, 0, 0, 0)))
            args.append(arr)
    return pl.pallas_call(
        _dil_attn_kernel,
        out_shape=jax.ShapeDtypeStruct((b_sz, s_len, HEADS_PER_BLOCK * HEAD_DIM), BF16),
        grid=(b_sz, HEADS_PER_BLOCK),
        in_specs=in_specs,
        out_specs=pl.BlockSpec((None, s_len, HEAD_DIM), lambda b, h: (b, 0, h)),
        scratch_shapes=[pltpu.VMEM((3, s_len, HEAD_DIM), F32), pltpu.VMEM((3, s_len, LANES), F32),
                        pltpu.VMEM((3, ATTN_K_WINDOW // N_SIDE, ATTN_Q_BLOCK, ATTN_K_WINDOW), F32)],
        compiler_params=_params("arbitrary", "arbitrary"),
        name="dilated_attention",
    )(*args)


def _mem_attn_kernel(q_ref, k_ref, v_ref, o_ref):
    s_len = q_ref.shape[0]
    scale = HEAD_DIM ** -0.5
    k = k_ref[...]
    v = v_ref[...]
    chunk = 256
    row_chunks = [pl.ds(c * chunk, chunk) for c in range(s_len // chunk)]
    scores = [lax.dot_general(q_ref[rows, :], k, (((1,), (1,)), ((), ())), preferred_element_type=F32)
              for rows in row_chunks]
    probs = []
    for s in scores:
        s = s * scale
        m = jnp.max(s, axis=-1, keepdims=True)
        p = jnp.exp(s - m)
        probs.append((p.astype(BF16), jnp.sum(p, axis=-1, keepdims=True)))
    for rows, (p, l) in zip(row_chunks, probs):
        o_ref[rows, :] = (jnp.dot(p, v, preferred_element_type=F32) / l).astype(o_ref.dtype)


def _memory_attention(q_heads, q_head0, kv_heads):
    b_sz, _, _, s_len, _ = q_heads.shape
    n_mem = kv_heads.shape[3]
    return pl.pallas_call(
        _mem_attn_kernel,
        out_shape=jax.ShapeDtypeStruct((b_sz, s_len, HEADS_PER_BLOCK * HEAD_DIM), BF16),
        grid=(b_sz, HEADS_PER_BLOCK),
        in_specs=[
            pl.BlockSpec((None, None, None, s_len, HEAD_DIM), lambda b, h: (b, q_head0 + h, 0, 0, 0)),
            pl.BlockSpec((None, None, None, n_mem, HEAD_DIM), lambda b, h: (b, h, 0, 0, 0)),
            pl.BlockSpec((None, None, None, n_mem, HEAD_DIM), lambda b, h: (b, HEADS_PER_BLOCK + h, 0, 0, 0)),
        ],
        out_specs=pl.BlockSpec((None, s_len, HEAD_DIM), lambda b, h: (b, 0, h)),
        compiler_params=_params("arbitrary", "arbitrary"),
        name="memory_attention",
    )(q_heads, kv_heads, kv_heads)


def _out_ln_kernel(a1_ref, a2_ref, w_ref, res_ref, g_ref, b_ref, o32_ref, o16_ref):
    k1 = a1_ref.shape[1]
    y = jnp.dot(a1_ref[...], w_ref[0:k1, :], preferred_element_type=F32)
    y = y + jnp.dot(a2_ref[...], w_ref[k1:, :], preferred_element_type=F32)
    z = _layer_norm(ALPHA * res_ref[...] + y, g_ref[...], b_ref[...])
    o32_ref[...] = z
    o16_ref[...] = z.astype(BF16)


def _out_proj_ln(a1, a2, w, res, gain, bias, *, tm=512):
    t_len, d_model = res.shape
    k1, k2 = a1.shape[1], a2.shape[1]
    return pl.pallas_call(
        _out_ln_kernel,
        out_shape=(jax.ShapeDtypeStruct((t_len, d_model), F32), jax.ShapeDtypeStruct((t_len, d_model), BF16)),
        grid=(t_len // tm,),
        in_specs=[
            pl.BlockSpec((tm, k1), lambda i: (i, 0)),
            pl.BlockSpec((tm, k2), lambda i: (i, 0)),
            pl.BlockSpec((k1 + k2, d_model), lambda i: (0, 0), pipeline_mode=pl.Buffered(1)),
            pl.BlockSpec((tm, d_model), lambda i: (i, 0)),
            pl.BlockSpec((1, d_model), lambda i: (0, 0)),
            pl.BlockSpec((1, d_model), lambda i: (0, 0)),
        ],
        out_specs=(pl.BlockSpec((tm, d_model), lambda i: (i, 0)), pl.BlockSpec((tm, d_model), lambda i: (i, 0))),
        compiler_params=_params("arbitrary"),
        name="out_proj_ln",
    )(a1, a2, w, res, gain.reshape(1, -1), bias.reshape(1, -1))


def _swiglu_chunk(x, wg, wu, wd):
    hg = jnp.dot(x, wg, preferred_element_type=F32)
    hu = jnp.dot(x, wu, preferred_element_type=F32)
    h = hg * (1.0 / (1.0 + jnp.exp(-hg))) * hu
    return jnp.dot(h.astype(BF16), wd, preferred_element_type=F32)


def _mixer_ffn_kernel(a1_ref, a2_ref, wo_ref, res_ref, g1_ref, b1_ref, wg_ref, wu_ref, wd_ref, g2_ref, b2_ref,
                      o32_ref, o16_ref, *, tf):
    k1 = a1_ref.shape[1]
    y = jnp.dot(a1_ref[...], wo_ref[0:k1, :], preferred_element_type=F32)
    y = y + jnp.dot(a2_ref[...], wo_ref[k1:, :], preferred_element_type=F32)
    x1 = _layer_norm(ALPHA * res_ref[...] + y, g1_ref[...], b1_ref[...])
    x16 = x1.astype(BF16)
    acc = None
    for f in range(wg_ref.shape[1] // tf):
        cols = pl.ds(f * tf, tf)
        part = _swiglu_chunk(x16, wg_ref[:, cols], wu_ref[:, cols], wd_ref[cols, :])
        acc = part if acc is None else acc + part
    z = _layer_norm(ALPHA * x1 + acc, g2_ref[...], b2_ref[...])
    o32_ref[...] = z
    o16_ref[...] = z.astype(BF16)


def _mixer_ffn_ln(a1, a2, wo, res, g1, b1, wg, wu, wd, g2, b2, *, tm=512, tf=256):
    t_len, d_model = res.shape
    k1, k2 = a1.shape[1], a2.shape[1]
    d_ff = wg.shape[1]
    resident = dict(pipeline_mode=pl.Buffered(1))
    vec = pl.BlockSpec((1, d_model), lambda i: (0, 0))
    return pl.pallas_call(
        functools.partial(_mixer_ffn_kernel, tf=tf),
        out_shape=(jax.ShapeDtypeStruct((t_len, d_model), F32), jax.ShapeDtypeStruct((t_len, d_model), BF16)),
        grid=(t_len // tm,),
        in_specs=[
            pl.BlockSpec((tm, k1), lambda i: (i, 0)),
            pl.BlockSpec((tm, k2), lambda i: (i, 0)),
            pl.BlockSpec((k1 + k2, d_model), lambda i: (0, 0), **resident),
            pl.BlockSpec((tm, d_model), lambda i: (i, 0)),
            vec, vec,
            pl.BlockSpec((d_model, d_ff), lambda i: (0, 0), **resident),
            pl.BlockSpec((d_model, d_ff), lambda i: (0, 0), **resident),
            pl.BlockSpec((d_ff, d_model), lambda i: (0, 0), **resident),
            vec, vec,
        ],
        out_specs=(pl.BlockSpec((tm, d_model), lambda i: (i, 0)),
                   pl.BlockSpec((tm, d_model), lambda i: (i, 0))),
        compiler_params=_params("arbitrary"),
        name="mixer_ffn_ln",
    )(a1, a2, wo, res, g1.reshape(1, -1), b1.reshape(1, -1), wg, wu, wd, g2.reshape(1, -1), b2.reshape(1, -1))


def _dft_constants(s_len):
    n = np.arange(s_len, dtype=np.int64)
    ang_s = 2.0 * np.pi * ((n[:, None] * n[None, :]) % s_len) / s_len
    c = np.arange(HEAD_DIM, dtype=np.int64)
    ang_c = 2.0 * np.pi * ((c[:, None] * c[None, :]) % HEAD_DIM) / HEAD_DIM
    norm = 1.0 / np.sqrt(float(s_len * HEAD_DIM))
    cs = np.concatenate([np.cos(ang_s), -np.sin(ang_s)], axis=1) * norm
    return tuple(jnp.asarray(a, dtype=F32).astype(BF16) for a in (cs, np.cos(ang_c), np.sin(ang_c)))


def _fourier_kernel(u_ref, cs_ref, cc_ref, sc_ref, o_ref, ab_scr):
    n_grp, s_len, _ = u_ref.shape
    cc = cc_ref[...]
    sc = sc_ref[...]
    for g in range(n_grp):
        u = u_ref[g]
        cols = pl.ds(g * HEAD_DIM, HEAD_DIM)
        ab_scr[0:s_len, cols] = jnp.dot(u, cc, preferred_element_type=F32).astype(BF16)
        ab_scr[s_len:, cols] = jnp.dot(u, sc, preferred_element_type=F32).astype(BF16)
    chunk = 512
    for c in range(s_len // chunk):
        rows = pl.ds(c * chunk, chunk)
        o_ref[rows, :] = jnp.dot(cs_ref[rows, :], ab_scr[...], preferred_element_type=F32).astype(o_ref.dtype)


def _fourier_mix(u_heads):
    b_sz, n_grp, _, s_len, _ = u_heads.shape
    cs, cc, sc = _dft_constants(s_len)
    const = dict(pipeline_mode=pl.Buffered(1))
    return pl.pallas_call(
        _fourier_kernel,
        out_shape=jax.ShapeDtypeStruct((b_sz, s_len, n_grp * HEAD_DIM), BF16),
        grid=(b_sz,),
        in_specs=[
            pl.BlockSpec((None, n_grp, None, s_len, HEAD_DIM), lambda b: (b, 0, 0, 0, 0)),
            pl.BlockSpec((s_len, 2 * s_len), lambda b: (0, 0), **const),
            pl.BlockSpec((HEAD_DIM, HEAD_DIM), lambda b: (0, 0), **const),
            pl.BlockSpec((HEAD_DIM, HEAD_DIM), lambda b: (0, 0), **const),
        ],
        out_specs=pl.BlockSpec((None, s_len, n_grp * HEAD_DIM), lambda b: (b, 0, 0)),
        scratch_shapes=[pltpu.VMEM((2 * s_len, n_grp * HEAD_DIM), BF16)],
        compiler_params=_params("arbitrary"),
        name="fourier_mix",
    )(u_heads, cs, cc, sc)


def _split_bf16(a):
    hi = a.astype(BF16)
    lo = (a - hi.astype(F32)).astype(BF16)
    return hi, lo


def _router_kernel(x_ref, w_ref, b_ref, route_ref, cnt_ref, carry_ref):
    tq = x_ref.shape[0]

    @pl.when(pl.program_id(0) == 0)
    def _():
        carry_ref[...] = jnp.zeros_like(carry_ref)

    x_hi, x_lo = _split_bf16(x_ref[...])
    w_hi, w_lo = _split_bf16(w_ref[...])
    logits = (jnp.dot(x_hi, w_hi, preferred_element_type=F32)
              + jnp.dot(x_lo, w_hi, preferred_element_type=F32)
              + jnp.dot(x_hi, w_lo, preferred_element_type=F32)) + b_ref[...]
    lane = lax.broadcasted_iota(I32, (tq, LANES), 1).astype(F32)
    logits = jnp.where(lane < N_EXPERTS, logits, NEG_INF)

    def top1(vals):
        m = jnp.max(vals, axis=-1, keepdims=True)
        idx = jnp.min(jnp.where(vals == m, lane, float(LANES)), axis=-1, keepdims=True)
        return m, idx

    m1, i1 = top1(logits)
    oh1 = lane == i1
    m2, i2 = top1(jnp.where(oh1, NEG_INF, logits))
    oh2 = lane == i2
    e2 = jnp.exp(m2 - m1)
    gate1 = 1.0 / (1.0 + e2)
    gate2 = e2 / (1.0 + e2)

    oh1f = oh1.astype(F32)
    oh2f = oh2.astype(F32)
    both = oh1f + oh2f
    tri = (lax.broadcasted_iota(I32, (tq, tq), 0) > lax.broadcasted_iota(I32, (tq, tq), 1)).astype(BF16)
    before = jnp.dot(tri, both.astype(BF16), preferred_element_type=F32) + carry_ref[0:1, :]
    rank1 = jnp.sum(oh1f * before, axis=-1, keepdims=True)
    rank2 = jnp.sum(oh2f * before, axis=-1, keepdims=True)
    carry_ref[...] = carry_ref[...] + jnp.sum(both, axis=0, keepdims=True)

    rec = jnp.zeros((tq, LANES), F32)
    for lane_idx, val in ((ROUTE_E, i1), (ROUTE_E + 1, i2),
                          (ROUTE_RANK, rank1), (ROUTE_RANK + 1, rank2),
                          (ROUTE_GATE, gate1), (ROUTE_GATE + 1, gate2)):
        rec = jnp.where(lane == lane_idx, val, rec)
    route_ref[...] = rec
    cnt_ref[...] = carry_ref[...]


def _router(x, w, b, *, tq=512):
    t_len, d_model = x.shape
    w_pad = jnp.zeros((d_model, LANES), F32).at[:, :N_EXPERTS].set(w)
    b_pad = jnp.zeros((1, LANES), F32).at[0, :N_EXPERTS].set(b)
    return pl.pallas_call(
        _router_kernel,
        out_shape=(jax.ShapeDtypeStruct((t_len, LANES), F32), jax.ShapeDtypeStruct((8, LANES), F32)),
        grid=(t_len // tq,),
        in_specs=[
            pl.BlockSpec((tq, d_model), lambda i: (i, 0)),
            pl.BlockSpec((d_model, LANES), lambda i: (0, 0)),
            pl.BlockSpec((1, LANES), lambda i: (0, 0)),
        ],
        out_specs=(pl.BlockSpec((tq, LANES), lambda i: (i, 0)), pl.BlockSpec((8, LANES), lambda i: (0, 0))),
        scratch_shapes=[pltpu.VMEM((8, LANES), F32)],
        compiler_params=_params("arbitrary"),
        name="router",
    )(x, w_pad, b_pad)


def _dispatch_kernel(pos_ref, x_ref, init_hbm, xs_hbm, sem):
    del init_hbm
    tq = x_ref.shape[0]

    def start(t, carry):
        for k in range(TOP_K):
            pltpu.make_async_copy(x_ref.at[pl.ds(t, 1)],
                                  xs_hbm.at[pl.ds(pos_ref[TOP_K * t + k], 1)], sem).start()
        return carry

    lax.fori_loop(0, tq, start, 0, unroll=8)
    for k in range(TOP_K):
        pltpu.make_async_copy(x_ref, xs_hbm.at[pl.ds(0, tq)], sem).wait()


def _dispatch(x, pos_flat, n_sorted_rows, *, tq=512):
    t_len, d_model = x.shape
    init = jnp.zeros((n_sorted_rows, d_model), x.dtype)
    return pl.pallas_call(
        _dispatch_kernel,
        out_shape=jax.ShapeDtypeStruct((n_sorted_rows, d_model), x.dtype),
        grid=(t_len // tq,),
        in_specs=[
            pl.BlockSpec((TOP_K * tq,), lambda i: (i,), memory_space=pltpu.SMEM),
            pl.BlockSpec((tq, d_model), lambda i: (i, 0)),
            pl.BlockSpec(memory_space=pl.ANY),
        ],
        out_specs=pl.BlockSpec(memory_space=pl.ANY),
        scratch_shapes=[pltpu.SemaphoreType.DMA],
        input_output_aliases={2: 0},
        compiler_params=_params("arbitrary"),
        name="moe_dispatch",
    )(pos_flat, x, init)


def _expert_kernel(tile_e_ref, nt_ref, x_ref, wg_ref, wu_ref, wd_ref, o_ref, x16_scr):
    del tile_e_ref
    i = pl.program_id(0)
    c = pl.program_id(1)

    @pl.when(c == 0)
    def _():
        o_ref[...] = jnp.zeros_like(o_ref)
        x16_scr[...] = x_ref[...].astype(BF16)

    @pl.when(i < nt_ref[0])
    def _():
        o_ref[...] += _swiglu_chunk(x16_scr[...], wg_ref[...].astype(BF16), wu_ref[...].astype(BF16),
                                    wd_ref[...].astype(BF16))


def _expert_ffn(xs, tile_e, n_tiles, wg, wu, wd, *, tm, tf=512):
    n_rows, d_model = xs.shape
    d_ff = wg.shape[2]
    n_chunks = d_ff // tf
    last = n_chunks - 1

    def row_map(i, c, te, nt):
        return (jnp.minimum(i, nt[0] - 1), 0)

    def chunk_of(i, c, nt):
        return jnp.where(i < nt[0], c, last)

    grid_spec = pltpu.PrefetchScalarGridSpec(
        num_scalar_prefetch=2,
        grid=(n_rows // tm, n_chunks),
        in_specs=[
            pl.BlockSpec((tm, d_model), row_map),
            pl.BlockSpec((None, d_model, tf), lambda i, c, te, nt: (te[i], 0, chunk_of(i, c, nt))),
            pl.BlockSpec((None, d_model, tf), lambda i, c, te, nt: (te[i], 0, chunk_of(i, c, nt))),
            pl.BlockSpec((None, tf, d_model), lambda i, c, te, nt: (te[i], chunk_of(i, c, nt), 0)),
        ],
        out_specs=pl.BlockSpec((tm, d_model), lambda i, c, te, nt: (i, 0)),
        scratch_shapes=[pltpu.VMEM((tm, d_model), BF16)],
    )
    return pl.pallas_call(
        _expert_kernel,
        out_shape=jax.ShapeDtypeStruct((n_rows, d_model), F32),
        grid_spec=grid_spec,
        compiler_params=_params("arbitrary", "arbitrary"),
        name="expert_ffn",
    )(tile_e, n_tiles, xs, wg, wu, wd)


def _combine_kernel(pos_ref, res_ref, route_ref, g_ref, b_ref, ys_hbm, o_ref, buf_ref, sem):
    tq = res_ref.shape[0]

    def start(t, carry):
        for k in range(TOP_K):
            pltpu.make_async_copy(ys_hbm.at[pl.ds(pos_ref[TOP_K * t + k], 1)],
                                  buf_ref.at[k, pl.ds(t, 1)], sem).start()
        return carry

    lax.fori_loop(0, tq, start, 0, unroll=8)
    for k in range(TOP_K):
        pltpu.make_async_copy(ys_hbm.at[pl.ds(0, tq)], buf_ref.at[k], sem).wait()

    route = route_ref[...]
    y = (route[:, ROUTE_GATE:ROUTE_GATE + 1] * buf_ref[0]
         + route[:, ROUTE_GATE + 1:ROUTE_GATE + 2] * buf_ref[1])
    o_ref[...] = _layer_norm(ALPHA * res_ref[...] + y, g_ref[...], b_ref[...])


def _combine_ln(ys, pos_flat, res, route, gain, bias, *, tq=256):
    t_len, d_model = res.shape
    return pl.pallas_call(
        _combine_kernel,
        out_shape=jax.ShapeDtypeStruct((t_len, d_model), F32),
        grid=(t_len // tq,),
        in_specs=[
            pl.BlockSpec((TOP_K * tq,), lambda i: (i,), memory_space=pltpu.SMEM),
            pl.BlockSpec((tq, d_model), lambda i: (i, 0)),
            pl.BlockSpec((tq, LANES), lambda i: (i, 0)),
            pl.BlockSpec((1, d_model), lambda i: (0, 0)),
            pl.BlockSpec((1, d_model), lambda i: (0, 0)),
            pl.BlockSpec(memory_space=pl.ANY),
        ],
        out_specs=pl.BlockSpec((tq, d_model), lambda i: (i, 0)),
        scratch_shapes=[pltpu.VMEM((TOP_K, tq, d_model), F32), pltpu.SemaphoreType.DMA],
        compiler_params=_params("arbitrary"),
        name="moe_combine_ln",
    )(pos_flat, res, route, gain.reshape(1, -1), bias.reshape(1, -1), ys)


def _moe_ffn_ln(x, router_w, router_b, wg, wu, wd, gain, bias, *, tm=1024):
    t_len, _ = x.shape
    route, counts = _router(x, router_w, router_b)
    experts = route[:, ROUTE_E:ROUTE_E + TOP_K].astype(I32)
    ranks = route[:, ROUTE_RANK:ROUTE_RANK + TOP_K].astype(I32)
    counts = counts[0, :N_EXPERTS].astype(I32)
    tiles_per_expert = (counts + tm - 1) // tm
    tile_end = jnp.cumsum(tiles_per_expert)
    row_start = (tile_end - tiles_per_expert) * tm
    pos_flat = (row_start[experts] + ranks).reshape(-1)
    n_tiles_max = (TOP_K * t_len) // tm + N_EXPERTS
    n_tiles = tile_end[-1:]
    tile_ids = jnp.minimum(jnp.arange(n_tiles_max, dtype=I32), n_tiles[0] - 1)
    tile_e = jnp.sum((tile_ids[:, None] >= tile_end[None, :-1]).astype(I32), axis=1)

    xs = _dispatch(x, pos_flat, n_tiles_max * tm)
    ys = _expert_ffn(xs, tile_e, n_tiles.astype(I32), wg, wu, wd, tm=tm)
    return _combine_ln(ys, pos_flat, x, route, gain, bias)


def _alibi_slopes():
    exps = np.arange(1, N_ATTN_HEADS + 1, dtype=np.float32) * np.float32(8.0 / N_ATTN_HEADS)
    return jnp.asarray(np.exp2(-exps).astype(np.float32))


def kernel(x, mem, a_w_in, a_w_mem_kv, a_w_out, a_ln1_g, a_ln1_b, a_ffn_gate, a_ffn_up, a_ffn_down, a_ln2_g, a_ln2_b, b_w_in, b_fourier_g, b_w_mem_kv, b_w_out, b_ln1_g, b_ln1_b, b_router_w, b_router_b, b_moe_gate, b_moe_up, b_moe_down, b_ln2_g, b_ln2_b):
    b_sz, s_len, d_model = x.shape
    t_len = b_sz * s_len
    x16 = x.astype(BF16)
    mem16 = mem.astype(BF16)
    x32 = x.reshape(t_len, d_model)
    a_w_in, a_w_mem_kv, a_w_out, a_ffn_gate, a_ffn_up, a_ffn_down, b_w_in, b_w_mem_kv, b_w_out = (
        w[0].astype(BF16) for w in (a_w_in, a_w_mem_kv, a_w_out, a_ffn_gate, a_ffn_up, a_ffn_down,
                                    b_w_in, b_w_mem_kv, b_w_out))

    n_qkv_blocks = 3 * len(DILATIONS)
    qkv = [_proj_heads(x16, a_w_in, col_block0=g, col_block_stride=len(DILATIONS), n_blocks=3, dil=dil)
           for g, dil in enumerate(DILATIONS)]
    q_mem = _proj_heads(x16, a_w_in, col_block0=n_qkv_blocks, col_block_stride=1, n_blocks=1)
    mem_kv = _proj_heads(mem16, a_w_mem_kv, col_block0=0, col_block_stride=1, n_blocks=2)
    attn = _dilated_attention(_alibi_slopes(), qkv).reshape(t_len, -1)
    memo = _memory_attention(q_mem, 0, mem_kv).reshape(t_len, -1)
    x32, x16 = _mixer_ffn_ln(attn, memo, a_w_out, x32, a_ln1_g[0], a_ln1_b[0],
                             a_ffn_gate, a_ffn_up, a_ffn_down, a_ln2_g[0], a_ln2_b[0])

    x16_b = x16.reshape(b_sz, s_len, d_model)
    u = _proj_heads(x16_b, b_w_in, col_block0=0, col_block_stride=1, n_blocks=1, gain=b_fourier_g[0])
    q_mem = _proj_heads(x16_b, b_w_in, col_block0=1, col_block_stride=1, n_blocks=1)
    mem_kv = _proj_heads(mem16, b_w_mem_kv, col_block0=0, col_block_stride=1, n_blocks=2)
    four = _fourier_mix(u).reshape(t_len, -1)
    memo = _memory_attention(q_mem, 0, mem_kv).reshape(t_len, -1)
    x32, x16 = _out_proj_ln(four, memo, b_w_out, x32, b_ln1_g[0], b_ln1_b[0])
    out = _moe_ffn_ln(x32, b_router_w[0], b_router_b[0], b_moe_gate[0], b_moe_up[0], b_moe_down[0],
                      b_ln2_g[0], b_ln2_b[0])
    return out.reshape(b_sz, s_len, d_model)
```

```python
import functools

import numpy as np
import jax
import jax.numpy as jnp
from jax import lax
from jax.experimental import pallas as pl
from jax.experimental.pallas import tpu as pltpu

F32 = jnp.float32
BF16 = jnp.bfloat16
I32 = jnp.int32

LANES = 128
HEAD_DIM = 128
HEADS_PER_BLOCK = 4
DILATIONS = (1, 4, 16)
N_SIDE = 64
ATTN_Q_BLOCK = 64
ATTN_K_WINDOW = 256
ATTN_BATCH = 16
N_ATTN_HEADS = 12
N_EXPERTS = 8
TOP_K = 2
MOE_ROW_TILE = 1024
MOE_FF_CHUNK = 512
SUBLANES = 8
ALPHA = (2.0 * 2) ** 0.25
LN_EPS = 1e-5
NEG_INF = -1e30
VMEM_LIMIT_BYTES = 56 * 1024 * 1024

ROUTE_E, ROUTE_RANK, ROUTE_GATE = 0, 2, 4


def _params(*sem):
    return pltpu.CompilerParams(dimension_semantics=sem, vmem_limit_bytes=VMEM_LIMIT_BYTES)


def _layer_norm(z, g, b=None):
    mu = jnp.mean(z, axis=-1, keepdims=True)
    zc = z - mu
    var = jnp.mean(zc * zc, axis=-1, keepdims=True)
    y = zc * lax.rsqrt(var + LN_EPS) * g
    return y if b is None else y + b


def _proj_kernel(*refs, dil, row_chunk, group_ln):
    if group_ln:
        x_ref, w_ref, g_ref, o_ref = refs[:4]
        scratch = refs[4:]
    else:
        x_ref, w_ref, o_ref = refs[:3]
        g_ref = None
        scratch = refs[3:]
    s_len = x_ref.shape[0]
    w = w_ref[...]
    for rc in range(s_len // row_chunk):
        rows = pl.ds(rc * row_chunk, row_chunk)
        r = jnp.dot(x_ref[rows, :], w, preferred_element_type=F32)
        for hh in range(HEADS_PER_BLOCK):
            rh = r[:, hh * HEAD_DIM:(hh + 1) * HEAD_DIM]
            if group_ln:
                rh = _layer_norm(rh, g_ref[hh:hh + 1, :])
            if dil == 1:
                o_ref[hh, 0, rows, :] = rh.astype(o_ref.dtype)
            else:
                scratch[0][hh, rows, :] = rh
    if dil > 1:
        acc_ref = scratch[0]
        cls_len = s_len // dil
        for hh in range(HEADS_PER_BLOCK):
            for r_ in range(dil):
                o_ref[hh, r_, :, :] = acc_ref.at[hh][pl.ds(r_, cls_len, stride=dil), :].astype(o_ref.dtype)


def _proj_heads(x, w, *, col_block0, col_block_stride, n_blocks, dil=1, gain=None):
    b_sz, s_len, d_model = x.shape
    blk_cols = HEADS_PER_BLOCK * HEAD_DIM
    row_chunk = min(512, s_len)
    in_specs = [
        pl.BlockSpec((None, s_len, d_model), lambda b, wb: (b, 0, 0)),
        pl.BlockSpec((d_model, blk_cols), lambda b, wb: (0, col_block0 + wb * col_block_stride)),
    ]
    args = [x, w]
    if gain is not None:
        in_specs.append(pl.BlockSpec((HEADS_PER_BLOCK, HEAD_DIM), lambda b, wb: (0, 0)))
        args.append(gain)
    scratch = []
    if dil > 1:
        scratch.append(pltpu.VMEM((HEADS_PER_BLOCK, s_len, HEAD_DIM), F32))
    return pl.pallas_call(
        functools.partial(_proj_kernel, dil=dil, row_chunk=row_chunk, group_ln=gain is not None),
        out_shape=jax.ShapeDtypeStruct((b_sz, n_blocks * HEADS_PER_BLOCK, dil, s_len // dil, HEAD_DIM), BF16),
        grid=(b_sz, n_blocks),
        in_specs=in_specs,
        out_specs=pl.BlockSpec((None, HEADS_PER_BLOCK, dil, s_len // dil, HEAD_DIM),
                               lambda b, wb: (b, wb, 0, 0, 0)),
        scratch_shapes=scratch,
        compiler_params=_params("arbitrary", "arbitrary"),
        name=f"proj_heads_d{dil}",
    )(*args)


def _dil_attn_kernel(slopes_ref, q0, k0, v0, q1, k1, v1, q2, k2, v2, o_ref, o_scr, l_scr, bias_scr):
    s_len = o_ref.shape[0]
    h = pl.program_id(1)
    scale = HEAD_DIM ** -0.5
    qb = ATTN_Q_BLOCK
    groups = ((q0, k0, v0), (q1, k1, v1), (q2, k2, v2))
    for g, (qr, kr, vr) in enumerate(groups):
        dil = DILATIONS[g]
        cls_len = s_len // dil
        kw = min(ATTN_K_WINDOW, cls_len)
        nb = cls_len // qb
        nb_shift = nb.bit_length() - 1

        neg_slope = jnp.full((1, kw), slopes_ref[g * HEADS_PER_BLOCK + h], F32) * (-float(dil))
        base_delta = (lax.broadcasted_iota(I32, (qb, kw), 1) - lax.broadcasted_iota(I32, (qb, kw), 0))
        for case in range(kw // N_SIDE):
            dist = jnp.abs(base_delta - case * N_SIDE)
            bias_scr[g, case, :, 0:kw] = jnp.where(dist <= N_SIDE, dist.astype(F32) * neg_slope, NEG_INF)

        def body(it, carry, qr=qr, kr=kr, vr=vr, dil=dil, cls_len=cls_len, kw=kw, nb=nb, g=g,
                 nb_shift=nb_shift):
            blocks = []
            for j in range(ATTN_BATCH):
                idx = it * ATTN_BATCH + j
                r = lax.shift_right_logical(idx, nb_shift)
                i = lax.bitwise_and(idx, nb - 1)
                qs = pl.multiple_of(i * qb, qb)
                ks = pl.multiple_of(jnp.clip(qs - N_SIDE, 0, cls_len - kw), N_SIDE)
                blocks.append((r, qs, ks))
            scores = []
            for r, qs, ks in blocks:
                q = qr[r, pl.ds(qs, qb), :]
                k = kr[r, pl.ds(ks, kw), :]
                scores.append(lax.dot_general(q, k, (((1,), (1,)), ((), ())), preferred_element_type=F32))
            probs = []
            for (r, qs, ks), s in zip(blocks, scores):
                bias = bias_scr[g, lax.shift_right_logical(qs - ks, N_SIDE.bit_length() - 1), :, 0:kw]
                s = s * scale + bias
                m = jnp.max(s, axis=-1, keepdims=True)
                p = jnp.exp(s - m)
                l = jnp.sum(p, axis=-1, keepdims=True)
                probs.append((p.astype(BF16), m, l))
            for (r, qs, ks), (p, m, l) in zip(blocks, probs):
                v = vr[r, pl.ds(ks, kw), :]
                o = jnp.dot(p, v, preferred_element_type=F32) / l
                lse = jnp.broadcast_to(m + jnp.log(l), (qb, LANES))
                row0 = qs * dil + r
                if dil == 1:
                    o_scr[g, pl.ds(row0, qb), :] = o
                    l_scr[g, pl.ds(row0, qb), :] = lse
                else:
                    o_scr.at[g][pl.ds(row0, qb, stride=dil), :] = o
                    l_scr.at[g][pl.ds(row0, qb, stride=dil), :] = lse
            return carry

        lax.fori_loop(0, dil * nb // ATTN_BATCH, body, 0)

    chunk = 256
    for c in range(s_len // chunk):
        rows = pl.ds(c * chunk, chunk)
        l0, l1, l2 = l_scr[0, rows, :], l_scr[1, rows, :], l_scr[2, rows, :]
        m = jnp.maximum(jnp.maximum(l0, l1), l2)
        e0, e1, e2 = jnp.exp(l0 - m), jnp.exp(l1 - m), jnp.exp(l2 - m)
        num = e0 * o_scr[0, rows, :] + e1 * o_scr[1, rows, :] + e2 * o_scr[2, rows, :]
        o_ref[rows, :] = (num / (e0 + e1 + e2)).astype(o_ref.dtype)


def _dilated_attention(slopes, qkv):
    b_sz = qkv[0].shape[0]
    s_len = qkv[0].shape[2] * qkv[0].shape[3]
    in_specs = [pl.BlockSpec(memory_space=pltpu.SMEM)]
    args = [slopes]
    for g, arr in enumerate(qkv):
        dil = DILATIONS[g]
        for which in range(3):
            in_specs.append(pl.BlockSpec((None, None, dil, s_len // dil, HEAD_DIM),
                                         lambda b, h, which=which: (b, which * HEADS_PER_BLOCK + h, 0, 0, 0)))
            args.append(arr)
    return pl.pallas_call(
        _dil_attn_kernel,
        out_shape=jax.ShapeDtypeStruct((b_sz, s_len, HEADS_PER_BLOCK * HEAD_DIM), BF16),
        grid=(b_sz, HEADS_PER_BLOCK),
        in_specs=in_specs,
        out_specs=pl.BlockSpec((None, s_len, HEAD_DIM), lambda b, h: (b, 0, h)),
        scratch_shapes=[pltpu.VMEM((3, s_len, HEAD_DIM), F32), pltpu.VMEM((3, s_len, LANES), F32),
                        pltpu.VMEM((3, ATTN_K_WINDOW // N_SIDE, ATTN_Q_BLOCK, ATTN_K_WINDOW), F32)],
        compiler_params=_params("arbitrary", "arbitrary"),
        name="dilated_attention",
    )(*args)


def _mem_attn_kernel(q_ref, k_ref, v_ref, o_ref):
    s_len = q_ref.shape[0]
    scale = HEAD_DIM ** -0.5
    k = k_ref[...]
    v = v_ref[...]
    chunk = 256
    row_chunks = [pl.ds(c * chunk, chunk) for c in range(s_len // chunk)]
    scores = [lax.dot_general(q_ref[rows, :], k, (((1,), (1,)), ((), ())), preferred_element_type=F32)
              for rows in row_chunks]
    probs = []
    for s in scores:
        s = s * scale
        m = jnp.max(s, axis=-1, keepdims=True)
        p = jnp.exp(s - m)
        probs.append((p.astype(BF16), jnp.sum(p, axis=-1, keepdims=True)))
    for rows, (p, l) in zip(row_chunks, probs):
        o_ref[rows, :] = (jnp.dot(p, v, preferred_element_type=F32) / l).astype(o_ref.dtype)


def _memory_attention(q_heads, q_head0, kv_heads):
    b_sz, _, _, s_len, _ = q_heads.shape
    n_mem = kv_heads.shape[3]
    return pl.pallas_call(
        _mem_attn_kernel,
        out_shape=jax.ShapeDtypeStruct((b_sz, s_len, HEADS_PER_BLOCK * HEAD_DIM), BF16),
        grid=(b_sz, HEADS_PER_BLOCK),
        in_specs=[
            pl.BlockSpec((None, None, None, s_len, HEAD_DIM), lambda b, h: (b, q_head0 + h, 0, 0, 0)),
            pl.BlockSpec((None, None, None, n_mem, HEAD_DIM), lambda b, h: (b, h, 0, 0, 0)),
            pl.BlockSpec((None, None, None, n_mem, HEAD_DIM), lambda b, h: (b, HEADS_PER_BLOCK + h, 0, 0, 0)),
        ],
        out_specs=pl.BlockSpec((None, s_len, HEAD_DIM), lambda b, h: (b, 0, h)),
        compiler_params=_params("arbitrary", "arbitrary"),
        name="memory_attention",
    )(q_heads, kv_heads, kv_heads)


def _out_ln_kernel(a1_ref, a2_ref, w_ref, res_ref, g_ref, b_ref, o_ref):
    k1 = a1_ref.shape[1]
    y = jnp.dot(a1_ref[...], w_ref[0:k1, :], preferred_element_type=F32)
    y = y + jnp.dot(a2_ref[...], w_ref[k1:, :], preferred_element_type=F32)
    _store_tiled_rows(o_ref, _layer_norm(ALPHA * res_ref[...] + y, g_ref[...], b_ref[...]))


def _out_proj_ln(a1, a2, w, res, gain, bias, *, tm=512):
    t_len, d_model = res.shape
    assert d_model == SUBLANES * LANES
    k1, k2 = a1.shape[1], a2.shape[1]
    return pl.pallas_call(
        _out_ln_kernel,
        out_shape=jax.ShapeDtypeStruct((t_len * SUBLANES, LANES), F32),
        grid=(t_len // tm,),
        in_specs=[
            pl.BlockSpec((tm, k1), lambda i: (i, 0)),
            pl.BlockSpec((tm, k2), lambda i: (i, 0)),
            pl.BlockSpec((k1 + k2, d_model), lambda i: (0, 0), pipeline_mode=pl.Buffered(1)),
            pl.BlockSpec((tm, d_model), lambda i: (i, 0)),
            pl.BlockSpec((1, d_model), lambda i: (0, 0)),
            pl.BlockSpec((1, d_model), lambda i: (0, 0)),
        ],
        out_specs=pl.BlockSpec((tm * SUBLANES, LANES), lambda i: (i, 0)),
        compiler_params=_params("arbitrary"),
        name="out_proj_ln",
    )(a1, a2, w, res, gain.reshape(1, -1), bias.reshape(1, -1))


def _swiglu_chunk(x, wg, wu, wd):
    hg = jnp.dot(x, wg, preferred_element_type=F32)
    hu = jnp.dot(x, wu, preferred_element_type=F32)
    h = hg * (1.0 / (1.0 + jnp.exp(-hg))) * hu
    return jnp.dot(h.astype(BF16), wd, preferred_element_type=F32)


def _mixer_ffn_kernel(a1_ref, a2_ref, wo_ref, res_ref, g1_ref, b1_ref, wg_ref, wu_ref, wd_ref, g2_ref, b2_ref,
                      o32_ref, o16_ref, *, tf):
    k1 = a1_ref.shape[1]
    y = jnp.dot(a1_ref[...], wo_ref[0:k1, :], preferred_element_type=F32)
    y = y + jnp.dot(a2_ref[...], wo_ref[k1:, :], preferred_element_type=F32)
    x1 = _layer_norm(ALPHA * res_ref[...] + y, g1_ref[...], b1_ref[...])
    x16 = x1.astype(BF16)
    acc = None
    for f in range(wg_ref.shape[1] // tf):
        cols = pl.ds(f * tf, tf)
        part = _swiglu_chunk(x16, wg_ref[:, cols], wu_ref[:, cols], wd_ref[cols, :])
        acc = part if acc is None else acc + part
    z = _layer_norm(ALPHA * x1 + acc, g2_ref[...], b2_ref[...])
    o32_ref[...] = z
    o16_ref[...] = z.astype(BF16)


def _mixer_ffn_ln(a1, a2, wo, res, g1, b1, wg, wu, wd, g2, b2, *, tm=512, tf=256):
    t_len, d_model = res.shape
    k1, k2 = a1.shape[1], a2.shape[1]
    d_ff = wg.shape[1]
    resident = dict(pipeline_mode=pl.Buffered(1))
    vec = pl.BlockSpec((1, d_model), lambda i: (0, 0))
    return pl.pallas_call(
        functools.partial(_mixer_ffn_kernel, tf=tf),
        out_shape=(jax.ShapeDtypeStruct((t_len, d_model), F32), jax.ShapeDtypeStruct((t_len, d_model), BF16)),
        grid=(t_len // tm,),
        in_specs=[
            pl.BlockSpec((tm, k1), lambda i: (i, 0)),
            pl.BlockSpec((tm, k2), lambda i: (i, 0)),
            pl.BlockSpec((k1 + k2, d_model), lambda i: (0, 0), **resident),
            pl.BlockSpec((tm, d_model), lambda i: (i, 0)),
            vec, vec,
            pl.BlockSpec((d_model, d_ff), lambda i: (0, 0), **resident),
            pl.BlockSpec((d_model, d_ff), lambda i: (0, 0), **resident),
            pl.BlockSpec((d_ff, d_model), lambda i: (0, 0), **resident),
            vec, vec,
        ],
        out_specs=(pl.BlockSpec((tm, d_model), lambda i: (i, 0)),
                   pl.BlockSpec((tm, d_model), lambda i: (i, 0))),
        compiler_params=_params("arbitrary"),
        name="mixer_ffn_ln",
    )(a1, a2, wo, res, g1.reshape(1, -1), b1.reshape(1, -1), wg, wu, wd, g2.reshape(1, -1), b2.reshape(1, -1))


def _dft_constants(s_len):
    n = np.arange(s_len, dtype=np.int64)
    ang_s = 2.0 * np.pi * ((n[:, None] * n[None, :]) % s_len) / s_len
    c = np.arange(HEAD_DIM, dtype=np.int64)
    ang_c = 2.0 * np.pi * ((c[:, None] * c[None, :]) % HEAD_DIM) / HEAD_DIM
    norm = 1.0 / np.sqrt(float(s_len * HEAD_DIM))
    cs = np.concatenate([np.cos(ang_s), -np.sin(ang_s)], axis=1) * norm
    return tuple(jnp.asarray(a, dtype=F32).astype(BF16) for a in (cs, np.cos(ang_c), np.sin(ang_c)))


def _fourier_kernel(u_ref, cs_ref, cc_ref, sc_ref, o_ref, ab_scr):
    n_grp, s_len, _ = u_ref.shape
    cc = cc_ref[...]
    sc = sc_ref[...]
    for g in range(n_grp):
        u = u_ref[g]
        cols = pl.ds(g * HEAD_DIM, HEAD_DIM)
        ab_scr[0:s_len, cols] = jnp.dot(u, cc, preferred_element_type=F32).astype(BF16)
        ab_scr[s_len:, cols] = jnp.dot(u, sc, preferred_element_type=F32).astype(BF16)
    chunk = 512
    for c in range(s_len // chunk):
        rows = pl.ds(c * chunk, chunk)
        o_ref[rows, :] = jnp.dot(cs_ref[rows, :], ab_scr[...], preferred_element_type=F32).astype(o_ref.dtype)


def _fourier_mix(u_heads):
    b_sz, n_grp, _, s_len, _ = u_heads.shape
    cs, cc, sc = _dft_constants(s_len)
    const = dict(pipeline_mode=pl.Buffered(1))
    return pl.pallas_call(
        _fourier_kernel,
        out_shape=jax.ShapeDtypeStruct((b_sz, s_len, n_grp * HEAD_DIM), BF16),
        grid=(b_sz,),
        in_specs=[
            pl.BlockSpec((None, n_grp, None, s_len, HEAD_DIM), lambda b: (b, 0, 0, 0, 0)),
            pl.BlockSpec((s_len, 2 * s_len), lambda b: (0, 0), **const),
            pl.BlockSpec((HEAD_DIM, HEAD_DIM), lambda b: (0, 0), **const),
            pl.BlockSpec((HEAD_DIM, HEAD_DIM), lambda b: (0, 0), **const),
        ],
        out_specs=pl.BlockSpec((None, s_len, n_grp * HEAD_DIM), lambda b: (b, 0, 0)),
        scratch_shapes=[pltpu.VMEM((2 * s_len, n_grp * HEAD_DIM), BF16)],
        compiler_params=_params("arbitrary"),
        name="fourier_mix",
    )(u_heads, cs, cc, sc)


def _split_bf16(a):
    hi = a.astype(BF16)
    lo = (a - hi.astype(F32)).astype(BF16)
    return hi, lo


def _router_kernel(x_ref, w_ref, b_ref, route_ref, cnt_ref, carry_ref):
    tq = route_ref.shape[0]

    @pl.when(pl.program_id(0) == 0)
    def _():
        carry_ref[...] = jnp.zeros_like(carry_ref)

    x_hi, x_lo = _split_bf16(_load_tiled_rows(x_ref))
    w_hi, w_lo = _split_bf16(w_ref[...])
    logits = (jnp.dot(x_hi, w_hi, preferred_element_type=F32)
              + jnp.dot(x_lo, w_hi, preferred_element_type=F32)
              + jnp.dot(x_hi, w_lo, preferred_element_type=F32)) + b_ref[...]
    lane = lax.broadcasted_iota(I32, (tq, LANES), 1).astype(F32)
    logits = jnp.where(lane < N_EXPERTS, logits, NEG_INF)

    def top1(vals):
        m = jnp.max(vals, axis=-1, keepdims=True)
        idx = jnp.min(jnp.where(vals == m, lane, float(LANES)), axis=-1, keepdims=True)
        return m, idx

    m1, i1 = top1(logits)
    oh1 = lane == i1
    m2, i2 = top1(jnp.where(oh1, NEG_INF, logits))
    oh2 = lane == i2
    e2 = jnp.exp(m2 - m1)
    gate1 = 1.0 / (1.0 + e2)
    gate2 = e2 / (1.0 + e2)

    oh1f = oh1.astype(F32)
    oh2f = oh2.astype(F32)
    both = oh1f + oh2f
    tri = (lax.broadcasted_iota(I32, (tq, tq), 0) > lax.broadcasted_iota(I32, (tq, tq), 1)).astype(BF16)
    before = jnp.dot(tri, both.astype(BF16), preferred_element_type=F32) + carry_ref[0:1, :]
    rank1 = jnp.sum(oh1f * before, axis=-1, keepdims=True)
    rank2 = jnp.sum(oh2f * before, axis=-1, keepdims=True)
    carry_ref[...] = carry_ref[...] + jnp.sum(both, axis=0, keepdims=True)

    rec = jnp.zeros((tq, LANES), F32)
    for lane_idx, val in ((ROUTE_E, i1), (ROUTE_E + 1, i2),
                          (ROUTE_RANK, rank1), (ROUTE_RANK + 1, rank2),
                          (ROUTE_GATE, gate1), (ROUTE_GATE + 1, gate2)):
        rec = jnp.where(lane == lane_idx, val, rec)
    route_ref[...] = rec
    cnt_ref[...] = carry_ref[...]


def _router(x_tiled, w, b, *, tq=512):
    t_len = x_tiled.shape[0] // SUBLANES
    d_model = SUBLANES * LANES
    w_pad = jnp.zeros((d_model, LANES), F32).at[:, :N_EXPERTS].set(w)
    b_pad = jnp.zeros((1, LANES), F32).at[0, :N_EXPERTS].set(b)
    return pl.pallas_call(
        _router_kernel,
        out_shape=(jax.ShapeDtypeStruct((t_len, LANES), F32), jax.ShapeDtypeStruct((8, LANES), F32)),
        grid=(t_len // tq,),
        in_specs=[
            pl.BlockSpec((tq * SUBLANES, LANES), lambda i: (i, 0)),
            pl.BlockSpec((d_model, LANES), lambda i: (0, 0)),
            pl.BlockSpec((1, LANES), lambda i: (0, 0)),
        ],
        out_specs=(pl.BlockSpec((tq, LANES), lambda i: (i, 0)), pl.BlockSpec((8, LANES), lambda i: (0, 0))),
        scratch_shapes=[pltpu.VMEM((8, LANES), F32)],
        compiler_params=_params("arbitrary"),
        name="router",
    )(x_tiled, w_pad, b_pad)


def _store_tiled_rows(dst_ref, val):
    n_rows, width = val.shape
    for j in range(width // LANES):
        dst_ref[pl.ds(j, n_rows, stride=width // LANES), :] = val[:, j * LANES:(j + 1) * LANES]


def _load_tiled_rows(src_ref):
    n_rows = src_ref.shape[0] // SUBLANES
    return jnp.concatenate([src_ref[pl.ds(j, n_rows, stride=SUBLANES), :] for j in range(SUBLANES)], axis=1)


def _tiled_row(ref, row):
    return ref.at[pl.ds(pl.multiple_of(row * SUBLANES, SUBLANES), SUBLANES)]


def _dispatch_kernel(pos_ref, x_ref, init_hbm, xs_hbm, sem):
    del init_hbm
    tq = x_ref.shape[0] // SUBLANES

    def start(t, carry):
        for k in range(TOP_K):
            pltpu.make_async_copy(_tiled_row(x_ref, t), _tiled_row(xs_hbm, pos_ref[TOP_K * t + k]), sem).start()
        return carry

    lax.fori_loop(0, tq, start, 0, unroll=8)
    for k in range(TOP_K):
        pltpu.make_async_copy(x_ref, xs_hbm.at[pl.ds(0, tq * SUBLANES)], sem).wait()


def _dispatch(x_tiled, pos_flat, n_sorted_rows, *, tq=512):
    t_len = x_tiled.shape[0] // SUBLANES
    init = jnp.zeros((n_sorted_rows * SUBLANES, LANES), x_tiled.dtype)
    return pl.pallas_call(
        _dispatch_kernel,
        out_shape=jax.ShapeDtypeStruct(init.shape, init.dtype),
        grid=(t_len // tq,),
        in_specs=[
            pl.BlockSpec((TOP_K * tq,), lambda i: (i,), memory_space=pltpu.SMEM),
            pl.BlockSpec((tq * SUBLANES, LANES), lambda i: (i, 0)),
            pl.BlockSpec(memory_space=pl.ANY),
        ],
        out_specs=pl.BlockSpec(memory_space=pl.ANY),
        scratch_shapes=[pltpu.SemaphoreType.DMA],
        input_output_aliases={2: 0},
        compiler_params=_params("arbitrary"),
        name="moe_dispatch",
    )(pos_flat, x_tiled, init)


def _expert_kernel(tile_e_ref, nt_ref, x_ref, wg_ref, wu_ref, wd_ref, o_ref, x16_scr, acc_ref):
    del tile_e_ref
    i = pl.program_id(0)
    c = pl.program_id(1)

    @pl.when(c == 0)
    def _():
        acc_ref[...] = jnp.zeros_like(acc_ref)
        x16_scr[...] = _load_tiled_rows(x_ref).astype(BF16)

    @pl.when(i < nt_ref[0])
    def _():
        acc_ref[...] += _swiglu_chunk(x16_scr[...], wg_ref[...].astype(BF16), wu_ref[...].astype(BF16),
                                      wd_ref[...].astype(BF16))

    @pl.when(c == pl.num_programs(1) - 1)
    def _():
        _store_tiled_rows(o_ref, acc_ref[...])


def _expert_ffn(xs, tile_e, n_tiles, wg, wu, wd, *, tm, tf):
    d_model = wg.shape[1]
    d_ff = wg.shape[2]
    n_chunks = d_ff // tf
    last = n_chunks - 1

    def row_map(i, c, te, nt):
        return (jnp.minimum(i, nt[0] - 1), 0)

    def chunk_of(i, c, nt):
        return jnp.where(i < nt[0], c, last)

    grid_spec = pltpu.PrefetchScalarGridSpec(
        num_scalar_prefetch=2,
        grid=(xs.shape[0] // (tm * SUBLANES), n_chunks),
        in_specs=[
            pl.BlockSpec((tm * SUBLANES, LANES), row_map),
            pl.BlockSpec((None, d_model, tf), lambda i, c, te, nt: (te[i], 0, chunk_of(i, c, nt))),
            pl.BlockSpec((None, d_model, tf), lambda i, c, te, nt: (te[i], 0, chunk_of(i, c, nt))),
            pl.BlockSpec((None, tf, d_model), lambda i, c, te, nt: (te[i], chunk_of(i, c, nt), 0)),
        ],
        out_specs=pl.BlockSpec((tm * SUBLANES, LANES), lambda i, c, te, nt: (i, 0)),
        scratch_shapes=[pltpu.VMEM((tm, d_model), BF16), pltpu.VMEM((tm, d_model), F32)],
    )
    return pl.pallas_call(
        _expert_kernel,
        out_shape=jax.ShapeDtypeStruct(xs.shape, F32),
        grid_spec=grid_spec,
        compiler_params=_params("arbitrary", "arbitrary"),
        name="expert_ffn",
    )(tile_e, n_tiles, xs, wg, wu, wd)


def _combine_kernel(pos_ref, res_ref, route_ref, g_ref, b_ref, ys_hbm, o_ref, buf_ref, sem):
    tq = o_ref.shape[0]

    def start(t, carry):
        for k in range(TOP_K):
            pltpu.make_async_copy(_tiled_row(ys_hbm, pos_ref[TOP_K * t + k]), _tiled_row(buf_ref.at[k], t),
                                  sem).start()
        return carry

    lax.fori_loop(0, tq, start, 0, unroll=8)
    for k in range(TOP_K):
        pltpu.make_async_copy(ys_hbm.at[pl.ds(0, tq * SUBLANES)], buf_ref.at[k], sem).wait()

    route = route_ref[...]
    y = (route[:, ROUTE_GATE:ROUTE_GATE + 1] * _load_tiled_rows(buf_ref.at[0])
         + route[:, ROUTE_GATE + 1:ROUTE_GATE + 2] * _load_tiled_rows(buf_ref.at[1]))
    o_ref[...] = _layer_norm(ALPHA * _load_tiled_rows(res_ref) + y, g_ref[...], b_ref[...])


def _combine_ln(ys, pos_flat, res_tiled, route, gain, bias, *, tq=256):
    t_len = res_tiled.shape[0] // SUBLANES
    d_model = SUBLANES * LANES
    return pl.pallas_call(
        _combine_kernel,
        out_shape=jax.ShapeDtypeStruct((t_len, d_model), F32),
        grid=(t_len // tq,),
        in_specs=[
            pl.BlockSpec((TOP_K * tq,), lambda i: (i,), memory_space=pltpu.SMEM),
            pl.BlockSpec((tq * SUBLANES, LANES), lambda i: (i, 0)),
            pl.BlockSpec((tq, LANES), lambda i: (i, 0)),
            pl.BlockSpec((1, d_model), lambda i: (0, 0)),
            pl.BlockSpec((1, d_model), lambda i: (0, 0)),
            pl.BlockSpec(memory_space=pl.ANY),
        ],
        out_specs=pl.BlockSpec((tq, d_model), lambda i: (i, 0)),
        scratch_shapes=[pltpu.VMEM((TOP_K, tq * SUBLANES, LANES), F32), pltpu.SemaphoreType.DMA],
        compiler_params=_params("arbitrary"),
        name="moe_combine_ln",
    )(pos_flat, res_tiled, route, gain.reshape(1, -1), bias.reshape(1, -1), ys)


def _moe_ffn_ln(x_tiled, router_w, router_b, wg, wu, wd, gain, bias, *, tm=MOE_ROW_TILE, tf=MOE_FF_CHUNK):
    t_len = x_tiled.shape[0] // SUBLANES
    route, counts = _router(x_tiled, router_w, router_b)
    experts = route[:, ROUTE_E:ROUTE_E + TOP_K].astype(I32)
    ranks = route[:, ROUTE_RANK:ROUTE_RANK + TOP_K].astype(I32)
    counts = counts[0, :N_EXPERTS].astype(I32)
    tiles_per_expert = (counts + tm - 1) // tm
    tile_end = jnp.cumsum(tiles_per_expert)
    row_start = (tile_end - tiles_per_expert) * tm
    pos_flat = (row_start[experts] + ranks).reshape(-1)
    n_tiles_max = (TOP_K * t_len) // tm + N_EXPERTS
    n_tiles = tile_end[-1:]
    tile_ids = jnp.minimum(jnp.arange(n_tiles_max, dtype=I32), n_tiles[0] - 1)
    tile_e = jnp.sum((tile_ids[:, None] >= tile_end[None, :-1]).astype(I32), axis=1)

    xs = _dispatch(x_tiled, pos_flat, n_tiles_max * tm)
    ys = _expert_ffn(xs, tile_e, n_tiles.astype(I32), wg, wu, wd, tm=tm, tf=tf)
    return _combine_ln(ys, pos_flat, x_tiled, route, gain, bias)


def _alibi_slopes():
    exps = np.arange(1, N_ATTN_HEADS + 1, dtype=np.float32) * np.float32(8.0 / N_ATTN_HEADS)
    return jnp.asarray(np.exp2(-exps).astype(np.float32))


def kernel(x, mem, a_w_in, a_w_mem_kv, a_w_out, a_ln1_g, a_ln1_b, a_ffn_gate, a_ffn_up, a_ffn_down, a_ln2_g, a_ln2_b, b_w_in, b_fourier_g, b_w_mem_kv, b_w_out, b_ln1_g, b_ln1_b, b_router_w, b_router_b, b_moe_gate, b_moe_up, b_moe_down, b_ln2_g, b_ln2_b):
    b_sz, s_len, d_model = x.shape
    t_len = b_sz * s_len
    x16 = x.astype(BF16)
    mem16 = mem.astype(BF16)
    x32 = x.reshape(t_len, d_model)
    a_w_in, a_w_mem_kv, a_w_out, a_ffn_gate, a_ffn_up, a_ffn_down, b_w_in, b_w_mem_kv, b_w_out = (
        w[0].astype(BF16) for w in (a_w_in, a_w_mem_kv, a_w_out, a_ffn_gate, a_ffn_up, a_ffn_down,
                                    b_w_in, b_w_mem_kv, b_w_out))

    n_qkv_blocks = 3 * len(DILATIONS)
    qkv = [_proj_heads(x16, a_w_in, col_block0=g, col_block_stride=len(DILATIONS), n_blocks=3, dil=dil)
           for g, dil in enumerate(DILATIONS)]
    q_mem = _proj_heads(x16, a_w_in, col_block0=n_qkv_blocks, col_block_stride=1, n_blocks=1)
    mem_kv = _proj_heads(mem16, a_w_mem_kv, col_block0=0, col_block_stride=1, n_blocks=2)
    attn = _dilated_attention(_alibi_slopes(), qkv).reshape(t_len, -1)
    memo = _memory_attention(q_mem, 0, mem_kv).reshape(t_len, -1)
    x32, x16 = _mixer_ffn_ln(attn, memo, a_w_out, x32, a_ln1_g[0], a_ln1_b[0],
                             a_ffn_gate, a_ffn_up, a_ffn_down, a_ln2_g[0], a_ln2_b[0])

    x16_b = x16.reshape(b_sz, s_len, d_model)
    u = _proj_heads(x16_b, b_w_in, col_block0=0, col_block_stride=1, n_blocks=1, gain=b_fourier_g[0])
    q_mem = _proj_heads(x16_b, b_w_in, col_block0=1, col_block_stride=1, n_blocks=1)
    mem_kv = _proj_heads(mem16, b_w_mem_kv, col_block0=0, col_block_stride=1, n_blocks=2)
    four = _fourier_mix(u).reshape(t_len, -1)
    memo = _memory_attention(q_mem, 0, mem_kv).reshape(t_len, -1)
    x_tiled = _out_proj_ln(four, memo, b_w_out, x32, b_ln1_g[0], b_ln1_b[0])
    out = _moe_ffn_ln(x_tiled, b_router_w[0], b_router_b[0], b_moe_gate[0], b_moe_up[0], b_moe_down[0],
                      b_ln2_g[0], b_ln2_b[0])
    return out.reshape(b_sz, s_len, d_model)
```

```python
import functools

import numpy as np
import jax
import jax.numpy as jnp
from jax import lax
from jax.experimental import pallas as pl
from jax.experimental.pallas import tpu as pltpu

F32 = jnp.float32
BF16 = jnp.bfloat16
I32 = jnp.int32

LANES = 128
HEAD_DIM = 128
HEADS_PER_BLOCK = 4
DILATIONS = (1, 4, 16)
N_SIDE = 64
ATTN_Q_BLOCK = 64
ATTN_K_WINDOW = 256
ATTN_BATCH = 16
N_ATTN_HEADS = 12
N_EXPERTS = 8
TOP_K = 2
MOE_ROW_TILE = 1024
MOE_FF_CHUNK = 512
SUBLANES = 8
ALPHA = (2.0 * 2) ** 0.25
LN_EPS = 1e-5
NEG_INF = -1e30
VMEM_LIMIT_BYTES = 56 * 1024 * 1024

ROUTE_E, ROUTE_RANK, ROUTE_GATE = 0, 2, 4


def _params(*sem):
    return pltpu.CompilerParams(dimension_semantics=sem, vmem_limit_bytes=VMEM_LIMIT_BYTES)


def _layer_norm(z, g, b=None):
    mu = jnp.mean(z, axis=-1, keepdims=True)
    zc = z - mu
    var = jnp.mean(zc * zc, axis=-1, keepdims=True)
    y = zc * lax.rsqrt(var + LN_EPS) * g
    return y if b is None else y + b


def _proj_kernel(*refs, dil, row_chunk, group_ln):
    if group_ln:
        x_ref, w_ref, g_ref, o_ref = refs[:4]
        scratch = refs[4:]
    else:
        x_ref, w_ref, o_ref = refs[:3]
        g_ref = None
        scratch = refs[3:]
    s_len = x_ref.shape[0]
    w = w_ref[...]
    for rc in range(s_len // row_chunk):
        rows = pl.ds(rc * row_chunk, row_chunk)
        r = jnp.dot(x_ref[rows, :], w, preferred_element_type=F32)
        for hh in range(HEADS_PER_BLOCK):
            rh = r[:, hh * HEAD_DIM:(hh + 1) * HEAD_DIM]
            if group_ln:
                rh = _layer_norm(rh, g_ref[hh:hh + 1, :])
            if dil == 1:
                o_ref[hh, 0, rows, :] = rh.astype(o_ref.dtype)
            else:
                scratch[0][hh, rows, :] = rh
    if dil > 1:
        acc_ref = scratch[0]
        cls_len = s_len // dil
        for hh in range(HEADS_PER_BLOCK):
            for r_ in range(dil):
                o_ref[hh, r_, :, :] = acc_ref.at[hh][pl.ds(r_, cls_len, stride=dil), :].astype(o_ref.dtype)


def _proj_heads(x, w, *, col_block0, col_block_stride, n_blocks, dil=1, gain=None):
    b_sz, s_len, d_model = x.shape
    blk_cols = HEADS_PER_BLOCK * HEAD_DIM
    row_chunk = min(512, s_len)
    in_specs = [
        pl.BlockSpec((None, s_len, d_model), lambda b, wb: (b, 0, 0)),
        pl.BlockSpec((d_model, blk_cols), lambda b, wb: (0, col_block0 + wb * col_block_stride)),
    ]
    args = [x, w]
    if gain is not None:
        in_specs.append(pl.BlockSpec((HEADS_PER_BLOCK, HEAD_DIM), lambda b, wb: (0, 0)))
        args.append(gain)
    scratch = []
    if dil > 1:
        scratch.append(pltpu.VMEM((HEADS_PER_BLOCK, s_len, HEAD_DIM), F32))
    return pl.pallas_call(
        functools.partial(_proj_kernel, dil=dil, row_chunk=row_chunk, group_ln=gain is not None),
        out_shape=jax.ShapeDtypeStruct((b_sz, n_blocks * HEADS_PER_BLOCK, dil, s_len // dil, HEAD_DIM), BF16),
        grid=(b_sz, n_blocks),
        in_specs=in_specs,
        out_specs=pl.BlockSpec((None, HEADS_PER_BLOCK, dil, s_len // dil, HEAD_DIM),
                               lambda b, wb: (b, wb, 0, 0, 0)),
        scratch_shapes=scratch,
        compiler_params=_params("arbitrary", "arbitrary"),
        name=f"proj_heads_d{dil}",
    )(*args)


def _dil_attn_kernel(slopes_ref, q0, k0, v0, q1, k1, v1, q2, k2, v2, o_ref, o_scr, l_scr, bias_scr):
    s_len = o_ref.shape[0]
    h = pl.program_id(1)
    scale = HEAD_DIM ** -0.5
    qb = ATTN_Q_BLOCK
    groups = ((q0, k0, v0), (q1, k1, v1), (q2, k2, v2))
    for g, (qr, kr, vr) in enumerate(groups):
        dil = DILATIONS[g]
        cls_len = s_len // dil
        kw = min(ATTN_K_WINDOW, cls_len)
        nb = cls_len // qb
        nb_shift = nb.bit_length() - 1

        neg_slope = jnp.full((1, kw), slopes_ref[g * HEADS_PER_BLOCK + h], F32) * (-float(dil))
        base_delta = (lax.broadcasted_iota(I32, (qb, kw), 1) - lax.broadcasted_iota(I32, (qb, kw), 0))
        for case in range(kw // N_SIDE):
            dist = jnp.abs(base_delta - case * N_SIDE)
            bias_scr[g, case, :, 0:kw] = jnp.where(dist <= N_SIDE, dist.astype(F32) * neg_slope, NEG_INF)

        def body(it, carry, qr=qr, kr=kr, vr=vr, dil=dil, cls_len=cls_len, kw=kw, nb=nb, g=g,
                 nb_shift=nb_shift):
            blocks = []
            for j in range(ATTN_BATCH):
                idx = it * ATTN_BATCH + j
                r = lax.shift_right_logical(idx, nb_shift)
                i = lax.bitwise_and(idx, nb - 1)
                qs = pl.multiple_of(i * qb, qb)
                ks = pl.multiple_of(jnp.clip(qs - N_SIDE, 0, cls_len - kw), N_SIDE)
                blocks.append((r, qs, ks))
            scores = []
            for r, qs, ks in blocks:
                q = qr[r, pl.ds(qs, qb), :]
                k = kr[r, pl.ds(ks, kw), :]
                scores.append(lax.dot_general(q, k, (((1,), (1,)), ((), ())), preferred_element_type=F32))
            probs = []
            for (r, qs, ks), s in zip(blocks, scores):
                bias = bias_scr[g, lax.shift_right_logical(qs - ks, N_SIDE.bit_length() - 1), :, 0:kw]
                s = s * scale + bias
                m = jnp.max(s, axis=-1, keepdims=True)
                p = jnp.exp(s - m)
                l = jnp.sum(p, axis=-1, keepdims=True)
                probs.append((p.astype(BF16), m, l))
            for (r, qs, ks), (p, m, l) in zip(blocks, probs):
                v = vr[r, pl.ds(ks, kw), :]
                o = jnp.dot(p, v, preferred_element_type=F32) / l
                lse = jnp.broadcast_to(m + jnp.log(l), (qb, LANES))
                row0 = qs * dil + r
                if dil == 1:
                    o_scr[g, pl.ds(row0, qb), :] = o
                    l_scr[g, pl.ds(row0, qb), :] = lse
                else:
                    o_scr.at[g][pl.ds(row0, qb, stride=dil), :] = o
                    l_scr.at[g][pl.ds(row0, qb, stride=dil), :] = lse
            return carry

        lax.fori_loop(0, dil * nb // ATTN_BATCH, body, 0)

    chunk = 256
    for c in range(s_len // chunk):
        rows = pl.ds(c * chunk, chunk)
        l0, l1, l2 = l_scr[0, rows, :], l_scr[1, rows, :], l_scr[2, rows, :]
        m = jnp.maximum(jnp.maximum(l0, l1), l2)
        e0, e1, e2 = jnp.exp(l0 - m), jnp.exp(l1 - m), jnp.exp(l2 - m)
        num = e0 * o_scr[0, rows, :] + e1 * o_scr[1, rows, :] + e2 * o_scr[2, rows, :]
        o_ref[rows, :] = (num / (e0 + e1 + e2)).astype(o_ref.dtype)


def _dilated_attention(slopes, qkv):
    b_sz = qkv[0].shape[0]
    s_len = qkv[0].shape[2] * qkv[0].shape[3]
    in_specs = [pl.BlockSpec(memory_space=pltpu.SMEM)]
    args = [slopes]
    for g, arr in enumerate(qkv):
        dil = DILATIONS[g]
        for which in range(3):
            in_specs.append(pl.BlockSpec((None, None, dil, s_len // dil, HEAD_DIM),
                                         lambda b, h, which=which: (b, which * HEADS_PER_BLOCK + h, 0, 0, 0)))
            args.append(arr)
    return pl.pallas_call(
        _dil_attn_kernel,
        out_shape=jax.ShapeDtypeStruct((b_sz, s_len, HEADS_PER_BLOCK * HEAD_DIM), BF16),
        grid=(b_sz, HEADS_PER_BLOCK),
        in_specs=in_specs,
        out_specs=pl.BlockSpec((None, s_len, HEAD_DIM), lambda b, h: (b, 0, h)),
        scratch_shapes=[pltpu.VMEM((3, s_len, HEAD_DIM), F32), pltpu.VMEM((3, s_len, LANES), F32),
                        pltpu.VMEM((3, ATTN_K_WINDOW // N_SIDE, ATTN_Q_BLOCK, ATTN_K_WINDOW), F32)],
        compiler_params=_params("arbitrary", "arbitrary"),
        name="dilated_attention",
    )(*args)


def _mem_attn_kernel(q_ref, k_ref, v_ref, o_ref):
    s_len = q_ref.shape[0]
    scale = HEAD_DIM ** -0.5
    k = k_ref[...]
    v = v_ref[...]
    chunk = 256
    row_chunks = [pl.ds(c * chunk, chunk) for c in range(s_len // chunk)]
    scores = [lax.dot_general(q_ref[rows, :], k, (((1,), (1,)), ((), ())), preferred_element_type=F32)
              for rows in row_chunks]
    probs = []
    for s in scores:
        s = s * scale
        m = jnp.max(s, axis=-1, keepdims=True)
        p = jnp.exp(s - m)
        probs.append((p.astype(BF16), jnp.sum(p, axis=-1, keepdims=True)))
    for rows, (p, l) in zip(row_chunks, probs):
        o_ref[rows, :] = (jnp.dot(p, v, preferred_element_type=F32) / l).astype(o_ref.dtype)


def _memory_attention(q_heads, q_head0, kv_heads):
    b_sz, _, _, s_len, _ = q_heads.shape
    n_mem = kv_heads.shape[3]
    return pl.pallas_call(
        _mem_attn_kernel,
        out_shape=jax.ShapeDtypeStruct((b_sz, s_len, HEADS_PER_BLOCK * HEAD_DIM), BF16),
        grid=(b_sz, HEADS_PER_BLOCK),
        in_specs=[
            pl.BlockSpec((None, None, None, s_len, HEAD_DIM), lambda b, h: (b, q_head0 + h, 0, 0, 0)),
            pl.BlockSpec((None, None, None, n_mem, HEAD_DIM), lambda b, h: (b, h, 0, 0, 0)),
            pl.BlockSpec((None, None, None, n_mem, HEAD_DIM), lambda b, h: (b, HEADS_PER_BLOCK + h, 0, 0, 0)),
        ],
        out_specs=pl.BlockSpec((None, s_len, HEAD_DIM), lambda b, h: (b, 0, h)),
        compiler_params=_params("arbitrary", "arbitrary"),
        name="memory_attention",
    )(q_heads, kv_heads, kv_heads)


def _out_ln_kernel(a1_ref, a2_ref, w_ref, res_ref, g_ref, b_ref, o_ref):
    k1 = a1_ref.shape[1]
    y = jnp.dot(a1_ref[...], w_ref[0:k1, :], preferred_element_type=F32)
    y = y + jnp.dot(a2_ref[...], w_ref[k1:, :], preferred_element_type=F32)
    _store_tiled_rows(o_ref, _layer_norm(ALPHA * res_ref[...] + y, g_ref[...], b_ref[...]))


def _out_proj_ln(a1, a2, w, res, gain, bias, *, tm=512):
    t_len, d_model = res.shape
    assert d_model == SUBLANES * LANES
    k1, k2 = a1.shape[1], a2.shape[1]
    return pl.pallas_call(
        _out_ln_kernel,
        out_shape=jax.ShapeDtypeStruct((t_len * SUBLANES, LANES), F32),
        grid=(t_len // tm,),
        in_specs=[
            pl.BlockSpec((tm, k1), lambda i: (i, 0)),
            pl.BlockSpec((tm, k2), lambda i: (i, 0)),
            pl.BlockSpec((k1 + k2, d_model), lambda i: (0, 0), pipeline_mode=pl.Buffered(1)),
            pl.BlockSpec((tm, d_model), lambda i: (i, 0)),
            pl.BlockSpec((1, d_model), lambda i: (0, 0)),
            pl.BlockSpec((1, d_model), lambda i: (0, 0)),
        ],
        out_specs=pl.BlockSpec((tm * SUBLANES, LANES), lambda i: (i, 0)),
        compiler_params=_params("arbitrary"),
        name="out_proj_ln",
    )(a1, a2, w, res, gain.reshape(1, -1), bias.reshape(1, -1))


def _swiglu_chunk(x, wg, wu, wd):
    hg = jnp.dot(x, wg, preferred_element_type=F32)
    hu = jnp.dot(x, wu, preferred_element_type=F32)
    h = hg * (1.0 / (1.0 + jnp.exp(-hg))) * hu
    return jnp.dot(h.astype(BF16), wd, preferred_element_type=F32)


def _mixer_ffn_kernel(a1_ref, a2_ref, wo_ref, res_ref, g1_ref, b1_ref, wg_ref, wu_ref, wd_ref, g2_ref, b2_ref,
                      o32_ref, o16_ref, *, tf):
    k1 = a1_ref.shape[1]
    y = jnp.dot(a1_ref[...], wo_ref[0:k1, :], preferred_element_type=F32)
    y = y + jnp.dot(a2_ref[...], wo_ref[k1:, :], preferred_element_type=F32)
    x1 = _layer_norm(ALPHA * res_ref[...] + y, g1_ref[...], b1_ref[...])
    x16 = x1.astype(BF16)
    acc = None
    for f in range(wg_ref.shape[1] // tf):
        cols = pl.ds(f * tf, tf)
        part = _swiglu_chunk(x16, wg_ref[:, cols], wu_ref[:, cols], wd_ref[cols, :])
        acc = part if acc is None else acc + part
    z = _layer_norm(ALPHA * x1 + acc, g2_ref[...], b2_ref[...])
    o32_ref[...] = z
    o16_ref[...] = z.astype(BF16)


def _mixer_ffn_ln(a1, a2, wo, res, g1, b1, wg, wu, wd, g2, b2, *, tm=512, tf=256):
    t_len, d_model = res.shape
    k1, k2 = a1.shape[1], a2.shape[1]
    d_ff = wg.shape[1]
    resident = dict(pipeline_mode=pl.Buffered(1))
    vec = pl.BlockSpec((1, d_model), lambda i: (0, 0))
    return pl.pallas_call(
        functools.partial(_mixer_ffn_kernel, tf=tf),
        out_shape=(jax.ShapeDtypeStruct((t_len, d_model), F32), jax.ShapeDtypeStruct((t_len, d_model), BF16)),
        grid=(t_len // tm,),
        in_specs=[
            pl.BlockSpec((tm, k1), lambda i: (i, 0)),
            pl.BlockSpec((tm, k2), lambda i: (i, 0)),
            pl.BlockSpec((k1 + k2, d_model), lambda i: (0, 0), **resident),
            pl.BlockSpec((tm, d_model), lambda i: (i, 0)),
            vec, vec,
            pl.BlockSpec((d_model, d_ff), lambda i: (0, 0), **resident),
            pl.BlockSpec((d_model, d_ff), lambda i: (0, 0), **resident),
            pl.BlockSpec((d_ff, d_model), lambda i: (0, 0), **resident),
            vec, vec,
        ],
        out_specs=(pl.BlockSpec((tm, d_model), lambda i: (i, 0)),
                   pl.BlockSpec((tm, d_model), lambda i: (i, 0))),
        compiler_params=_params("arbitrary"),
        name="mixer_ffn_ln",
    )(a1, a2, wo, res, g1.reshape(1, -1), b1.reshape(1, -1), wg, wu, wd, g2.reshape(1, -1), b2.reshape(1, -1))


def _dft_constants(s_len):
    n = np.arange(s_len, dtype=np.int64)
    ang_s = 2.0 * np.pi * ((n[:, None] * n[None, :]) % s_len) / s_len
    c = np.arange(HEAD_DIM, dtype=np.int64)
    ang_c = 2.0 * np.pi * ((c[:, None] * c[None, :]) % HEAD_DIM) / HEAD_DIM
    norm = 1.0 / np.sqrt(float(s_len * HEAD_DIM))
    cs = np.concatenate([np.cos(ang_s), -np.sin(ang_s)], axis=1) * norm
    return tuple(jnp.asarray(a, dtype=F32).astype(BF16) for a in (cs, np.cos(ang_c), np.sin(ang_c)))


def _fourier_kernel(u_ref, cs_ref, cc_ref, sc_ref, o_ref, ab_scr):
    n_grp, s_len, _ = u_ref.shape
    cc = cc_ref[...]
    sc = sc_ref[...]
    for g in range(n_grp):
        u = u_ref[g]
        cols = pl.ds(g * HEAD_DIM, HEAD_DIM)
        ab_scr[0:s_len, cols] = jnp.dot(u, cc, preferred_element_type=F32).astype(BF16)
        ab_scr[s_len:, cols] = jnp.dot(u, sc, preferred_element_type=F32).astype(BF16)
    chunk = 512
    for c in range(s_len // chunk):
        rows = pl.ds(c * chunk, chunk)
        o_ref[rows, :] = jnp.dot(cs_ref[rows, :], ab_scr[...], preferred_element_type=F32).astype(o_ref.dtype)


def _fourier_mix(u_heads):
    b_sz, n_grp, _, s_len, _ = u_heads.shape
    cs, cc, sc = _dft_constants(s_len)
    const = dict(pipeline_mode=pl.Buffered(1))
    return pl.pallas_call(
        _fourier_kernel,
        out_shape=jax.ShapeDtypeStruct((b_sz, s_len, n_grp * HEAD_DIM), BF16),
        grid=(b_sz,),
        in_specs=[
            pl.BlockSpec((None, n_grp, None, s_len, HEAD_DIM), lambda b: (b, 0, 0, 0, 0)),
            pl.BlockSpec((s_len, 2 * s_len), lambda b: (0, 0), **const),
            pl.BlockSpec((HEAD_DIM, HEAD_DIM), lambda b: (0, 0), **const),
            pl.BlockSpec((HEAD_DIM, HEAD_DIM), lambda b: (0, 0), **const),
        ],
        out_specs=pl.BlockSpec((None, s_len, n_grp * HEAD_DIM), lambda b: (b, 0, 0)),
        scratch_shapes=[pltpu.VMEM((2 * s_len, n_grp * HEAD_DIM), BF16)],
        compiler_params=_params("arbitrary"),
        name="fourier_mix",
    )(u_heads, cs, cc, sc)


def _split_bf16(a):
    hi = a.astype(BF16)
    lo = (a - hi.astype(F32)).astype(BF16)
    return hi, lo


def _router_kernel(x_ref, w_ref, b_ref, route_ref, cnt_ref, carry_ref):
    tq = route_ref.shape[0]

    @pl.when(pl.program_id(0) == 0)
    def _():
        carry_ref[...] = jnp.zeros_like(carry_ref)

    x_hi, x_lo = _split_bf16(_load_tiled_rows(x_ref))
    w_hi, w_lo = _split_bf16(w_ref[...])
    logits = (jnp.dot(x_hi, w_hi, preferred_element_type=F32)
              + jnp.dot(x_lo, w_hi, preferred_element_type=F32)
              + jnp.dot(x_hi, w_lo, preferred_element_type=F32)) + b_ref[...]
    lane = lax.broadcasted_iota(I32, (tq, LANES), 1).astype(F32)
    logits = jnp.where(lane < N_EXPERTS, logits, NEG_INF)

    def top1(vals):
        m = jnp.max(vals, axis=-1, keepdims=True)
        idx = jnp.min(jnp.where(vals == m, lane, float(LANES)), axis=-1, keepdims=True)
        return m, idx

    m1, i1 = top1(logits)
    oh1 = lane == i1
    m2, i2 = top1(jnp.where(oh1, NEG_INF, logits))
    oh2 = lane == i2
    e2 = jnp.exp(m2 - m1)
    gate1 = 1.0 / (1.0 + e2)
    gate2 = e2 / (1.0 + e2)

    oh1f = oh1.astype(F32)
    oh2f = oh2.astype(F32)
    both = oh1f + oh2f
    tri = (lax.broadcasted_iota(I32, (tq, tq), 0) > lax.broadcasted_iota(I32, (tq, tq), 1)).astype(BF16)
    before = jnp.dot(tri, both.astype(BF16), preferred_element_type=F32) + carry_ref[0:1, :]
    rank1 = jnp.sum(oh1f * before, axis=-1, keepdims=True)
    rank2 = jnp.sum(oh2f * before, axis=-1, keepdims=True)
    carry_ref[...] = carry_ref[...] + jnp.sum(both, axis=0, keepdims=True)

    rec = jnp.zeros((tq, LANES), F32)
    for lane_idx, val in ((ROUTE_E, i1), (ROUTE_E + 1, i2),
                          (ROUTE_RANK, rank1), (ROUTE_RANK + 1, rank2),
                          (ROUTE_GATE, gate1), (ROUTE_GATE + 1, gate2)):
        rec = jnp.where(lane == lane_idx, val, rec)
    route_ref[...] = rec
    cnt_ref[...] = carry_ref[...]


def _router(x_tiled, w, b, *, tq=512):
    t_len = x_tiled.shape[0] // SUBLANES
    d_model = SUBLANES * LANES
    w_pad = jnp.zeros((d_model, LANES), F32).at[:, :N_EXPERTS].set(w)
    b_pad = jnp.zeros((1, LANES), F32).at[0, :N_EXPERTS].set(b)
    return pl.pallas_call(
        _router_kernel,
        out_shape=(jax.ShapeDtypeStruct((t_len, LANES), F32), jax.ShapeDtypeStruct((8, LANES), F32)),
        grid=(t_len // tq,),
        in_specs=[
            pl.BlockSpec((tq * SUBLANES, LANES), lambda i: (i, 0)),
            pl.BlockSpec((d_model, LANES), lambda i: (0, 0)),
            pl.BlockSpec((1, LANES), lambda i: (0, 0)),
        ],
        out_specs=(pl.BlockSpec((tq, LANES), lambda i: (i, 0)), pl.BlockSpec((8, LANES), lambda i: (0, 0))),
        scratch_shapes=[pltpu.VMEM((8, LANES), F32)],
        compiler_params=_params("arbitrary"),
        name="router",
    )(x_tiled, w_pad, b_pad)


def _store_tiled_rows(dst_ref, val):
    n_rows, width = val.shape
    for j in range(width // LANES):
        dst_ref[pl.ds(j, n_rows, stride=width // LANES), :] = val[:, j * LANES:(j + 1) * LANES]


def _load_tiled_rows(src_ref):
    n_rows = src_ref.shape[0] // SUBLANES
    return jnp.concatenate([src_ref[pl.ds(j, n_rows, stride=SUBLANES), :] for j in range(SUBLANES)], axis=1)


def _tiled_row(ref, row):
    return ref.at[pl.ds(pl.multiple_of(row * SUBLANES, SUBLANES), SUBLANES)]


def _dispatch_kernel(pos_ref, x_ref, init_hbm, xs_hbm, sem):
    del init_hbm
    tq = x_ref.shape[0] // SUBLANES

    def start(t, carry):
        for k in range(TOP_K):
            pltpu.make_async_copy(_tiled_row(x_ref, t), _tiled_row(xs_hbm, pos_ref[TOP_K * t + k]),
                                  sem).start(priority=k)
        return carry

    lax.fori_loop(0, tq, start, 0, unroll=8)
    for k in range(TOP_K):
        pltpu.make_async_copy(x_ref, xs_hbm.at[pl.ds(0, tq * SUBLANES)], sem).wait()


def _dispatch(x_tiled, pos_flat, n_sorted_rows, *, tq=512):
    t_len = x_tiled.shape[0] // SUBLANES
    init = jnp.zeros((n_sorted_rows * SUBLANES, LANES), x_tiled.dtype)
    return pl.pallas_call(
        _dispatch_kernel,
        out_shape=jax.ShapeDtypeStruct(init.shape, init.dtype),
        grid=(t_len // tq,),
        in_specs=[
            pl.BlockSpec((TOP_K * tq,), lambda i: (i,), memory_space=pltpu.SMEM),
            pl.BlockSpec((tq * SUBLANES, LANES), lambda i: (i, 0)),
            pl.BlockSpec(memory_space=pl.ANY),
        ],
        out_specs=pl.BlockSpec(memory_space=pl.ANY),
        scratch_shapes=[pltpu.SemaphoreType.DMA],
        input_output_aliases={2: 0},
        compiler_params=_params("arbitrary"),
        name="moe_dispatch",
    )(pos_flat, x_tiled, init)


def _expert_kernel(tile_e_ref, nt_ref, x_ref, wg_ref, wu_ref, wd_ref, o_ref, x16_scr, acc_ref):
    del tile_e_ref
    i = pl.program_id(0)
    c = pl.program_id(1)

    @pl.when(c == 0)
    def _():
        acc_ref[...] = jnp.zeros_like(acc_ref)
        x16_scr[...] = _load_tiled_rows(x_ref).astype(BF16)

    @pl.when(i < nt_ref[0])
    def _():
        acc_ref[...] += _swiglu_chunk(x16_scr[...], wg_ref[...].astype(BF16), wu_ref[...].astype(BF16),
                                      wd_ref[...].astype(BF16))

    @pl.when(c == pl.num_programs(1) - 1)
    def _():
        _store_tiled_rows(o_ref, acc_ref[...])


def _expert_ffn(xs, tile_e, n_tiles, wg, wu, wd, *, tm, tf):
    d_model = wg.shape[1]
    d_ff = wg.shape[2]
    n_chunks = d_ff // tf
    last = n_chunks - 1

    def row_map(i, c, te, nt):
        return (jnp.minimum(i, nt[0] - 1), 0)

    def chunk_of(i, c, nt):
        return jnp.where(i < nt[0], c, last)

    grid_spec = pltpu.PrefetchScalarGridSpec(
        num_scalar_prefetch=2,
        grid=(xs.shape[0] // (tm * SUBLANES), n_chunks),
        in_specs=[
            pl.BlockSpec((tm * SUBLANES, LANES), row_map),
            pl.BlockSpec((None, d_model, tf), lambda i, c, te, nt: (te[i], 0, chunk_of(i, c, nt))),
            pl.BlockSpec((None, d_model, tf), lambda i, c, te, nt: (te[i], 0, chunk_of(i, c, nt))),
            pl.BlockSpec((None, tf, d_model), lambda i, c, te, nt: (te[i], chunk_of(i, c, nt), 0)),
        ],
        out_specs=pl.BlockSpec((tm * SUBLANES, LANES), lambda i, c, te, nt: (i, 0)),
        scratch_shapes=[pltpu.VMEM((tm, d_model), BF16), pltpu.VMEM((tm, d_model), F32)],
    )
    return pl.pallas_call(
        _expert_kernel,
        out_shape=jax.ShapeDtypeStruct(xs.shape, F32),
        grid_spec=grid_spec,
        compiler_params=_params("arbitrary", "arbitrary"),
        name="expert_ffn",
    )(tile_e, n_tiles, xs, wg, wu, wd)


def _combine_kernel(pos_ref, res_ref, route_ref, g_ref, b_ref, ys_hbm, o_ref, buf_ref, sem):
    tq = o_ref.shape[0]

    def start(t, carry):
        for k in range(TOP_K):
            pltpu.make_async_copy(_tiled_row(ys_hbm, pos_ref[TOP_K * t + k]), _tiled_row(buf_ref.at[k], t),
                                  sem).start(priority=k)
        return carry

    lax.fori_loop(0, tq, start, 0, unroll=8)
    for k in range(TOP_K):
        pltpu.make_async_copy(ys_hbm.at[pl.ds(0, tq * SUBLANES)], buf_ref.at[k], sem).wait()

    route = route_ref[...]
    y = (route[:, ROUTE_GATE:ROUTE_GATE + 1] * _load_tiled_rows(buf_ref.at[0])
         + route[:, ROUTE_GATE + 1:ROUTE_GATE + 2] * _load_tiled_rows(buf_ref.at[1]))
    o_ref[...] = _layer_norm(ALPHA * _load_tiled_rows(res_ref) + y, g_ref[...], b_ref[...])


def _combine_ln(ys, pos_flat, res_tiled, route, gain, bias, *, tq=256):
    t_len = res_tiled.shape[0] // SUBLANES
    d_model = SUBLANES * LANES
    return pl.pallas_call(
        _combine_kernel,
        out_shape=jax.ShapeDtypeStruct((t_len, d_model), F32),
        grid=(t_len // tq,),
        in_specs=[
            pl.BlockSpec((TOP_K * tq,), lambda i: (i,), memory_space=pltpu.SMEM),
            pl.BlockSpec((tq * SUBLANES, LANES), lambda i: (i, 0)),
            pl.BlockSpec((tq, LANES), lambda i: (i, 0)),
            pl.BlockSpec((1, d_model), lambda i: (0, 0)),
            pl.BlockSpec((1, d_model), lambda i: (0, 0)),
            pl.BlockSpec(memory_space=pl.ANY),
        ],
        out_specs=pl.BlockSpec((tq, d_model), lambda i: (i, 0)),
        scratch_shapes=[pltpu.VMEM((TOP_K, tq * SUBLANES, LANES), F32), pltpu.SemaphoreType.DMA],
        compiler_params=_params("arbitrary"),
        name="moe_combine_ln",
    )(pos_flat, res_tiled, route, gain.reshape(1, -1), bias.reshape(1, -1), ys)


def _moe_ffn_ln(x_tiled, router_w, router_b, wg, wu, wd, gain, bias, *, tm=MOE_ROW_TILE, tf=MOE_FF_CHUNK):
    t_len = x_tiled.shape[0] // SUBLANES
    route, counts = _router(x_tiled, router_w, router_b)
    experts = route[:, ROUTE_E:ROUTE_E + TOP_K].astype(I32)
    ranks = route[:, ROUTE_RANK:ROUTE_RANK + TOP_K].astype(I32)
    counts = counts[0, :N_EXPERTS].astype(I32)
    tiles_per_expert = (counts + tm - 1) // tm
    tile_end = jnp.cumsum(tiles_per_expert)
    row_start = (tile_end - tiles_per_expert) * tm
    pos_flat = (row_start[experts] + ranks).reshape(-1)
    n_tiles_max = (TOP_K * t_len) // tm + N_EXPERTS
    n_tiles = tile_end[-1:]
    tile_ids = jnp.minimum(jnp.arange(n_tiles_max, dtype=I32), n_tiles[0] - 1)
    tile_e = jnp.sum((tile_ids[:, None] >= tile_end[None, :-1]).astype(I32), axis=1)

    xs = _dispatch(x_tiled, pos_flat, n_tiles_max * tm)
    ys = _expert_ffn(xs, tile_e, n_tiles.astype(I32), wg, wu, wd, tm=tm, tf=tf)
    return _combine_ln(ys, pos_flat, x_tiled, route, gain, bias)


def _alibi_slopes():
    exps = np.arange(1, N_ATTN_HEADS + 1, dtype=np.float32) * np.float32(8.0 / N_ATTN_HEADS)
    return jnp.asarray(np.exp2(-exps).astype(np.float32))


def kernel(x, mem, a_w_in, a_w_mem_kv, a_w_out, a_ln1_g, a_ln1_b, a_ffn_gate, a_ffn_up, a_ffn_down, a_ln2_g, a_ln2_b, b_w_in, b_fourier_g, b_w_mem_kv, b_w_out, b_ln1_g, b_ln1_b, b_router_w, b_router_b, b_moe_gate, b_moe_up, b_moe_down, b_ln2_g, b_ln2_b):
    b_sz, s_len, d_model = x.shape
    t_len = b_sz * s_len
    x16 = x.astype(BF16)
    mem16 = mem.astype(BF16)
    x32 = x.reshape(t_len, d_model)
    a_w_in, a_w_mem_kv, a_w_out, a_ffn_gate, a_ffn_up, a_ffn_down, b_w_in, b_w_mem_kv, b_w_out = (
        w[0].astype(BF16) for w in (a_w_in, a_w_mem_kv, a_w_out, a_ffn_gate, a_ffn_up, a_ffn_down,
                                    b_w_in, b_w_mem_kv, b_w_out))

    n_qkv_blocks = 3 * len(DILATIONS)
    qkv = [_proj_heads(x16, a_w_in, col_block0=g, col_block_stride=len(DILATIONS), n_blocks=3, dil=dil)
           for g, dil in enumerate(DILATIONS)]
    q_mem = _proj_heads(x16, a_w_in, col_block0=n_qkv_blocks, col_block_stride=1, n_blocks=1)
    mem_kv = _proj_heads(mem16, a_w_mem_kv, col_block0=0, col_block_stride=1, n_blocks=2)
    attn = _dilated_attention(_alibi_slopes(), qkv).reshape(t_len, -1)
    memo = _memory_attention(q_mem, 0, mem_kv).reshape(t_len, -1)
    x32, x16 = _mixer_ffn_ln(attn, memo, a_w_out, x32, a_ln1_g[0], a_ln1_b[0],
                             a_ffn_gate, a_ffn_up, a_ffn_down, a_ln2_g[0], a_ln2_b[0])

    x16_b = x16.reshape(b_sz, s_len, d_model)
    u = _proj_heads(x16_b, b_w_in, col_block0=0, col_block_stride=1, n_blocks=1, gain=b_fourier_g[0])
    q_mem = _proj_heads(x16_b, b_w_in, col_block0=1, col_block_stride=1, n_blocks=1)
    mem_kv = _proj_heads(mem16, b_w_mem_kv, col_block0=0, col_block_stride=1, n_blocks=2)
    four = _fourier_mix(u).reshape(t_len, -1)
    memo = _memory_attention(q_mem, 0, mem_kv).reshape(t_len, -1)
    x_tiled = _out_proj_ln(four, memo, b_w_out, x32, b_ln1_g[0], b_ln1_b[0])
    out = _moe_ffn_ln(x_tiled, b_router_w[0], b_router_b[0], b_moe_gate[0], b_moe_up[0], b_moe_down[0],
                      b_ln2_g[0], b_ln2_b[0])
    return out.reshape(b_sz, s_len, d_model)
```

```python
import functools

import numpy as np
import jax
import jax.numpy as jnp
from jax import lax
from jax.experimental import pallas as pl
from jax.experimental.pallas import tpu as pltpu

F32 = jnp.float32
BF16 = jnp.bfloat16
I32 = jnp.int32

LANES = 128
HEAD_DIM = 128
HEADS_PER_BLOCK = 4
DILATIONS = (1, 4, 16)
N_SIDE = 64
ATTN_Q_BLOCK = 64
ATTN_K_WINDOW = 256
ATTN_BATCH = 16
N_ATTN_HEADS = 12
N_EXPERTS = 8
TOP_K = 2
MOE_ROW_TILE = 1024
MOE_FF_CHUNK = 512
SUBLANES = 8
ALPHA = (2.0 * 2) ** 0.25
LN_EPS = 1e-5
NEG_INF = -1e30
VMEM_LIMIT_BYTES = 56 * 1024 * 1024

ROUTE_E, ROUTE_RANK, ROUTE_GATE = 0, 2, 4


def _params(*sem):
    return pltpu.CompilerParams(dimension_semantics=sem, vmem_limit_bytes=VMEM_LIMIT_BYTES)


def _layer_norm(z, g, b=None):
    mu = jnp.mean(z, axis=-1, keepdims=True)
    zc = z - mu
    var = jnp.mean(zc * zc, axis=-1, keepdims=True)
    y = zc * lax.rsqrt(var + LN_EPS) * g
    return y if b is None else y + b


def _proj_kernel(*refs, dil, row_chunk, group_ln):
    if group_ln:
        x_ref, w_ref, g_ref, o_ref = refs[:4]
        scratch = refs[4:]
    else:
        x_ref, w_ref, o_ref = refs[:3]
        g_ref = None
        scratch = refs[3:]
    s_len = x_ref.shape[0]
    w = w_ref[...]
    for rc in range(s_len // row_chunk):
        rows = pl.ds(rc * row_chunk, row_chunk)
        r = jnp.dot(x_ref[rows, :], w, preferred_element_type=F32)
        for hh in range(HEADS_PER_BLOCK):
            rh = r[:, hh * HEAD_DIM:(hh + 1) * HEAD_DIM]
            if group_ln:
                rh = _layer_norm(rh, g_ref[hh:hh + 1, :])
            if dil == 1:
                o_ref[hh, 0, rows, :] = rh.astype(o_ref.dtype)
            else:
                scratch[0][hh, rows, :] = rh
    if dil > 1:
        acc_ref = scratch[0]
        cls_len = s_len // dil
        for hh in range(HEADS_PER_BLOCK):
            for r_ in range(dil):
                o_ref[hh, r_, :, :] = acc_ref.at[hh][pl.ds(r_, cls_len, stride=dil), :].astype(o_ref.dtype)


def _proj_heads(x, w, *, col_block0, col_block_stride, n_blocks, dil=1, gain=None):
    b_sz, s_len, d_model = x.shape
    blk_cols = HEADS_PER_BLOCK * HEAD_DIM
    row_chunk = min(512, s_len)
    in_specs = [
        pl.BlockSpec((None, s_len, d_model), lambda b, wb: (b, 0, 0)),
        pl.BlockSpec((d_model, blk_cols), lambda b, wb: (0, col_block0 + wb * col_block_stride)),
    ]
    args = [x, w]
    if gain is not None:
        in_specs.append(pl.BlockSpec((HEADS_PER_BLOCK, HEAD_DIM), lambda b, wb: (0, 0)))
        args.append(gain)
    scratch = []
    if dil > 1:
        scratch.append(pltpu.VMEM((HEADS_PER_BLOCK, s_len, HEAD_DIM), F32))
    return pl.pallas_call(
        functools.partial(_proj_kernel, dil=dil, row_chunk=row_chunk, group_ln=gain is not None),
        out_shape=jax.ShapeDtypeStruct((b_sz, n_blocks * HEADS_PER_BLOCK, dil, s_len // dil, HEAD_DIM), BF16),
        grid=(b_sz, n_blocks),
        in_specs=in_specs,
        out_specs=pl.BlockSpec((None, HEADS_PER_BLOCK, dil, s_len // dil, HEAD_DIM),
                               lambda b, wb: (b, wb, 0, 0, 0)),
        scratch_shapes=scratch,
        compiler_params=_params("arbitrary", "arbitrary"),
        name=f"proj_heads_d{dil}",
    )(*args)


def _dil_attn_kernel(slopes_ref, q0, k0, v0, q1, k1, v1, q2, k2, v2, o_ref, o_scr, l_scr, bias_scr):
    s_len = o_ref.shape[0]
    h = pl.program_id(1)
    scale = HEAD_DIM ** -0.5
    qb = ATTN_Q_BLOCK
    groups = ((q0, k0, v0), (q1, k1, v1), (q2, k2, v2))
    for g, (qr, kr, vr) in enumerate(groups):
        dil = DILATIONS[g]
        cls_len = s_len // dil
        kw = min(ATTN_K_WINDOW, cls_len)
        nb = cls_len // qb
        nb_shift = nb.bit_length() - 1

        neg_slope = jnp.full((1, kw), slopes_ref[g * HEADS_PER_BLOCK + h], F32) * (-float(dil))
        base_delta = (lax.broadcasted_iota(I32, (qb, kw), 1) - lax.broadcasted_iota(I32, (qb, kw), 0))
        for case in range(kw // N_SIDE):
            dist = jnp.abs(base_delta - case * N_SIDE)
            bias_scr[g, case, :, 0:kw] = jnp.where(dist <= N_SIDE, dist.astype(F32) * neg_slope, NEG_INF)

        def body(it, carry, qr=qr, kr=kr, vr=vr, dil=dil, cls_len=cls_len, kw=kw, nb=nb, g=g,
                 nb_shift=nb_shift):
            blocks = []
            for j in range(ATTN_BATCH):
                idx = it * ATTN_BATCH + j
                r = lax.shift_right_logical(idx, nb_shift)
                i = lax.bitwise_and(idx, nb - 1)
                qs = pl.multiple_of(i * qb, qb)
                ks = pl.multiple_of(jnp.clip(qs - N_SIDE, 0, cls_len - kw), N_SIDE)
                blocks.append((r, qs, ks))
            scores = []
            for r, qs, ks in blocks:
                q = qr[r, pl.ds(qs, qb), :]
                k = kr[r, pl.ds(ks, kw), :]
                scores.append(lax.dot_general(q, k, (((1,), (1,)), ((), ())), preferred_element_type=F32))
            probs = []
            for (r, qs, ks), s in zip(blocks, scores):
                bias = bias_scr[g, lax.shift_right_logical(qs - ks, N_SIDE.bit_length() - 1), :, 0:kw]
                s = s * scale + bias
                m = jnp.max(s, axis=-1, keepdims=True)
                p = jnp.exp(s - m)
                l = jnp.sum(p, axis=-1, keepdims=True)
                probs.append((p.astype(BF16), m, l))
            for (r, qs, ks), (p, m, l) in zip(blocks, probs):
                v = vr[r, pl.ds(ks, kw), :]
                o = jnp.dot(p, v, preferred_element_type=F32) / l
                lse = jnp.broadcast_to(m + jnp.log(l), (qb, LANES))
                row0 = qs * dil + r
                if dil == 1:
                    o_scr[g, pl.ds(row0, qb), :] = o
                    l_scr[g, pl.ds(row0, qb), :] = lse
                else:
                    o_scr.at[g][pl.ds(row0, qb, stride=dil), :] = o
                    l_scr.at[g][pl.ds(row0, qb, stride=dil), :] = lse
            return carry

        lax.fori_loop(0, dil * nb // ATTN_BATCH, body, 0)

    chunk = 256
    for c in range(s_len // chunk):
        rows = pl.ds(c * chunk, chunk)
        l0, l1, l2 = l_scr[0, rows, :], l_scr[1, rows, :], l_scr[2, rows, :]
        m = jnp.maximum(jnp.maximum(l0, l1), l2)
        e0, e1, e2 = jnp.exp(l0 - m), jnp.exp(l1 - m), jnp.exp(l2 - m)
        num = e0 * o_scr[0, rows, :] + e1 * o_scr[1, rows, :] + e2 * o_scr[2, rows, :]
        o_ref[rows, :] = (num / (e0 + e1 + e2)).astype(o_ref.dtype)


def _dilated_attention(slopes, qkv):
    b_sz = qkv[0].shape[0]
    s_len = qkv[0].shape[2] * qkv[0].shape[3]
    in_specs = [pl.BlockSpec(memory_space=pltpu.SMEM)]
    args = [slopes]
    for g, arr in enumerate(qkv):
        dil = DILATIONS[g]
        for which in range(3):
            in_specs.append(pl.BlockSpec((None, None, dil, s_len // dil, HEAD_DIM),
                                         lambda b, h, which=which: (b, which * HEADS_PER_BLOCK + h, 0, 0, 0)))
            args.append(arr)
    return pl.pallas_call(
        _dil_attn_kernel,
        out_shape=jax.ShapeDtypeStruct((b_sz, s_len, HEADS_PER_BLOCK * HEAD_DIM), BF16),
        grid=(b_sz, HEADS_PER_BLOCK),
        in_specs=in_specs,
        out_specs=pl.BlockSpec((None, s_len, HEAD_DIM), lambda b, h: (b, 0, h)),
        scratch_shapes=[pltpu.VMEM((3, s_len, HEAD_DIM), F32), pltpu.VMEM((3, s_len, LANES), F32),
                        pltpu.VMEM((3, ATTN_K_WINDOW // N_SIDE, ATTN_Q_BLOCK, ATTN_K_WINDOW), F32)],
        compiler_params=_params("arbitrary", "arbitrary"),
        name="dilated_attention",
    )(*args)


def _mem_attn_kernel(q_ref, k_ref, v_ref, o_ref):
    s_len = q_ref.shape[0]
    scale = HEAD_DIM ** -0.5
    k = k_ref[...]
    v = v_ref[...]
    chunk = 256
    row_chunks = [pl.ds(c * chunk, chunk) for c in range(s_len // chunk)]
    scores = [lax.dot_general(q_ref[rows, :], k, (((1,), (1,)), ((), ())), preferred_element_type=F32)
              for rows in row_chunks]
    probs = []
    for s in scores:
        s = s * scale
        m = jnp.max(s, axis=-1, keepdims=True)
        p = jnp.exp(s - m)
        probs.append((p.astype(BF16), jnp.sum(p, axis=-1, keepdims=True)))
    for rows, (p, l) in zip(row_chunks, probs):
        o_ref[rows, :] = (jnp.dot(p, v, preferred_element_type=F32) / l).astype(o_ref.dtype)


def _memory_attention(q_heads, q_head0, kv_heads):
    b_sz, _, _, s_len, _ = q_heads.shape
    n_mem = kv_heads.shape[3]
    return pl.pallas_call(
        _mem_attn_kernel,
        out_shape=jax.ShapeDtypeStruct((b_sz, s_len, HEADS_PER_BLOCK * HEAD_DIM), BF16),
        grid=(b_sz, HEADS_PER_BLOCK),
        in_specs=[
            pl.BlockSpec((None, None, None, s_len, HEAD_DIM), lambda b, h: (b, q_head0 + h, 0, 0, 0)),
            pl.BlockSpec((None, None, None, n_mem, HEAD_DIM), lambda b, h: (b, h, 0, 0, 0)),
            pl.BlockSpec((None, None, None, n_mem, HEAD_DIM), lambda b, h: (b, HEADS_PER_BLOCK + h, 0, 0, 0)),
        ],
        out_specs=pl.BlockSpec((None, s_len, HEAD_DIM), lambda b, h: (b, 0, h)),
        compiler_params=_params("arbitrary", "arbitrary"),
        name="memory_attention",
    )(q_heads, kv_heads, kv_heads)


def _out_ln_kernel(a1_ref, a2_ref, w_ref, res_ref, g_ref, b_ref, o_ref):
    k1 = a1_ref.shape[1]
    y = jnp.dot(a1_ref[...], w_ref[0:k1, :], preferred_element_type=F32)
    y = y + jnp.dot(a2_ref[...], w_ref[k1:, :], preferred_element_type=F32)
    _store_tiled_rows(o_ref, _layer_norm(ALPHA * res_ref[...] + y, g_ref[...], b_ref[...]))


def _out_proj_ln(a1, a2, w, res, gain, bias, *, tm=512):
    t_len, d_model = res.shape
    assert d_model == SUBLANES * LANES
    k1, k2 = a1.shape[1], a2.shape[1]
    return pl.pallas_call(
        _out_ln_kernel,
        out_shape=jax.ShapeDtypeStruct((t_len * SUBLANES, LANES), F32),
        grid=(t_len // tm,),
        in_specs=[
            pl.BlockSpec((tm, k1), lambda i: (i, 0)),
            pl.BlockSpec((tm, k2), lambda i: (i, 0)),
            pl.BlockSpec((k1 + k2, d_model), lambda i: (0, 0), pipeline_mode=pl.Buffered(1)),
            pl.BlockSpec((tm, d_model), lambda i: (i, 0)),
            pl.BlockSpec((1, d_model), lambda i: (0, 0)),
            pl.BlockSpec((1, d_model), lambda i: (0, 0)),
        ],
        out_specs=pl.BlockSpec((tm * SUBLANES, LANES), lambda i: (i, 0)),
        compiler_params=_params("arbitrary"),
        name="out_proj_ln",
    )(a1, a2, w, res, gain.reshape(1, -1), bias.reshape(1, -1))


def _swiglu_chunk(x, wg, wu, wd):
    hg = jnp.dot(x, wg, preferred_element_type=F32)
    hu = jnp.dot(x, wu, preferred_element_type=F32)
    h = hg * (1.0 / (1.0 + jnp.exp(-hg))) * hu
    return jnp.dot(h.astype(BF16), wd, preferred_element_type=F32)


def _mixer_ffn_kernel(a1_ref, a2_ref, wo_ref, res_ref, g1_ref, b1_ref, wg_ref, wu_ref, wd_ref, g2_ref, b2_ref,
                      o32_ref, o16_ref, *, tf):
    k1 = a1_ref.shape[1]
    y = jnp.dot(a1_ref[...], wo_ref[0:k1, :], preferred_element_type=F32)
    y = y + jnp.dot(a2_ref[...], wo_ref[k1:, :], preferred_element_type=F32)
    x1 = _layer_norm(ALPHA * res_ref[...] + y, g1_ref[...], b1_ref[...])
    x16 = x1.astype(BF16)
    acc = None
    for f in range(wg_ref.shape[1] // tf):
        cols = pl.ds(f * tf, tf)
        part = _swiglu_chunk(x16, wg_ref[:, cols], wu_ref[:, cols], wd_ref[cols, :])
        acc = part if acc is None else acc + part
    z = _layer_norm(ALPHA * x1 + acc, g2_ref[...], b2_ref[...])
    o32_ref[...] = z
    o16_ref[...] = z.astype(BF16)


def _mixer_ffn_ln(a1, a2, wo, res, g1, b1, wg, wu, wd, g2, b2, *, tm=512, tf=256):
    t_len, d_model = res.shape
    k1, k2 = a1.shape[1], a2.shape[1]
    d_ff = wg.shape[1]
    resident = dict(pipeline_mode=pl.Buffered(1))
    vec = pl.BlockSpec((1, d_model), lambda i: (0, 0))
    return pl.pallas_call(
        functools.partial(_mixer_ffn_kernel, tf=tf),
        out_shape=(jax.ShapeDtypeStruct((t_len, d_model), F32), jax.ShapeDtypeStruct((t_len, d_model), BF16)),
        grid=(t_len // tm,),
        in_specs=[
            pl.BlockSpec((tm, k1), lambda i: (i, 0)),
            pl.BlockSpec((tm, k2), lambda i: (i, 0)),
            pl.BlockSpec((k1 + k2, d_model), lambda i: (0, 0), **resident),
            pl.BlockSpec((tm, d_model), lambda i: (i, 0)),
            vec, vec,
            pl.BlockSpec((d_model, d_ff), lambda i: (0, 0), **resident),
            pl.BlockSpec((d_model, d_ff), lambda i: (0, 0), **resident),
            pl.BlockSpec((d_ff, d_model), lambda i: (0, 0), **resident),
            vec, vec,
        ],
        out_specs=(pl.BlockSpec((tm, d_model), lambda i: (i, 0)),
                   pl.BlockSpec((tm, d_model), lambda i: (i, 0))),
        compiler_params=_params("arbitrary"),
        name="mixer_ffn_ln",
    )(a1, a2, wo, res, g1.reshape(1, -1), b1.reshape(1, -1), wg, wu, wd, g2.reshape(1, -1), b2.reshape(1, -1))


def _dft_constants(s_len):
    n = np.arange(s_len, dtype=np.int64)
    ang_s = 2.0 * np.pi * ((n[:, None] * n[None, :]) % s_len) / s_len
    c = np.arange(HEAD_DIM, dtype=np.int64)
    ang_c = 2.0 * np.pi * ((c[:, None] * c[None, :]) % HEAD_DIM) / HEAD_DIM
    norm = 1.0 / np.sqrt(float(s_len * HEAD_DIM))
    cs = np.concatenate([np.cos(ang_s), -np.sin(ang_s)], axis=1) * norm
    return tuple(jnp.asarray(a, dtype=F32).astype(BF16) for a in (cs, np.cos(ang_c), np.sin(ang_c)))


def _fourier_kernel(u_ref, cs_ref, cc_ref, sc_ref, o_ref, ab_scr):
    n_grp, s_len, _ = u_ref.shape
    cc = cc_ref[...]
    sc = sc_ref[...]
    for g in range(n_grp):
        u = u_ref[g]
        cols = pl.ds(g * HEAD_DIM, HEAD_DIM)
        ab_scr[0:s_len, cols] = jnp.dot(u, cc, preferred_element_type=F32).astype(BF16)
        ab_scr[s_len:, cols] = jnp.dot(u, sc, preferred_element_type=F32).astype(BF16)
    chunk = 512
    for c in range(s_len // chunk):
        rows = pl.ds(c * chunk, chunk)
        o_ref[rows, :] = jnp.dot(cs_ref[rows, :], ab_scr[...], preferred_element_type=F32).astype(o_ref.dtype)


def _fourier_mix(u_heads):
    b_sz, n_grp, _, s_len, _ = u_heads.shape
    cs, cc, sc = _dft_constants(s_len)
    const = dict(pipeline_mode=pl.Buffered(1))
    return pl.pallas_call(
        _fourier_kernel,
        out_shape=jax.ShapeDtypeStruct((b_sz, s_len, n_grp * HEAD_DIM), BF16),
        grid=(b_sz,),
        in_specs=[
            pl.BlockSpec((None, n_grp, None, s_len, HEAD_DIM), lambda b: (b, 0, 0, 0, 0)),
            pl.BlockSpec((s_len, 2 * s_len), lambda b: (0, 0), **const),
            pl.BlockSpec((HEAD_DIM, HEAD_DIM), lambda b: (0, 0), **const),
            pl.BlockSpec((HEAD_DIM, HEAD_DIM), lambda b: (0, 0), **const),
        ],
        out_specs=pl.BlockSpec((None, s_len, n_grp * HEAD_DIM), lambda b: (b, 0, 0)),
        scratch_shapes=[pltpu.VMEM((2 * s_len, n_grp * HEAD_DIM), BF16)],
        compiler_params=_params("arbitrary"),
        name="fourier_mix",
    )(u_heads, cs, cc, sc)


def _split_bf16(a):
    hi = a.astype(BF16)
    lo = (a - hi.astype(F32)).astype(BF16)
    return hi, lo


def _router_kernel(x_ref, w_ref, b_ref, route_ref, cnt_ref, carry_ref):
    tq = route_ref.shape[0]

    @pl.when(pl.program_id(0) == 0)
    def _():
        carry_ref[...] = jnp.zeros_like(carry_ref)

    x_hi, x_lo = _split_bf16(_load_tiled_rows(x_ref))
    w_hi, w_lo = _split_bf16(w_ref[...])
    logits = (jnp.dot(x_hi, w_hi, preferred_element_type=F32)
              + jnp.dot(x_lo, w_hi, preferred_element_type=F32)
              + jnp.dot(x_hi, w_lo, preferred_element_type=F32)) + b_ref[...]
    lane = lax.broadcasted_iota(I32, (tq, LANES), 1).astype(F32)
    logits = jnp.where(lane < N_EXPERTS, logits, NEG_INF)

    def top1(vals):
        m = jnp.max(vals, axis=-1, keepdims=True)
        idx = jnp.min(jnp.where(vals == m, lane, float(LANES)), axis=-1, keepdims=True)
        return m, idx

    m1, i1 = top1(logits)
    oh1 = lane == i1
    m2, i2 = top1(jnp.where(oh1, NEG_INF, logits))
    oh2 = lane == i2
    e2 = jnp.exp(m2 - m1)
    gate1 = 1.0 / (1.0 + e2)
    gate2 = e2 / (1.0 + e2)

    oh1f = oh1.astype(F32)
    oh2f = oh2.astype(F32)
    both = oh1f + oh2f
    tri = (lax.broadcasted_iota(I32, (tq, tq), 0) > lax.broadcasted_iota(I32, (tq, tq), 1)).astype(BF16)
    before = jnp.dot(tri, both.astype(BF16), preferred_element_type=F32) + carry_ref[0:1, :]
    rank1 = jnp.sum(oh1f * before, axis=-1, keepdims=True)
    rank2 = jnp.sum(oh2f * before, axis=-1, keepdims=True)
    carry_ref[...] = carry_ref[...] + jnp.sum(both, axis=0, keepdims=True)

    rec = jnp.zeros((tq, LANES), F32)
    for lane_idx, val in ((ROUTE_E, i1), (ROUTE_E + 1, i2),
                          (ROUTE_RANK, rank1), (ROUTE_RANK + 1, rank2),
                          (ROUTE_GATE, gate1), (ROUTE_GATE + 1, gate2)):
        rec = jnp.where(lane == lane_idx, val, rec)
    route_ref[...] = rec
    cnt_ref[...] = carry_ref[...]


def _router(x_tiled, w, b, *, tq=512):
    t_len = x_tiled.shape[0] // SUBLANES
    d_model = SUBLANES * LANES
    w_pad = jnp.zeros((d_model, LANES), F32).at[:, :N_EXPERTS].set(w)
    b_pad = jnp.zeros((1, LANES), F32).at[0, :N_EXPERTS].set(b)
    return pl.pallas_call(
        _router_kernel,
        out_shape=(jax.ShapeDtypeStruct((t_len, LANES), F32), jax.ShapeDtypeStruct((8, LANES), F32)),
        grid=(t_len // tq,),
        in_specs=[
            pl.BlockSpec((tq * SUBLANES, LANES), lambda i: (i, 0)),
            pl.BlockSpec((d_model, LANES), lambda i: (0, 0)),
            pl.BlockSpec((1, LANES), lambda i: (0, 0)),
        ],
        out_specs=(pl.BlockSpec((tq, LANES), lambda i: (i, 0)), pl.BlockSpec((8, LANES), lambda i: (0, 0))),
        scratch_shapes=[pltpu.VMEM((8, LANES), F32)],
        compiler_params=_params("arbitrary"),
        name="router",
    )(x_tiled, w_pad, b_pad)


def _store_tiled_rows(dst_ref, val):
    n_rows, width = val.shape
    for j in range(width // LANES):
        dst_ref[pl.ds(j, n_rows, stride=width // LANES), :] = val[:, j * LANES:(j + 1) * LANES]


def _load_tiled_rows(src_ref):
    n_rows = src_ref.shape[0] // SUBLANES
    return jnp.concatenate([src_ref[pl.ds(j, n_rows, stride=SUBLANES), :] for j in range(SUBLANES)], axis=1)


def _tiled_row(ref, row):
    return ref.at[pl.ds(pl.multiple_of(row * SUBLANES, SUBLANES), SUBLANES)]


def _dispatch_kernel(fill_ref, pos_ref, x_ref, xs_hbm, zero_scr, sem, *, tm):
    tq = x_ref.shape[0] // SUBLANES

    @pl.when(pl.program_id(0) == 0)
    def _():
        zero_scr[...] = jnp.zeros_like(zero_scr)

        def fill_copies(tile):
            copies = []
            for piece in range(tm // tq):
                row0 = pl.multiple_of((tile * tm + piece * tq) * SUBLANES, SUBLANES)
                copies.append(pltpu.make_async_copy(zero_scr, xs_hbm.at[pl.ds(row0, tq * SUBLANES)], sem))
            return copies

        def start_fill(tile, carry):
            @pl.when(fill_ref[tile] == 1)
            def _():
                for copy in fill_copies(tile):
                    copy.start()
            return carry

        def wait_fill(tile, carry):
            @pl.when(fill_ref[tile] == 1)
            def _():
                for copy in fill_copies(tile):
                    copy.wait()
            return carry

        lax.fori_loop(0, fill_ref.shape[0], start_fill, 0)
        lax.fori_loop(0, fill_ref.shape[0], wait_fill, 0)

    def start(t, carry):
        for k in range(TOP_K):
            pltpu.make_async_copy(_tiled_row(x_ref, t), _tiled_row(xs_hbm, pos_ref[TOP_K * t + k]),
                                  sem).start(priority=k)
        return carry

    lax.fori_loop(0, tq, start, 0, unroll=8)
    for k in range(TOP_K):
        pltpu.make_async_copy(x_ref, xs_hbm.at[pl.ds(0, tq * SUBLANES)], sem).wait()


def _dispatch(x_tiled, pos_flat, fill_tiles, *, tm, tq=512):
    t_len = x_tiled.shape[0] // SUBLANES
    assert tm % tq == 0
    grid_spec = pltpu.PrefetchScalarGridSpec(
        num_scalar_prefetch=1,
        grid=(t_len // tq,),
        in_specs=[
            pl.BlockSpec((TOP_K * tq,), lambda i, fill: (i,), memory_space=pltpu.SMEM),
            pl.BlockSpec((tq * SUBLANES, LANES), lambda i, fill: (i, 0)),
        ],
        out_specs=pl.BlockSpec(memory_space=pl.ANY),
        scratch_shapes=[pltpu.VMEM((tq * SUBLANES, LANES), x_tiled.dtype), pltpu.SemaphoreType.DMA],
    )
    return pl.pallas_call(
        functools.partial(_dispatch_kernel, tm=tm),
        out_shape=jax.ShapeDtypeStruct((fill_tiles.shape[0] * tm * SUBLANES, LANES), x_tiled.dtype),
        grid_spec=grid_spec,
        compiler_params=_params("arbitrary"),
        name="moe_dispatch",
    )(fill_tiles, pos_flat, x_tiled)


def _expert_kernel(tile_e_ref, nt_ref, nv_ref, x_ref, wg_ref, wu_ref, wd_ref, o_ref, x16_scr, acc_ref):
    del tile_e_ref, nt_ref
    c = pl.program_id(1)
    n_valid = nv_ref[pl.program_id(0)]
    half = x16_scr.shape[0] // 2

    @pl.when(c == 0)
    def _():
        acc_ref[...] = jnp.zeros_like(acc_ref)
        x16_scr[...] = _load_tiled_rows(x_ref).astype(BF16)

    def accumulate(rows):
        acc_ref[rows, :] += _swiglu_chunk(x16_scr[rows, :], wg_ref[...].astype(BF16), wu_ref[...].astype(BF16),
                                          wd_ref[...].astype(BF16))

    @pl.when(n_valid > half)
    def _():
        accumulate(pl.ds(0, 2 * half))

    @pl.when(jnp.logical_and(n_valid > 0, n_valid <= half))
    def _():
        accumulate(pl.ds(0, half))

    @pl.when(c == pl.num_programs(1) - 1)
    def _():
        _store_tiled_rows(o_ref, acc_ref[...])


def _expert_ffn(xs, tile_e, n_tiles, n_valid, wg, wu, wd, *, tm, tf):
    d_model = wg.shape[1]
    d_ff = wg.shape[2]
    n_chunks = d_ff // tf
    last = n_chunks - 1

    def row_map(i, c, te, nt, nv):
        return (jnp.minimum(i, nt[0] - 1), 0)

    def chunk_of(i, c, nt):
        return jnp.where(i < nt[0], c, last)

    grid_spec = pltpu.PrefetchScalarGridSpec(
        num_scalar_prefetch=3,
        grid=(xs.shape[0] // (tm * SUBLANES), n_chunks),
        in_specs=[
            pl.BlockSpec((tm * SUBLANES, LANES), row_map),
            pl.BlockSpec((None, d_model, tf), lambda i, c, te, nt, nv: (te[i], 0, chunk_of(i, c, nt))),
            pl.BlockSpec((None, d_model, tf), lambda i, c, te, nt, nv: (te[i], 0, chunk_of(i, c, nt))),
            pl.BlockSpec((None, tf, d_model), lambda i, c, te, nt, nv: (te[i], chunk_of(i, c, nt), 0)),
        ],
        out_specs=pl.BlockSpec((tm * SUBLANES, LANES), lambda i, c, te, nt, nv: (i, 0)),
        scratch_shapes=[pltpu.VMEM((tm, d_model), BF16), pltpu.VMEM((tm, d_model), F32)],
    )
    return pl.pallas_call(
        _expert_kernel,
        out_shape=jax.ShapeDtypeStruct(xs.shape, F32),
        grid_spec=grid_spec,
        compiler_params=_params("arbitrary", "arbitrary"),
        name="expert_ffn",
    )(tile_e, n_tiles, n_valid, xs, wg, wu, wd)


def _combine_kernel(pos_ref, pos_next_ref, res_ref, route_ref, g_ref, b_ref, ys_hbm, o_ref, buf_ref, sem):
    i = pl.program_id(0)
    tq = o_ref.shape[0]
    slot = lax.bitwise_and(i, 1)

    def gather(positions, dst_slot):
        def start(t, carry):
            for k in range(TOP_K):
                pltpu.make_async_copy(_tiled_row(ys_hbm, positions[TOP_K * t + k]),
                                      _tiled_row(buf_ref.at[dst_slot, k], t), sem.at[dst_slot]).start(priority=k)
            return carry
        lax.fori_loop(0, tq, start, 0, unroll=8)

    @pl.when(i == 0)
    def _():
        gather(pos_ref, 0)

    @pl.when(i + 1 < pl.num_programs(0))
    def _():
        gather(pos_next_ref, 1 - slot)

    for k in range(TOP_K):
        pltpu.make_async_copy(ys_hbm.at[pl.ds(0, tq * SUBLANES)], buf_ref.at[slot, k], sem.at[slot]).wait()

    route = route_ref[...]
    y = (route[:, ROUTE_GATE:ROUTE_GATE + 1] * _load_tiled_rows(buf_ref.at[slot, 0])
         + route[:, ROUTE_GATE + 1:ROUTE_GATE + 2] * _load_tiled_rows(buf_ref.at[slot, 1]))
    o_ref[...] = _layer_norm(ALPHA * _load_tiled_rows(res_ref) + y, g_ref[...], b_ref[...])


def _combine_ln(ys, pos_flat, res_tiled, route, gain, bias, *, tq=256):
    t_len = res_tiled.shape[0] // SUBLANES
    d_model = SUBLANES * LANES
    n_steps = t_len // tq
    return pl.pallas_call(
        _combine_kernel,
        out_shape=jax.ShapeDtypeStruct((t_len, d_model), F32),
        grid=(n_steps,),
        in_specs=[
            pl.BlockSpec((TOP_K * tq,), lambda i: (i,), memory_space=pltpu.SMEM),
            pl.BlockSpec((TOP_K * tq,), lambda i: (jnp.minimum(i + 1, n_steps - 1),), memory_space=pltpu.SMEM),
            pl.BlockSpec((tq * SUBLANES, LANES), lambda i: (i, 0)),
            pl.BlockSpec((tq, LANES), lambda i: (i, 0)),
            pl.BlockSpec((1, d_model), lambda i: (0, 0)),
            pl.BlockSpec((1, d_model), lambda i: (0, 0)),
            pl.BlockSpec(memory_space=pl.ANY),
        ],
        out_specs=pl.BlockSpec((tq, d_model), lambda i: (i, 0)),
        scratch_shapes=[pltpu.VMEM((2, TOP_K, tq * SUBLANES, LANES), F32), pltpu.SemaphoreType.DMA((2,))],
        compiler_params=_params("arbitrary"),
        name="moe_combine_ln",
    )(pos_flat, pos_flat, res_tiled, route, gain.reshape(1, -1), bias.reshape(1, -1), ys)


def _moe_ffn_ln(x_tiled, router_w, router_b, wg, wu, wd, gain, bias, *, tm=MOE_ROW_TILE, tf=MOE_FF_CHUNK):
    t_len = x_tiled.shape[0] // SUBLANES
    route, counts = _router(x_tiled, router_w, router_b)
    experts = route[:, ROUTE_E:ROUTE_E + TOP_K].astype(I32)
    ranks = route[:, ROUTE_RANK:ROUTE_RANK + TOP_K].astype(I32)
    counts = counts[0, :N_EXPERTS].astype(I32)
    tiles_per_expert = (counts + tm - 1) // tm
    tile_end = jnp.cumsum(tiles_per_expert)
    row_start = (tile_end - tiles_per_expert) * tm
    pos_flat = (row_start[experts] + ranks).reshape(-1)
    n_tiles_max = (TOP_K * t_len) // tm + N_EXPERTS
    n_tiles = tile_end[-1:]
    all_tiles = jnp.arange(n_tiles_max, dtype=I32)
    tile_ids = jnp.minimum(all_tiles, n_tiles[0] - 1)
    tile_e = jnp.sum((tile_ids[:, None] >= tile_end[None, :-1]).astype(I32), axis=1)
    is_last = jnp.any((all_tiles[:, None] == tile_end[None, :] - 1) & (tiles_per_expert[None, :] > 0), axis=1)
    fill_tiles = (is_last | (all_tiles >= n_tiles[0])).astype(I32)

    first_tile = (tile_end - tiles_per_expert)[tile_e]
    n_valid = jnp.where(all_tiles < n_tiles[0], jnp.clip(counts[tile_e] - (all_tiles - first_tile) * tm, 0, tm), 0)

    xs = _dispatch(x_tiled, pos_flat, fill_tiles, tm=tm)
    ys = _expert_ffn(xs, tile_e, n_tiles.astype(I32), n_valid.astype(I32), wg, wu, wd, tm=tm, tf=tf)
    return _combine_ln(ys, pos_flat, x_tiled, route, gain, bias)


def _alibi_slopes():
    exps = np.arange(1, N_ATTN_HEADS + 1, dtype=np.float32) * np.float32(8.0 / N_ATTN_HEADS)
    return jnp.asarray(np.exp2(-exps).astype(np.float32))


def kernel(x, mem, a_w_in, a_w_mem_kv, a_w_out, a_ln1_g, a_ln1_b, a_ffn_gate, a_ffn_up, a_ffn_down, a_ln2_g, a_ln2_b, b_w_in, b_fourier_g, b_w_mem_kv, b_w_out, b_ln1_g, b_ln1_b, b_router_w, b_router_b, b_moe_gate, b_moe_up, b_moe_down, b_ln2_g, b_ln2_b):
    b_sz, s_len, d_model = x.shape
    t_len = b_sz * s_len
    x16 = x.astype(BF16)
    mem16 = mem.astype(BF16)
    x32 = x.reshape(t_len, d_model)
    a_w_in, a_w_mem_kv, a_w_out, a_ffn_gate, a_ffn_up, a_ffn_down, b_w_in, b_w_mem_kv, b_w_out = (
        w[0].astype(BF16) for w in (a_w_in, a_w_mem_kv, a_w_out, a_ffn_gate, a_ffn_up, a_ffn_down,
                                    b_w_in, b_w_mem_kv, b_w_out))

    n_qkv_blocks = 3 * len(DILATIONS)
    qkv = [_proj_heads(x16, a_w_in, col_block0=g, col_block_stride=len(DILATIONS), n_blocks=3, dil=dil)
           for g, dil in enumerate(DILATIONS)]
    q_mem = _proj_heads(x16, a_w_in, col_block0=n_qkv_blocks, col_block_stride=1, n_blocks=1)
    mem_kv = _proj_heads(mem16, a_w_mem_kv, col_block0=0, col_block_stride=1, n_blocks=2)
    attn = _dilated_attention(_alibi_slopes(), qkv).reshape(t_len, -1)
    memo = _memory_attention(q_mem, 0, mem_kv).reshape(t_len, -1)
    x32, x16 = _mixer_ffn_ln(attn, memo, a_w_out, x32, a_ln1_g[0], a_ln1_b[0],
                             a_ffn_gate, a_ffn_up, a_ffn_down, a_ln2_g[0], a_ln2_b[0])

    x16_b = x16.reshape(b_sz, s_len, d_model)
    u = _proj_heads(x16_b, b_w_in, col_block0=0, col_block_stride=1, n_blocks=1, gain=b_fourier_g[0])
    q_mem = _proj_heads(x16_b, b_w_in, col_block0=1, col_block_stride=1, n_blocks=1)
    mem_kv = _proj_heads(mem16, b_w_mem_kv, col_block0=0, col_block_stride=1, n_blocks=2)
    four = _fourier_mix(u).reshape(t_len, -1)
    memo = _memory_attention(q_mem, 0, mem_kv).reshape(t_len, -1)
    x_tiled = _out_proj_ln(four, memo, b_w_out, x32, b_ln1_g[0], b_ln1_b[0])
    out = _moe_ffn_ln(x_tiled, b_router_w[0], b_router_b[0], b_moe_gate[0], b_moe_up[0], b_moe_down[0],
                      b_ln2_g[0], b_ln2_b[0])
    return out.reshape(b_sz, s_len, d_model)
```

```python
import functools

import numpy as np
import jax
import jax.numpy as jnp
from jax import lax
from jax.experimental import pallas as pl
from jax.experimental.pallas import tpu as pltpu

F32 = jnp.float32
BF16 = jnp.bfloat16
I32 = jnp.int32

LANES = 128
HEAD_DIM = 128
HEADS_PER_BLOCK = 4
DILATIONS = (1, 4, 16)
N_SIDE = 64
ATTN_Q_BLOCK = 64
ATTN_K_WINDOW = 256
ATTN_BATCH = 16
PROJ_MAX_ROW_STRIDE = 4
N_ATTN_HEADS = 12
N_EXPERTS = 8
TOP_K = 2
MOE_ROW_TILE = 1024
MOE_FF_CHUNK = 512
MOE_DISPATCH_TILE = 512
SUBLANES = 8
ALPHA = (2.0 * 2) ** 0.25
LN_EPS = 1e-5
NEG_INF = -1e30
VMEM_LIMIT_BYTES = 56 * 1024 * 1024

ROUTE_E, ROUTE_RANK, ROUTE_GATE = 0, 2, 4


def _params(*sem):
    return pltpu.CompilerParams(dimension_semantics=sem, vmem_limit_bytes=VMEM_LIMIT_BYTES)


def _layer_norm(z, g, b=None):
    mu = jnp.mean(z, axis=-1, keepdims=True)
    zc = z - mu
    var = jnp.mean(zc * zc, axis=-1, keepdims=True)
    y = zc * lax.rsqrt(var + LN_EPS) * g
    return y if b is None else y + b


def _proj_kernel(*refs, dil, row_chunk, ln_blocks):
    n_blocks = len(ln_blocks)
    x_ref, w_refs = refs[0], refs[1:1 + n_blocks]
    rest = refs[1 + n_blocks:]
    g_ref = None
    if any(ln_blocks):
        g_ref, rest = rest[0], rest[1:]
    o_ref, scratch = rest[0], rest[1:]
    s_len = x_ref.shape[0]
    for blk in range(n_blocks):
        w = w_refs[blk][...]
        acc_ref = scratch[blk % 2] if dil > 1 else None
        for rc in range(s_len // row_chunk):
            rows = pl.ds(rc * row_chunk, row_chunk)
            r = jnp.dot(x_ref[rows, :], w, preferred_element_type=F32)
            for hh in range(HEADS_PER_BLOCK):
                rh = r[:, hh * HEAD_DIM:(hh + 1) * HEAD_DIM]
                if ln_blocks[blk]:
                    rh = _layer_norm(rh, g_ref[hh:hh + 1, :])
                if dil == 1:
                    o_ref[blk * HEADS_PER_BLOCK + hh, 0, rows, :] = rh.astype(o_ref.dtype)
                else:
                    acc_ref[hh, rows, :] = rh
        if dil == 1:
            continue
        for hh in range(HEADS_PER_BLOCK):
            head = blk * HEADS_PER_BLOCK + hh
            if dil <= PROJ_MAX_ROW_STRIDE:
                for r_ in range(dil):
                    o_ref[head, r_, :, :] = acc_ref.at[hh][pl.ds(r_, s_len // dil, stride=dil), :].astype(o_ref.dtype)
            else:
                s1, s2 = PROJ_MAX_ROW_STRIDE, dil // PROJ_MAX_ROW_STRIDE
                tmp_ref = scratch[2]
                for a in range(s1):
                    tmp_ref[a] = acc_ref.at[hh][pl.ds(a, s_len // s1, stride=s1), :]
                for a in range(s1):
                    for b in range(s2):
                        o_ref[head, s1 * b + a, :, :] = tmp_ref.at[a][pl.ds(b, s_len // dil, stride=s2), :].astype(
                            o_ref.dtype)


def _proj_heads(x, w, col_blocks, *, dil=1, gain=None, ln_blocks=None):
    b_sz, s_len, d_model = x.shape
    blk_cols = HEADS_PER_BLOCK * HEAD_DIM
    n_blocks = len(col_blocks)
    ln_blocks = tuple(ln_blocks) if ln_blocks is not None else (False,) * n_blocks
    in_specs = [pl.BlockSpec((None, s_len, d_model), lambda b: (b, 0, 0))]
    in_specs += [pl.BlockSpec((d_model, blk_cols), lambda b, cb=cb: (0, cb), pipeline_mode=pl.Buffered(1))
                 for cb in col_blocks]
    args = [x] + [w] * n_blocks
    if any(ln_blocks):
        in_specs.append(pl.BlockSpec((HEADS_PER_BLOCK, HEAD_DIM), lambda b: (0, 0)))
        args.append(gain)
    scratch = []
    if dil > 1:
        scratch += [pltpu.VMEM((HEADS_PER_BLOCK, s_len, HEAD_DIM), F32)] * 2
    if dil > PROJ_MAX_ROW_STRIDE:
        scratch.append(pltpu.VMEM((PROJ_MAX_ROW_STRIDE, s_len // PROJ_MAX_ROW_STRIDE, HEAD_DIM), F32))
    n_heads = n_blocks * HEADS_PER_BLOCK
    return pl.pallas_call(
        functools.partial(_proj_kernel, dil=dil, row_chunk=min(512, s_len), ln_blocks=ln_blocks),
        out_shape=jax.ShapeDtypeStruct((b_sz, n_heads, dil, s_len // dil, HEAD_DIM), BF16),
        grid=(b_sz,),
        in_specs=in_specs,
        out_specs=pl.BlockSpec((None, n_heads, dil, s_len // dil, HEAD_DIM), lambda b: (b, 0, 0, 0, 0)),
        scratch_shapes=scratch,
        compiler_params=_params("arbitrary"),
        name=f"proj_heads_d{dil}",
    )(*args)


def _dil_attn_kernel(slopes_ref, q0, k0, v0, q1, k1, v1, q2, k2, v2, o_ref, o_scr, l_scr, bias_scr):
    s_len = o_ref.shape[0]
    h = pl.program_id(1)
    scale = HEAD_DIM ** -0.5
    qb = ATTN_Q_BLOCK
    groups = ((q0, k0, v0), (q1, k1, v1), (q2, k2, v2))
    for g, (qr, kr, vr) in enumerate(groups):
        dil = DILATIONS[g]
        cls_len = s_len // dil
        kw = min(ATTN_K_WINDOW, cls_len)
        nb = cls_len // qb
        nb_shift = nb.bit_length() - 1

        neg_slope = jnp.full((1, kw), slopes_ref[g * HEADS_PER_BLOCK + h], F32) * (-float(dil))
        base_delta = (lax.broadcasted_iota(I32, (qb, kw), 1) - lax.broadcasted_iota(I32, (qb, kw), 0))
        for case in range(kw // N_SIDE):
            dist = jnp.abs(base_delta - case * N_SIDE)
            bias_scr[g, case, :, 0:kw] = jnp.where(dist <= N_SIDE, dist.astype(F32) * neg_slope, NEG_INF)

        def body(it, carry, qr=qr, kr=kr, vr=vr, dil=dil, cls_len=cls_len, kw=kw, nb=nb, g=g,
                 nb_shift=nb_shift):
            blocks = []
            for j in range(ATTN_BATCH):
                idx = it * ATTN_BATCH + j
                r = lax.shift_right_logical(idx, nb_shift)
                i = lax.bitwise_and(idx, nb - 1)
                qs = pl.multiple_of(i * qb, qb)
                ks = pl.multiple_of(jnp.clip(qs - N_SIDE, 0, cls_len - kw), N_SIDE)
                blocks.append((r, qs, ks))
            scores = []
            for r, qs, ks in blocks:
                q = qr[r, pl.ds(qs, qb), :]
                k = kr[r, pl.ds(ks, kw), :]
                scores.append(lax.dot_general(q, k, (((1,), (1,)), ((), ())), preferred_element_type=F32))
            probs = []
            for (r, qs, ks), s in zip(blocks, scores):
                bias = bias_scr[g, lax.shift_right_logical(qs - ks, N_SIDE.bit_length() - 1), :, 0:kw]
                s = s * scale + bias
                m = jnp.max(s, axis=-1, keepdims=True)
                p = jnp.exp(s - m)
                l = jnp.sum(p, axis=-1, keepdims=True)
                probs.append((p.astype(BF16), m, l))
            for (r, qs, ks), (p, m, l) in zip(blocks, probs):
                v = vr[r, pl.ds(ks, kw), :]
                o = jnp.dot(p, v, preferred_element_type=F32) / l
                lse = jnp.broadcast_to(m + jnp.log(l), (qb, LANES))
                row0 = qs * dil + r
                if dil == 1:
                    o_scr[g, pl.ds(row0, qb), :] = o
                    l_scr[g, pl.ds(row0, qb), :] = lse
                else:
                    o_scr.at[g][pl.ds(row0, qb, stride=dil), :] = o
                    l_scr.at[g][pl.ds(row0, qb, stride=dil), :] = lse
            return carry

        lax.fori_loop(0, dil * nb // ATTN_BATCH, body, 0)

    chunk = 256
    for c in range(s_len // chunk):
        rows = pl.ds(c * chunk, chunk)
        l0, l1, l2 = l_scr[0, rows, :], l_scr[1, rows, :], l_scr[2, rows, :]
        m = jnp.maximum(jnp.maximum(l0, l1), l2)
        e0, e1, e2 = jnp.exp(l0 - m), jnp.exp(l1 - m), jnp.exp(l2 - m)
        num = e0 * o_scr[0, rows, :] + e1 * o_scr[1, rows, :] + e2 * o_scr[2, rows, :]
        o_ref[rows, :] = (num / (e0 + e1 + e2)).astype(o_ref.dtype)


def _dilated_attention(slopes, qkv):
    b_sz = qkv[0].shape[0]
    s_len = qkv[0].shape[2] * qkv[0].shape[3]
    in_specs = [pl.BlockSpec(memory_space=pltpu.SMEM)]
    args = [slopes]
    for g, arr in enumerate(qkv):
        dil = DILATIONS[g]
        for which in range(3):
            in_specs.append(pl.BlockSpec((None, None, dil, s_len // dil, HEAD_DIM),
                                         lambda b, h, which=which: (b, which * HEADS_PER_BLOCK + h, 0, 0, 0)))
            args.append(arr)
    return pl.pallas_call(
        _dil_attn_kernel,
        out_shape=jax.ShapeDtypeStruct((b_sz, s_len, HEADS_PER_BLOCK * HEAD_DIM), BF16),
        grid=(b_sz, HEADS_PER_BLOCK),
        in_specs=in_specs,
        out_specs=pl.BlockSpec((None, s_len, HEAD_DIM), lambda b, h: (b, 0, h)),
        scratch_shapes=[pltpu.VMEM((3, s_len, HEAD_DIM), F32), pltpu.VMEM((3, s_len, LANES), F32),
                        pltpu.VMEM((3, ATTN_K_WINDOW // N_SIDE, ATTN_Q_BLOCK, ATTN_K_WINDOW), F32)],
        compiler_params=_params("arbitrary", "arbitrary"),
        name="dilated_attention",
    )(*args)


def _mem_attn_kernel(q_ref, k_ref, v_ref, o_ref):
    s_len = q_ref.shape[0]
    scale = HEAD_DIM ** -0.5
    k = k_ref[...]
    v = v_ref[...]
    chunk = 256
    row_chunks = [pl.ds(c * chunk, chunk) for c in range(s_len // chunk)]
    scores = [lax.dot_general(q_ref[rows, :], k, (((1,), (1,)), ((), ())), preferred_element_type=F32)
              for rows in row_chunks]
    probs = []
    for s in scores:
        s = s * scale
        m = jnp.max(s, axis=-1, keepdims=True)
        p = jnp.exp(s - m)
        probs.append((p.astype(BF16), jnp.sum(p, axis=-1, keepdims=True)))
    for rows, (p, l) in zip(row_chunks, probs):
        o_ref[rows, :] = (jnp.dot(p, v, preferred_element_type=F32) / l).astype(o_ref.dtype)


def _memory_attention(q_heads, q_head0, kv_heads):
    b_sz, _, _, s_len, _ = q_heads.shape
    n_mem = kv_heads.shape[3]
    return pl.pallas_call(
        _mem_attn_kernel,
        out_shape=jax.ShapeDtypeStruct((b_sz, s_len, HEADS_PER_BLOCK * HEAD_DIM), BF16),
        grid=(b_sz, HEADS_PER_BLOCK),
        in_specs=[
            pl.BlockSpec((None, None, None, s_len, HEAD_DIM), lambda b, h: (b, q_head0 + h, 0, 0, 0)),
            pl.BlockSpec((None, None, None, n_mem, HEAD_DIM), lambda b, h: (b, h, 0, 0, 0)),
            pl.BlockSpec((None, None, None, n_mem, HEAD_DIM), lambda b, h: (b, HEADS_PER_BLOCK + h, 0, 0, 0)),
        ],
        out_specs=pl.BlockSpec((None, s_len, HEAD_DIM), lambda b, h: (b, 0, h)),
        compiler_params=_params("arbitrary", "arbitrary"),
        name="memory_attention",
    )(q_heads, kv_heads, kv_heads)


def _out_ln_kernel(a1_ref, a2_ref, w_ref, res_ref, g_ref, b_ref, o_ref):
    k1 = a1_ref.shape[1]
    y = jnp.dot(a1_ref[...], w_ref[0:k1, :], preferred_element_type=F32)
    y = y + jnp.dot(a2_ref[...], w_ref[k1:, :], preferred_element_type=F32)
    _store_tiled_rows(o_ref, _layer_norm(ALPHA * res_ref[...] + y, g_ref[...], b_ref[...]))


def _out_proj_ln(a1, a2, w, res, gain, bias, *, tm=512):
    t_len, d_model = res.shape
    assert d_model == SUBLANES * LANES
    k1, k2 = a1.shape[1], a2.shape[1]
    return pl.pallas_call(
        _out_ln_kernel,
        out_shape=jax.ShapeDtypeStruct((t_len * SUBLANES, LANES), F32),
        grid=(t_len // tm,),
        in_specs=[
            pl.BlockSpec((tm, k1), lambda i: (i, 0)),
            pl.BlockSpec((tm, k2), lambda i: (i, 0)),
            pl.BlockSpec((k1 + k2, d_model), lambda i: (0, 0), pipeline_mode=pl.Buffered(1)),
            pl.BlockSpec((tm, d_model), lambda i: (i, 0)),
            pl.BlockSpec((1, d_model), lambda i: (0, 0)),
            pl.BlockSpec((1, d_model), lambda i: (0, 0)),
        ],
        out_specs=pl.BlockSpec((tm * SUBLANES, LANES), lambda i: (i, 0)),
        compiler_params=_params("arbitrary"),
        name="out_proj_ln",
    )(a1, a2, w, res, gain.reshape(1, -1), bias.reshape(1, -1))


def _swiglu_chunk(x, wg, wu, wd):
    hg = jnp.dot(x, wg, preferred_element_type=F32)
    hu = jnp.dot(x, wu, preferred_element_type=F32)
    h = hg * (1.0 / (1.0 + jnp.exp(-hg))) * hu
    return jnp.dot(h.astype(BF16), wd, preferred_element_type=F32)


def _mixer_ffn_kernel(a1_ref, a2_ref, wo_ref, res_ref, g1_ref, b1_ref, wg_ref, wu_ref, wd_ref, g2_ref, b2_ref,
                      o32_ref, o16_ref, *, tf):
    k1 = a1_ref.shape[1]
    y = jnp.dot(a1_ref[...], wo_ref[0:k1, :], preferred_element_type=F32)
    y = y + jnp.dot(a2_ref[...], wo_ref[k1:, :], preferred_element_type=F32)
    x1 = _layer_norm(ALPHA * res_ref[...] + y, g1_ref[...], b1_ref[...])
    x16 = x1.astype(BF16)
    acc = None
    for f in range(wg_ref.shape[1] // tf):
        cols = pl.ds(f * tf, tf)
        part = _swiglu_chunk(x16, wg_ref[:, cols], wu_ref[:, cols], wd_ref[cols, :])
        acc = part if acc is None else acc + part
    z = _layer_norm(ALPHA * x1 + acc, g2_ref[...], b2_ref[...])
    o32_ref[...] = z
    o16_ref[...] = z.astype(BF16)


def _mixer_ffn_ln(a1, a2, wo, res, g1, b1, wg, wu, wd, g2, b2, *, tm=512, tf=256):
    t_len, d_model = res.shape
    k1, k2 = a1.shape[1], a2.shape[1]
    d_ff = wg.shape[1]
    resident = dict(pipeline_mode=pl.Buffered(1))
    vec = pl.BlockSpec((1, d_model), lambda i: (0, 0))
    return pl.pallas_call(
        functools.partial(_mixer_ffn_kernel, tf=tf),
        out_shape=(jax.ShapeDtypeStruct((t_len, d_model), F32), jax.ShapeDtypeStruct((t_len, d_model), BF16)),
        grid=(t_len // tm,),
        in_specs=[
            pl.BlockSpec((tm, k1), lambda i: (i, 0)),
            pl.BlockSpec((tm, k2), lambda i: (i, 0)),
            pl.BlockSpec((k1 + k2, d_model), lambda i: (0, 0), **resident),
            pl.BlockSpec((tm, d_model), lambda i: (i, 0)),
            vec, vec,
            pl.BlockSpec((d_model, d_ff), lambda i: (0, 0), **resident),
            pl.BlockSpec((d_model, d_ff), lambda i: (0, 0), **resident),
            pl.BlockSpec((d_ff, d_model), lambda i: (0, 0), **resident),
            vec, vec,
        ],
        out_specs=(pl.BlockSpec((tm, d_model), lambda i: (i, 0)),
                   pl.BlockSpec((tm, d_model), lambda i: (i, 0))),
        compiler_params=_params("arbitrary"),
        name="mixer_ffn_ln",
    )(a1, a2, wo, res, g1.reshape(1, -1), b1.reshape(1, -1), wg, wu, wd, g2.reshape(1, -1), b2.reshape(1, -1))


def _dft_constants(s_len):
    n = np.arange(s_len, dtype=np.int64)
    ang_s = 2.0 * np.pi * ((n[:, None] * n[None, :]) % s_len) / s_len
    c = np.arange(HEAD_DIM, dtype=np.int64)
    ang_c = 2.0 * np.pi * ((c[:, None] * c[None, :]) % HEAD_DIM) / HEAD_DIM
    norm = 1.0 / np.sqrt(float(s_len * HEAD_DIM))
    cs = np.concatenate([np.cos(ang_s), -np.sin(ang_s)], axis=1) * norm
    return tuple(jnp.asarray(a, dtype=F32).astype(BF16) for a in (cs, np.cos(ang_c), np.sin(ang_c)))


def _fourier_kernel(u_ref, cs_ref, cc_ref, sc_ref, o_ref, ab_scr):
    n_grp, s_len, _ = u_ref.shape
    cc = cc_ref[...]
    sc = sc_ref[...]
    for g in range(n_grp):
        u = u_ref[g]
        cols = pl.ds(g * HEAD_DIM, HEAD_DIM)
        ab_scr[0:s_len, cols] = jnp.dot(u, cc, preferred_element_type=F32).astype(BF16)
        ab_scr[s_len:, cols] = jnp.dot(u, sc, preferred_element_type=F32).astype(BF16)
    chunk = 512
    for c in range(s_len // chunk):
        rows = pl.ds(c * chunk, chunk)
        o_ref[rows, :] = jnp.dot(cs_ref[rows, :], ab_scr[...], preferred_element_type=F32).astype(o_ref.dtype)


def _fourier_mix(u_heads):
    b_sz, _, _, s_len, _ = u_heads.shape
    n_grp = HEADS_PER_BLOCK
    cs, cc, sc = _dft_constants(s_len)
    const = dict(pipeline_mode=pl.Buffered(1))
    return pl.pallas_call(
        _fourier_kernel,
        out_shape=jax.ShapeDtypeStruct((b_sz, s_len, n_grp * HEAD_DIM), BF16),
        grid=(b_sz,),
        in_specs=[
            pl.BlockSpec((None, n_grp, None, s_len, HEAD_DIM), lambda b: (b, 0, 0, 0, 0)),
            pl.BlockSpec((s_len, 2 * s_len), lambda b: (0, 0), **const),
            pl.BlockSpec((HEAD_DIM, HEAD_DIM), lambda b: (0, 0), **const),
            pl.BlockSpec((HEAD_DIM, HEAD_DIM), lambda b: (0, 0), **const),
        ],
        out_specs=pl.BlockSpec((None, s_len, n_grp * HEAD_DIM), lambda b: (b, 0, 0)),
        scratch_shapes=[pltpu.VMEM((2 * s_len, n_grp * HEAD_DIM), BF16)],
        compiler_params=_params("arbitrary"),
        name="fourier_mix",
    )(u_heads, cs, cc, sc)


def _split_bf16(a):
    hi = a.astype(BF16)
    lo = (a - hi.astype(F32)).astype(BF16)
    return hi, lo


def _router_kernel(x_ref, w_ref, b_ref, route_ref, cnt_ref, carry_ref):
    tq = route_ref.shape[0]

    @pl.when(pl.program_id(0) == 0)
    def _():
        carry_ref[...] = jnp.zeros_like(carry_ref)

    x_hi, x_lo = _split_bf16(_load_tiled_rows(x_ref))
    w_hi, w_lo = _split_bf16(w_ref[...])
    logits = (jnp.dot(x_hi, w_hi, preferred_element_type=F32)
              + jnp.dot(x_lo, w_hi, preferred_element_type=F32)
              + jnp.dot(x_hi, w_lo, preferred_element_type=F32)) + b_ref[...]
    lane = lax.broadcasted_iota(I32, (tq, LANES), 1).astype(F32)
    logits = jnp.where(lane < N_EXPERTS, logits, NEG_INF)

    def top1(vals):
        m = jnp.max(vals, axis=-1, keepdims=True)
        idx = jnp.min(jnp.where(vals == m, lane, float(LANES)), axis=-1, keepdims=True)
        return m, idx

    m1, i1 = top1(logits)
    oh1 = lane == i1
    m2, i2 = top1(jnp.where(oh1, NEG_INF, logits))
    oh2 = lane == i2
    e2 = jnp.exp(m2 - m1)
    gate1 = 1.0 / (1.0 + e2)
    gate2 = e2 / (1.0 + e2)

    oh1f = oh1.astype(F32)
    oh2f = oh2.astype(F32)
    both = oh1f + oh2f
    tri = (lax.broadcasted_iota(I32, (tq, tq), 0) > lax.broadcasted_iota(I32, (tq, tq), 1)).astype(BF16)
    before = jnp.dot(tri, both.astype(BF16), preferred_element_type=F32) + carry_ref[0:1, :]
    rank1 = jnp.sum(oh1f * before, axis=-1, keepdims=True)
    rank2 = jnp.sum(oh2f * before, axis=-1, keepdims=True)
    carry_ref[...] = carry_ref[...] + jnp.sum(both, axis=0, keepdims=True)

    rec = jnp.zeros((tq, LANES), F32)
    for lane_idx, val in ((ROUTE_E, i1), (ROUTE_E + 1, i2),
                          (ROUTE_RANK, rank1), (ROUTE_RANK + 1, rank2),
                          (ROUTE_GATE, gate1), (ROUTE_GATE + 1, gate2)):
        rec = jnp.where(lane == lane_idx, val, rec)
    route_ref[...] = rec
    cnt_ref[...] = carry_ref[...]


def _router(x_tiled, w, b, *, tq=512):
    t_len = x_tiled.shape[0] // SUBLANES
    d_model = SUBLANES * LANES
    w_pad = jnp.zeros((d_model, LANES), F32).at[:, :N_EXPERTS].set(w)
    b_pad = jnp.zeros((1, LANES), F32).at[0, :N_EXPERTS].set(b)
    return pl.pallas_call(
        _router_kernel,
        out_shape=(jax.ShapeDtypeStruct((t_len, LANES), F32), jax.ShapeDtypeStruct((8, LANES), F32)),
        grid=(t_len // tq,),
        in_specs=[
            pl.BlockSpec((tq * SUBLANES, LANES), lambda i: (i, 0)),
            pl.BlockSpec((d_model, LANES), lambda i: (0, 0)),
            pl.BlockSpec((1, LANES), lambda i: (0, 0)),
        ],
        out_specs=(pl.BlockSpec((tq, LANES), lambda i: (i, 0)), pl.BlockSpec((8, LANES), lambda i: (0, 0))),
        scratch_shapes=[pltpu.VMEM((8, LANES), F32)],
        compiler_params=_params("arbitrary"),
        name="router",
    )(x_tiled, w_pad, b_pad)


def _store_tiled_rows(dst_ref, val):
    n_rows, width = val.shape
    for j in range(width // LANES):
        dst_ref[pl.ds(j, n_rows, stride=width // LANES), :] = val[:, j * LANES:(j + 1) * LANES]


def _load_tiled_rows(src_ref):
    n_rows = src_ref.shape[0] // SUBLANES
    return jnp.concatenate([src_ref[pl.ds(j, n_rows, stride=SUBLANES), :] for j in range(SUBLANES)], axis=1)


def _tiled_row(ref, row):
    return ref.at[pl.ds(pl.multiple_of(row * SUBLANES, SUBLANES), SUBLANES)]


def _dispatch_kernel(fill_ref, pos_ref, x_ref, xs_hbm, zero_scr, sem, *, tm):
    tq = x_ref.shape[0] // SUBLANES

    @pl.when(pl.program_id(0) == 0)
    def _():
        zero_scr[...] = jnp.zeros_like(zero_scr)

        def fill_copies(tile):
            copies = []
            for piece in range(tm // tq):
                row0 = pl.multiple_of((tile * tm + piece * tq) * SUBLANES, SUBLANES)
                copies.append(pltpu.make_async_copy(zero_scr, xs_hbm.at[pl.ds(row0, tq * SUBLANES)], sem))
            return copies

        def start_fill(tile, carry):
            @pl.when(fill_ref[tile] == 1)
            def _():
                for copy in fill_copies(tile):
                    copy.start()
            return carry

        def wait_fill(tile, carry):
            @pl.when(fill_ref[tile] == 1)
            def _():
                for copy in fill_copies(tile):
                    copy.wait()
            return carry

        lax.fori_loop(0, fill_ref.shape[0], start_fill, 0)
        lax.fori_loop(0, fill_ref.shape[0], wait_fill, 0)

    def start(t, carry):
        for k in range(TOP_K):
            pltpu.make_async_copy(_tiled_row(x_ref, t), _tiled_row(xs_hbm, pos_ref[TOP_K * t + k]),
                                  sem).start(priority=k)
        return carry

    lax.fori_loop(0, tq, start, 0, unroll=8)
    for k in range(TOP_K):
        pltpu.make_async_copy(x_ref, xs_hbm.at[pl.ds(0, tq * SUBLANES)], sem).wait()


def _dispatch(x_tiled, pos_flat, fill_tiles, *, tm, tq=512):
    t_len = x_tiled.shape[0] // SUBLANES
    assert tm % tq == 0
    grid_spec = pltpu.PrefetchScalarGridSpec(
        num_scalar_prefetch=1,
        grid=(t_len // tq,),
        in_specs=[
            pl.BlockSpec((TOP_K * tq,), lambda i, fill: (i,), memory_space=pltpu.SMEM),
            pl.BlockSpec((tq * SUBLANES, LANES), lambda i, fill: (i, 0)),
        ],
        out_specs=pl.BlockSpec(memory_space=pl.ANY),
        scratch_shapes=[pltpu.VMEM((tq * SUBLANES, LANES), x_tiled.dtype), pltpu.SemaphoreType.DMA],
    )
    return pl.pallas_call(
        functools.partial(_dispatch_kernel, tm=tm),
        out_shape=jax.ShapeDtypeStruct((fill_tiles.shape[0] * tm * SUBLANES, LANES), x_tiled.dtype),
        grid_spec=grid_spec,
        compiler_params=_params("arbitrary"),
        name="moe_dispatch",
    )(fill_tiles, pos_flat, x_tiled)


def _expert_kernel(tile_e_ref, nt_ref, nv_ref, x_ref, wg_ref, wu_ref, wd_ref, o_ref, x16_scr, acc_ref):
    del tile_e_ref, nt_ref
    c = pl.program_id(1)
    n_valid = nv_ref[pl.program_id(0)]
    half = x16_scr.shape[0] // 2

    @pl.when(c == 0)
    def _():
        acc_ref[...] = jnp.zeros_like(acc_ref)
        x16_scr[...] = _load_tiled_rows(x_ref).astype(BF16)

    def accumulate(rows):
        acc_ref[rows, :] += _swiglu_chunk(x16_scr[rows, :], wg_ref[...].astype(BF16), wu_ref[...].astype(BF16),
                                          wd_ref[...].astype(BF16))

    @pl.when(n_valid > half)
    def _():
        accumulate(pl.ds(0, 2 * half))

    @pl.when(jnp.logical_and(n_valid > 0, n_valid <= half))
    def _():
        accumulate(pl.ds(0, half))

    @pl.when(c == pl.num_programs(1) - 1)
    def _():
        _store_tiled_rows(o_ref, acc_ref[...])


def _expert_ffn(xs, tile_e, n_tiles, n_valid, wg, wu, wd, *, tm, tf):
    d_model = wg.shape[1]
    d_ff = wg.shape[2]
    n_chunks = d_ff // tf
    last = n_chunks - 1

    def row_map(i, c, te, nt, nv):
        return (jnp.minimum(i, nt[0] - 1), 0)

    def chunk_of(i, c, nt):
        return jnp.where(i < nt[0], c, last)

    grid_spec = pltpu.PrefetchScalarGridSpec(
        num_scalar_prefetch=3,
        grid=(xs.shape[0] // (tm * SUBLANES), n_chunks),
        in_specs=[
            pl.BlockSpec((tm * SUBLANES, LANES), row_map),
            pl.BlockSpec((None, d_model, tf), lambda i, c, te, nt, nv: (te[i], 0, chunk_of(i, c, nt))),
            pl.BlockSpec((None, d_model, tf), lambda i, c, te, nt, nv: (te[i], 0, chunk_of(i, c, nt))),
            pl.BlockSpec((None, tf, d_model), lambda i, c, te, nt, nv: (te[i], chunk_of(i, c, nt), 0)),
        ],
        out_specs=pl.BlockSpec((tm * SUBLANES, LANES), lambda i, c, te, nt, nv: (i, 0)),
        scratch_shapes=[pltpu.VMEM((tm, d_model), BF16), pltpu.VMEM((tm, d_model), F32)],
    )
    return pl.pallas_call(
        _expert_kernel,
        out_shape=jax.ShapeDtypeStruct(xs.shape, F32),
        grid_spec=grid_spec,
        compiler_params=_params("arbitrary", "arbitrary"),
        name="expert_ffn",
    )(tile_e, n_tiles, n_valid, xs, wg, wu, wd)


def _combine_kernel(pos_ref, pos_next_ref, res_ref, route_ref, g_ref, b_ref, ys_hbm, o_ref, buf_ref, sem):
    i = pl.program_id(0)
    tq = o_ref.shape[0]
    slot = lax.bitwise_and(i, 1)

    def gather(positions, dst_slot):
        def start(t, carry):
            for k in range(TOP_K):
                pltpu.make_async_copy(_tiled_row(ys_hbm, positions[TOP_K * t + k]),
                                      _tiled_row(buf_ref.at[dst_slot, k], t), sem.at[dst_slot]).start(priority=k)
            return carry
        lax.fori_loop(0, tq, start, 0, unroll=8)

    @pl.when(i == 0)
    def _():
        gather(pos_ref, 0)

    @pl.when(i + 1 < pl.num_programs(0))
    def _():
        gather(pos_next_ref, 1 - slot)

    for k in range(TOP_K):
        pltpu.make_async_copy(ys_hbm.at[pl.ds(0, tq * SUBLANES)], buf_ref.at[slot, k], sem.at[slot]).wait()

    route = route_ref[...]
    y = (route[:, ROUTE_GATE:ROUTE_GATE + 1] * _load_tiled_rows(buf_ref.at[slot, 0])
         + route[:, ROUTE_GATE + 1:ROUTE_GATE + 2] * _load_tiled_rows(buf_ref.at[slot, 1]))
    o_ref[...] = _layer_norm(ALPHA * _load_tiled_rows(res_ref) + y, g_ref[...], b_ref[...])


def _combine_ln(ys, pos_flat, res_tiled, route, gain, bias, *, tq=256):
    t_len = res_tiled.shape[0] // SUBLANES
    d_model = SUBLANES * LANES
    n_steps = t_len // tq
    return pl.pallas_call(
        _combine_kernel,
        out_shape=jax.ShapeDtypeStruct((t_len, d_model), F32),
        grid=(n_steps,),
        in_specs=[
            pl.BlockSpec((TOP_K * tq,), lambda i: (i,), memory_space=pltpu.SMEM),
            pl.BlockSpec((TOP_K * tq,), lambda i: (jnp.minimum(i + 1, n_steps - 1),), memory_space=pltpu.SMEM),
            pl.BlockSpec((tq * SUBLANES, LANES), lambda i: (i, 0)),
            pl.BlockSpec((tq, LANES), lambda i: (i, 0)),
            pl.BlockSpec((1, d_model), lambda i: (0, 0)),
            pl.BlockSpec((1, d_model), lambda i: (0, 0)),
            pl.BlockSpec(memory_space=pl.ANY),
        ],
        out_specs=pl.BlockSpec((tq, d_model), lambda i: (i, 0)),
        scratch_shapes=[pltpu.VMEM((2, TOP_K, tq * SUBLANES, LANES), F32), pltpu.SemaphoreType.DMA((2,))],
        compiler_params=_params("arbitrary"),
        name="moe_combine_ln",
    )(pos_flat, pos_flat, res_tiled, route, gain.reshape(1, -1), bias.reshape(1, -1), ys)


def _moe_ffn_ln(x_tiled, router_w, router_b, wg, wu, wd, gain, bias, *, tm=MOE_ROW_TILE, tf=MOE_FF_CHUNK):
    t_len = x_tiled.shape[0] // SUBLANES
    route, counts = _router(x_tiled, router_w, router_b)
    experts = route[:, ROUTE_E:ROUTE_E + TOP_K].astype(I32)
    ranks = route[:, ROUTE_RANK:ROUTE_RANK + TOP_K].astype(I32)
    counts = counts[0, :N_EXPERTS].astype(I32)
    tiles_per_expert = (counts + tm - 1) // tm
    tile_end = jnp.cumsum(tiles_per_expert)
    row_start = (tile_end - tiles_per_expert) * tm
    pos_flat = (row_start[experts] + ranks).reshape(-1)
    n_tiles_max = (TOP_K * t_len) // tm + N_EXPERTS
    n_tiles = tile_end[-1:]
    all_tiles = jnp.arange(n_tiles_max, dtype=I32)
    tile_ids = jnp.minimum(all_tiles, n_tiles[0] - 1)
    tile_e = jnp.sum((tile_ids[:, None] >= tile_end[None, :-1]).astype(I32), axis=1)
    is_last = jnp.any((all_tiles[:, None] == tile_end[None, :] - 1) & (tiles_per_expert[None, :] > 0), axis=1)
    fill_tiles = (is_last | (all_tiles >= n_tiles[0])).astype(I32)

    first_tile = (tile_end - tiles_per_expert)[tile_e]
    n_valid = jnp.where(all_tiles < n_tiles[0], jnp.clip(counts[tile_e] - (all_tiles - first_tile) * tm, 0, tm), 0)

    xs = _dispatch(x_tiled, pos_flat, fill_tiles, tm=tm, tq=MOE_DISPATCH_TILE)
    ys = _expert_ffn(xs, tile_e, n_tiles.astype(I32), n_valid.astype(I32), wg, wu, wd, tm=tm, tf=tf)
    return _combine_ln(ys, pos_flat, x_tiled, route, gain, bias)


def _alibi_slopes():
    exps = np.arange(1, N_ATTN_HEADS + 1, dtype=np.float32) * np.float32(8.0 / N_ATTN_HEADS)
    return jnp.asarray(np.exp2(-exps).astype(np.float32))


def kernel(x, mem, a_w_in, a_w_mem_kv, a_w_out, a_ln1_g, a_ln1_b, a_ffn_gate, a_ffn_up, a_ffn_down, a_ln2_g, a_ln2_b, b_w_in, b_fourier_g, b_w_mem_kv, b_w_out, b_ln1_g, b_ln1_b, b_router_w, b_router_b, b_moe_gate, b_moe_up, b_moe_down, b_ln2_g, b_ln2_b):
    b_sz, s_len, d_model = x.shape
    t_len = b_sz * s_len
    x16 = x.astype(BF16)
    mem16 = mem.astype(BF16)
    x32 = x.reshape(t_len, d_model)
    a_w_in, a_w_mem_kv, a_w_out, a_ffn_gate, a_ffn_up, a_ffn_down, b_w_in, b_w_mem_kv, b_w_out = (
        w[0].astype(BF16) for w in (a_w_in, a_w_mem_kv, a_w_out, a_ffn_gate, a_ffn_up, a_ffn_down,
                                    b_w_in, b_w_mem_kv, b_w_out))

    n_grp = len(DILATIONS)
    qkv = []
    for g, dil in enumerate(DILATIONS):
        blocks = (g, n_grp + g, 2 * n_grp + g)
        qkv.append(_proj_heads(x16, a_w_in, blocks + (3 * n_grp,) if dil == 1 else blocks, dil=dil))
    mem_kv = _proj_heads(mem16, a_w_mem_kv, (0, 1))
    attn = _dilated_attention(_alibi_slopes(), qkv).reshape(t_len, -1)
    memo = _memory_attention(qkv[0], 3 * HEADS_PER_BLOCK, mem_kv).reshape(t_len, -1)
    x32, x16 = _mixer_ffn_ln(attn, memo, a_w_out, x32, a_ln1_g[0], a_ln1_b[0],
                             a_ffn_gate, a_ffn_up, a_ffn_down, a_ln2_g[0], a_ln2_b[0])

    x16_b = x16.reshape(b_sz, s_len, d_model)
    u_q = _proj_heads(x16_b, b_w_in, (0, 1), gain=b_fourier_g[0], ln_blocks=(True, False))
    mem_kv = _proj_heads(mem16, b_w_mem_kv, (0, 1))
    four = _fourier_mix(u_q).reshape(t_len, -1)
    memo = _memory_attention(u_q, HEADS_PER_BLOCK, mem_kv).reshape(t_len, -1)
    x_tiled = _out_proj_ln(four, memo, b_w_out, x32, b_ln1_g[0], b_ln1_b[0])
    out = _moe_ffn_ln(x_tiled, b_router_w[0], b_router_b[0], b_moe_gate[0], b_moe_up[0], b_moe_down[0],
                      b_ln2_g[0], b_ln2_b[0])
    return out.reshape(b_sz, s_len, d_model)
```

```python
import functools

import numpy as np
import jax
import jax.numpy as jnp
from jax import lax
from jax.experimental import pallas as pl
from jax.experimental.pallas import tpu as pltpu

F32 = jnp.float32
BF16 = jnp.bfloat16
I32 = jnp.int32

LANES = 128
HEAD_DIM = 128
HEADS_PER_BLOCK = 4
DILATIONS = (1, 4, 16)
N_SIDE = 64
ATTN_Q_BLOCK = 64
ATTN_K_WINDOW = 256
ATTN_BATCH = 32
PROJ_MAX_ROW_STRIDE = 4
N_ATTN_HEADS = 12
N_EXPERTS = 8
TOP_K = 2
MOE_ROW_TILE = 1024
MOE_FF_CHUNK = 512
MOE_DISPATCH_TILE = 512
SUBLANES = 8
ALPHA = (2.0 * 2) ** 0.25
LN_EPS = 1e-5
NEG_INF = -1e30
VMEM_LIMIT_BYTES = 56 * 1024 * 1024

ROUTE_E, ROUTE_RANK, ROUTE_GATE = 0, 2, 4


def _params(*sem):
    return pltpu.CompilerParams(dimension_semantics=sem, vmem_limit_bytes=VMEM_LIMIT_BYTES)


def _layer_norm(z, g, b=None):
    mu = jnp.mean(z, axis=-1, keepdims=True)
    zc = z - mu
    var = jnp.mean(zc * zc, axis=-1, keepdims=True)
    y = zc * lax.rsqrt(var + LN_EPS) * g
    return y if b is None else y + b


def _proj_kernel(*refs, dil, row_chunk, ln_blocks, cast_x):
    n_blocks = len(ln_blocks)
    x_ref, w_refs = refs[0], refs[1:1 + n_blocks]
    rest = refs[1 + n_blocks:]
    g_ref = None
    if any(ln_blocks):
        g_ref, rest = rest[0], rest[1:]
    o_ref, scratch = rest[0], rest[1:]
    if cast_x:
        x16_ref, scratch = scratch[0], scratch[1:]
        x16_ref[...] = x_ref[...].astype(BF16)
        x_ref = x16_ref
    s_len = x_ref.shape[0]
    for blk in range(n_blocks):
        w = w_refs[blk][...]
        acc_ref = scratch[blk % 2] if dil > 1 else None
        for rc in range(s_len // row_chunk):
            rows = pl.ds(rc * row_chunk, row_chunk)
            r = jnp.dot(x_ref[rows, :], w, preferred_element_type=F32)
            for hh in range(HEADS_PER_BLOCK):
                rh = r[:, hh * HEAD_DIM:(hh + 1) * HEAD_DIM]
                if ln_blocks[blk]:
                    rh = _layer_norm(rh, g_ref[hh:hh + 1, :])
                if dil == 1:
                    o_ref[blk * HEADS_PER_BLOCK + hh, 0, rows, :] = rh.astype(o_ref.dtype)
                else:
                    acc_ref[hh, rows, :] = rh
        if dil == 1:
            continue
        for hh in range(HEADS_PER_BLOCK):
            head = blk * HEADS_PER_BLOCK + hh
            if dil <= PROJ_MAX_ROW_STRIDE:
                for r_ in range(dil):
                    o_ref[head, r_, :, :] = acc_ref.at[hh][pl.ds(r_, s_len // dil, stride=dil), :].astype(o_ref.dtype)
            else:
                s1, s2 = PROJ_MAX_ROW_STRIDE, dil // PROJ_MAX_ROW_STRIDE
                tmp_ref = scratch[2]
                for a in range(s1):
                    tmp_ref[a] = acc_ref.at[hh][pl.ds(a, s_len // s1, stride=s1), :]
                for a in range(s1):
                    for b in range(s2):
                        o_ref[head, s1 * b + a, :, :] = tmp_ref.at[a][pl.ds(b, s_len // dil, stride=s2), :].astype(
                            o_ref.dtype)


def _proj_heads(x, w, col_blocks, *, dil=1, gain=None, ln_blocks=None):
    b_sz, s_len, d_model = x.shape
    cast_x = x.dtype != BF16
    blk_cols = HEADS_PER_BLOCK * HEAD_DIM
    n_blocks = len(col_blocks)
    ln_blocks = tuple(ln_blocks) if ln_blocks is not None else (False,) * n_blocks
    in_specs = [pl.BlockSpec((None, s_len, d_model), lambda b: (b, 0, 0))]
    in_specs += [pl.BlockSpec((d_model, blk_cols), lambda b, cb=cb: (0, cb), pipeline_mode=pl.Buffered(1))
                 for cb in col_blocks]
    args = [x] + [w] * n_blocks
    if any(ln_blocks):
        in_specs.append(pl.BlockSpec((HEADS_PER_BLOCK, HEAD_DIM), lambda b: (0, 0)))
        args.append(gain)
    scratch = []
    if dil > 1:
        scratch += [pltpu.VMEM((HEADS_PER_BLOCK, s_len, HEAD_DIM), F32)] * 2
    if dil > PROJ_MAX_ROW_STRIDE:
        scratch.append(pltpu.VMEM((PROJ_MAX_ROW_STRIDE, s_len // PROJ_MAX_ROW_STRIDE, HEAD_DIM), F32))
    n_heads = n_blocks * HEADS_PER_BLOCK
    out_shape = jax.ShapeDtypeStruct((b_sz, n_heads, dil, s_len // dil, HEAD_DIM), BF16)
    out_specs = pl.BlockSpec((None, n_heads, dil, s_len // dil, HEAD_DIM), lambda b: (b, 0, 0, 0, 0))
    if cast_x:
        out_shape = (out_shape, jax.ShapeDtypeStruct(x.shape, BF16))
        out_specs = (out_specs, pl.BlockSpec((None, s_len, d_model), lambda b: (b, 0, 0)))
    return pl.pallas_call(
        functools.partial(_proj_kernel, dil=dil, row_chunk=min(512, s_len), ln_blocks=ln_blocks, cast_x=cast_x),
        out_shape=out_shape,
        grid=(b_sz,),
        in_specs=in_specs,
        out_specs=out_specs,
        scratch_shapes=scratch,
        compiler_params=_params("arbitrary"),
        name=f"proj_heads_d{dil}",
    )(*args)


def _dil_attn_kernel(slopes_ref, q0, k0, v0, q1, k1, v1, q2, k2, v2, o_ref, o_scr, l_scr, bias_scr):
    s_len = o_ref.shape[0]
    h = pl.program_id(1)
    scale = HEAD_DIM ** -0.5
    qb = ATTN_Q_BLOCK
    groups = ((q0, k0, v0), (q1, k1, v1), (q2, k2, v2))
    for g, (qr, kr, vr) in enumerate(groups):
        dil = DILATIONS[g]
        cls_len = s_len // dil
        kw = min(ATTN_K_WINDOW, cls_len)
        nb = cls_len // qb
        nb_shift = nb.bit_length() - 1

        neg_slope = jnp.full((1, kw), slopes_ref[g * HEADS_PER_BLOCK + h], F32) * (-float(dil))
        base_delta = (lax.broadcasted_iota(I32, (qb, kw), 1) - lax.broadcasted_iota(I32, (qb, kw), 0))
        for case in range(kw // N_SIDE):
            dist = jnp.abs(base_delta - case * N_SIDE)
            bias_scr[g, case, :, 0:kw] = jnp.where(dist <= N_SIDE, dist.astype(F32) * neg_slope, NEG_INF)

        def body(it, carry, qr=qr, kr=kr, vr=vr, dil=dil, cls_len=cls_len, kw=kw, nb=nb, g=g,
                 nb_shift=nb_shift):
            blocks = []
            for j in range(ATTN_BATCH):
                idx = it * ATTN_BATCH + j
                r = lax.shift_right_logical(idx, nb_shift)
                i = lax.bitwise_and(idx, nb - 1)
                qs = pl.multiple_of(i * qb, qb)
                ks = pl.multiple_of(jnp.clip(qs - N_SIDE, 0, cls_len - kw), N_SIDE)
                blocks.append((r, qs, ks))
            scores = []
            for r, qs, ks in blocks:
                q = qr[r, pl.ds(qs, qb), :]
                k = kr[r, pl.ds(ks, kw), :]
                scores.append(lax.dot_general(q, k, (((1,), (1,)), ((), ())), preferred_element_type=F32))
            probs = []
            for (r, qs, ks), s in zip(blocks, scores):
                bias = bias_scr[g, lax.shift_right_logical(qs - ks, N_SIDE.bit_length() - 1), :, 0:kw]
                s = s * scale + bias
                m = jnp.max(s, axis=-1, keepdims=True)
                p = jnp.exp(s - m)
                l = jnp.sum(p, axis=-1, keepdims=True)
                probs.append((p.astype(BF16), m, l))
            for (r, qs, ks), (p, m, l) in zip(blocks, probs):
                v = vr[r, pl.ds(ks, kw), :]
                o = jnp.dot(p, v, preferred_element_type=F32) / l
                lse = jnp.broadcast_to(m + jnp.log(l), (qb, LANES))
                row0 = qs * dil + r
                if dil == 1:
                    o_scr[g, pl.ds(row0, qb), :] = o
                    l_scr[g, pl.ds(row0, qb), :] = lse
                else:
                    o_scr.at[g][pl.ds(row0, qb, stride=dil), :] = o
                    l_scr.at[g][pl.ds(row0, qb, stride=dil), :] = lse
            return carry

        lax.fori_loop(0, dil * nb // ATTN_BATCH, body, 0)

    chunk = 256
    for c in range(s_len // chunk):
        rows = pl.ds(c * chunk, chunk)
        l0, l1, l2 = l_scr[0, rows, :], l_scr[1, rows, :], l_scr[2, rows, :]
        m = jnp.maximum(jnp.maximum(l0, l1), l2)
        e0, e1, e2 = jnp.exp(l0 - m), jnp.exp(l1 - m), jnp.exp(l2 - m)
        num = e0 * o_scr[0, rows, :] + e1 * o_scr[1, rows, :] + e2 * o_scr[2, rows, :]
        o_ref[rows, :] = (num / (e0 + e1 + e2)).astype(o_ref.dtype)


def _dilated_attention(slopes, qkv):
    b_sz = qkv[0].shape[0]
    s_len = qkv[0].shape[2] * qkv[0].shape[3]
    in_specs = [pl.BlockSpec(memory_space=pltpu.SMEM)]
    args = [slopes]
    for g, arr in enumerate(qkv):
        dil = DILATIONS[g]
        for which in range(3):
            in_specs.append(pl.BlockSpec((None, None, dil, s_len // dil, HEAD_DIM),
                                         lambda b, h, which=which: (b, which * HEADS_PER_BLOCK + h, 0, 0, 0)))
            args.append(arr)
    return pl.pallas_call(
        _dil_attn_kernel,
        out_shape=jax.ShapeDtypeStruct((b_sz, s_len, HEADS_PER_BLOCK * HEAD_DIM), BF16),
        grid=(b_sz, HEADS_PER_BLOCK),
        in_specs=in_specs,
        out_specs=pl.BlockSpec((None, s_len, HEAD_DIM), lambda b, h: (b, 0, h)),
        scratch_shapes=[pltpu.VMEM((3, s_len, HEAD_DIM), F32), pltpu.VMEM((3, s_len, LANES), F32),
                        pltpu.VMEM((3, ATTN_K_WINDOW // N_SIDE, ATTN_Q_BLOCK, ATTN_K_WINDOW), F32)],
        compiler_params=_params("arbitrary", "arbitrary"),
        name="dilated_attention",
    )(*args)


def _mem_attn_kernel(q_ref, k_ref, v_ref, o_ref):
    s_len = q_ref.shape[0]
    scale = HEAD_DIM ** -0.5
    k = k_ref[...]
    v = v_ref[...]
    chunk = 256
    row_chunks = [pl.ds(c * chunk, chunk) for c in range(s_len // chunk)]
    scores = [lax.dot_general(q_ref[rows, :], k, (((1,), (1,)), ((), ())), preferred_element_type=F32)
              for rows in row_chunks]
    probs = []
    for s in scores:
        s = s * scale
        m = jnp.max(s, axis=-1, keepdims=True)
        p = jnp.exp(s - m)
        probs.append((p.astype(BF16), jnp.sum(p, axis=-1, keepdims=True)))
    for rows, (p, l) in zip(row_chunks, probs):
        o_ref[rows, :] = (jnp.dot(p, v, preferred_element_type=F32) / l).astype(o_ref.dtype)


def _memory_attention(q_heads, q_head0, kv_heads):
    b_sz, _, _, s_len, _ = q_heads.shape
    n_mem = kv_heads.shape[3]
    return pl.pallas_call(
        _mem_attn_kernel,
        out_shape=jax.ShapeDtypeStruct((b_sz, s_len, HEADS_PER_BLOCK * HEAD_DIM), BF16),
        grid=(b_sz, HEADS_PER_BLOCK),
        in_specs=[
            pl.BlockSpec((None, None, None, s_len, HEAD_DIM), lambda b, h: (b, q_head0 + h, 0, 0, 0)),
            pl.BlockSpec((None, None, None, n_mem, HEAD_DIM), lambda b, h: (b, h, 0, 0, 0)),
            pl.BlockSpec((None, None, None, n_mem, HEAD_DIM), lambda b, h: (b, HEADS_PER_BLOCK + h, 0, 0, 0)),
        ],
        out_specs=pl.BlockSpec((None, s_len, HEAD_DIM), lambda b, h: (b, 0, h)),
        compiler_params=_params("arbitrary", "arbitrary"),
        name="memory_attention",
    )(q_heads, kv_heads, kv_heads)


def _out_ln_kernel(a1_ref, a2_ref, w_ref, res_ref, g_ref, b_ref, o_ref):
    k1 = a1_ref.shape[1]
    y = jnp.dot(a1_ref[...], w_ref[0:k1, :], preferred_element_type=F32)
    y = y + jnp.dot(a2_ref[...], w_ref[k1:, :], preferred_element_type=F32)
    _store_tiled_rows(o_ref, _layer_norm(ALPHA * res_ref[...] + y, g_ref[...], b_ref[...]))


def _out_proj_ln(a1, a2, w, res, gain, bias, *, tm=512):
    t_len, d_model = res.shape
    assert d_model == SUBLANES * LANES
    k1, k2 = a1.shape[1], a2.shape[1]
    return pl.pallas_call(
        _out_ln_kernel,
        out_shape=jax.ShapeDtypeStruct((t_len * SUBLANES, LANES), F32),
        grid=(t_len // tm,),
        in_specs=[
            pl.BlockSpec((tm, k1), lambda i: (i, 0)),
            pl.BlockSpec((tm, k2), lambda i: (i, 0)),
            pl.BlockSpec((k1 + k2, d_model), lambda i: (0, 0), pipeline_mode=pl.Buffered(1)),
            pl.BlockSpec((tm, d_model), lambda i: (i, 0)),
            pl.BlockSpec((1, d_model), lambda i: (0, 0)),
            pl.BlockSpec((1, d_model), lambda i: (0, 0)),
        ],
        out_specs=pl.BlockSpec((tm * SUBLANES, LANES), lambda i: (i, 0)),
        compiler_params=_params("arbitrary"),
        name="out_proj_ln",
    )(a1, a2, w, res, gain.reshape(1, -1), bias.reshape(1, -1))


def _swiglu_chunk(x, wg, wu, wd):
    hg = jnp.dot(x, wg, preferred_element_type=F32)
    hu = jnp.dot(x, wu, preferred_element_type=F32)
    h = hg * (1.0 / (1.0 + jnp.exp(-hg))) * hu
    return jnp.dot(h.astype(BF16), wd, preferred_element_type=F32)


def _mixer_ffn_kernel(a1_ref, a2_ref, wo_ref, res_ref, g1_ref, b1_ref, wg_ref, wu_ref, wd_ref, g2_ref, b2_ref,
                      o32_ref, o16_ref, *, tf):
    k1 = a1_ref.shape[1]
    y = jnp.dot(a1_ref[...], wo_ref[0:k1, :], preferred_element_type=F32)
    y = y + jnp.dot(a2_ref[...], wo_ref[k1:, :], preferred_element_type=F32)
    x1 = _layer_norm(ALPHA * res_ref[...] + y, g1_ref[...], b1_ref[...])
    x16 = x1.astype(BF16)
    acc = None
    for f in range(wg_ref.shape[1] // tf):
        cols = pl.ds(f * tf, tf)
        part = _swiglu_chunk(x16, wg_ref[:, cols], wu_ref[:, cols], wd_ref[cols, :])
        acc = part if acc is None else acc + part
    z = _layer_norm(ALPHA * x1 + acc, g2_ref[...], b2_ref[...])
    o32_ref[...] = z
    o16_ref[...] = z.astype(BF16)


def _mixer_ffn_ln(a1, a2, wo, res, g1, b1, wg, wu, wd, g2, b2, *, tm=512, tf=256):
    t_len, d_model = res.shape
    k1, k2 = a1.shape[1], a2.shape[1]
    d_ff = wg.shape[1]
    resident = dict(pipeline_mode=pl.Buffered(1))
    vec = pl.BlockSpec((1, d_model), lambda i: (0, 0))
    return pl.pallas_call(
        functools.partial(_mixer_ffn_kernel, tf=tf),
        out_shape=(jax.ShapeDtypeStruct((t_len, d_model), F32), jax.ShapeDtypeStruct((t_len, d_model), BF16)),
        grid=(t_len // tm,),
        in_specs=[
            pl.BlockSpec((tm, k1), lambda i: (i, 0)),
            pl.BlockSpec((tm, k2), lambda i: (i, 0)),
            pl.BlockSpec((k1 + k2, d_model), lambda i: (0, 0), **resident),
            pl.BlockSpec((tm, d_model), lambda i: (i, 0)),
            vec, vec,
            pl.BlockSpec((d_model, d_ff), lambda i: (0, 0), **resident),
            pl.BlockSpec((d_model, d_ff), lambda i: (0, 0), **resident),
            pl.BlockSpec((d_ff, d_model), lambda i: (0, 0), **resident),
            vec, vec,
        ],
        out_specs=(pl.BlockSpec((tm, d_model), lambda i: (i, 0)),
                   pl.BlockSpec((tm, d_model), lambda i: (i, 0))),
        compiler_params=_params("arbitrary"),
        name="mixer_ffn_ln",
    )(a1, a2, wo, res, g1.reshape(1, -1), b1.reshape(1, -1), wg, wu, wd, g2.reshape(1, -1), b2.reshape(1, -1))


def _dft_constants(s_len):
    n = np.arange(s_len, dtype=np.int64)
    ang_s = 2.0 * np.pi * ((n[:, None] * n[None, :]) % s_len) / s_len
    c = np.arange(HEAD_DIM, dtype=np.int64)
    ang_c = 2.0 * np.pi * ((c[:, None] * c[None, :]) % HEAD_DIM) / HEAD_DIM
    norm = 1.0 / np.sqrt(float(s_len * HEAD_DIM))
    cs = np.concatenate([np.cos(ang_s), -np.sin(ang_s)], axis=1) * norm
    return tuple(jnp.asarray(a, dtype=F32).astype(BF16) for a in (cs, np.cos(ang_c), np.sin(ang_c)))


def _fourier_kernel(u_ref, cs_ref, cc_ref, sc_ref, o_ref, ab_scr):
    n_grp, s_len, _ = u_ref.shape
    cc = cc_ref[...]
    sc = sc_ref[...]
    for g in range(n_grp):
        u = u_ref[g]
        cols = pl.ds(g * HEAD_DIM, HEAD_DIM)
        ab_scr[0:s_len, cols] = jnp.dot(u, cc, preferred_element_type=F32).astype(BF16)
        ab_scr[s_len:, cols] = jnp.dot(u, sc, preferred_element_type=F32).astype(BF16)
    chunk = 512
    for c in range(s_len // chunk):
        rows = pl.ds(c * chunk, chunk)
        o_ref[rows, :] = jnp.dot(cs_ref[rows, :], ab_scr[...], preferred_element_type=F32).astype(o_ref.dtype)


def _fourier_mix(u_heads):
    b_sz, _, _, s_len, _ = u_heads.shape
    n_grp = HEADS_PER_BLOCK
    cs, cc, sc = _dft_constants(s_len)
    const = dict(pipeline_mode=pl.Buffered(1))
    return pl.pallas_call(
        _fourier_kernel,
        out_shape=jax.ShapeDtypeStruct((b_sz, s_len, n_grp * HEAD_DIM), BF16),
        grid=(b_sz,),
        in_specs=[
            pl.BlockSpec((None, n_grp, None, s_len, HEAD_DIM), lambda b: (b, 0, 0, 0, 0)),
            pl.BlockSpec((s_len, 2 * s_len), lambda b: (0, 0), **const),
            pl.BlockSpec((HEAD_DIM, HEAD_DIM), lambda b: (0, 0), **const),
            pl.BlockSpec((HEAD_DIM, HEAD_DIM), lambda b: (0, 0), **const),
        ],
        out_specs=pl.BlockSpec((None, s_len, n_grp * HEAD_DIM), lambda b: (b, 0, 0)),
        scratch_shapes=[pltpu.VMEM((2 * s_len, n_grp * HEAD_DIM), BF16)],
        compiler_params=_params("arbitrary"),
        name="fourier_mix",
    )(u_heads, cs, cc, sc)


def _split_bf16(a):
    hi = a.astype(BF16)
    lo = (a - hi.astype(F32)).astype(BF16)
    return hi, lo


def _router_kernel(x_ref, w_ref, b_ref, route_ref, route_t_ref, cnt_ref, carry_ref):
    tq = route_ref.shape[0]

    @pl.when(pl.program_id(0) == 0)
    def _():
        carry_ref[...] = jnp.zeros_like(carry_ref)

    x_hi, x_lo = _split_bf16(_load_tiled_rows(x_ref))
    w_hi, w_lo = _split_bf16(w_ref[...])
    logits = (jnp.dot(x_hi, w_hi, preferred_element_type=F32)
              + jnp.dot(x_lo, w_hi, preferred_element_type=F32)
              + jnp.dot(x_hi, w_lo, preferred_element_type=F32)) + b_ref[...]
    lane = lax.broadcasted_iota(I32, (tq, LANES), 1).astype(F32)
    logits = jnp.where(lane < N_EXPERTS, logits, NEG_INF)

    def top1(vals):
        m = jnp.max(vals, axis=-1, keepdims=True)
        idx = jnp.min(jnp.where(vals == m, lane, float(LANES)), axis=-1, keepdims=True)
        return m, idx

    m1, i1 = top1(logits)
    oh1 = lane == i1
    m2, i2 = top1(jnp.where(oh1, NEG_INF, logits))
    oh2 = lane == i2
    e2 = jnp.exp(m2 - m1)
    gate1 = 1.0 / (1.0 + e2)
    gate2 = e2 / (1.0 + e2)

    oh1f = oh1.astype(F32)
    oh2f = oh2.astype(F32)
    both = oh1f + oh2f
    tri = (lax.broadcasted_iota(I32, (tq, tq), 0) > lax.broadcasted_iota(I32, (tq, tq), 1)).astype(BF16)
    before = jnp.dot(tri, both.astype(BF16), preferred_element_type=F32) + carry_ref[0:1, :]
    rank1 = jnp.sum(oh1f * before, axis=-1, keepdims=True)
    rank2 = jnp.sum(oh2f * before, axis=-1, keepdims=True)
    carry_ref[...] = carry_ref[...] + jnp.sum(both, axis=0, keepdims=True)

    rec = jnp.zeros((tq, LANES), F32)
    for lane_idx, val in ((ROUTE_E, i1), (ROUTE_E + 1, i2),
                          (ROUTE_RANK, rank1), (ROUTE_RANK + 1, rank2),
                          (ROUTE_GATE, gate1), (ROUTE_GATE + 1, gate2)):
        rec = jnp.where(lane == lane_idx, val, rec)
    route_ref[...] = rec
    route_t_ref[...] = rec.T[0:route_t_ref.shape[0], :]
    cnt_ref[...] = carry_ref[...]


def _router(x_tiled, w, b, *, tq=512):
    t_len = x_tiled.shape[0] // SUBLANES
    d_model = SUBLANES * LANES
    w_pad = jnp.zeros((d_model, LANES), F32).at[:, :N_EXPERTS].set(w)
    b_pad = jnp.zeros((1, LANES), F32).at[0, :N_EXPERTS].set(b)
    return pl.pallas_call(
        _router_kernel,
        out_shape=(jax.ShapeDtypeStruct((t_len, LANES), F32), jax.ShapeDtypeStruct((SUBLANES, t_len), F32),
                   jax.ShapeDtypeStruct((8, LANES), F32)),
        grid=(t_len // tq,),
        in_specs=[
            pl.BlockSpec((tq * SUBLANES, LANES), lambda i: (i, 0)),
            pl.BlockSpec((d_model, LANES), lambda i: (0, 0)),
            pl.BlockSpec((1, LANES), lambda i: (0, 0)),
        ],
        out_specs=(pl.BlockSpec((tq, LANES), lambda i: (i, 0)), pl.BlockSpec((SUBLANES, tq), lambda i: (0, i)),
                   pl.BlockSpec((8, LANES), lambda i: (0, 0))),
        scratch_shapes=[pltpu.VMEM((8, LANES), F32)],
        compiler_params=_params("arbitrary"),
        name="router",
    )(x_tiled, w_pad, b_pad)


def _store_tiled_rows(dst_ref, val):
    n_rows, width = val.shape
    for j in range(width // LANES):
        dst_ref[pl.ds(j, n_rows, stride=width // LANES), :] = val[:, j * LANES:(j + 1) * LANES]


def _load_tiled_rows(src_ref):
    n_rows = src_ref.shape[0] // SUBLANES
    return jnp.concatenate([src_ref[pl.ds(j, n_rows, stride=SUBLANES), :] for j in range(SUBLANES)], axis=1)


def _tiled_row(ref, row):
    return ref.at[pl.ds(pl.multiple_of(row * SUBLANES, SUBLANES), SUBLANES)]


def _dispatch_kernel(fill_ref, pos0_ref, pos1_ref, x_ref, xs_hbm, zero_scr, sem, *, tm):
    tq = x_ref.shape[0] // SUBLANES
    pos_refs = (pos0_ref, pos1_ref)

    @pl.when(pl.program_id(0) == 0)
    def _():
        zero_scr[...] = jnp.zeros_like(zero_scr)

        def fill_copies(tile):
            copies = []
            for piece in range(tm // tq):
                row0 = pl.multiple_of((tile * tm + piece * tq) * SUBLANES, SUBLANES)
                copies.append(pltpu.make_async_copy(zero_scr, xs_hbm.at[pl.ds(row0, tq * SUBLANES)], sem))
            return copies

        def start_fill(tile, carry):
            @pl.when(fill_ref[tile] == 1)
            def _():
                for copy in fill_copies(tile):
                    copy.start()
            return carry

        def wait_fill(tile, carry):
            @pl.when(fill_ref[tile] == 1)
            def _():
                for copy in fill_copies(tile):
                    copy.wait()
            return carry

        lax.fori_loop(0, fill_ref.shape[0], start_fill, 0)
        lax.fori_loop(0, fill_ref.shape[0], wait_fill, 0)

    def start(t, carry):
        for k in range(TOP_K):
            pltpu.make_async_copy(_tiled_row(x_ref, t), _tiled_row(xs_hbm, pos_refs[k][t]), sem).start(priority=k)
        return carry

    lax.fori_loop(0, tq, start, 0, unroll=8)
    for k in range(TOP_K):
        pltpu.make_async_copy(x_ref, xs_hbm.at[pl.ds(0, tq * SUBLANES)], sem).wait()


def _dispatch(x_tiled, pos, fill_tiles, *, tm, tq=512):
    t_len = x_tiled.shape[0] // SUBLANES
    assert tm % tq == 0
    grid_spec = pltpu.PrefetchScalarGridSpec(
        num_scalar_prefetch=1,
        grid=(t_len // tq,),
        in_specs=[
            pl.BlockSpec((tq,), lambda i, fill: (i,), memory_space=pltpu.SMEM),
            pl.BlockSpec((tq,), lambda i, fill: (i,), memory_space=pltpu.SMEM),
            pl.BlockSpec((tq * SUBLANES, LANES), lambda i, fill: (i, 0)),
        ],
        out_specs=pl.BlockSpec(memory_space=pl.ANY),
        scratch_shapes=[pltpu.VMEM((tq * SUBLANES, LANES), x_tiled.dtype), pltpu.SemaphoreType.DMA],
    )
    return pl.pallas_call(
        functools.partial(_dispatch_kernel, tm=tm),
        out_shape=jax.ShapeDtypeStruct((fill_tiles.shape[0] * tm * SUBLANES, LANES), x_tiled.dtype),
        grid_spec=grid_spec,
        compiler_params=_params("arbitrary"),
        name="moe_dispatch",
    )(fill_tiles, pos[0], pos[1], x_tiled)


def _expert_kernel(tile_e_ref, nt_ref, nv_ref, x_ref, wg_ref, wu_ref, wd_ref, o_ref, x16_scr, acc_ref):
    del tile_e_ref, nt_ref
    c = pl.program_id(1)
    n_valid = nv_ref[pl.program_id(0)]
    half = x16_scr.shape[0] // 2

    @pl.when(c == 0)
    def _():
        acc_ref[...] = jnp.zeros_like(acc_ref)
        x16_scr[...] = _load_tiled_rows(x_ref).astype(BF16)

    def accumulate(rows):
        acc_ref[rows, :] += _swiglu_chunk(x16_scr[rows, :], wg_ref[...].astype(BF16), wu_ref[...].astype(BF16),
                                          wd_ref[...].astype(BF16))

    @pl.when(n_valid > half)
    def _():
        accumulate(pl.ds(0, 2 * half))

    @pl.when(jnp.logical_and(n_valid > 0, n_valid <= half))
    def _():
        accumulate(pl.ds(0, half))

    @pl.when(c == pl.num_programs(1) - 1)
    def _():
        _store_tiled_rows(o_ref, acc_ref[...])


def _expert_ffn(xs, tile_e, n_tiles, n_valid, wg, wu, wd, *, tm, tf):
    d_model = wg.shape[1]
    d_ff = wg.shape[2]
    n_chunks = d_ff // tf
    last = n_chunks - 1

    def row_map(i, c, te, nt, nv):
        return (jnp.minimum(i, nt[0] - 1), 0)

    def chunk_of(i, c, nt):
        return jnp.where(i < nt[0], c, last)

    grid_spec = pltpu.PrefetchScalarGridSpec(
        num_scalar_prefetch=3,
        grid=(xs.shape[0] // (tm * SUBLANES), n_chunks),
        in_specs=[
            pl.BlockSpec((tm * SUBLANES, LANES), row_map),
            pl.BlockSpec((None, d_model, tf), lambda i, c, te, nt, nv: (te[i], 0, chunk_of(i, c, nt))),
            pl.BlockSpec((None, d_model, tf), lambda i, c, te, nt, nv: (te[i], 0, chunk_of(i, c, nt))),
            pl.BlockSpec((None, tf, d_model), lambda i, c, te, nt, nv: (te[i], chunk_of(i, c, nt), 0)),
        ],
        out_specs=pl.BlockSpec((tm * SUBLANES, LANES), lambda i, c, te, nt, nv: (i, 0)),
        scratch_shapes=[pltpu.VMEM((tm, d_model), BF16), pltpu.VMEM((tm, d_model), F32)],
    )
    return pl.pallas_call(
        _expert_kernel,
        out_shape=jax.ShapeDtypeStruct(xs.shape, F32),
        grid_spec=grid_spec,
        compiler_params=_params("arbitrary", "arbitrary"),
        name="expert_ffn",
    )(tile_e, n_tiles, n_valid, xs, wg, wu, wd)


def _combine_kernel(pos0_ref, pos1_ref, pos0_next_ref, pos1_next_ref, res_ref, route_ref, g_ref, b_ref, ys_hbm,
                    o_ref, buf_ref, sem):
    i = pl.program_id(0)
    tq = o_ref.shape[0]
    slot = lax.bitwise_and(i, 1)

    def gather(positions, dst_slot):
        def start(t, carry):
            for k in range(TOP_K):
                pltpu.make_async_copy(_tiled_row(ys_hbm, positions[k][t]),
                                      _tiled_row(buf_ref.at[dst_slot, k], t), sem.at[dst_slot]).start(priority=k)
            return carry
        lax.fori_loop(0, tq, start, 0, unroll=8)

    @pl.when(i == 0)
    def _():
        gather((pos0_ref, pos1_ref), 0)

    @pl.when(i + 1 < pl.num_programs(0))
    def _():
        gather((pos0_next_ref, pos1_next_ref), 1 - slot)

    for k in range(TOP_K):
        pltpu.make_async_copy(ys_hbm.at[pl.ds(0, tq * SUBLANES)], buf_ref.at[slot, k], sem.at[slot]).wait()

    route = route_ref[...]
    y = (route[:, ROUTE_GATE:ROUTE_GATE + 1] * _load_tiled_rows(buf_ref.at[slot, 0])
         + route[:, ROUTE_GATE + 1:ROUTE_GATE + 2] * _load_tiled_rows(buf_ref.at[slot, 1]))
    o_ref[...] = _layer_norm(ALPHA * _load_tiled_rows(res_ref) + y, g_ref[...], b_ref[...])


def _combine_ln(ys, pos, res_tiled, route, gain, bias, *, tq=256):
    t_len = res_tiled.shape[0] // SUBLANES
    d_model = SUBLANES * LANES
    n_steps = t_len // tq
    this_tile = pl.BlockSpec((tq,), lambda i: (i,), memory_space=pltpu.SMEM)
    next_tile = pl.BlockSpec((tq,), lambda i: (jnp.minimum(i + 1, n_steps - 1),), memory_space=pltpu.SMEM)
    return pl.pallas_call(
        _combine_kernel,
        out_shape=jax.ShapeDtypeStruct((t_len, d_model), F32),
        grid=(n_steps,),
        in_specs=[
            this_tile, this_tile, next_tile, next_tile,
            pl.BlockSpec((tq * SUBLANES, LANES), lambda i: (i, 0)),
            pl.BlockSpec((tq, LANES), lambda i: (i, 0)),
            pl.BlockSpec((1, d_model), lambda i: (0, 0)),
            pl.BlockSpec((1, d_model), lambda i: (0, 0)),
            pl.BlockSpec(memory_space=pl.ANY),
        ],
        out_specs=pl.BlockSpec((tq, d_model), lambda i: (i, 0)),
        scratch_shapes=[pltpu.VMEM((2, TOP_K, tq * SUBLANES, LANES), F32), pltpu.SemaphoreType.DMA((2,))],
        compiler_params=_params("arbitrary"),
        name="moe_combine_ln",
    )(pos[0], pos[1], pos[0], pos[1], res_tiled, route, gain.reshape(1, -1), bias.reshape(1, -1), ys)


def _moe_ffn_ln(x_tiled, router_w, router_b, wg, wu, wd, gain, bias, *, tm=MOE_ROW_TILE, tf=MOE_FF_CHUNK):
    t_len = x_tiled.shape[0] // SUBLANES
    route, route_t, counts = _router(x_tiled, router_w, router_b)
    experts = route_t[ROUTE_E:ROUTE_E + TOP_K].astype(I32)
    ranks = route_t[ROUTE_RANK:ROUTE_RANK + TOP_K].astype(I32)
    counts = counts[0, :N_EXPERTS].astype(I32)
    tiles_per_expert = (counts + tm - 1) // tm
    tile_end = jnp.cumsum(tiles_per_expert)
    row_start = (tile_end - tiles_per_expert) * tm
    pos = ranks + sum(jnp.where(experts == e, row_start[e], 0) for e in range(N_EXPERTS))
    n_tiles_max = (TOP_K * t_len) // tm + N_EXPERTS
    n_tiles = tile_end[-1:]
    all_tiles = jnp.arange(n_tiles_max, dtype=I32)
    tile_ids = jnp.minimum(all_tiles, n_tiles[0] - 1)
    tile_e = jnp.sum((tile_ids[:, None] >= tile_end[None, :-1]).astype(I32), axis=1)
    is_last = jnp.any((all_tiles[:, None] == tile_end[None, :] - 1) & (tiles_per_expert[None, :] > 0), axis=1)
    fill_tiles = (is_last | (all_tiles >= n_tiles[0])).astype(I32)

    first_tile = (tile_end - tiles_per_expert)[tile_e]
    n_valid = jnp.where(all_tiles < n_tiles[0], jnp.clip(counts[tile_e] - (all_tiles - first_tile) * tm, 0, tm), 0)

    xs = _dispatch(x_tiled, pos, fill_tiles, tm=tm, tq=MOE_DISPATCH_TILE)
    ys = _expert_ffn(xs, tile_e, n_tiles.astype(I32), n_valid.astype(I32), wg, wu, wd, tm=tm, tf=tf)
    return _combine_ln(ys, pos, x_tiled, route, gain, bias)


def _alibi_slopes():
    exps = np.arange(1, N_ATTN_HEADS + 1, dtype=np.float32) * np.float32(8.0 / N_ATTN_HEADS)
    return jnp.asarray(np.exp2(-exps).astype(np.float32))


def kernel(x, mem, a_w_in, a_w_mem_kv, a_w_out, a_ln1_g, a_ln1_b, a_ffn_gate, a_ffn_up, a_ffn_down, a_ln2_g, a_ln2_b, b_w_in, b_fourier_g, b_w_mem_kv, b_w_out, b_ln1_g, b_ln1_b, b_router_w, b_router_b, b_moe_gate, b_moe_up, b_moe_down, b_ln2_g, b_ln2_b):
    b_sz, s_len, d_model = x.shape
    t_len = b_sz * s_len
    x32 = x.reshape(t_len, d_model)
    a_w_in, a_w_mem_kv, a_w_out, a_ffn_gate, a_ffn_up, a_ffn_down, b_w_in, b_w_mem_kv, b_w_out = (
        w[0].astype(BF16) for w in (a_w_in, a_w_mem_kv, a_w_out, a_ffn_gate, a_ffn_up, a_ffn_down,
                                    b_w_in, b_w_mem_kv, b_w_out))

    n_grp = len(DILATIONS)
    qkv0, x16 = _proj_heads(x, a_w_in, (0, n_grp, 2 * n_grp, 3 * n_grp))
    qkv = [qkv0] + [_proj_heads(x16, a_w_in, (g, n_grp + g, 2 * n_grp + g), dil=DILATIONS[g])
                    for g in range(1, n_grp)]
    mem_kv, mem16 = _proj_heads(mem, a_w_mem_kv, (0, 1))
    attn = _dilated_attention(_alibi_slopes(), qkv).reshape(t_len, -1)
    memo = _memory_attention(qkv[0], 3 * HEADS_PER_BLOCK, mem_kv).reshape(t_len, -1)
    x32, x16 = _mixer_ffn_ln(attn, memo, a_w_out, x32, a_ln1_g[0], a_ln1_b[0],
                             a_ffn_gate, a_ffn_up, a_ffn_down, a_ln2_g[0], a_ln2_b[0])

    x16_b = x16.reshape(b_sz, s_len, d_model)
    u_q = _proj_heads(x16_b, b_w_in, (0, 1), gain=b_fourier_g[0], ln_blocks=(True, False))
    mem_kv = _proj_heads(mem16, b_w_mem_kv, (0, 1))
    four = _fourier_mix(u_q).reshape(t_len, -1)
    memo = _memory_attention(u_q, HEADS_PER_BLOCK, mem_kv).reshape(t_len, -1)
    x_tiled = _out_proj_ln(four, memo, b_w_out, x32, b_ln1_g[0], b_ln1_b[0])
    out = _moe_ffn_ln(x_tiled, b_router_w[0], b_router_b[0], b_moe_gate[0], b_moe_up[0], b_moe_down[0],
                      b_ln2_g[0], b_ln2_b[0])
    return out.reshape(b_sz, s_len, d_model)
```

```python
import functools

import numpy as np
import jax
import jax.numpy as jnp
from jax import lax
from jax.experimental import pallas as pl
from jax.experimental.pallas import tpu as pltpu

F32 = jnp.float32
BF16 = jnp.bfloat16
I32 = jnp.int32

LANES = 128
HEAD_DIM = 128
HEADS_PER_BLOCK = 4
DILATIONS = (1, 4, 16)
N_SIDE = 64
ATTN_Q_BLOCK = 64
ATTN_K_WINDOW = 256
ATTN_BATCH = 32
PROJ_MAX_ROW_STRIDE = 4
N_ATTN_HEADS = 12
N_EXPERTS = 8
TOP_K = 2
MOE_ROW_TILE = 1024
MOE_FF_CHUNK = 512
MOE_DISPATCH_TILE = 1024
SUBLANES = 8
ALPHA = (2.0 * 2) ** 0.25
LN_EPS = 1e-5
NEG_INF = -1e30
VMEM_LIMIT_BYTES = 56 * 1024 * 1024

ROUTE_E, ROUTE_RANK, ROUTE_GATE = 0, 2, 4


def _params(*sem):
    return pltpu.CompilerParams(dimension_semantics=sem, vmem_limit_bytes=VMEM_LIMIT_BYTES)


def _layer_norm(z, g, b=None):
    mu = jnp.mean(z, axis=-1, keepdims=True)
    zc = z - mu
    var = jnp.mean(zc * zc, axis=-1, keepdims=True)
    y = zc * lax.rsqrt(var + LN_EPS) * g
    return y if b is None else y + b


def _proj_kernel(*refs, dil, row_chunk, ln_blocks, cast_x):
    n_blocks = len(ln_blocks)
    x_ref, w_refs = refs[0], refs[1:1 + n_blocks]
    rest = refs[1 + n_blocks:]
    g_ref = None
    if any(ln_blocks):
        g_ref, rest = rest[0], rest[1:]
    o_ref, scratch = rest[0], rest[1:]
    if cast_x:
        x16_ref, scratch = scratch[0], scratch[1:]
        x16_ref[...] = x_ref[...].astype(BF16)
        x_ref = x16_ref
    s_len = x_ref.shape[0]
    for blk in range(n_blocks):
        w = w_refs[blk][...]
        acc_ref = scratch[blk % 2] if dil > 1 else None
        for rc in range(s_len // row_chunk):
            rows = pl.ds(rc * row_chunk, row_chunk)
            r = jnp.dot(x_ref[rows, :], w, preferred_element_type=F32)
            for hh in range(HEADS_PER_BLOCK):
                rh = r[:, hh * HEAD_DIM:(hh + 1) * HEAD_DIM]
                if ln_blocks[blk]:
                    rh = _layer_norm(rh, g_ref[hh:hh + 1, :])
                if dil == 1:
                    o_ref[blk * HEADS_PER_BLOCK + hh, 0, rows, :] = rh.astype(o_ref.dtype)
                else:
                    acc_ref[hh, rows, :] = rh
        if dil == 1:
            continue
        for hh in range(HEADS_PER_BLOCK):
            head = blk * HEADS_PER_BLOCK + hh
            if dil <= PROJ_MAX_ROW_STRIDE:
                for r_ in range(dil):
                    o_ref[head, r_, :, :] = acc_ref.at[hh][pl.ds(r_, s_len // dil, stride=dil), :].astype(o_ref.dtype)
            else:
                s1, s2 = PROJ_MAX_ROW_STRIDE, dil // PROJ_MAX_ROW_STRIDE
                tmp_ref = scratch[2]
                for a in range(s1):
                    tmp_ref[a] = acc_ref.at[hh][pl.ds(a, s_len // s1, stride=s1), :]
                for a in range(s1):
                    for b in range(s2):
                        o_ref[head, s1 * b + a, :, :] = tmp_ref.at[a][pl.ds(b, s_len // dil, stride=s2), :].astype(
                            o_ref.dtype)


def _proj_heads(x, w, col_blocks, *, dil=1, gain=None, ln_blocks=None):
    b_sz, s_len, d_model = x.shape
    cast_x = x.dtype != BF16
    blk_cols = HEADS_PER_BLOCK * HEAD_DIM
    n_blocks = len(col_blocks)
    ln_blocks = tuple(ln_blocks) if ln_blocks is not None else (False,) * n_blocks
    in_specs = [pl.BlockSpec((None, s_len, d_model), lambda b: (b, 0, 0))]
    in_specs += [pl.BlockSpec((d_model, blk_cols), lambda b, cb=cb: (0, cb), pipeline_mode=pl.Buffered(1))
                 for cb in col_blocks]
    args = [x] + [w] * n_blocks
    if any(ln_blocks):
        in_specs.append(pl.BlockSpec((HEADS_PER_BLOCK, HEAD_DIM), lambda b: (0, 0)))
        args.append(gain)
    scratch = []
    if dil > 1:
        scratch += [pltpu.VMEM((HEADS_PER_BLOCK, s_len, HEAD_DIM), F32)] * 2
    if dil > PROJ_MAX_ROW_STRIDE:
        scratch.append(pltpu.VMEM((PROJ_MAX_ROW_STRIDE, s_len // PROJ_MAX_ROW_STRIDE, HEAD_DIM), F32))
    n_heads = n_blocks * HEADS_PER_BLOCK
    out_shape = jax.ShapeDtypeStruct((b_sz, n_heads, dil, s_len // dil, HEAD_DIM), BF16)
    out_specs = pl.BlockSpec((None, n_heads, dil, s_len // dil, HEAD_DIM), lambda b: (b, 0, 0, 0, 0))
    if cast_x:
        out_shape = (out_shape, jax.ShapeDtypeStruct(x.shape, BF16))
        out_specs = (out_specs, pl.BlockSpec((None, s_len, d_model), lambda b: (b, 0, 0)))
    return pl.pallas_call(
        functools.partial(_proj_kernel, dil=dil, row_chunk=min(512, s_len), ln_blocks=ln_blocks, cast_x=cast_x),
        out_shape=out_shape,
        grid=(b_sz,),
        in_specs=in_specs,
        out_specs=out_specs,
        scratch_shapes=scratch,
        compiler_params=_params("arbitrary"),
        name=f"proj_heads_d{dil}",
    )(*args)


def _dil_attn_kernel(slopes_ref, q0, k0, v0, q1, k1, v1, q2, k2, v2, o_ref, o_scr, l_scr, bias_scr):
    s_len = o_ref.shape[0]
    h = pl.program_id(1)
    scale = HEAD_DIM ** -0.5
    qb = ATTN_Q_BLOCK
    groups = ((q0, k0, v0), (q1, k1, v1), (q2, k2, v2))
    for g, (qr, kr, vr) in enumerate(groups):
        dil = DILATIONS[g]
        cls_len = s_len // dil
        kw = min(ATTN_K_WINDOW, cls_len)
        nb = cls_len // qb
        nb_shift = nb.bit_length() - 1

        neg_slope = jnp.full((1, kw), slopes_ref[g * HEADS_PER_BLOCK + h], F32) * (-float(dil))
        base_delta = (lax.broadcasted_iota(I32, (qb, kw), 1) - lax.broadcasted_iota(I32, (qb, kw), 0))
        for case in range(kw // N_SIDE):
            dist = jnp.abs(base_delta - case * N_SIDE)
            bias_scr[g, case, :, 0:kw] = jnp.where(dist <= N_SIDE, dist.astype(F32) * neg_slope, NEG_INF)

        def body(it, carry, qr=qr, kr=kr, vr=vr, dil=dil, cls_len=cls_len, kw=kw, nb=nb, g=g,
                 nb_shift=nb_shift):
            blocks = []
            for j in range(ATTN_BATCH):
                idx = it * ATTN_BATCH + j
                r = lax.shift_right_logical(idx, nb_shift)
                i = lax.bitwise_and(idx, nb - 1)
                qs = pl.multiple_of(i * qb, qb)
                ks = pl.multiple_of(jnp.clip(qs - N_SIDE, 0, cls_len - kw), N_SIDE)
                blocks.append((r, qs, ks))
            scores = []
            for r, qs, ks in blocks:
                q = qr[r, pl.ds(qs, qb), :]
                k = kr[r, pl.ds(ks, kw), :]
                scores.append(lax.dot_general(q, k, (((1,), (1,)), ((), ())), preferred_element_type=F32))
            probs = []
            for (r, qs, ks), s in zip(blocks, scores):
                bias = bias_scr[g, lax.shift_right_logical(qs - ks, N_SIDE.bit_length() - 1), :, 0:kw]
                s = s * scale + bias
                m = jnp.max(s, axis=-1, keepdims=True)
                p = jnp.exp(s - m)
                l = jnp.sum(p, axis=-1, keepdims=True)
                probs.append((p.astype(BF16), m, l))
            for (r, qs, ks), (p, m, l) in zip(blocks, probs):
                v = vr[r, pl.ds(ks, kw), :]
                o = jnp.dot(p, v, preferred_element_type=F32) / l
                lse = jnp.broadcast_to(m + jnp.log(l), (qb, LANES))
                row0 = qs * dil + r
                if dil == 1:
                    o_scr[g, pl.ds(row0, qb), :] = o
                    l_scr[g, pl.ds(row0, qb), :] = lse
                else:
                    o_scr.at[g][pl.ds(row0, qb, stride=dil), :] = o
                    l_scr.at[g][pl.ds(row0, qb, stride=dil), :] = lse
            return carry

        lax.fori_loop(0, dil * nb // ATTN_BATCH, body, 0)

    chunk = 256
    for c in range(s_len // chunk):
        rows = pl.ds(c * chunk, chunk)
        l0, l1, l2 = l_scr[0, rows, :], l_scr[1, rows, :], l_scr[2, rows, :]
        m = jnp.maximum(jnp.maximum(l0, l1), l2)
        e0, e1, e2 = jnp.exp(l0 - m), jnp.exp(l1 - m), jnp.exp(l2 - m)
        num = e0 * o_scr[0, rows, :] + e1 * o_scr[1, rows, :] + e2 * o_scr[2, rows, :]
        o_ref[rows, :] = (num / (e0 + e1 + e2)).astype(o_ref.dtype)


def _dilated_attention(slopes, qkv):
    b_sz = qkv[0].shape[0]
    s_len = qkv[0].shape[2] * qkv[0].shape[3]
    in_specs = [pl.BlockSpec(memory_space=pltpu.SMEM)]
    args = [slopes]
    for g, arr in enumerate(qkv):
        dil = DILATIONS[g]
        for which in range(3):
            in_specs.append(pl.BlockSpec((None, None, dil, s_len // dil, HEAD_DIM),
                                         lambda b, h, which=which: (b, which * HEADS_PER_BLOCK + h, 0, 0, 0)))
            args.append(arr)
    return pl.pallas_call(
        _dil_attn_kernel,
        out_shape=jax.ShapeDtypeStruct((b_sz, s_len, HEADS_PER_BLOCK * HEAD_DIM), BF16),
        grid=(b_sz, HEADS_PER_BLOCK),
        in_specs=in_specs,
        out_specs=pl.BlockSpec((None, s_len, HEAD_DIM), lambda b, h: (b, 0, h)),
        scratch_shapes=[pltpu.VMEM((3, s_len, HEAD_DIM), F32), pltpu.VMEM((3, s_len, LANES), F32),
                        pltpu.VMEM((3, ATTN_K_WINDOW // N_SIDE, ATTN_Q_BLOCK, ATTN_K_WINDOW), F32)],
        compiler_params=_params("arbitrary", "arbitrary"),
        name="dilated_attention",
    )(*args)


def _mem_attn_kernel(q_ref, k_ref, v_ref, o_ref):
    s_len = q_ref.shape[0]
    scale = HEAD_DIM ** -0.5
    k = k_ref[...]
    v = v_ref[...]
    chunk = 256
    row_chunks = [pl.ds(c * chunk, chunk) for c in range(s_len // chunk)]
    scores = [lax.dot_general(q_ref[rows, :], k, (((1,), (1,)), ((), ())), preferred_element_type=F32)
              for rows in row_chunks]
    probs = []
    for s in scores:
        s = s * scale
        m = jnp.max(s, axis=-1, keepdims=True)
        p = jnp.exp(s - m)
        probs.append((p.astype(BF16), jnp.sum(p, axis=-1, keepdims=True)))
    for rows, (p, l) in zip(row_chunks, probs):
        o_ref[rows, :] = (jnp.dot(p, v, preferred_element_type=F32) / l).astype(o_ref.dtype)


def _memory_attention(q_heads, q_head0, kv_heads):
    b_sz, _, _, s_len, _ = q_heads.shape
    n_mem = kv_heads.shape[3]
    return pl.pallas_call(
        _mem_attn_kernel,
        out_shape=jax.ShapeDtypeStruct((b_sz, s_len, HEADS_PER_BLOCK * HEAD_DIM), BF16),
        grid=(b_sz, HEADS_PER_BLOCK),
        in_specs=[
            pl.BlockSpec((None, None, None, s_len, HEAD_DIM), lambda b, h: (b, q_head0 + h, 0, 0, 0)),
            pl.BlockSpec((None, None, None, n_mem, HEAD_DIM), lambda b, h: (b, h, 0, 0, 0)),
            pl.BlockSpec((None, None, None, n_mem, HEAD_DIM), lambda b, h: (b, HEADS_PER_BLOCK + h, 0, 0, 0)),
        ],
        out_specs=pl.BlockSpec((None, s_len, HEAD_DIM), lambda b, h: (b, 0, h)),
        compiler_params=_params("arbitrary", "arbitrary"),
        name="memory_attention",
    )(q_heads, kv_heads, kv_heads)


def _out_ln_kernel(a1_ref, a2_ref, w_ref, res_ref, g_ref, b_ref, o_ref):
    k1 = a1_ref.shape[1]
    y = jnp.dot(a1_ref[...], w_ref[0:k1, :], preferred_element_type=F32)
    y = y + jnp.dot(a2_ref[...], w_ref[k1:, :], preferred_element_type=F32)
    _store_tiled_rows(o_ref, _layer_norm(ALPHA * res_ref[...] + y, g_ref[...], b_ref[...]))


def _out_proj_ln(a1, a2, w, res, gain, bias, *, tm=512):
    t_len, d_model = res.shape
    assert d_model == SUBLANES * LANES
    k1, k2 = a1.shape[1], a2.shape[1]
    return pl.pallas_call(
        _out_ln_kernel,
        out_shape=jax.ShapeDtypeStruct((t_len * SUBLANES, LANES), F32),
        grid=(t_len // tm,),
        in_specs=[
            pl.BlockSpec((tm, k1), lambda i: (i, 0)),
            pl.BlockSpec((tm, k2), lambda i: (i, 0)),
            pl.BlockSpec((k1 + k2, d_model), lambda i: (0, 0), pipeline_mode=pl.Buffered(1)),
            pl.BlockSpec((tm, d_model), lambda i: (i, 0)),
            pl.BlockSpec((1, d_model), lambda i: (0, 0)),
            pl.BlockSpec((1, d_model), lambda i: (0, 0)),
        ],
        out_specs=pl.BlockSpec((tm * SUBLANES, LANES), lambda i: (i, 0)),
        compiler_params=_params("arbitrary"),
        name="out_proj_ln",
    )(a1, a2, w, res, gain.reshape(1, -1), bias.reshape(1, -1))


def _swiglu_chunk(x, wg, wu, wd):
    hg = jnp.dot(x, wg, preferred_element_type=F32)
    hu = jnp.dot(x, wu, preferred_element_type=F32)
    h = hg * (1.0 / (1.0 + jnp.exp(-hg))) * hu
    return jnp.dot(h.astype(BF16), wd, preferred_element_type=F32)


def _mixer_ffn_kernel(a1_ref, a2_ref, wo_ref, res_ref, g1_ref, b1_ref, wg_ref, wu_ref, wd_ref, g2_ref, b2_ref,
                      o32_ref, o16_ref, *, tf):
    k1 = a1_ref.shape[1]
    y = jnp.dot(a1_ref[...], wo_ref[0:k1, :], preferred_element_type=F32)
    y = y + jnp.dot(a2_ref[...], wo_ref[k1:, :], preferred_element_type=F32)
    x1 = _layer_norm(ALPHA * res_ref[...] + y, g1_ref[...], b1_ref[...])
    x16 = x1.astype(BF16)
    acc = None
    for f in range(wg_ref.shape[1] // tf):
        cols = pl.ds(f * tf, tf)
        part = _swiglu_chunk(x16, wg_ref[:, cols], wu_ref[:, cols], wd_ref[cols, :])
        acc = part if acc is None else acc + part
    z = _layer_norm(ALPHA * x1 + acc, g2_ref[...], b2_ref[...])
    o32_ref[...] = z
    o16_ref[...] = z.astype(BF16)


def _mixer_ffn_ln(a1, a2, wo, res, g1, b1, wg, wu, wd, g2, b2, *, tm=512, tf=256):
    t_len, d_model = res.shape
    k1, k2 = a1.shape[1], a2.shape[1]
    d_ff = wg.shape[1]
    resident = dict(pipeline_mode=pl.Buffered(1))
    vec = pl.BlockSpec((1, d_model), lambda i: (0, 0))
    return pl.pallas_call(
        functools.partial(_mixer_ffn_kernel, tf=tf),
        out_shape=(jax.ShapeDtypeStruct((t_len, d_model), F32), jax.ShapeDtypeStruct((t_len, d_model), BF16)),
        grid=(t_len // tm,),
        in_specs=[
            pl.BlockSpec((tm, k1), lambda i: (i, 0)),
            pl.BlockSpec((tm, k2), lambda i: (i, 0)),
            pl.BlockSpec((k1 + k2, d_model), lambda i: (0, 0), **resident),
            pl.BlockSpec((tm, d_model), lambda i: (i, 0)),
            vec, vec,
            pl.BlockSpec((d_model, d_ff), lambda i: (0, 0), **resident),
            pl.BlockSpec((d_model, d_ff), lambda i: (0, 0), **resident),
            pl.BlockSpec((d_ff, d_model), lambda i: (0, 0), **resident),
            vec, vec,
        ],
        out_specs=(pl.BlockSpec((tm, d_model), lambda i: (i, 0)),
                   pl.BlockSpec((tm, d_model), lambda i: (i, 0))),
        compiler_params=_params("arbitrary"),
        name="mixer_ffn_ln",
    )(a1, a2, wo, res, g1.reshape(1, -1), b1.reshape(1, -1), wg, wu, wd, g2.reshape(1, -1), b2.reshape(1, -1))


def _dft_constants(s_len):
    half = s_len // 2
    n = np.arange(half, dtype=np.int64)
    ang_s = 2.0 * np.pi * ((n[:, None] * n[None, :]) % s_len) / s_len
    c = np.arange(HEAD_DIM, dtype=np.int64)
    ang_c = 2.0 * np.pi * ((c[:, None] * c[None, :]) % HEAD_DIM) / HEAD_DIM
    norm = 1.0 / np.sqrt(float(s_len * HEAD_DIM))
    mirror = (n[:, None] + n[None, :] == half).astype(np.float64)
    return tuple(jnp.asarray(a, dtype=F32).astype(BF16)
                 for a in (np.cos(ang_s) * norm, np.sin(ang_s) * norm, mirror, np.cos(ang_c), np.sin(ang_c)))


def _fourier_kernel(u_ref, c1_ref, s1_ref, mirror_ref, cc_ref, sc_ref, o_ref, ae_scr, bo_scr, *, norm):
    n_grp, s_len, _ = u_ref.shape
    half = s_len // 2
    cc = cc_ref[...]
    sc = sc_ref[...]
    lanes = lambda parts: jnp.concatenate(parts, axis=1)

    u_low = lanes([u_ref[g, 0:half, :] for g in range(n_grp)])
    u_high = lanes([u_ref[g, half:, :] for g in range(n_grp)])
    u_mirror = jnp.dot(mirror_ref[...], u_high, preferred_element_type=F32)
    even = (u_low.astype(F32) + u_mirror).astype(BF16)
    odd = (u_low.astype(F32) - u_mirror).astype(BF16)

    row_sign = 1.0 - 2.0 * lax.bitwise_and(lax.broadcasted_iota(I32, (half, 1), 0), 1).astype(F32)
    pad_row = lax.broadcasted_iota(I32, (SUBLANES, 1), 0)
    mid_rows, alt_rows = [], []
    for g in range(n_grp):
        cols = slice(g * HEAD_DIM, (g + 1) * HEAD_DIM)
        ae_scr[:, cols] = jnp.dot(even[:, cols], cc, preferred_element_type=F32).astype(BF16)
        bo_scr[:, cols] = jnp.dot(odd[:, cols], sc, preferred_element_type=F32).astype(BF16)
        u_g = u_ref[g].astype(F32)
        alt = jnp.sum((u_g[0:half, :] + u_g[half:, :]) * row_sign, axis=0, keepdims=True)
        mid = u_g[half:half + 1, :]
        pad = jnp.where(pad_row == 0, mid, jnp.where(pad_row == 1, alt, 0.0)).astype(BF16)
        both = jnp.dot(pad, cc, preferred_element_type=F32) * norm
        mid_rows.append(both[0:1, :])
        alt_rows.append(both[1:2, :])
    mid_term = row_sign * lanes(mid_rows)
    p = jnp.dot(c1_ref[...], ae_scr[...], preferred_element_type=F32) + mid_term
    q = jnp.dot(s1_ref[...], bo_scr[...], preferred_element_type=F32)
    o_ref[0:half, :] = (p - q).astype(o_ref.dtype)
    upper = jnp.dot(mirror_ref[...], (p + q).astype(BF16), preferred_element_type=F32)
    is_row0 = lax.broadcasted_iota(I32, (half, 1), 0) == 0
    o_ref[half:, :] = jnp.where(is_row0, lanes(alt_rows), upper).astype(o_ref.dtype)


def _fourier_mix(u_heads):
    b_sz, _, _, s_len, _ = u_heads.shape
    n_grp = HEADS_PER_BLOCK
    half = s_len // 2
    consts = _dft_constants(s_len)
    const_spec = lambda n: pl.BlockSpec((n, n), lambda b: (0, 0), pipeline_mode=pl.Buffered(1))
    return pl.pallas_call(
        functools.partial(_fourier_kernel, norm=1.0 / np.sqrt(float(s_len * HEAD_DIM))),
        out_shape=jax.ShapeDtypeStruct((b_sz, s_len, n_grp * HEAD_DIM), BF16),
        grid=(b_sz,),
        in_specs=[pl.BlockSpec((None, n_grp, None, s_len, HEAD_DIM), lambda b: (b, 0, 0, 0, 0)),
                  const_spec(half), const_spec(half), const_spec(half), const_spec(HEAD_DIM), const_spec(HEAD_DIM)],
        out_specs=pl.BlockSpec((None, s_len, n_grp * HEAD_DIM), lambda b: (b, 0, 0)),
        scratch_shapes=[pltpu.VMEM((half, n_grp * HEAD_DIM), BF16), pltpu.VMEM((half, n_grp * HEAD_DIM), BF16)],
        compiler_params=_params("arbitrary"),
        name="fourier_mix",
    )(u_heads, *consts)


def _split_bf16(a):
    hi = a.astype(BF16)
    lo = (a - hi.astype(F32)).astype(BF16)
    return hi, lo


def _router_kernel(x_ref, w_ref, b_ref, route_ref, route_t_ref, cnt_ref, carry_ref):
    tq = route_ref.shape[0]

    @pl.when(pl.program_id(0) == 0)
    def _():
        carry_ref[...] = jnp.zeros_like(carry_ref)

    x_hi, x_lo = _split_bf16(_load_tiled_rows(x_ref))
    w_hi, w_lo = _split_bf16(w_ref[...])
    logits = (jnp.dot(x_hi, w_hi, preferred_element_type=F32)
              + jnp.dot(x_lo, w_hi, preferred_element_type=F32)
              + jnp.dot(x_hi, w_lo, preferred_element_type=F32)) + b_ref[...]
    lane = lax.broadcasted_iota(I32, (tq, LANES), 1).astype(F32)
    logits = jnp.where(lane < N_EXPERTS, logits, NEG_INF)

    def top1(vals):
        m = jnp.max(vals, axis=-1, keepdims=True)
        idx = jnp.min(jnp.where(vals == m, lane, float(LANES)), axis=-1, keepdims=True)
        return m, idx

    m1, i1 = top1(logits)
    oh1 = lane == i1
    m2, i2 = top1(jnp.where(oh1, NEG_INF, logits))
    oh2 = lane == i2
    e2 = jnp.exp(m2 - m1)
    gate1 = 1.0 / (1.0 + e2)
    gate2 = e2 / (1.0 + e2)

    oh1f = oh1.astype(F32)
    oh2f = oh2.astype(F32)
    both = oh1f + oh2f
    tri = (lax.broadcasted_iota(I32, (tq, tq), 0) > lax.broadcasted_iota(I32, (tq, tq), 1)).astype(BF16)
    before = jnp.dot(tri, both.astype(BF16), preferred_element_type=F32) + carry_ref[0:1, :]
    rank1 = jnp.sum(oh1f * before, axis=-1, keepdims=True)
    rank2 = jnp.sum(oh2f * before, axis=-1, keepdims=True)
    carry_ref[...] = carry_ref[...] + jnp.sum(both, axis=0, keepdims=True)

    rec = jnp.zeros((tq, LANES), F32)
    for lane_idx, val in ((ROUTE_E, i1), (ROUTE_E + 1, i2),
                          (ROUTE_RANK, rank1), (ROUTE_RANK + 1, rank2),
                          (ROUTE_GATE, gate1), (ROUTE_GATE + 1, gate2)):
        rec = jnp.where(lane == lane_idx, val, rec)
    route_ref[...] = rec
    route_t_ref[...] = rec.T[0:route_t_ref.shape[0], :]
    cnt_ref[...] = carry_ref[...]


def _router(x_tiled, w, b, *, tq=512):
    t_len = x_tiled.shape[0] // SUBLANES
    d_model = SUBLANES * LANES
    w_pad = jnp.zeros((d_model, LANES), F32).at[:, :N_EXPERTS].set(w)
    b_pad = jnp.zeros((1, LANES), F32).at[0, :N_EXPERTS].set(b)
    return pl.pallas_call(
        _router_kernel,
        out_shape=(jax.ShapeDtypeStruct((t_len, LANES), F32), jax.ShapeDtypeStruct((SUBLANES, t_len), F32),
                   jax.ShapeDtypeStruct((8, LANES), F32)),
        grid=(t_len // tq,),
        in_specs=[
            pl.BlockSpec((tq * SUBLANES, LANES), lambda i: (i, 0)),
            pl.BlockSpec((d_model, LANES), lambda i: (0, 0)),
            pl.BlockSpec((1, LANES), lambda i: (0, 0)),
        ],
        out_specs=(pl.BlockSpec((tq, LANES), lambda i: (i, 0)), pl.BlockSpec((SUBLANES, tq), lambda i: (0, i)),
                   pl.BlockSpec((8, LANES), lambda i: (0, 0))),
        scratch_shapes=[pltpu.VMEM((8, LANES), F32)],
        compiler_params=_params("arbitrary"),
        name="router",
    )(x_tiled, w_pad, b_pad)


def _store_tiled_rows(dst_ref, val):
    n_rows, width = val.shape
    for j in range(width // LANES):
        dst_ref[pl.ds(j, n_rows, stride=width // LANES), :] = val[:, j * LANES:(j + 1) * LANES]


def _load_tiled_rows(src_ref):
    n_rows = src_ref.shape[0] // SUBLANES
    return jnp.concatenate([src_ref[pl.ds(j, n_rows, stride=SUBLANES), :] for j in range(SUBLANES)], axis=1)


def _tiled_row(ref, row):
    return ref.at[pl.ds(pl.multiple_of(row * SUBLANES, SUBLANES), SUBLANES)]


def _dispatch_kernel(fill_ref, pos0_ref, pos1_ref, x_ref, xs_hbm, zero_scr, sem, *, tm):
    tq = x_ref.shape[0] // SUBLANES
    pos_refs = (pos0_ref, pos1_ref)

    @pl.when(pl.program_id(0) == 0)
    def _():
        zero_scr[...] = jnp.zeros_like(zero_scr)

        def fill_copies(tile):
            copies = []
            for piece in range(tm // tq):
                row0 = pl.multiple_of((tile * tm + piece * tq) * SUBLANES, SUBLANES)
                copies.append(pltpu.make_async_copy(zero_scr, xs_hbm.at[pl.ds(row0, tq * SUBLANES)], sem))
            return copies

        def start_fill(tile, carry):
            @pl.when(fill_ref[tile] == 1)
            def _():
                for copy in fill_copies(tile):
                    copy.start()
            return carry

        def wait_fill(tile, carry):
            @pl.when(fill_ref[tile] == 1)
            def _():
                for copy in fill_copies(tile):
                    copy.wait()
            return carry

        lax.fori_loop(0, fill_ref.shape[0], start_fill, 0)
        lax.fori_loop(0, fill_ref.shape[0], wait_fill, 0)

    def start(t, carry):
        for k in range(TOP_K):
            pltpu.make_async_copy(_tiled_row(x_ref, t), _tiled_row(xs_hbm, pos_refs[k][t]), sem).start(priority=k)
        return carry

    lax.fori_loop(0, tq, start, 0, unroll=8)
    for k in range(TOP_K):
        pltpu.make_async_copy(x_ref, xs_hbm.at[pl.ds(0, tq * SUBLANES)], sem).wait()


def _dispatch(x_tiled, pos, fill_tiles, *, tm, tq=512):
    t_len = x_tiled.shape[0] // SUBLANES
    assert tm % tq == 0
    grid_spec = pltpu.PrefetchScalarGridSpec(
        num_scalar_prefetch=1,
        grid=(t_len // tq,),
        in_specs=[
            pl.BlockSpec((tq,), lambda i, fill: (i,), memory_space=pltpu.SMEM),
            pl.BlockSpec((tq,), lambda i, fill: (i,), memory_space=pltpu.SMEM),
            pl.BlockSpec((tq * SUBLANES, LANES), lambda i, fill: (i, 0)),
        ],
        out_specs=pl.BlockSpec(memory_space=pl.ANY),
        scratch_shapes=[pltpu.VMEM((tq * SUBLANES, LANES), x_tiled.dtype), pltpu.SemaphoreType.DMA],
    )
    return pl.pallas_call(
        functools.partial(_dispatch_kernel, tm=tm),
        out_shape=jax.ShapeDtypeStruct((fill_tiles.shape[0] * tm * SUBLANES, LANES), x_tiled.dtype),
        grid_spec=grid_spec,
        compiler_params=_params("arbitrary"),
        name="moe_dispatch",
    )(fill_tiles, pos[0], pos[1], x_tiled)


def _expert_kernel(tile_e_ref, nt_ref, nv_ref, x_ref, wg_ref, wu_ref, wd_ref, o_ref, x16_scr, acc_ref):
    del tile_e_ref, nt_ref
    c = pl.program_id(1)
    n_valid = nv_ref[pl.program_id(0)]
    half = x16_scr.shape[0] // 2

    @pl.when(c == 0)
    def _():
        acc_ref[...] = jnp.zeros_like(acc_ref)
        x16_scr[...] = _load_tiled_rows(x_ref).astype(BF16)

    def accumulate(rows):
        acc_ref[rows, :] += _swiglu_chunk(x16_scr[rows, :], wg_ref[...].astype(BF16), wu_ref[...].astype(BF16),
                                          wd_ref[...].astype(BF16))

    @pl.when(n_valid > half)
    def _():
        accumulate(pl.ds(0, 2 * half))

    @pl.when(jnp.logical_and(n_valid > 0, n_valid <= half))
    def _():
        accumulate(pl.ds(0, half))

    @pl.when(c == pl.num_programs(1) - 1)
    def _():
        _store_tiled_rows(o_ref, acc_ref[...])


def _expert_ffn(xs, tile_e, n_tiles, n_valid, wg, wu, wd, *, tm, tf):
    d_model = wg.shape[1]
    d_ff = wg.shape[2]
    n_chunks = d_ff // tf
    last = n_chunks - 1

    def row_map(i, c, te, nt, nv):
        return (jnp.minimum(i, nt[0] - 1), 0)

    def chunk_of(i, c, nt):
        return jnp.where(i < nt[0], c, last)

    grid_spec = pltpu.PrefetchScalarGridSpec(
        num_scalar_prefetch=3,
        grid=(xs.shape[0] // (tm * SUBLANES), n_chunks),
        in_specs=[
            pl.BlockSpec((tm * SUBLANES, LANES), row_map),
            pl.BlockSpec((None, d_model, tf), lambda i, c, te, nt, nv: (te[i], 0, chunk_of(i, c, nt))),
            pl.BlockSpec((None, d_model, tf), lambda i, c, te, nt, nv: (te[i], 0, chunk_of(i, c, nt))),
            pl.BlockSpec((None, tf, d_model), lambda i, c, te, nt, nv: (te[i], chunk_of(i, c, nt), 0)),
        ],
        out_specs=pl.BlockSpec((tm * SUBLANES, LANES), lambda i, c, te, nt, nv: (i, 0)),
        scratch_shapes=[pltpu.VMEM((tm, d_model), BF16), pltpu.VMEM((tm, d_model), F32)],
    )
    return pl.pallas_call(
        _expert_kernel,
        out_shape=jax.ShapeDtypeStruct(xs.shape, F32),
        grid_spec=grid_spec,
        compiler_params=_params("arbitrary", "arbitrary"),
        name="expert_ffn",
    )(tile_e, n_tiles, n_valid, xs, wg, wu, wd)


def _combine_kernel(pos0_ref, pos1_ref, pos0_next_ref, pos1_next_ref, res_ref, route_ref, g_ref, b_ref, ys_hbm,
                    o_ref, buf_ref, sem):
    i = pl.program_id(0)
    tq = o_ref.shape[0]
    slot = lax.bitwise_and(i, 1)

    def gather(positions, dst_slot):
        def start(t, carry):
            for k in range(TOP_K):
                pltpu.make_async_copy(_tiled_row(ys_hbm, positions[k][t]),
                                      _tiled_row(buf_ref.at[dst_slot, k], t), sem.at[dst_slot]).start(priority=k)
            return carry
        lax.fori_loop(0, tq, start, 0, unroll=8)

    @pl.when(i == 0)
    def _():
        gather((pos0_ref, pos1_ref), 0)

    @pl.when(i + 1 < pl.num_programs(0))
    def _():
        gather((pos0_next_ref, pos1_next_ref), 1 - slot)

    for k in range(TOP_K):
        pltpu.make_async_copy(ys_hbm.at[pl.ds(0, tq * SUBLANES)], buf_ref.at[slot, k], sem.at[slot]).wait()

    route = route_ref[...]
    y = (route[:, ROUTE_GATE:ROUTE_GATE + 1] * _load_tiled_rows(buf_ref.at[slot, 0])
         + route[:, ROUTE_GATE + 1:ROUTE_GATE + 2] * _load_tiled_rows(buf_ref.at[slot, 1]))
    o_ref[...] = _layer_norm(ALPHA * _load_tiled_rows(res_ref) + y, g_ref[...], b_ref[...])


def _combine_ln(ys, pos, res_tiled, route, gain, bias, *, tq=512):
    t_len = res_tiled.shape[0] // SUBLANES
    d_model = SUBLANES * LANES
    n_steps = t_len // tq
    this_tile = pl.BlockSpec((tq,), lambda i: (i,), memory_space=pltpu.SMEM)
    next_tile = pl.BlockSpec((tq,), lambda i: (jnp.minimum(i + 1, n_steps - 1),), memory_space=pltpu.SMEM)
    return pl.pallas_call(
        _combine_kernel,
        out_shape=jax.ShapeDtypeStruct((t_len, d_model), F32),
        grid=(n_steps,),
        in_specs=[
            this_tile, this_tile, next_tile, next_tile,
            pl.BlockSpec((tq * SUBLANES, LANES), lambda i: (i, 0)),
            pl.BlockSpec((tq, LANES), lambda i: (i, 0)),
            pl.BlockSpec((1, d_model), lambda i: (0, 0)),
            pl.BlockSpec((1, d_model), lambda i: (0, 0)),
            pl.BlockSpec(memory_space=pl.ANY),
        ],
        out_specs=pl.BlockSpec((tq, d_model), lambda i: (i, 0)),
        scratch_shapes=[pltpu.VMEM((2, TOP_K, tq * SUBLANES, LANES), F32), pltpu.SemaphoreType.DMA((2,))],
        compiler_params=_params("arbitrary"),
        name="moe_combine_ln",
    )(pos[0], pos[1], pos[0], pos[1], res_tiled, route, gain.reshape(1, -1), bias.reshape(1, -1), ys)


def _moe_ffn_ln(x_tiled, router_w, router_b, wg, wu, wd, gain, bias, *, tm=MOE_ROW_TILE, tf=MOE_FF_CHUNK):
    t_len = x_tiled.shape[0] // SUBLANES
    route, route_t, counts = _router(x_tiled, router_w, router_b)
    experts = route_t[ROUTE_E:ROUTE_E + TOP_K].astype(I32)
    ranks = route_t[ROUTE_RANK:ROUTE_RANK + TOP_K].astype(I32)
    counts = counts[0, :N_EXPERTS].astype(I32)
    tiles_per_expert = (counts + tm - 1) // tm
    tile_end = jnp.cumsum(tiles_per_expert)
    row_start = (tile_end - tiles_per_expert) * tm
    pos = ranks + sum(jnp.where(experts == e, row_start[e], 0) for e in range(N_EXPERTS))
    n_tiles_max = (TOP_K * t_len) // tm + N_EXPERTS
    n_tiles = tile_end[-1:]
    all_tiles = jnp.arange(n_tiles_max, dtype=I32)
    tile_ids = jnp.minimum(all_tiles, n_tiles[0] - 1)
    tile_e = jnp.sum((tile_ids[:, None] >= tile_end[None, :-1]).astype(I32), axis=1)
    is_last = jnp.any((all_tiles[:, None] == tile_end[None, :] - 1) & (tiles_per_expert[None, :] > 0), axis=1)
    fill_tiles = (is_last | (all_tiles >= n_tiles[0])).astype(I32)

    first_tile = (tile_end - tiles_per_expert)[tile_e]
    n_valid = jnp.where(all_tiles < n_tiles[0], jnp.clip(counts[tile_e] - (all_tiles - first_tile) * tm, 0, tm), 0)

    xs = _dispatch(x_tiled, pos, fill_tiles, tm=tm, tq=MOE_DISPATCH_TILE)
    ys = _expert_ffn(xs, tile_e, n_tiles.astype(I32), n_valid.astype(I32), wg, wu, wd, tm=tm, tf=tf)
    return _combine_ln(ys, pos, x_tiled, route, gain, bias)


def _alibi_slopes():
    exps = np.arange(1, N_ATTN_HEADS + 1, dtype=np.float32) * np.float32(8.0 / N_ATTN_HEADS)
    return jnp.asarray(np.exp2(-exps).astype(np.float32))


def kernel(x, mem, a_w_in, a_w_mem_kv, a_w_out, a_ln1_g, a_ln1_b, a_ffn_gate, a_ffn_up, a_ffn_down, a_ln2_g, a_ln2_b, b_w_in, b_fourier_g, b_w_mem_kv, b_w_out, b_ln1_g, b_ln1_b, b_router_w, b_router_b, b_moe_gate, b_moe_up, b_moe_down, b_ln2_g, b_ln2_b):
    b_sz, s_len, d_model = x.shape
    t_len = b_sz * s_len
    x32 = x.reshape(t_len, d_model)
    a_w_in, a_w_mem_kv, a_w_out, a_ffn_gate, a_ffn_up, a_ffn_down, b_w_in, b_w_mem_kv, b_w_out = (
        w[0].astype(BF16) for w in (a_w_in, a_w_mem_kv, a_w_out, a_ffn_gate, a_ffn_up, a_ffn_down,
                                    b_w_in, b_w_mem_kv, b_w_out))

    n_grp = len(DILATIONS)
    qkv0, x16 = _proj_heads(x, a_w_in, (0, n_grp, 2 * n_grp, 3 * n_grp))
    qkv = [qkv0] + [_proj_heads(x16, a_w_in, (g, n_grp + g, 2 * n_grp + g), dil=DILATIONS[g])
                    for g in range(1, n_grp)]
    mem_kv, mem16 = _proj_heads(mem, a_w_mem_kv, (0, 1))
    attn = _dilated_attention(_alibi_slopes(), qkv).reshape(t_len, -1)
    memo = _memory_attention(qkv[0], 3 * HEADS_PER_BLOCK, mem_kv).reshape(t_len, -1)
    x32, x16 = _mixer_ffn_ln(attn, memo, a_w_out, x32, a_ln1_g[0], a_ln1_b[0],
                             a_ffn_gate, a_ffn_up, a_ffn_down, a_ln2_g[0], a_ln2_b[0])

    x16_b = x16.reshape(b_sz, s_len, d_model)
    u_q = _proj_heads(x16_b, b_w_in, (0, 1), gain=b_fourier_g[0], ln_blocks=(True, False))
    mem_kv = _proj_heads(mem16, b_w_mem_kv, (0, 1))
    four = _fourier_mix(u_q).reshape(t_len, -1)
    memo = _memory_attention(u_q, HEADS_PER_BLOCK, mem_kv).reshape(t_len, -1)
    x_tiled = _out_proj_ln(four, memo, b_w_out, x32, b_ln1_g[0], b_ln1_b[0])
    out = _moe_ffn_ln(x_tiled, b_router_w[0], b_router_b[0], b_moe_gate[0], b_moe_up[0], b_moe_down[0],
                      b_ln2_g[0], b_ln2_b[0])
    return out.reshape(b_sz, s_len, d_model)
```

```python
import functools

import numpy as np
import jax
import jax.numpy as jnp
from jax import lax
from jax.experimental import pallas as pl
from jax.experimental.pallas import tpu as pltpu

F32 = jnp.float32
BF16 = jnp.bfloat16
I32 = jnp.int32

LANES = 128
HEAD_DIM = 128
HEADS_PER_BLOCK = 4
DILATIONS = (1, 4, 16)
N_SIDE = 64
ATTN_Q_BLOCK = 64
ATTN_K_WINDOW = 256
ATTN_BATCH = 32
PROJ_MAX_ROW_STRIDE = 4
N_ATTN_HEADS = 12
N_EXPERTS = 8
TOP_K = 2
MOE_ROW_TILE = 1024
MOE_FF_CHUNK = 512
MOE_DISPATCH_TILE = 1024
SUBLANES = 8
ALPHA = (2.0 * 2) ** 0.25
LN_EPS = 1e-5
NEG_INF = -1e30
VMEM_LIMIT_BYTES = 56 * 1024 * 1024

ROUTE_E, ROUTE_RANK, ROUTE_GATE = 0, 2, 4


def _params(*sem):
    return pltpu.CompilerParams(dimension_semantics=sem, vmem_limit_bytes=VMEM_LIMIT_BYTES)


def _layer_norm(z, g, b=None):
    mu = jnp.mean(z, axis=-1, keepdims=True)
    zc = z - mu
    var = jnp.mean(zc * zc, axis=-1, keepdims=True)
    y = zc * lax.rsqrt(var + LN_EPS) * g
    return y if b is None else y + b


def _proj_kernel(*refs, dil, row_chunk, ln_blocks, cast_x):
    n_blocks = len(ln_blocks)
    x_ref, w_refs = refs[0], refs[1:1 + n_blocks]
    rest = refs[1 + n_blocks:]
    g_ref = None
    if any(ln_blocks):
        g_ref, rest = rest[0], rest[1:]
    o_ref, scratch = rest[0], rest[1:]
    if cast_x:
        x16_ref, scratch = scratch[0], scratch[1:]
        x16_ref[...] = x_ref[...].astype(BF16)
        x_ref = x16_ref
    s_len = x_ref.shape[0]
    for blk in range(n_blocks):
        w = w_refs[blk][...]
        acc_ref = scratch[blk % 2] if dil > 1 else None
        for rc in range(s_len // row_chunk):
            rows = pl.ds(rc * row_chunk, row_chunk)
            r = jnp.dot(x_ref[rows, :], w, preferred_element_type=F32)
            for hh in range(HEADS_PER_BLOCK):
                rh = r[:, hh * HEAD_DIM:(hh + 1) * HEAD_DIM]
                if ln_blocks[blk]:
                    rh = _layer_norm(rh, g_ref[hh:hh + 1, :])
                if dil == 1:
                    o_ref[blk * HEADS_PER_BLOCK + hh, 0, rows, :] = rh.astype(o_ref.dtype)
                else:
                    acc_ref[hh, rows, :] = rh
        if dil == 1:
            continue
        for hh in range(HEADS_PER_BLOCK):
            head = blk * HEADS_PER_BLOCK + hh
            if dil <= PROJ_MAX_ROW_STRIDE:
                for r_ in range(dil):
                    o_ref[head, r_, :, :] = acc_ref.at[hh][pl.ds(r_, s_len // dil, stride=dil), :].astype(o_ref.dtype)
            else:
                s1, s2 = PROJ_MAX_ROW_STRIDE, dil // PROJ_MAX_ROW_STRIDE
                tmp_ref = scratch[2]
                for a in range(s1):
                    tmp_ref[a] = acc_ref.at[hh][pl.ds(a, s_len // s1, stride=s1), :]
                for a in range(s1):
                    for b in range(s2):
                        o_ref[head, s1 * b + a, :, :] = tmp_ref.at[a][pl.ds(b, s_len // dil, stride=s2), :].astype(
                            o_ref.dtype)


def _proj_heads(x, w, col_blocks, *, dil=1, gain=None, ln_blocks=None):
    b_sz, s_len, d_model = x.shape
    cast_x = x.dtype != BF16
    blk_cols = HEADS_PER_BLOCK * HEAD_DIM
    n_blocks = len(col_blocks)
    ln_blocks = tuple(ln_blocks) if ln_blocks is not None else (False,) * n_blocks
    in_specs = [pl.BlockSpec((None, s_len, d_model), lambda b: (b, 0, 0))]
    in_specs += [pl.BlockSpec((d_model, blk_cols), lambda b, cb=cb: (0, cb), pipeline_mode=pl.Buffered(1))
                 for cb in col_blocks]
    args = [x] + [w] * n_blocks
    if any(ln_blocks):
        in_specs.append(pl.BlockSpec((HEADS_PER_BLOCK, HEAD_DIM), lambda b: (0, 0)))
        args.append(gain)
    scratch = []
    if dil > 1:
        scratch += [pltpu.VMEM((HEADS_PER_BLOCK, s_len, HEAD_DIM), F32)] * 2
    if dil > PROJ_MAX_ROW_STRIDE:
        scratch.append(pltpu.VMEM((PROJ_MAX_ROW_STRIDE, s_len // PROJ_MAX_ROW_STRIDE, HEAD_DIM), F32))
    n_heads = n_blocks * HEADS_PER_BLOCK
    out_shape = jax.ShapeDtypeStruct((b_sz, n_heads, dil, s_len // dil, HEAD_DIM), BF16)
    out_specs = pl.BlockSpec((None, n_heads, dil, s_len // dil, HEAD_DIM), lambda b: (b, 0, 0, 0, 0))
    if cast_x:
        out_shape = (out_shape, jax.ShapeDtypeStruct(x.shape, BF16))
        out_specs = (out_specs, pl.BlockSpec((None, s_len, d_model), lambda b: (b, 0, 0)))
    return pl.pallas_call(
        functools.partial(_proj_kernel, dil=dil, row_chunk=min(512, s_len), ln_blocks=ln_blocks, cast_x=cast_x),
        out_shape=out_shape,
        grid=(b_sz,),
        in_specs=in_specs,
        out_specs=out_specs,
        scratch_shapes=scratch,
        compiler_params=_params("arbitrary"),
        name=f"proj_heads_d{dil}",
    )(*args)


def _dil_attn_kernel(slopes_ref, q0, k0, v0, q1, k1, v1, q2, k2, v2, o_ref, o_scr, l_scr, bias_scr):
    s_len = o_ref.shape[0]
    h = pl.program_id(1)
    scale = HEAD_DIM ** -0.5
    qb = ATTN_Q_BLOCK
    groups = ((q0, k0, v0), (q1, k1, v1), (q2, k2, v2))
    for g, (qr, kr, vr) in enumerate(groups):
        dil = DILATIONS[g]
        cls_len = s_len // dil
        kw = min(ATTN_K_WINDOW, cls_len)
        nb = cls_len // qb
        nb_shift = nb.bit_length() - 1

        neg_slope = jnp.full((1, kw), slopes_ref[g * HEADS_PER_BLOCK + h], F32) * (-float(dil))
        base_delta = (lax.broadcasted_iota(I32, (qb, kw), 1) - lax.broadcasted_iota(I32, (qb, kw), 0))
        for case in range(kw // N_SIDE):
            dist = jnp.abs(base_delta - case * N_SIDE)
            bias_scr[g, case, :, 0:kw] = jnp.where(dist <= N_SIDE, dist.astype(F32) * neg_slope, NEG_INF)

        def body(it, carry, qr=qr, kr=kr, vr=vr, dil=dil, cls_len=cls_len, kw=kw, nb=nb, g=g,
                 nb_shift=nb_shift):
            blocks = []
            for j in range(ATTN_BATCH):
                idx = it * ATTN_BATCH + j
                r = lax.shift_right_logical(idx, nb_shift)
                i = lax.bitwise_and(idx, nb - 1)
                qs = pl.multiple_of(i * qb, qb)
                ks = pl.multiple_of(jnp.clip(qs - N_SIDE, 0, cls_len - kw), N_SIDE)
                blocks.append((r, qs, ks))
            scores = []
            for r, qs, ks in blocks:
                q = qr[r, pl.ds(qs, qb), :]
                k = kr[r, pl.ds(ks, kw), :]
                scores.append(lax.dot_general(q, k, (((1,), (1,)), ((), ())), preferred_element_type=F32))
            probs = []
            for (r, qs, ks), s in zip(blocks, scores):
                bias = bias_scr[g, lax.shift_right_logical(qs - ks, N_SIDE.bit_length() - 1), :, 0:kw]
                s = s * scale + bias
                m = jnp.max(s, axis=-1, keepdims=True)
                p = jnp.exp(s - m)
                l = jnp.sum(p, axis=-1, keepdims=True)
                probs.append((p.astype(BF16), m, l))
            for (r, qs, ks), (p, m, l) in zip(blocks, probs):
                v = vr[r, pl.ds(ks, kw), :]
                o = jnp.dot(p, v, preferred_element_type=F32) / l
                lse = jnp.broadcast_to(m + jnp.log(l), (qb, LANES))
                row0 = qs * dil + r
                if dil == 1:
                    o_scr[g, pl.ds(row0, qb), :] = o
                    l_scr[g, pl.ds(row0, qb), :] = lse
                else:
                    o_scr.at[g][pl.ds(row0, qb, stride=dil), :] = o
                    l_scr.at[g][pl.ds(row0, qb, stride=dil), :] = lse
            return carry

        lax.fori_loop(0, dil * nb // ATTN_BATCH, body, 0)

    chunk = 256
    for c in range(s_len // chunk):
        rows = pl.ds(c * chunk, chunk)
        l0, l1, l2 = l_scr[0, rows, :], l_scr[1, rows, :], l_scr[2, rows, :]
        m = jnp.maximum(jnp.maximum(l0, l1), l2)
        e0, e1, e2 = jnp.exp(l0 - m), jnp.exp(l1 - m), jnp.exp(l2 - m)
        num = e0 * o_scr[0, rows, :] + e1 * o_scr[1, rows, :] + e2 * o_scr[2, rows, :]
        o_ref[rows, :] = (num / (e0 + e1 + e2)).astype(o_ref.dtype)


def _dilated_attention(slopes, qkv):
    b_sz = qkv[0].shape[0]
    s_len = qkv[0].shape[2] * qkv[0].shape[3]
    in_specs = [pl.BlockSpec(memory_space=pltpu.SMEM)]
    args = [slopes]
    for g, arr in enumerate(qkv):
        dil = DILATIONS[g]
        for which in range(3):
            in_specs.append(pl.BlockSpec((None, None, dil, s_len // dil, HEAD_DIM),
                                         lambda b, h, which=which: (b, which * HEADS_PER_BLOCK + h, 0, 0, 0)))
            args.append(arr)
    return pl.pallas_call(
        _dil_attn_kernel,
        out_shape=jax.ShapeDtypeStruct((b_sz, s_len, HEADS_PER_BLOCK * HEAD_DIM), BF16),
        grid=(b_sz, HEADS_PER_BLOCK),
        in_specs=in_specs,
        out_specs=pl.BlockSpec((None, s_len, HEAD_DIM), lambda b, h: (b, 0, h)),
        scratch_shapes=[pltpu.VMEM((3, s_len, HEAD_DIM), F32), pltpu.VMEM((3, s_len, LANES), F32),
                        pltpu.VMEM((3, ATTN_K_WINDOW // N_SIDE, ATTN_Q_BLOCK, ATTN_K_WINDOW), F32)],
        compiler_params=_params("arbitrary", "arbitrary"),
        name="dilated_attention",
    )(*args)


def _mem_attn_kernel(q_ref, kv_ref, o_ref):
    n_heads, s_len, _ = q_ref.shape
    scale = HEAD_DIM ** -0.5
    chunk = 256
    row_chunks = [pl.ds(c * chunk, chunk) for c in range(s_len // chunk)]
    for h in range(n_heads):
        k = kv_ref[h]
        v = kv_ref[n_heads + h]
        scores = [lax.dot_general(q_ref[h, rows, :], k, (((1,), (1,)), ((), ())), preferred_element_type=F32)
                  for rows in row_chunks]
        probs = []
        for s in scores:
            s = s * scale
            m = jnp.max(s, axis=-1, keepdims=True)
            p = jnp.exp(s - m)
            probs.append((p.astype(BF16), jnp.sum(p, axis=-1, keepdims=True)))
        for rows, (p, l) in zip(row_chunks, probs):
            o_ref[rows, h * HEAD_DIM:(h + 1) * HEAD_DIM] = (
                jnp.dot(p, v, preferred_element_type=F32) / l).astype(o_ref.dtype)


def _memory_attention(q_heads, q_head0, kv_heads):
    b_sz, _, _, s_len, _ = q_heads.shape
    n_mem = kv_heads.shape[3]
    assert q_head0 % HEADS_PER_BLOCK == 0
    return pl.pallas_call(
        _mem_attn_kernel,
        out_shape=jax.ShapeDtypeStruct((b_sz, s_len, HEADS_PER_BLOCK * HEAD_DIM), BF16),
        grid=(b_sz,),
        in_specs=[
            pl.BlockSpec((None, HEADS_PER_BLOCK, None, s_len, HEAD_DIM),
                         lambda b: (b, q_head0 // HEADS_PER_BLOCK, 0, 0, 0)),
            pl.BlockSpec((None, 2 * HEADS_PER_BLOCK, None, n_mem, HEAD_DIM), lambda b: (b, 0, 0, 0, 0)),
        ],
        out_specs=pl.BlockSpec((None, s_len, HEADS_PER_BLOCK * HEAD_DIM), lambda b: (b, 0, 0)),
        compiler_params=_params("arbitrary"),
        name="memory_attention",
    )(q_heads, kv_heads)


def _out_ln_kernel(a1_ref, a2_ref, w_ref, res_ref, g_ref, b_ref, o_ref):
    k1 = a1_ref.shape[1]
    y = jnp.dot(a1_ref[...], w_ref[0:k1, :], preferred_element_type=F32)
    y = y + jnp.dot(a2_ref[...], w_ref[k1:, :], preferred_element_type=F32)
    _store_tiled_rows(o_ref, _layer_norm(ALPHA * res_ref[...] + y, g_ref[...], b_ref[...]))


def _out_proj_ln(a1, a2, w, res, gain, bias, *, tm=512):
    t_len, d_model = res.shape
    assert d_model == SUBLANES * LANES
    k1, k2 = a1.shape[1], a2.shape[1]
    return pl.pallas_call(
        _out_ln_kernel,
        out_shape=jax.ShapeDtypeStruct((t_len * SUBLANES, LANES), F32),
        grid=(t_len // tm,),
        in_specs=[
            pl.BlockSpec((tm, k1), lambda i: (i, 0)),
            pl.BlockSpec((tm, k2), lambda i: (i, 0)),
            pl.BlockSpec((k1 + k2, d_model), lambda i: (0, 0), pipeline_mode=pl.Buffered(1)),
            pl.BlockSpec((tm, d_model), lambda i: (i, 0)),
            pl.BlockSpec((1, d_model), lambda i: (0, 0)),
            pl.BlockSpec((1, d_model), lambda i: (0, 0)),
        ],
        out_specs=pl.BlockSpec((tm * SUBLANES, LANES), lambda i: (i, 0)),
        compiler_params=_params("arbitrary"),
        name="out_proj_ln",
    )(a1, a2, w, res, gain.reshape(1, -1), bias.reshape(1, -1))


def _swiglu_chunk(x, wg, wu, wd):
    hg = jnp.dot(x, wg, preferred_element_type=F32)
    hu = jnp.dot(x, wu, preferred_element_type=F32)
    h = hg * (1.0 / (1.0 + jnp.exp(-hg))) * hu
    return jnp.dot(h.astype(BF16), wd, preferred_element_type=F32)


def _mixer_ffn_kernel(a1_ref, a2_ref, wo_ref, res_ref, g1_ref, b1_ref, wg_ref, wu_ref, wd_ref, g2_ref, b2_ref,
                      o32_ref, o16_ref, *, tf):
    k1 = a1_ref.shape[1]
    y = jnp.dot(a1_ref[...], wo_ref[0:k1, :], preferred_element_type=F32)
    y = y + jnp.dot(a2_ref[...], wo_ref[k1:, :], preferred_element_type=F32)
    x1 = _layer_norm(ALPHA * res_ref[...] + y, g1_ref[...], b1_ref[...])
    x16 = x1.astype(BF16)
    acc = None
    for f in range(wg_ref.shape[1] // tf):
        cols = pl.ds(f * tf, tf)
        part = _swiglu_chunk(x16, wg_ref[:, cols], wu_ref[:, cols], wd_ref[cols, :])
        acc = part if acc is None else acc + part
    z = _layer_norm(ALPHA * x1 + acc, g2_ref[...], b2_ref[...])
    o32_ref[...] = z
    o16_ref[...] = z.astype(BF16)


def _mixer_ffn_ln(a1, a2, wo, res, g1, b1, wg, wu, wd, g2, b2, *, tm=512, tf=256):
    t_len, d_model = res.shape
    k1, k2 = a1.shape[1], a2.shape[1]
    d_ff = wg.shape[1]
    resident = dict(pipeline_mode=pl.Buffered(1))
    vec = pl.BlockSpec((1, d_model), lambda i: (0, 0))
    return pl.pallas_call(
        functools.partial(_mixer_ffn_kernel, tf=tf),
        out_shape=(jax.ShapeDtypeStruct((t_len, d_model), F32), jax.ShapeDtypeStruct((t_len, d_model), BF16)),
        grid=(t_len // tm,),
        in_specs=[
            pl.BlockSpec((tm, k1), lambda i: (i, 0)),
            pl.BlockSpec((tm, k2), lambda i: (i, 0)),
            pl.BlockSpec((k1 + k2, d_model), lambda i: (0, 0), **resident),
            pl.BlockSpec((tm, d_model), lambda i: (i, 0)),
            vec, vec,
            pl.BlockSpec((d_model, d_ff), lambda i: (0, 0), **resident),
            pl.BlockSpec((d_model, d_ff), lambda i: (0, 0), **resident),
            pl.BlockSpec((d_ff, d_model), lambda i: (0, 0), **resident),
            vec, vec,
        ],
        out_specs=(pl.BlockSpec((tm, d_model), lambda i: (i, 0)),
                   pl.BlockSpec((tm, d_model), lambda i: (i, 0))),
        compiler_params=_params("arbitrary"),
        name="mixer_ffn_ln",
    )(a1, a2, wo, res, g1.reshape(1, -1), b1.reshape(1, -1), wg, wu, wd, g2.reshape(1, -1), b2.reshape(1, -1))


def _dft_constants(s_len):
    half = s_len // 2
    n = np.arange(half, dtype=np.int64)
    ang_s = 2.0 * np.pi * ((n[:, None] * n[None, :]) % s_len) / s_len
    c = np.arange(HEAD_DIM, dtype=np.int64)
    ang_c = 2.0 * np.pi * ((c[:, None] * c[None, :]) % HEAD_DIM) / HEAD_DIM
    norm = 1.0 / np.sqrt(float(s_len * HEAD_DIM))
    mirror = (n[:, None] + n[None, :] == half).astype(np.float64)
    return tuple(jnp.asarray(a, dtype=F32).astype(BF16)
                 for a in (np.cos(ang_s) * norm, np.sin(ang_s) * norm, mirror, np.cos(ang_c), np.sin(ang_c)))


def _fourier_kernel(u_ref, c1_ref, s1_ref, mirror_ref, cc_ref, sc_ref, o_ref, ae_scr, bo_scr, *, norm):
    n_grp, s_len, _ = u_ref.shape
    half = s_len // 2
    cc = cc_ref[...]
    sc = sc_ref[...]
    lanes = lambda parts: jnp.concatenate(parts, axis=1)

    u_low = lanes([u_ref[g, 0:half, :] for g in range(n_grp)])
    u_high = lanes([u_ref[g, half:, :] for g in range(n_grp)])
    u_mirror = jnp.dot(mirror_ref[...], u_high, preferred_element_type=F32)
    even = (u_low.astype(F32) + u_mirror).astype(BF16)
    odd = (u_low.astype(F32) - u_mirror).astype(BF16)

    row_sign = 1.0 - 2.0 * lax.bitwise_and(lax.broadcasted_iota(I32, (half, 1), 0), 1).astype(F32)
    pad_row = lax.broadcasted_iota(I32, (SUBLANES, 1), 0)
    mid_rows, alt_rows = [], []
    for g in range(n_grp):
        cols = slice(g * HEAD_DIM, (g + 1) * HEAD_DIM)
        ae_scr[:, cols] = jnp.dot(even[:, cols], cc, preferred_element_type=F32).astype(BF16)
        bo_scr[:, cols] = jnp.dot(odd[:, cols], sc, preferred_element_type=F32).astype(BF16)
        u_g = u_ref[g].astype(F32)
        alt = jnp.sum((u_g[0:half, :] + u_g[half:, :]) * row_sign, axis=0, keepdims=True)
        mid = u_g[half:half + 1, :]
        pad = jnp.where(pad_row == 0, mid, jnp.where(pad_row == 1, alt, 0.0)).astype(BF16)
        both = jnp.dot(pad, cc, preferred_element_type=F32) * norm
        mid_rows.append(both[0:1, :])
        alt_rows.append(both[1:2, :])
    mid_term = row_sign * lanes(mid_rows)
    p = jnp.dot(c1_ref[...], ae_scr[...], preferred_element_type=F32) + mid_term
    q = jnp.dot(s1_ref[...], bo_scr[...], preferred_element_type=F32)
    o_ref[0:half, :] = (p - q).astype(o_ref.dtype)
    upper = jnp.dot(mirror_ref[...], (p + q).astype(BF16), preferred_element_type=F32)
    is_row0 = lax.broadcasted_iota(I32, (half, 1), 0) == 0
    o_ref[half:, :] = jnp.where(is_row0, lanes(alt_rows), upper).astype(o_ref.dtype)


def _fourier_mix(u_heads):
    b_sz, _, _, s_len, _ = u_heads.shape
    n_grp = HEADS_PER_BLOCK
    half = s_len // 2
    consts = _dft_constants(s_len)
    const_spec = lambda n: pl.BlockSpec((n, n), lambda b: (0, 0), pipeline_mode=pl.Buffered(1))
    return pl.pallas_call(
        functools.partial(_fourier_kernel, norm=1.0 / np.sqrt(float(s_len * HEAD_DIM))),
        out_shape=jax.ShapeDtypeStruct((b_sz, s_len, n_grp * HEAD_DIM), BF16),
        grid=(b_sz,),
        in_specs=[pl.BlockSpec((None, n_grp, None, s_len, HEAD_DIM), lambda b: (b, 0, 0, 0, 0)),
                  const_spec(half), const_spec(half), const_spec(half), const_spec(HEAD_DIM), const_spec(HEAD_DIM)],
        out_specs=pl.BlockSpec((None, s_len, n_grp * HEAD_DIM), lambda b: (b, 0, 0)),
        scratch_shapes=[pltpu.VMEM((half, n_grp * HEAD_DIM), BF16), pltpu.VMEM((half, n_grp * HEAD_DIM), BF16)],
        compiler_params=_params("arbitrary"),
        name="fourier_mix",
    )(u_heads, *consts)


def _split_bf16(a):
    hi = a.astype(BF16)
    lo = (a - hi.astype(F32)).astype(BF16)
    return hi, lo


def _router_kernel(x_ref, w_ref, b_ref, route_ref, route_t_ref, cnt_ref, carry_ref):
    tq = route_ref.shape[0]

    @pl.when(pl.program_id(0) == 0)
    def _():
        carry_ref[...] = jnp.zeros_like(carry_ref)

    x_hi, x_lo = _split_bf16(_load_tiled_rows(x_ref))
    w_hi, w_lo = _split_bf16(w_ref[...])
    hi_terms = jnp.dot(x_hi, jnp.concatenate([w_hi, w_lo], axis=1), preferred_element_type=F32)
    logits = (hi_terms[:, 0:LANES] + hi_terms[:, LANES:]
              + jnp.dot(x_lo, w_hi, preferred_element_type=F32)) + b_ref[...]
    lane = lax.broadcasted_iota(I32, (tq, LANES), 1).astype(F32)
    logits = jnp.where(lane < N_EXPERTS, logits, NEG_INF)

    def top1(vals):
        m = jnp.max(vals, axis=-1, keepdims=True)
        idx = jnp.min(jnp.where(vals == m, lane, float(LANES)), axis=-1, keepdims=True)
        return m, idx

    m1, i1 = top1(logits)
    oh1 = lane == i1
    m2, i2 = top1(jnp.where(oh1, NEG_INF, logits))
    oh2 = lane == i2
    e2 = jnp.exp(m2 - m1)
    gate1 = 1.0 / (1.0 + e2)
    gate2 = e2 / (1.0 + e2)

    oh1f = oh1.astype(F32)
    oh2f = oh2.astype(F32)
    both = oh1f + oh2f
    tri = (lax.broadcasted_iota(I32, (tq, tq), 0) > lax.broadcasted_iota(I32, (tq, tq), 1)).astype(BF16)
    before = jnp.dot(tri, both.astype(BF16), preferred_element_type=F32) + carry_ref[0:1, :]
    rank1 = jnp.sum(oh1f * before, axis=-1, keepdims=True)
    rank2 = jnp.sum(oh2f * before, axis=-1, keepdims=True)
    carry_ref[...] = carry_ref[...] + jnp.sum(both, axis=0, keepdims=True)

    rec = jnp.zeros((tq, LANES), F32)
    for lane_idx, val in ((ROUTE_E, i1), (ROUTE_E + 1, i2),
                          (ROUTE_RANK, rank1), (ROUTE_RANK + 1, rank2),
                          (ROUTE_GATE, gate1), (ROUTE_GATE + 1, gate2)):
        rec = jnp.where(lane == lane_idx, val, rec)
    route_ref[...] = rec
    route_t_ref[...] = rec.T[0:route_t_ref.shape[0], :]
    cnt_ref[...] = carry_ref[...]


def _router(x_tiled, w, b, *, tq=512):
    t_len = x_tiled.shape[0] // SUBLANES
    d_model = SUBLANES * LANES
    w_pad = jnp.zeros((d_model, LANES), F32).at[:, :N_EXPERTS].set(w)
    b_pad = jnp.zeros((1, LANES), F32).at[0, :N_EXPERTS].set(b)
    return pl.pallas_call(
        _router_kernel,
        out_shape=(jax.ShapeDtypeStruct((t_len, LANES), F32), jax.ShapeDtypeStruct((SUBLANES, t_len), F32),
                   jax.ShapeDtypeStruct((8, LANES), F32)),
        grid=(t_len // tq,),
        in_specs=[
            pl.BlockSpec((tq * SUBLANES, LANES), lambda i: (i, 0)),
            pl.BlockSpec((d_model, LANES), lambda i: (0, 0)),
            pl.BlockSpec((1, LANES), lambda i: (0, 0)),
        ],
        out_specs=(pl.BlockSpec((tq, LANES), lambda i: (i, 0)), pl.BlockSpec((SUBLANES, tq), lambda i: (0, i)),
                   pl.BlockSpec((8, LANES), lambda i: (0, 0))),
        scratch_shapes=[pltpu.VMEM((8, LANES), F32)],
        compiler_params=_params("arbitrary"),
        name="router",
    )(x_tiled, w_pad, b_pad)


def _store_tiled_rows(dst_ref, val):
    n_rows, width = val.shape
    for j in range(width // LANES):
        dst_ref[pl.ds(j, n_rows, stride=width // LANES), :] = val[:, j * LANES:(j + 1) * LANES]


def _load_tiled_rows(src_ref):
    n_rows = src_ref.shape[0] // SUBLANES
    return jnp.concatenate([src_ref[pl.ds(j, n_rows, stride=SUBLANES), :] for j in range(SUBLANES)], axis=1)


def _tiled_row(ref, row):
    return ref.at[pl.ds(pl.multiple_of(row * SUBLANES, SUBLANES), SUBLANES)]


def _dispatch_kernel(fill_ref, pos0_ref, pos1_ref, x_ref, xs_hbm, zero_scr, sem, *, tm):
    tq = x_ref.shape[0] // SUBLANES
    pos_refs = (pos0_ref, pos1_ref)

    @pl.when(pl.program_id(0) == 0)
    def _():
        zero_scr[...] = jnp.zeros_like(zero_scr)

        def fill_copies(tile):
            copies = []
            for piece in range(tm // tq):
                row0 = pl.multiple_of((tile * tm + piece * tq) * SUBLANES, SUBLANES)
                copies.append(pltpu.make_async_copy(zero_scr, xs_hbm.at[pl.ds(row0, tq * SUBLANES)], sem))
            return copies

        def start_fill(tile, carry):
            @pl.when(fill_ref[tile] == 1)
            def _():
                for copy in fill_copies(tile):
                    copy.start()
            return carry

        def wait_fill(tile, carry):
            @pl.when(fill_ref[tile] == 1)
            def _():
                for copy in fill_copies(tile):
                    copy.wait()
            return carry

        lax.fori_loop(0, fill_ref.shape[0], start_fill, 0)
        lax.fori_loop(0, fill_ref.shape[0], wait_fill, 0)

    def start(t, carry):
        for k in range(TOP_K):
            pltpu.make_async_copy(_tiled_row(x_ref, t), _tiled_row(xs_hbm, pos_refs[k][t]), sem).start(priority=k)
        return carry

    lax.fori_loop(0, tq, start, 0, unroll=8)
    for k in range(TOP_K):
        pltpu.make_async_copy(x_ref, xs_hbm.at[pl.ds(0, tq * SUBLANES)], sem).wait()


def _dispatch(x_tiled, pos, fill_tiles, *, tm, tq=512):
    t_len = x_tiled.shape[0] // SUBLANES
    assert tm % tq == 0
    grid_spec = pltpu.PrefetchScalarGridSpec(
        num_scalar_prefetch=1,
        grid=(t_len // tq,),
        in_specs=[
            pl.BlockSpec((tq,), lambda i, fill: (i,), memory_space=pltpu.SMEM),
            pl.BlockSpec((tq,), lambda i, fill: (i,), memory_space=pltpu.SMEM),
            pl.BlockSpec((tq * SUBLANES, LANES), lambda i, fill: (i, 0)),
        ],
        out_specs=pl.BlockSpec(memory_space=pl.ANY),
        scratch_shapes=[pltpu.VMEM((tq * SUBLANES, LANES), x_tiled.dtype), pltpu.SemaphoreType.DMA],
    )
    return pl.pallas_call(
        functools.partial(_dispatch_kernel, tm=tm),
        out_shape=jax.ShapeDtypeStruct((fill_tiles.shape[0] * tm * SUBLANES, LANES), x_tiled.dtype),
        grid_spec=grid_spec,
        compiler_params=_params("arbitrary"),
        name="moe_dispatch",
    )(fill_tiles, pos[0], pos[1], x_tiled)


def _expert_kernel(tile_e_ref, nt_ref, nv_ref, x_ref, wg_ref, wu_ref, wd_ref, o_ref, x16_scr, acc_ref):
    del tile_e_ref, nt_ref
    c = pl.program_id(1)
    n_valid = nv_ref[pl.program_id(0)]
    half = x16_scr.shape[0] // 2

    @pl.when(c == 0)
    def _():
        acc_ref[...] = jnp.zeros_like(acc_ref)
        x16_scr[...] = _load_tiled_rows(x_ref).astype(BF16)

    def accumulate(rows):
        acc_ref[rows, :] += _swiglu_chunk(x16_scr[rows, :], wg_ref[...].astype(BF16), wu_ref[...].astype(BF16),
                                          wd_ref[...].astype(BF16))

    @pl.when(n_valid > half)
    def _():
        accumulate(pl.ds(0, 2 * half))

    @pl.when(jnp.logical_and(n_valid > 0, n_valid <= half))
    def _():
        accumulate(pl.ds(0, half))

    @pl.when(c == pl.num_programs(1) - 1)
    def _():
        _store_tiled_rows(o_ref, acc_ref[...])


def _expert_ffn(xs, tile_e, n_tiles, n_valid, wg, wu, wd, *, tm, tf):
    d_model = wg.shape[1]
    d_ff = wg.shape[2]
    n_chunks = d_ff // tf
    last = n_chunks - 1

    def row_map(i, c, te, nt, nv):
        return (jnp.minimum(i, nt[0] - 1), 0)

    def chunk_of(i, c, nt):
        return jnp.where(i < nt[0], c, last)

    grid_spec = pltpu.PrefetchScalarGridSpec(
        num_scalar_prefetch=3,
        grid=(xs.shape[0] // (tm * SUBLANES), n_chunks),
        in_specs=[
            pl.BlockSpec((tm * SUBLANES, LANES), row_map),
            pl.BlockSpec((None, d_model, tf), lambda i, c, te, nt, nv: (te[i], 0, chunk_of(i, c, nt))),
            pl.BlockSpec((None, d_model, tf), lambda i, c, te, nt, nv: (te[i], 0, chunk_of(i, c, nt))),
            pl.BlockSpec((None, tf, d_model), lambda i, c, te, nt, nv: (te[i], chunk_of(i, c, nt), 0)),
        ],
        out_specs=pl.BlockSpec((tm * SUBLANES, LANES), lambda i, c, te, nt, nv: (i, 0)),
        scratch_shapes=[pltpu.VMEM((tm, d_model), BF16), pltpu.VMEM((tm, d_model), F32)],
    )
    return pl.pallas_call(
        _expert_kernel,
        out_shape=jax.ShapeDtypeStruct(xs.shape, F32),
        grid_spec=grid_spec,
        compiler_params=_params("arbitrary", "arbitrary"),
        name="expert_ffn",
    )(tile_e, n_tiles, n_valid, xs, wg, wu, wd)


def _combine_kernel(pos0_ref, pos1_ref, pos0_next_ref, pos1_next_ref, res_ref, route_ref, g_ref, b_ref, ys_hbm,
                    o_ref, buf_ref, sem):
    i = pl.program_id(0)
    tq = o_ref.shape[0]
    slot = lax.bitwise_and(i, 1)

    def gather(positions, dst_slot):
        def start(t, carry):
            for k in range(TOP_K):
                pltpu.make_async_copy(_tiled_row(ys_hbm, positions[k][t]),
                                      _tiled_row(buf_ref.at[dst_slot, k], t), sem.at[dst_slot]).start(priority=k)
            return carry
        lax.fori_loop(0, tq, start, 0, unroll=8)

    @pl.when(i == 0)
    def _():
        gather((pos0_ref, pos1_ref), 0)

    @pl.when(i + 1 < pl.num_programs(0))
    def _():
        gather((pos0_next_ref, pos1_next_ref), 1 - slot)

    for k in range(TOP_K):
        pltpu.make_async_copy(ys_hbm.at[pl.ds(0, tq * SUBLANES)], buf_ref.at[slot, k], sem.at[slot]).wait()

    route = route_ref[...]
    y = (route[:, ROUTE_GATE:ROUTE_GATE + 1] * _load_tiled_rows(buf_ref.at[slot, 0])
         + route[:, ROUTE_GATE + 1:ROUTE_GATE + 2] * _load_tiled_rows(buf_ref.at[slot, 1]))
    o_ref[...] = _layer_norm(ALPHA * _load_tiled_rows(res_ref) + y, g_ref[...], b_ref[...])


def _combine_ln(ys, pos, res_tiled, route, gain, bias, *, tq=256):
    t_len = res_tiled.shape[0] // SUBLANES
    d_model = SUBLANES * LANES
    n_steps = t_len // tq
    this_tile = pl.BlockSpec((tq,), lambda i: (i,), memory_space=pltpu.SMEM)
    next_tile = pl.BlockSpec((tq,), lambda i: (jnp.minimum(i + 1, n_steps - 1),), memory_space=pltpu.SMEM)
    return pl.pallas_call(
        _combine_kernel,
        out_shape=jax.ShapeDtypeStruct((t_len, d_model), F32),
        grid=(n_steps,),
        in_specs=[
            this_tile, this_tile, next_tile, next_tile,
            pl.BlockSpec((tq * SUBLANES, LANES), lambda i: (i, 0)),
            pl.BlockSpec((tq, LANES), lambda i: (i, 0)),
            pl.BlockSpec((1, d_model), lambda i: (0, 0)),
            pl.BlockSpec((1, d_model), lambda i: (0, 0)),
            pl.BlockSpec(memory_space=pl.ANY),
        ],
        out_specs=pl.BlockSpec((tq, d_model), lambda i: (i, 0)),
        scratch_shapes=[pltpu.VMEM((2, TOP_K, tq * SUBLANES, LANES), F32), pltpu.SemaphoreType.DMA((2,))],
        compiler_params=_params("arbitrary"),
        name="moe_combine_ln",
    )(pos[0], pos[1], pos[0], pos[1], res_tiled, route, gain.reshape(1, -1), bias.reshape(1, -1), ys)


def _moe_ffn_ln(x_tiled, router_w, router_b, wg, wu, wd, gain, bias, *, tm=MOE_ROW_TILE, tf=MOE_FF_CHUNK):
    t_len = x_tiled.shape[0] // SUBLANES
    route, route_t, counts = _router(x_tiled, router_w, router_b)
    experts = route_t[ROUTE_E:ROUTE_E + TOP_K].astype(I32)
    ranks = route_t[ROUTE_RANK:ROUTE_RANK + TOP_K].astype(I32)
    counts = counts[0, :N_EXPERTS].astype(I32)
    tiles_per_expert = (counts + tm - 1) // tm
    tile_end = jnp.cumsum(tiles_per_expert)
    row_start = (tile_end - tiles_per_expert) * tm
    pos = ranks + sum(jnp.where(experts == e, row_start[e], 0) for e in range(N_EXPERTS))
    n_tiles_max = (TOP_K * t_len) // tm + N_EXPERTS
    n_tiles = tile_end[-1:]
    all_tiles = jnp.arange(n_tiles_max, dtype=I32)
    tile_ids = jnp.minimum(all_tiles, n_tiles[0] - 1)
    tile_e = jnp.sum((tile_ids[:, None] >= tile_end[None, :-1]).astype(I32), axis=1)
    is_last = jnp.any((all_tiles[:, None] == tile_end[None, :] - 1) & (tiles_per_expert[None, :] > 0), axis=1)
    fill_tiles = (is_last | (all_tiles >= n_tiles[0])).astype(I32)

    first_tile = (tile_end - tiles_per_expert)[tile_e]
    n_valid = jnp.where(all_tiles < n_tiles[0], jnp.clip(counts[tile_e] - (all_tiles - first_tile) * tm, 0, tm), 0)

    xs = _dispatch(x_tiled, pos, fill_tiles, tm=tm, tq=MOE_DISPATCH_TILE)
    ys = _expert_ffn(xs, tile_e, n_tiles.astype(I32), n_valid.astype(I32), wg, wu, wd, tm=tm, tf=tf)
    return _combine_ln(ys, pos, x_tiled, route, gain, bias)


def _alibi_slopes():
    exps = np.arange(1, N_ATTN_HEADS + 1, dtype=np.float32) * np.float32(8.0 / N_ATTN_HEADS)
    return jnp.asarray(np.exp2(-exps).astype(np.float32))


def kernel(x, mem, a_w_in, a_w_mem_kv, a_w_out, a_ln1_g, a_ln1_b, a_ffn_gate, a_ffn_up, a_ffn_down, a_ln2_g, a_ln2_b, b_w_in, b_fourier_g, b_w_mem_kv, b_w_out, b_ln1_g, b_ln1_b, b_router_w, b_router_b, b_moe_gate, b_moe_up, b_moe_down, b_ln2_g, b_ln2_b):
    b_sz, s_len, d_model = x.shape
    t_len = b_sz * s_len
    x32 = x.reshape(t_len, d_model)
    a_w_in, a_w_mem_kv, a_w_out, a_ffn_gate, a_ffn_up, a_ffn_down, b_w_in, b_w_mem_kv, b_w_out = (
        w[0].astype(BF16) for w in (a_w_in, a_w_mem_kv, a_w_out, a_ffn_gate, a_ffn_up, a_ffn_down,
                                    b_w_in, b_w_mem_kv, b_w_out))

    n_grp = len(DILATIONS)
    qkv0, x16 = _proj_heads(x, a_w_in, (0, n_grp, 2 * n_grp, 3 * n_grp))
    qkv = [qkv0] + [_proj_heads(x16, a_w_in, (g, n_grp + g, 2 * n_grp + g), dil=DILATIONS[g])
                    for g in range(1, n_grp)]
    mem_kv, mem16 = _proj_heads(mem, a_w_mem_kv, (0, 1))
    attn = _dilated_attention(_alibi_slopes(), qkv).reshape(t_len, -1)
    memo = _memory_attention(qkv[0], 3 * HEADS_PER_BLOCK, mem_kv).reshape(t_len, -1)
    x32, x16 = _mixer_ffn_ln(attn, memo, a_w_out, x32, a_ln1_g[0], a_ln1_b[0],
                             a_ffn_gate, a_ffn_up, a_ffn_down, a_ln2_g[0], a_ln2_b[0])

    x16_b = x16.reshape(b_sz, s_len, d_model)
    u_q = _proj_heads(x16_b, b_w_in, (0, 1), gain=b_fourier_g[0], ln_blocks=(True, False))
    mem_kv = _proj_heads(mem16, b_w_mem_kv, (0, 1))
    four = _fourier_mix(u_q).reshape(t_len, -1)
    memo = _memory_attention(u_q, HEADS_PER_BLOCK, mem_kv).reshape(t_len, -1)
    x_tiled = _out_proj_ln(four, memo, b_w_out, x32, b_ln1_g[0], b_ln1_b[0])
    out = _moe_ffn_ln(x_tiled, b_router_w[0], b_router_b[0], b_moe_gate[0], b_moe_up[0], b_moe_down[0],
                      b_ln2_g[0], b_ln2_b[0])
    return out.reshape(b_sz, s_len, d_model)
```

```python
import functools

import numpy as np
import jax
import jax.numpy as jnp
from jax import lax
from jax.experimental import pallas as pl
from jax.experimental.pallas import tpu as pltpu

F32 = jnp.float32
BF16 = jnp.bfloat16
I32 = jnp.int32

LANES = 128
HEAD_DIM = 128
HEADS_PER_BLOCK = 4
DILATIONS = (1, 4, 16)
N_SIDE = 64
ATTN_Q_BLOCK = 64
ATTN_K_WINDOW = 256
ATTN_BATCH = 32
PROJ_MAX_ROW_STRIDE = 4
N_ATTN_HEADS = 12
N_EXPERTS = 8
TOP_K = 2
MOE_ROW_TILE = 1024
MOE_FF_CHUNK = 512
MOE_DISPATCH_TILE = 1024
EXPERT_ROW_PARTS = 4
SUBLANES = 8
ALPHA = (2.0 * 2) ** 0.25
LN_EPS = 1e-5
NEG_INF = -1e30
VMEM_LIMIT_BYTES = 56 * 1024 * 1024

ROUTE_E, ROUTE_RANK, ROUTE_GATE = 0, 2, 4


def _params(*sem):
    return pltpu.CompilerParams(dimension_semantics=sem, vmem_limit_bytes=VMEM_LIMIT_BYTES)


def _layer_norm(z, g, b=None):
    mu = jnp.mean(z, axis=-1, keepdims=True)
    zc = z - mu
    var = jnp.mean(zc * zc, axis=-1, keepdims=True)
    y = zc * lax.rsqrt(var + LN_EPS) * g
    return y if b is None else y + b


def _proj_kernel(*refs, dil, row_chunk, ln_blocks, cast_x):
    n_blocks = len(ln_blocks)
    x_ref, w_refs = refs[0], refs[1:1 + n_blocks]
    rest = refs[1 + n_blocks:]
    g_ref = None
    if any(ln_blocks):
        g_ref, rest = rest[0], rest[1:]
    o_ref, scratch = rest[0], rest[1:]
    if cast_x:
        x16_ref, scratch = scratch[0], scratch[1:]
        x16_ref[...] = x_ref[...].astype(BF16)
        x_ref = x16_ref
    s_len = x_ref.shape[0]
    for blk in range(n_blocks):
        w = w_refs[blk][...]
        acc_ref = scratch[blk % 2] if dil > 1 else None
        for rc in range(s_len // row_chunk):
            rows = pl.ds(rc * row_chunk, row_chunk)
            r = jnp.dot(x_ref[rows, :], w, preferred_element_type=F32)
            for hh in range(HEADS_PER_BLOCK):
                rh = r[:, hh * HEAD_DIM:(hh + 1) * HEAD_DIM]
                if ln_blocks[blk]:
                    rh = _layer_norm(rh, g_ref[hh:hh + 1, :])
                if dil == 1:
                    o_ref[blk * HEADS_PER_BLOCK + hh, 0, rows, :] = rh.astype(o_ref.dtype)
                else:
                    acc_ref[hh, rows, :] = rh
        if dil == 1:
            continue
        for hh in range(HEADS_PER_BLOCK):
            head = blk * HEADS_PER_BLOCK + hh
            if dil <= PROJ_MAX_ROW_STRIDE:
                for r_ in range(dil):
                    o_ref[head, r_, :, :] = acc_ref.at[hh][pl.ds(r_, s_len // dil, stride=dil), :].astype(o_ref.dtype)
            else:
                s1, s2 = PROJ_MAX_ROW_STRIDE, dil // PROJ_MAX_ROW_STRIDE
                tmp_ref = scratch[2]
                for a in range(s1):
                    tmp_ref[a] = acc_ref.at[hh][pl.ds(a, s_len // s1, stride=s1), :]
                for a in range(s1):
                    for b in range(s2):
                        o_ref[head, s1 * b + a, :, :] = tmp_ref.at[a][pl.ds(b, s_len // dil, stride=s2), :].astype(
                            o_ref.dtype)


def _proj_heads(x, w, col_blocks, *, dil=1, gain=None, ln_blocks=None):
    b_sz, s_len, d_model = x.shape
    cast_x = x.dtype != BF16
    blk_cols = HEADS_PER_BLOCK * HEAD_DIM
    n_blocks = len(col_blocks)
    ln_blocks = tuple(ln_blocks) if ln_blocks is not None else (False,) * n_blocks
    in_specs = [pl.BlockSpec((None, s_len, d_model), lambda b: (b, 0, 0))]
    in_specs += [pl.BlockSpec((d_model, blk_cols), lambda b, cb=cb: (0, cb), pipeline_mode=pl.Buffered(1))
                 for cb in col_blocks]
    args = [x] + [w] * n_blocks
    if any(ln_blocks):
        in_specs.append(pl.BlockSpec((HEADS_PER_BLOCK, HEAD_DIM), lambda b: (0, 0)))
        args.append(gain)
    scratch = []
    if dil > 1:
        scratch += [pltpu.VMEM((HEADS_PER_BLOCK, s_len, HEAD_DIM), F32)] * 2
    if dil > PROJ_MAX_ROW_STRIDE:
        scratch.append(pltpu.VMEM((PROJ_MAX_ROW_STRIDE, s_len // PROJ_MAX_ROW_STRIDE, HEAD_DIM), F32))
    n_heads = n_blocks * HEADS_PER_BLOCK
    out_shape = jax.ShapeDtypeStruct((b_sz, n_heads, dil, s_len // dil, HEAD_DIM), BF16)
    out_specs = pl.BlockSpec((None, n_heads, dil, s_len // dil, HEAD_DIM), lambda b: (b, 0, 0, 0, 0))
    if cast_x:
        out_shape = (out_shape, jax.ShapeDtypeStruct(x.shape, BF16))
        out_specs = (out_specs, pl.BlockSpec((None, s_len, d_model), lambda b: (b, 0, 0)))
    return pl.pallas_call(
        functools.partial(_proj_kernel, dil=dil, row_chunk=min(512, s_len), ln_blocks=ln_blocks, cast_x=cast_x),
        out_shape=out_shape,
        grid=(b_sz,),
        in_specs=in_specs,
        out_specs=out_specs,
        scratch_shapes=scratch,
        compiler_params=_params("arbitrary"),
        name=f"proj_heads_d{dil}",
    )(*args)


def _dil_attn_kernel(slopes_ref, q0, k0, v0, q1, k1, v1, q2, k2, v2, o_ref, o_scr, l_scr, bias_scr):
    s_len = o_ref.shape[0]
    h = pl.program_id(1)
    scale = HEAD_DIM ** -0.5
    qb = ATTN_Q_BLOCK
    groups = ((q0, k0, v0), (q1, k1, v1), (q2, k2, v2))
    for g, (qr, kr, vr) in enumerate(groups):
        dil = DILATIONS[g]
        cls_len = s_len // dil
        kw = min(ATTN_K_WINDOW, cls_len)
        nb = cls_len // qb
        nb_shift = nb.bit_length() - 1

        neg_slope = jnp.full((1, kw), slopes_ref[g * HEADS_PER_BLOCK + h], F32) * (-float(dil))
        base_delta = (lax.broadcasted_iota(I32, (qb, kw), 1) - lax.broadcasted_iota(I32, (qb, kw), 0))
        for case in range(kw // N_SIDE):
            dist = jnp.abs(base_delta - case * N_SIDE)
            bias_scr[g, case, :, 0:kw] = jnp.where(dist <= N_SIDE, dist.astype(F32) * neg_slope, NEG_INF)

        def body(it, carry, qr=qr, kr=kr, vr=vr, dil=dil, cls_len=cls_len, kw=kw, nb=nb, g=g,
                 nb_shift=nb_shift):
            blocks = []
            for j in range(ATTN_BATCH):
                idx = it * ATTN_BATCH + j
                r = lax.shift_right_logical(idx, nb_shift)
                i = lax.bitwise_and(idx, nb - 1)
                qs = pl.multiple_of(i * qb, qb)
                ks = pl.multiple_of(jnp.clip(qs - N_SIDE, 0, cls_len - kw), N_SIDE)
                blocks.append((r, qs, ks))
            scores = []
            for r, qs, ks in blocks:
                q = qr[r, pl.ds(qs, qb), :]
                k = kr[r, pl.ds(ks, kw), :]
                scores.append(lax.dot_general(q, k, (((1,), (1,)), ((), ())), preferred_element_type=F32))
            probs = []
            for (r, qs, ks), s in zip(blocks, scores):
                bias = bias_scr[g, lax.shift_right_logical(qs - ks, N_SIDE.bit_length() - 1), :, 0:kw]
                s = s * scale + bias
                m = jnp.max(s, axis=-1, keepdims=True)
                p = jnp.exp(s - m)
                l = jnp.sum(p, axis=-1, keepdims=True)
                probs.append((p.astype(BF16), m, l))
            for (r, qs, ks), (p, m, l) in zip(blocks, probs):
                v = vr[r, pl.ds(ks, kw), :]
                o = jnp.dot(p, v, preferred_element_type=F32) / l
                lse = jnp.broadcast_to(m + jnp.log(l), (qb, LANES))
                row0 = qs * dil + r
                if dil == 1:
                    o_scr[g, pl.ds(row0, qb), :] = o
                    l_scr[g, pl.ds(row0, qb), :] = lse
                else:
                    o_scr.at[g][pl.ds(row0, qb, stride=dil), :] = o
                    l_scr.at[g][pl.ds(row0, qb, stride=dil), :] = lse
            return carry

        lax.fori_loop(0, dil * nb // ATTN_BATCH, body, 0)

    chunk = 256
    for c in range(s_len // chunk):
        rows = pl.ds(c * chunk, chunk)
        l0, l1, l2 = l_scr[0, rows, :], l_scr[1, rows, :], l_scr[2, rows, :]
        m = jnp.maximum(jnp.maximum(l0, l1), l2)
        e0, e1, e2 = jnp.exp(l0 - m), jnp.exp(l1 - m), jnp.exp(l2 - m)
        num = e0 * o_scr[0, rows, :] + e1 * o_scr[1, rows, :] + e2 * o_scr[2, rows, :]
        o_ref[rows, :] = (num / (e0 + e1 + e2)).astype(o_ref.dtype)


def _dilated_attention(slopes, qkv):
    b_sz = qkv[0].shape[0]
    s_len = qkv[0].shape[2] * qkv[0].shape[3]
    in_specs = [pl.BlockSpec(memory_space=pltpu.SMEM)]
    args = [slopes]
    for g, arr in enumerate(qkv):
        dil = DILATIONS[g]
        for which in range(3):
            in_specs.append(pl.BlockSpec((None, None, dil, s_len // dil, HEAD_DIM),
                                         lambda b, h, which=which: (b, which * HEADS_PER_BLOCK + h, 0, 0, 0)))
            args.append(arr)
    return pl.pallas_call(
        _dil_attn_kernel,
        out_shape=jax.ShapeDtypeStruct((b_sz, s_len, HEADS_PER_BLOCK * HEAD_DIM), BF16),
        grid=(b_sz, HEADS_PER_BLOCK),
        in_specs=in_specs,
        out_specs=pl.BlockSpec((None, s_len, HEAD_DIM), lambda b, h: (b, 0, h)),
        scratch_shapes=[pltpu.VMEM((3, s_len, HEAD_DIM), F32), pltpu.VMEM((3, s_len, LANES), F32),
                        pltpu.VMEM((3, ATTN_K_WINDOW // N_SIDE, ATTN_Q_BLOCK, ATTN_K_WINDOW), F32)],
        compiler_params=_params("arbitrary", "arbitrary"),
        name="dilated_attention",
    )(*args)


def _mem_attn_kernel(q_ref, kv_ref, o_ref):
    n_heads, s_len, _ = q_ref.shape
    scale = HEAD_DIM ** -0.5
    chunk = 256
    row_chunks = [pl.ds(c * chunk, chunk) for c in range(s_len // chunk)]
    for h in range(n_heads):
        k = kv_ref[h]
        v = kv_ref[n_heads + h]
        scores = [lax.dot_general(q_ref[h, rows, :], k, (((1,), (1,)), ((), ())), preferred_element_type=F32)
                  for rows in row_chunks]
        probs = []
        for s in scores:
            s = s * scale
            m = jnp.max(s, axis=-1, keepdims=True)
            p = jnp.exp(s - m)
            probs.append((p.astype(BF16), jnp.sum(p, axis=-1, keepdims=True)))
        for rows, (p, l) in zip(row_chunks, probs):
            o_ref[rows, h * HEAD_DIM:(h + 1) * HEAD_DIM] = (
                jnp.dot(p, v, preferred_element_type=F32) / l).astype(o_ref.dtype)


def _memory_attention(q_heads, q_head0, kv_heads):
    b_sz, _, _, s_len, _ = q_heads.shape
    n_mem = kv_heads.shape[3]
    assert q_head0 % HEADS_PER_BLOCK == 0
    return pl.pallas_call(
        _mem_attn_kernel,
        out_shape=jax.ShapeDtypeStruct((b_sz, s_len, HEADS_PER_BLOCK * HEAD_DIM), BF16),
        grid=(b_sz,),
        in_specs=[
            pl.BlockSpec((None, HEADS_PER_BLOCK, None, s_len, HEAD_DIM),
                         lambda b: (b, q_head0 // HEADS_PER_BLOCK, 0, 0, 0)),
            pl.BlockSpec((None, 2 * HEADS_PER_BLOCK, None, n_mem, HEAD_DIM), lambda b: (b, 0, 0, 0, 0)),
        ],
        out_specs=pl.BlockSpec((None, s_len, HEADS_PER_BLOCK * HEAD_DIM), lambda b: (b, 0, 0)),
        compiler_params=_params("arbitrary"),
        name="memory_attention",
    )(q_heads, kv_heads)


def _out_ln_kernel(a1_ref, a2_ref, w_ref, res_ref, g_ref, b_ref, o_ref):
    k1 = a1_ref.shape[1]
    y = jnp.dot(a1_ref[...], w_ref[0:k1, :], preferred_element_type=F32)
    y = y + jnp.dot(a2_ref[...], w_ref[k1:, :], preferred_element_type=F32)
    _store_tiled_rows(o_ref, _layer_norm(ALPHA * res_ref[...] + y, g_ref[...], b_ref[...]))


def _out_proj_ln(a1, a2, w, res, gain, bias, *, tm=512):
    t_len, d_model = res.shape
    assert d_model == SUBLANES * LANES
    k1, k2 = a1.shape[1], a2.shape[1]
    return pl.pallas_call(
        _out_ln_kernel,
        out_shape=jax.ShapeDtypeStruct((t_len * SUBLANES, LANES), F32),
        grid=(t_len // tm,),
        in_specs=[
            pl.BlockSpec((tm, k1), lambda i: (i, 0)),
            pl.BlockSpec((tm, k2), lambda i: (i, 0)),
            pl.BlockSpec((k1 + k2, d_model), lambda i: (0, 0), pipeline_mode=pl.Buffered(1)),
            pl.BlockSpec((tm, d_model), lambda i: (i, 0)),
            pl.BlockSpec((1, d_model), lambda i: (0, 0)),
            pl.BlockSpec((1, d_model), lambda i: (0, 0)),
        ],
        out_specs=pl.BlockSpec((tm * SUBLANES, LANES), lambda i: (i, 0)),
        compiler_params=_params("arbitrary"),
        name="out_proj_ln",
    )(a1, a2, w, res, gain.reshape(1, -1), bias.reshape(1, -1))


def _swiglu_chunk(x, wg, wu, wd):
    hg = jnp.dot(x, wg, preferred_element_type=F32)
    hu = jnp.dot(x, wu, preferred_element_type=F32)
    h = hg * (1.0 / (1.0 + jnp.exp(-hg))) * hu
    return jnp.dot(h.astype(BF16), wd, preferred_element_type=F32)


def _mixer_ffn_kernel(a1_ref, a2_ref, wo_ref, res_ref, g1_ref, b1_ref, wg_ref, wu_ref, wd_ref, g2_ref, b2_ref,
                      o32_ref, o16_ref, *, tf):
    k1 = a1_ref.shape[1]
    tm = res_ref.shape[0]
    halves = [pl.ds(hf * (tm // 2), tm // 2) for hf in range(2)]
    ys = [jnp.dot(a1_ref[rows, :], wo_ref[0:k1, :], preferred_element_type=F32)
          + jnp.dot(a2_ref[rows, :], wo_ref[k1:, :], preferred_element_type=F32) for rows in halves]
    x1s = [_layer_norm(ALPHA * res_ref[rows, :] + y, g1_ref[...], b1_ref[...]) for rows, y in zip(halves, ys)]
    x16s = [x1.astype(BF16) for x1 in x1s]
    accs = [None, None]
    for f in range(wg_ref.shape[1] // tf):
        cols = pl.ds(f * tf, tf)
        for hf in range(2):
            part = _swiglu_chunk(x16s[hf], wg_ref[:, cols], wu_ref[:, cols], wd_ref[cols, :])
            accs[hf] = part if accs[hf] is None else accs[hf] + part
    for rows, x1, acc in zip(halves, x1s, accs):
        z = _layer_norm(ALPHA * x1 + acc, g2_ref[...], b2_ref[...])
        o32_ref[rows, :] = z
        o16_ref[rows, :] = z.astype(BF16)


def _mixer_ffn_ln(a1, a2, wo, res, g1, b1, wg, wu, wd, g2, b2, *, tm=1024, tf=256):
    t_len, d_model = res.shape
    k1, k2 = a1.shape[1], a2.shape[1]
    d_ff = wg.shape[1]
    resident = dict(pipeline_mode=pl.Buffered(1))
    vec = pl.BlockSpec((1, d_model), lambda i: (0, 0))
    return pl.pallas_call(
        functools.partial(_mixer_ffn_kernel, tf=tf),
        out_shape=(jax.ShapeDtypeStruct((t_len, d_model), F32), jax.ShapeDtypeStruct((t_len, d_model), BF16)),
        grid=(t_len // tm,),
        in_specs=[
            pl.BlockSpec((tm, k1), lambda i: (i, 0)),
            pl.BlockSpec((tm, k2), lambda i: (i, 0)),
            pl.BlockSpec((k1 + k2, d_model), lambda i: (0, 0), **resident),
            pl.BlockSpec((tm, d_model), lambda i: (i, 0)),
            vec, vec,
            pl.BlockSpec((d_model, d_ff), lambda i: (0, 0), **resident),
            pl.BlockSpec((d_model, d_ff), lambda i: (0, 0), **resident),
            pl.BlockSpec((d_ff, d_model), lambda i: (0, 0), **resident),
            vec, vec,
        ],
        out_specs=(pl.BlockSpec((tm, d_model), lambda i: (i, 0)),
                   pl.BlockSpec((tm, d_model), lambda i: (i, 0))),
        compiler_params=_params("arbitrary"),
        name="mixer_ffn_ln",
    )(a1, a2, wo, res, g1.reshape(1, -1), b1.reshape(1, -1), wg, wu, wd, g2.reshape(1, -1), b2.reshape(1, -1))


def _dft_constants(s_len):
    half = s_len // 2
    n = np.arange(half, dtype=np.int64)
    ang_s = 2.0 * np.pi * ((n[:, None] * n[None, :]) % s_len) / s_len
    c = np.arange(HEAD_DIM, dtype=np.int64)
    ang_c = 2.0 * np.pi * ((c[:, None] * c[None, :]) % HEAD_DIM) / HEAD_DIM
    norm = 1.0 / np.sqrt(float(s_len * HEAD_DIM))
    mirror = (n[:, None] + n[None, :] == half).astype(np.float64)
    return tuple(jnp.asarray(a, dtype=F32).astype(BF16)
                 for a in (np.cos(ang_s) * norm, np.sin(ang_s) * norm, mirror, np.cos(ang_c), np.sin(ang_c)))


def _fourier_kernel(u_ref, c1_ref, s1_ref, mirror_ref, cc_ref, sc_ref, o_ref, ae_scr, bo_scr, *, norm):
    n_grp, s_len, _ = u_ref.shape
    half = s_len // 2
    cc = cc_ref[...]
    sc = sc_ref[...]
    lanes = lambda parts: jnp.concatenate(parts, axis=1)

    u_low = lanes([u_ref[g, 0:half, :] for g in range(n_grp)])
    u_high = lanes([u_ref[g, half:, :] for g in range(n_grp)])
    u_mirror = jnp.dot(mirror_ref[...], u_high, preferred_element_type=F32)
    even = (u_low.astype(F32) + u_mirror).astype(BF16)
    odd = (u_low.astype(F32) - u_mirror).astype(BF16)

    row_sign = 1.0 - 2.0 * lax.bitwise_and(lax.broadcasted_iota(I32, (half, 1), 0), 1).astype(F32)
    pad_row = lax.broadcasted_iota(I32, (SUBLANES, 1), 0)
    mid_rows, alt_rows = [], []
    for g in range(n_grp):
        cols = slice(g * HEAD_DIM, (g + 1) * HEAD_DIM)
        ae_scr[:, cols] = jnp.dot(even[:, cols], cc, preferred_element_type=F32).astype(BF16)
        bo_scr[:, cols] = jnp.dot(odd[:, cols], sc, preferred_element_type=F32).astype(BF16)
        u_g = u_ref[g].astype(F32)
        alt = jnp.sum((u_g[0:half, :] + u_g[half:, :]) * row_sign, axis=0, keepdims=True)
        mid = u_g[half:half + 1, :]
        pad = jnp.where(pad_row == 0, mid, jnp.where(pad_row == 1, alt, 0.0)).astype(BF16)
        both = jnp.dot(pad, cc, preferred_element_type=F32) * norm
        mid_rows.append(both[0:1, :])
        alt_rows.append(both[1:2, :])
    mid_term = row_sign * lanes(mid_rows)
    p = jnp.dot(c1_ref[...], ae_scr[...], preferred_element_type=F32) + mid_term
    q = jnp.dot(s1_ref[...], bo_scr[...], preferred_element_type=F32)
    o_ref[0:half, :] = (p - q).astype(o_ref.dtype)
    upper = jnp.dot(mirror_ref[...], (p + q).astype(BF16), preferred_element_type=F32)
    is_row0 = lax.broadcasted_iota(I32, (half, 1), 0) == 0
    o_ref[half:, :] = jnp.where(is_row0, lanes(alt_rows), upper).astype(o_ref.dtype)


def _fourier_mix(u_heads):
    b_sz, _, _, s_len, _ = u_heads.shape
    n_grp = HEADS_PER_BLOCK
    half = s_len // 2
    consts = _dft_constants(s_len)
    const_spec = lambda n: pl.BlockSpec((n, n), lambda b: (0, 0), pipeline_mode=pl.Buffered(1))
    return pl.pallas_call(
        functools.partial(_fourier_kernel, norm=1.0 / np.sqrt(float(s_len * HEAD_DIM))),
        out_shape=jax.ShapeDtypeStruct((b_sz, s_len, n_grp * HEAD_DIM), BF16),
        grid=(b_sz,),
        in_specs=[pl.BlockSpec((None, n_grp, None, s_len, HEAD_DIM), lambda b: (b, 0, 0, 0, 0)),
                  const_spec(half), const_spec(half), const_spec(half), const_spec(HEAD_DIM), const_spec(HEAD_DIM)],
        out_specs=pl.BlockSpec((None, s_len, n_grp * HEAD_DIM), lambda b: (b, 0, 0)),
        scratch_shapes=[pltpu.VMEM((half, n_grp * HEAD_DIM), BF16), pltpu.VMEM((half, n_grp * HEAD_DIM), BF16)],
        compiler_params=_params("arbitrary"),
        name="fourier_mix",
    )(u_heads, *consts)


def _split_bf16(a):
    hi = a.astype(BF16)
    lo = (a - hi.astype(F32)).astype(BF16)
    return hi, lo


def _router_kernel(x_ref, w_ref, b_ref, route_ref, route_t_ref, cnt_ref, carry_ref):
    tq = route_ref.shape[0]

    @pl.when(pl.program_id(0) == 0)
    def _():
        carry_ref[...] = jnp.zeros_like(carry_ref)

    x_hi, x_lo = _split_bf16(_load_tiled_rows(x_ref))
    w_hi, w_lo = _split_bf16(w_ref[...])
    hi_terms = jnp.dot(x_hi, jnp.concatenate([w_hi, w_lo], axis=1), preferred_element_type=F32)
    logits = (hi_terms[:, 0:LANES] + hi_terms[:, LANES:]
              + jnp.dot(x_lo, w_hi, preferred_element_type=F32)) + b_ref[...]
    lane = lax.broadcasted_iota(I32, (tq, LANES), 1).astype(F32)
    logits = jnp.where(lane < N_EXPERTS, logits, NEG_INF)

    def top1(vals):
        m = jnp.max(vals, axis=-1, keepdims=True)
        idx = jnp.min(jnp.where(vals == m, lane, float(LANES)), axis=-1, keepdims=True)
        return m, idx

    m1, i1 = top1(logits)
    oh1 = lane == i1
    m2, i2 = top1(jnp.where(oh1, NEG_INF, logits))
    oh2 = lane == i2
    e2 = jnp.exp(m2 - m1)
    gate1 = 1.0 / (1.0 + e2)
    gate2 = e2 / (1.0 + e2)

    oh1f = oh1.astype(F32)
    oh2f = oh2.astype(F32)
    both = oh1f + oh2f
    tri = (lax.broadcasted_iota(I32, (tq, tq), 0) > lax.broadcasted_iota(I32, (tq, tq), 1)).astype(BF16)
    before = jnp.dot(tri, both.astype(BF16), preferred_element_type=F32) + carry_ref[0:1, :]
    rank1 = jnp.sum(oh1f * before, axis=-1, keepdims=True)
    rank2 = jnp.sum(oh2f * before, axis=-1, keepdims=True)
    carry_ref[...] = carry_ref[...] + jnp.sum(both, axis=0, keepdims=True)

    rec = jnp.zeros((tq, LANES), F32)
    for lane_idx, val in ((ROUTE_E, i1), (ROUTE_E + 1, i2),
                          (ROUTE_RANK, rank1), (ROUTE_RANK + 1, rank2),
                          (ROUTE_GATE, gate1), (ROUTE_GATE + 1, gate2)):
        rec = jnp.where(lane == lane_idx, val, rec)
    route_ref[...] = rec
    route_t_ref[...] = rec.T[0:route_t_ref.shape[0], :]
    cnt_ref[...] = carry_ref[...]


def _router(x_tiled, w, b, *, tq=512):
    t_len = x_tiled.shape[0] // SUBLANES
    d_model = SUBLANES * LANES
    w_pad = jnp.zeros((d_model, LANES), F32).at[:, :N_EXPERTS].set(w)
    b_pad = jnp.zeros((1, LANES), F32).at[0, :N_EXPERTS].set(b)
    return pl.pallas_call(
        _router_kernel,
        out_shape=(jax.ShapeDtypeStruct((t_len, LANES), F32), jax.ShapeDtypeStruct((SUBLANES, t_len), F32),
                   jax.ShapeDtypeStruct((8, LANES), F32)),
        grid=(t_len // tq,),
        in_specs=[
            pl.BlockSpec((tq * SUBLANES, LANES), lambda i: (i, 0)),
            pl.BlockSpec((d_model, LANES), lambda i: (0, 0)),
            pl.BlockSpec((1, LANES), lambda i: (0, 0)),
        ],
        out_specs=(pl.BlockSpec((tq, LANES), lambda i: (i, 0)), pl.BlockSpec((SUBLANES, tq), lambda i: (0, i)),
                   pl.BlockSpec((8, LANES), lambda i: (0, 0))),
        scratch_shapes=[pltpu.VMEM((8, LANES), F32)],
        compiler_params=_params("arbitrary"),
        name="router",
    )(x_tiled, w_pad, b_pad)


def _store_tiled_rows(dst_ref, val):
    n_rows, width = val.shape
    for j in range(width // LANES):
        dst_ref[pl.ds(j, n_rows, stride=width // LANES), :] = val[:, j * LANES:(j + 1) * LANES]


def _load_tiled_rows(src_ref):
    n_rows = src_ref.shape[0] // SUBLANES
    return jnp.concatenate([src_ref[pl.ds(j, n_rows, stride=SUBLANES), :] for j in range(SUBLANES)], axis=1)


def _tiled_row(ref, row):
    return ref.at[pl.ds(pl.multiple_of(row * SUBLANES, SUBLANES), SUBLANES)]


def _dispatch_kernel(fill_ref, pos0_ref, pos1_ref, x_ref, xs_hbm, zero_scr, sem, *, tm):
    tq = x_ref.shape[0] // SUBLANES
    pos_refs = (pos0_ref, pos1_ref)

    @pl.when(pl.program_id(0) == 0)
    def _():
        zero_scr[...] = jnp.zeros_like(zero_scr)

        def fill_copies(tile):
            copies = []
            for piece in range(tm // tq):
                row0 = pl.multiple_of((tile * tm + piece * tq) * SUBLANES, SUBLANES)
                copies.append(pltpu.make_async_copy(zero_scr, xs_hbm.at[pl.ds(row0, tq * SUBLANES)], sem))
            return copies

        def start_fill(tile, carry):
            @pl.when(fill_ref[tile] == 1)
            def _():
                for copy in fill_copies(tile):
                    copy.start()
            return carry

        def wait_fill(tile, carry):
            @pl.when(fill_ref[tile] == 1)
            def _():
                for copy in fill_copies(tile):
                    copy.wait()
            return carry

        lax.fori_loop(0, fill_ref.shape[0], start_fill, 0)
        lax.fori_loop(0, fill_ref.shape[0], wait_fill, 0)

    def start(t, carry):
        for k in range(TOP_K):
            pltpu.make_async_copy(_tiled_row(x_ref, t), _tiled_row(xs_hbm, pos_refs[k][t]), sem).start(priority=k)
        return carry

    lax.fori_loop(0, tq, start, 0, unroll=8)
    for k in range(TOP_K):
        pltpu.make_async_copy(x_ref, xs_hbm.at[pl.ds(0, tq * SUBLANES)], sem).wait()


def _dispatch(x_tiled, pos, fill_tiles, *, tm, tq=512):
    t_len = x_tiled.shape[0] // SUBLANES
    assert tm % tq == 0
    grid_spec = pltpu.PrefetchScalarGridSpec(
        num_scalar_prefetch=1,
        grid=(t_len // tq,),
        in_specs=[
            pl.BlockSpec((tq,), lambda i, fill: (i,), memory_space=pltpu.SMEM),
            pl.BlockSpec((tq,), lambda i, fill: (i,), memory_space=pltpu.SMEM),
            pl.BlockSpec((tq * SUBLANES, LANES), lambda i, fill: (i, 0)),
        ],
        out_specs=pl.BlockSpec(memory_space=pl.ANY),
        scratch_shapes=[pltpu.VMEM((tq * SUBLANES, LANES), x_tiled.dtype), pltpu.SemaphoreType.DMA],
    )
    return pl.pallas_call(
        functools.partial(_dispatch_kernel, tm=tm),
        out_shape=jax.ShapeDtypeStruct((fill_tiles.shape[0] * tm * SUBLANES, LANES), x_tiled.dtype),
        grid_spec=grid_spec,
        compiler_params=_params("arbitrary"),
        name="moe_dispatch",
    )(fill_tiles, pos[0], pos[1], x_tiled)


def _expert_kernel(tile_e_ref, nt_ref, nv_ref, x_ref, wg_ref, wu_ref, wd_ref, o_ref, x16_scr, acc_ref):
    del tile_e_ref, nt_ref
    c = pl.program_id(1)
    n_valid = nv_ref[pl.program_id(0)]
    part = x16_scr.shape[0] // EXPERT_ROW_PARTS

    @pl.when(c == 0)
    def _():
        acc_ref[...] = jnp.zeros_like(acc_ref)
        x16_scr[...] = _load_tiled_rows(x_ref).astype(BF16)

    def accumulate(rows):
        acc_ref[rows, :] += _swiglu_chunk(x16_scr[rows, :], wg_ref[...].astype(BF16), wu_ref[...].astype(BF16),
                                          wd_ref[...].astype(BF16))

    for n_parts in range(1, EXPERT_ROW_PARTS + 1):
        @pl.when(jnp.logical_and(n_valid > (n_parts - 1) * part, n_valid <= n_parts * part))
        def _(n_parts=n_parts):
            accumulate(pl.ds(0, n_parts * part))

    @pl.when(c == pl.num_programs(1) - 1)
    def _():
        _store_tiled_rows(o_ref, acc_ref[...])


def _expert_ffn(xs, tile_e, n_tiles, n_valid, wg, wu, wd, *, tm, tf):
    d_model = wg.shape[1]
    d_ff = wg.shape[2]
    n_chunks = d_ff // tf
    last = n_chunks - 1

    def row_map(i, c, te, nt, nv):
        return (jnp.minimum(i, nt[0] - 1), 0)

    def chunk_of(i, c, nt):
        return jnp.where(i < nt[0], c, last)

    grid_spec = pltpu.PrefetchScalarGridSpec(
        num_scalar_prefetch=3,
        grid=(xs.shape[0] // (tm * SUBLANES), n_chunks),
        in_specs=[
            pl.BlockSpec((tm * SUBLANES, LANES), row_map),
            pl.BlockSpec((None, d_model, tf), lambda i, c, te, nt, nv: (te[i], 0, chunk_of(i, c, nt))),
            pl.BlockSpec((None, d_model, tf), lambda i, c, te, nt, nv: (te[i], 0, chunk_of(i, c, nt))),
            pl.BlockSpec((None, tf, d_model), lambda i, c, te, nt, nv: (te[i], chunk_of(i, c, nt), 0)),
        ],
        out_specs=pl.BlockSpec((tm * SUBLANES, LANES), lambda i, c, te, nt, nv: (i, 0)),
        scratch_shapes=[pltpu.VMEM((tm, d_model), BF16), pltpu.VMEM((tm, d_model), F32)],
    )
    return pl.pallas_call(
        _expert_kernel,
        out_shape=jax.ShapeDtypeStruct(xs.shape, F32),
        grid_spec=grid_spec,
        compiler_params=_params("arbitrary", "arbitrary"),
        name="expert_ffn",
    )(tile_e, n_tiles, n_valid, xs, wg, wu, wd)


def _combine_kernel(pos0_ref, pos1_ref, pos0_next_ref, pos1_next_ref, res_ref, route_ref, g_ref, b_ref, ys_hbm,
                    o_ref, buf_ref, sem):
    i = pl.program_id(0)
    tq = o_ref.shape[0]
    slot = lax.bitwise_and(i, 1)

    def gather(positions, dst_slot):
        def start(t, carry):
            for k in range(TOP_K):
                pltpu.make_async_copy(_tiled_row(ys_hbm, positions[k][t]),
                                      _tiled_row(buf_ref.at[dst_slot, k], t), sem.at[dst_slot]).start(priority=k)
            return carry
        lax.fori_loop(0, tq, start, 0, unroll=8)

    @pl.when(i == 0)
    def _():
        gather((pos0_ref, pos1_ref), 0)

    @pl.when(i + 1 < pl.num_programs(0))
    def _():
        gather((pos0_next_ref, pos1_next_ref), 1 - slot)

    for k in range(TOP_K):
        pltpu.make_async_copy(ys_hbm.at[pl.ds(0, tq * SUBLANES)], buf_ref.at[slot, k], sem.at[slot]).wait()

    route = route_ref[...]
    y = (route[:, ROUTE_GATE:ROUTE_GATE + 1] * _load_tiled_rows(buf_ref.at[slot, 0])
         + route[:, ROUTE_GATE + 1:ROUTE_GATE + 2] * _load_tiled_rows(buf_ref.at[slot, 1]))
    o_ref[...] = _layer_norm(ALPHA * _load_tiled_rows(res_ref) + y, g_ref[...], b_ref[...])


def _combine_ln(ys, pos, res_tiled, route, gain, bias, *, tq=256):
    t_len = res_tiled.shape[0] // SUBLANES
    d_model = SUBLANES * LANES
    n_steps = t_len // tq
    this_tile = pl.BlockSpec((tq,), lambda i: (i,), memory_space=pltpu.SMEM)
    next_tile = pl.BlockSpec((tq,), lambda i: (jnp.minimum(i + 1, n_steps - 1),), memory_space=pltpu.SMEM)
    return pl.pallas_call(
        _combine_kernel,
        out_shape=jax.ShapeDtypeStruct((t_len, d_model), F32),
        grid=(n_steps,),
        in_specs=[
            this_tile, this_tile, next_tile, next_tile,
            pl.BlockSpec((tq * SUBLANES, LANES), lambda i: (i, 0)),
            pl.BlockSpec((tq, LANES), lambda i: (i, 0)),
            pl.BlockSpec((1, d_model), lambda i: (0, 0)),
            pl.BlockSpec((1, d_model), lambda i: (0, 0)),
            pl.BlockSpec(memory_space=pl.ANY),
        ],
        out_specs=pl.BlockSpec((tq, d_model), lambda i: (i, 0)),
        scratch_shapes=[pltpu.VMEM((2, TOP_K, tq * SUBLANES, LANES), F32), pltpu.SemaphoreType.DMA((2,))],
        compiler_params=_params("arbitrary"),
        name="moe_combine_ln",
    )(pos[0], pos[1], pos[0], pos[1], res_tiled, route, gain.reshape(1, -1), bias.reshape(1, -1), ys)


def _moe_ffn_ln(x_tiled, router_w, router_b, wg, wu, wd, gain, bias, *, tm=MOE_ROW_TILE, tf=MOE_FF_CHUNK):
    t_len = x_tiled.shape[0] // SUBLANES
    route, route_t, counts = _router(x_tiled, router_w, router_b)
    experts = route_t[ROUTE_E:ROUTE_E + TOP_K].astype(I32)
    ranks = route_t[ROUTE_RANK:ROUTE_RANK + TOP_K].astype(I32)
    counts = counts[0, :N_EXPERTS].astype(I32)
    tiles_per_expert = (counts + tm - 1) // tm
    tile_end = jnp.cumsum(tiles_per_expert)
    row_start = (tile_end - tiles_per_expert) * tm
    pos = ranks + sum(jnp.where(experts == e, row_start[e], 0) for e in range(N_EXPERTS))
    n_tiles_max = (TOP_K * t_len) // tm + N_EXPERTS
    n_tiles = tile_end[-1:]
    all_tiles = jnp.arange(n_tiles_max, dtype=I32)
    tile_ids = jnp.minimum(all_tiles, n_tiles[0] - 1)
    tile_e = jnp.sum((tile_ids[:, None] >= tile_end[None, :-1]).astype(I32), axis=1)
    is_last = jnp.any((all_tiles[:, None] == tile_end[None, :] - 1) & (tiles_per_expert[None, :] > 0), axis=1)
    fill_tiles = (is_last | (all_tiles >= n_tiles[0])).astype(I32)

    first_tile = (tile_end - tiles_per_expert)[tile_e]
    n_valid = jnp.where(all_tiles < n_tiles[0], jnp.clip(counts[tile_e] - (all_tiles - first_tile) * tm, 0, tm), 0)

    xs = _dispatch(x_tiled, pos, fill_tiles, tm=tm, tq=MOE_DISPATCH_TILE)
    ys = _expert_ffn(xs, tile_e, n_tiles.astype(I32), n_valid.astype(I32), wg, wu, wd, tm=tm, tf=tf)
    return _combine_ln(ys, pos, x_tiled, route, gain, bias)


def _alibi_slopes():
    exps = np.arange(1, N_ATTN_HEADS + 1, dtype=np.float32) * np.float32(8.0 / N_ATTN_HEADS)
    return jnp.asarray(np.exp2(-exps).astype(np.float32))


def kernel(x, mem, a_w_in, a_w_mem_kv, a_w_out, a_ln1_g, a_ln1_b, a_ffn_gate, a_ffn_up, a_ffn_down, a_ln2_g, a_ln2_b, b_w_in, b_fourier_g, b_w_mem_kv, b_w_out, b_ln1_g, b_ln1_b, b_router_w, b_router_b, b_moe_gate, b_moe_up, b_moe_down, b_ln2_g, b_ln2_b):
    b_sz, s_len, d_model = x.shape
    t_len = b_sz * s_len
    x32 = x.reshape(t_len, d_model)
    a_w_in, a_w_mem_kv, a_w_out, a_ffn_gate, a_ffn_up, a_ffn_down, b_w_in, b_w_mem_kv, b_w_out = (
        w[0].astype(BF16) for w in (a_w_in, a_w_mem_kv, a_w_out, a_ffn_gate, a_ffn_up, a_ffn_down,
                                    b_w_in, b_w_mem_kv, b_w_out))

    n_grp = len(DILATIONS)
    qkv0, x16 = _proj_heads(x, a_w_in, (0, n_grp, 2 * n_grp, 3 * n_grp))
    qkv = [qkv0] + [_proj_heads(x16, a_w_in, (g, n_grp + g, 2 * n_grp + g), dil=DILATIONS[g])
                    for g in range(1, n_grp)]
    mem_kv, mem16 = _proj_heads(mem, a_w_mem_kv, (0, 1))
    attn = _dilated_attention(_alibi_slopes(), qkv).reshape(t_len, -1)
    memo = _memory_attention(qkv[0], 3 * HEADS_PER_BLOCK, mem_kv).reshape(t_len, -1)
    x32, x16 = _mixer_ffn_ln(attn, memo, a_w_out, x32, a_ln1_g[0], a_ln1_b[0],
                             a_ffn_gate, a_ffn_up, a_ffn_down, a_ln2_g[0], a_ln2_b[0])

    x16_b = x16.reshape(b_sz, s_len, d_model)
    u_q = _proj_heads(x16_b, b_w_in, (0, 1), gain=b_fourier_g[0], ln_blocks=(True, False))
    mem_kv = _proj_heads(mem16, b_w_mem_kv, (0, 1))
    four = _fourier_mix(u_q).reshape(t_len, -1)
    memo = _memory_attention(u_q, HEADS_PER_BLOCK, mem_kv).reshape(t_len, -1)
    x_tiled = _out_proj_ln(four, memo, b_w_out, x32, b_ln1_g[0], b_ln1_b[0])
    out = _moe_ffn_ln(x_tiled, b_router_w[0], b_router_b[0], b_moe_gate[0], b_moe_up[0], b_moe_down[0],
                      b_ln2_g[0], b_ln2_b[0])
    return out.reshape(b_sz, s_len, d_model)
```

```python
import functools

import numpy as np
import jax
import jax.numpy as jnp
from jax import lax
from jax.experimental import pallas as pl
from jax.experimental.pallas import tpu as pltpu

F32 = jnp.float32
BF16 = jnp.bfloat16
I32 = jnp.int32

LANES = 128
HEAD_DIM = 128
HEADS_PER_BLOCK = 4
DILATIONS = (1, 4, 16)
N_SIDE = 64
ATTN_Q_BLOCK = 64
ATTN_K_WINDOW = 256
ATTN_BATCH = 32
PROJ_MAX_ROW_STRIDE = 4
N_ATTN_HEADS = 12
N_EXPERTS = 8
TOP_K = 2
MOE_ROW_TILE = 1024
MOE_FF_CHUNK = 512
MOE_DISPATCH_TILE = 1024
EXPERT_ROW_PARTS = 4
SUBLANES = 8
ALPHA = (2.0 * 2) ** 0.25
LN_EPS = 1e-5
NEG_INF = -1e30
VMEM_LIMIT_BYTES = 56 * 1024 * 1024

ROUTE_E, ROUTE_RANK, ROUTE_GATE = 0, 2, 4


def _params(*sem):
    return pltpu.CompilerParams(dimension_semantics=sem, vmem_limit_bytes=VMEM_LIMIT_BYTES)


def _layer_norm(z, g, b=None):
    mu = jnp.mean(z, axis=-1, keepdims=True)
    zc = z - mu
    var = jnp.mean(zc * zc, axis=-1, keepdims=True)
    y = zc * lax.rsqrt(var + LN_EPS) * g
    return y if b is None else y + b


def _proj_kernel(*refs, dil, row_chunk, ln_blocks, cast_x):
    n_blocks = len(ln_blocks)
    x_ref, w_refs = refs[0], refs[1:1 + n_blocks]
    rest = refs[1 + n_blocks:]
    g_ref = None
    if any(ln_blocks):
        g_ref, rest = rest[0], rest[1:]
    o_ref, scratch = rest[0], rest[1:]
    if cast_x:
        x16_ref, scratch = scratch[0], scratch[1:]
        x16_ref[...] = x_ref[...].astype(BF16)
        x_ref = x16_ref
    s_len = x_ref.shape[0]
    for blk in range(n_blocks):
        w = w_refs[blk][...]
        acc_ref = scratch[blk % 2] if dil > 1 else None
        for rc in range(s_len // row_chunk):
            rows = pl.ds(rc * row_chunk, row_chunk)
            r = jnp.dot(x_ref[rows, :], w, preferred_element_type=F32)
            for hh in range(HEADS_PER_BLOCK):
                rh = r[:, hh * HEAD_DIM:(hh + 1) * HEAD_DIM]
                if ln_blocks[blk]:
                    rh = _layer_norm(rh, g_ref[hh:hh + 1, :])
                if dil == 1:
                    o_ref[blk * HEADS_PER_BLOCK + hh, 0, rows, :] = rh.astype(o_ref.dtype)
                else:
                    acc_ref[hh, rows, :] = rh
        if dil == 1:
            continue
        for hh in range(HEADS_PER_BLOCK):
            head = blk * HEADS_PER_BLOCK + hh
            if dil <= PROJ_MAX_ROW_STRIDE:
                for r_ in range(dil):
                    o_ref[head, r_, :, :] = acc_ref.at[hh][pl.ds(r_, s_len // dil, stride=dil), :].astype(o_ref.dtype)
            else:
                s1, s2 = PROJ_MAX_ROW_STRIDE, dil // PROJ_MAX_ROW_STRIDE
                tmp_ref = scratch[2]
                for a in range(s1):
                    tmp_ref[a] = acc_ref.at[hh][pl.ds(a, s_len // s1, stride=s1), :]
                for a in range(s1):
                    for b in range(s2):
                        o_ref[head, s1 * b + a, :, :] = tmp_ref.at[a][pl.ds(b, s_len // dil, stride=s2), :].astype(
                            o_ref.dtype)


def _proj_heads(x, w, col_blocks, *, dil=1, gain=None, ln_blocks=None):
    b_sz, s_len, d_model = x.shape
    cast_x = x.dtype != BF16
    blk_cols = HEADS_PER_BLOCK * HEAD_DIM
    n_blocks = len(col_blocks)
    ln_blocks = tuple(ln_blocks) if ln_blocks is not None else (False,) * n_blocks
    in_specs = [pl.BlockSpec((None, s_len, d_model), lambda b: (b, 0, 0))]
    in_specs += [pl.BlockSpec((d_model, blk_cols), lambda b, cb=cb: (0, cb), pipeline_mode=pl.Buffered(1))
                 for cb in col_blocks]
    args = [x] + [w] * n_blocks
    if any(ln_blocks):
        in_specs.append(pl.BlockSpec((HEADS_PER_BLOCK, HEAD_DIM), lambda b: (0, 0)))
        args.append(gain)
    scratch = []
    if dil > 1:
        scratch += [pltpu.VMEM((HEADS_PER_BLOCK, s_len, HEAD_DIM), F32)] * 2
    if dil > PROJ_MAX_ROW_STRIDE:
        scratch.append(pltpu.VMEM((PROJ_MAX_ROW_STRIDE, s_len // PROJ_MAX_ROW_STRIDE, HEAD_DIM), F32))
    n_heads = n_blocks * HEADS_PER_BLOCK
    out_shape = jax.ShapeDtypeStruct((b_sz, n_heads, dil, s_len // dil, HEAD_DIM), BF16)
    out_specs = pl.BlockSpec((None, n_heads, dil, s_len // dil, HEAD_DIM), lambda b: (b, 0, 0, 0, 0))
    if cast_x:
        out_shape = (out_shape, jax.ShapeDtypeStruct(x.shape, BF16))
        out_specs = (out_specs, pl.BlockSpec((None, s_len, d_model), lambda b: (b, 0, 0)))
    return pl.pallas_call(
        functools.partial(_proj_kernel, dil=dil, row_chunk=min(512, s_len), ln_blocks=ln_blocks, cast_x=cast_x),
        out_shape=out_shape,
        grid=(b_sz,),
        in_specs=in_specs,
        out_specs=out_specs,
        scratch_shapes=scratch,
        compiler_params=_params("arbitrary"),
        name=f"proj_heads_d{dil}",
    )(*args)


def _dil_attn_kernel(slopes_ref, q0, k0, v0, q1, k1, v1, q2, k2, v2, o_ref, o_scr, l_scr, bias_scr):
    s_len = o_ref.shape[0]
    h = pl.program_id(1)
    scale = HEAD_DIM ** -0.5
    qb = ATTN_Q_BLOCK
    groups = ((q0, k0, v0), (q1, k1, v1), (q2, k2, v2))
    for g, (qr, kr, vr) in enumerate(groups):
        dil = DILATIONS[g]
        cls_len = s_len // dil
        kw = min(ATTN_K_WINDOW, cls_len)
        nb = cls_len // qb
        nb_shift = nb.bit_length() - 1

        neg_slope = jnp.full((1, kw), slopes_ref[g * HEADS_PER_BLOCK + h], F32) * (-float(dil))
        base_delta = (lax.broadcasted_iota(I32, (qb, kw), 1) - lax.broadcasted_iota(I32, (qb, kw), 0))
        for case in range(kw // N_SIDE):
            dist = jnp.abs(base_delta - case * N_SIDE)
            bias_scr[g, case, :, 0:kw] = jnp.where(dist <= N_SIDE, dist.astype(F32) * neg_slope, NEG_INF)

        def body(it, carry, qr=qr, kr=kr, vr=vr, dil=dil, cls_len=cls_len, kw=kw, nb=nb, g=g,
                 nb_shift=nb_shift):
            blocks = []
            for j in range(ATTN_BATCH):
                idx = it * ATTN_BATCH + j
                r = lax.shift_right_logical(idx, nb_shift)
                i = lax.bitwise_and(idx, nb - 1)
                qs = pl.multiple_of(i * qb, qb)
                ks = pl.multiple_of(jnp.clip(qs - N_SIDE, 0, cls_len - kw), N_SIDE)
                blocks.append((r, qs, ks))
            scores = []
            for r, qs, ks in blocks:
                q = qr[r, pl.ds(qs, qb), :]
                k = kr[r, pl.ds(ks, kw), :]
                scores.append(lax.dot_general(q, k, (((1,), (1,)), ((), ())), preferred_element_type=F32))
            probs = []
            for (r, qs, ks), s in zip(blocks, scores):
                bias = bias_scr[g, lax.shift_right_logical(qs - ks, N_SIDE.bit_length() - 1), :, 0:kw]
                s = s * scale + bias
                m = jnp.max(s, axis=-1, keepdims=True)
                p = jnp.exp(s - m)
                l = jnp.sum(p, axis=-1, keepdims=True)
                probs.append((p.astype(BF16), m, l))
            for (r, qs, ks), (p, m, l) in zip(blocks, probs):
                v = vr[r, pl.ds(ks, kw), :]
                o = jnp.dot(p, v, preferred_element_type=F32) / l
                lse = jnp.broadcast_to(m + jnp.log(l), (qb, LANES))
                row0 = qs * dil + r
                if dil == 1:
                    o_scr[g, pl.ds(row0, qb), :] = o
                    l_scr[g, pl.ds(row0, qb), :] = lse
                else:
                    o_scr.at[g][pl.ds(row0, qb, stride=dil), :] = o
                    l_scr.at[g][pl.ds(row0, qb, stride=dil), :] = lse
            return carry

        lax.fori_loop(0, dil * nb // ATTN_BATCH, body, 0)

    chunk = 256
    for c in range(s_len // chunk):
        rows = pl.ds(c * chunk, chunk)
        l0, l1, l2 = l_scr[0, rows, :], l_scr[1, rows, :], l_scr[2, rows, :]
        m = jnp.maximum(jnp.maximum(l0, l1), l2)
        e0, e1, e2 = jnp.exp(l0 - m), jnp.exp(l1 - m), jnp.exp(l2 - m)
        num = e0 * o_scr[0, rows, :] + e1 * o_scr[1, rows, :] + e2 * o_scr[2, rows, :]
        o_ref[rows, :] = (num / (e0 + e1 + e2)).astype(o_ref.dtype)


def _dilated_attention(slopes, qkv):
    b_sz = qkv[0].shape[0]
    s_len = qkv[0].shape[2] * qkv[0].shape[3]
    in_specs = [pl.BlockSpec(memory_space=pltpu.SMEM)]
    args = [slopes]
    for g, arr in enumerate(qkv):
        dil = DILATIONS[g]
        for which in range(3):
            in_specs.append(pl.BlockSpec((None, None, dil, s_len // dil, HEAD_DIM),
                                         lambda b, h, which=which: (b, which * HEADS_PER_BLOCK + h, 0, 0, 0)))
            args.append(arr)
    return pl.pallas_call(
        _dil_attn_kernel,
        out_shape=jax.ShapeDtypeStruct((b_sz, s_len, HEADS_PER_BLOCK * HEAD_DIM), BF16),
        grid=(b_sz, HEADS_PER_BLOCK),
        in_specs=in_specs,
        out_specs=pl.BlockSpec((None, s_len, HEAD_DIM), lambda b, h: (b, 0, h)),
        scratch_shapes=[pltpu.VMEM((3, s_len, HEAD_DIM), F32), pltpu.VMEM((3, s_len, LANES), F32),
                        pltpu.VMEM((3, ATTN_K_WINDOW // N_SIDE, ATTN_Q_BLOCK, ATTN_K_WINDOW), F32)],
        compiler_params=_params("arbitrary", "arbitrary"),
        name="dilated_attention",
    )(*args)


def _mem_attn_kernel(q_ref, kv_ref, o_ref):
    n_heads, s_len, _ = q_ref.shape
    scale = HEAD_DIM ** -0.5
    chunk = 256
    row_chunks = [pl.ds(c * chunk, chunk) for c in range(s_len // chunk)]
    for h in range(n_heads):
        k = kv_ref[h]
        v = kv_ref[n_heads + h]
        scores = [lax.dot_general(q_ref[h, rows, :], k, (((1,), (1,)), ((), ())), preferred_element_type=F32)
                  for rows in row_chunks]
        probs = []
        for s in scores:
            s = s * scale
            m = jnp.max(s, axis=-1, keepdims=True)
            p = jnp.exp(s - m)
            probs.append((p.astype(BF16), jnp.sum(p, axis=-1, keepdims=True)))
        for rows, (p, l) in zip(row_chunks, probs):
            o_ref[rows, h * HEAD_DIM:(h + 1) * HEAD_DIM] = (
                jnp.dot(p, v, preferred_element_type=F32) / l).astype(o_ref.dtype)


def _memory_attention(q_heads, q_head0, kv_heads):
    b_sz, _, _, s_len, _ = q_heads.shape
    n_mem = kv_heads.shape[3]
    assert q_head0 % HEADS_PER_BLOCK == 0
    return pl.pallas_call(
        _mem_attn_kernel,
        out_shape=jax.ShapeDtypeStruct((b_sz, s_len, HEADS_PER_BLOCK * HEAD_DIM), BF16),
        grid=(b_sz,),
        in_specs=[
            pl.BlockSpec((None, HEADS_PER_BLOCK, None, s_len, HEAD_DIM),
                         lambda b: (b, q_head0 // HEADS_PER_BLOCK, 0, 0, 0)),
            pl.BlockSpec((None, 2 * HEADS_PER_BLOCK, None, n_mem, HEAD_DIM), lambda b: (b, 0, 0, 0, 0)),
        ],
        out_specs=pl.BlockSpec((None, s_len, HEADS_PER_BLOCK * HEAD_DIM), lambda b: (b, 0, 0)),
        compiler_params=_params("arbitrary"),
        name="memory_attention",
    )(q_heads, kv_heads)


def _out_ln_kernel(a1_ref, a2_ref, w_ref, res_ref, g_ref, b_ref, o_ref):
    k1 = a1_ref.shape[1]
    half = res_ref.shape[0] // 2
    halves = [pl.ds(hf * half, half) for hf in range(2)]
    ys = [jnp.dot(a1_ref[rows, :], w_ref[0:k1, :], preferred_element_type=F32)
          + jnp.dot(a2_ref[rows, :], w_ref[k1:, :], preferred_element_type=F32) for rows in halves]
    for hf, (rows, y) in enumerate(zip(halves, ys)):
        z = _layer_norm(ALPHA * res_ref[rows, :] + y, g_ref[...], b_ref[...])
        _store_tiled_rows(o_ref.at[pl.ds(hf * half * SUBLANES, half * SUBLANES)], z)


def _out_proj_ln(a1, a2, w, res, gain, bias, *, tm=1024):
    t_len, d_model = res.shape
    assert d_model == SUBLANES * LANES
    k1, k2 = a1.shape[1], a2.shape[1]
    return pl.pallas_call(
        _out_ln_kernel,
        out_shape=jax.ShapeDtypeStruct((t_len * SUBLANES, LANES), F32),
        grid=(t_len // tm,),
        in_specs=[
            pl.BlockSpec((tm, k1), lambda i: (i, 0)),
            pl.BlockSpec((tm, k2), lambda i: (i, 0)),
            pl.BlockSpec((k1 + k2, d_model), lambda i: (0, 0), pipeline_mode=pl.Buffered(1)),
            pl.BlockSpec((tm, d_model), lambda i: (i, 0)),
            pl.BlockSpec((1, d_model), lambda i: (0, 0)),
            pl.BlockSpec((1, d_model), lambda i: (0, 0)),
        ],
        out_specs=pl.BlockSpec((tm * SUBLANES, LANES), lambda i: (i, 0)),
        compiler_params=_params("arbitrary"),
        name="out_proj_ln",
    )(a1, a2, w, res, gain.reshape(1, -1), bias.reshape(1, -1))


def _swiglu_chunk(x, wg, wu, wd):
    hg = jnp.dot(x, wg, preferred_element_type=F32)
    hu = jnp.dot(x, wu, preferred_element_type=F32)
    h = hg * (1.0 / (1.0 + jnp.exp(-hg))) * hu
    return jnp.dot(h.astype(BF16), wd, preferred_element_type=F32)


def _mixer_ffn_kernel(a1_ref, a2_ref, wo_ref, res_ref, g1_ref, b1_ref, wg_ref, wu_ref, wd_ref, g2_ref, b2_ref,
                      o32_ref, o16_ref, *, tf):
    k1 = a1_ref.shape[1]
    tm = res_ref.shape[0]
    halves = [pl.ds(hf * (tm // 2), tm // 2) for hf in range(2)]
    ys = [jnp.dot(a1_ref[rows, :], wo_ref[0:k1, :], preferred_element_type=F32)
          + jnp.dot(a2_ref[rows, :], wo_ref[k1:, :], preferred_element_type=F32) for rows in halves]
    x1s = [_layer_norm(ALPHA * res_ref[rows, :] + y, g1_ref[...], b1_ref[...]) for rows, y in zip(halves, ys)]
    x16s = [x1.astype(BF16) for x1 in x1s]
    accs = [None, None]
    for f in range(wg_ref.shape[1] // tf):
        cols = pl.ds(f * tf, tf)
        for hf in range(2):
            part = _swiglu_chunk(x16s[hf], wg_ref[:, cols], wu_ref[:, cols], wd_ref[cols, :])
            accs[hf] = part if accs[hf] is None else accs[hf] + part
    for rows, x1, acc in zip(halves, x1s, accs):
        z = _layer_norm(ALPHA * x1 + acc, g2_ref[...], b2_ref[...])
        o32_ref[rows, :] = z
        o16_ref[rows, :] = z.astype(BF16)


def _mixer_ffn_ln(a1, a2, wo, res, g1, b1, wg, wu, wd, g2, b2, *, tm=1024, tf=256):
    t_len, d_model = res.shape
    k1, k2 = a1.shape[1], a2.shape[1]
    d_ff = wg.shape[1]
    resident = dict(pipeline_mode=pl.Buffered(1))
    vec = pl.BlockSpec((1, d_model), lambda i: (0, 0))
    return pl.pallas_call(
        functools.partial(_mixer_ffn_kernel, tf=tf),
        out_shape=(jax.ShapeDtypeStruct((t_len, d_model), F32), jax.ShapeDtypeStruct((t_len, d_model), BF16)),
        grid=(t_len // tm,),
        in_specs=[
            pl.BlockSpec((tm, k1), lambda i: (i, 0)),
            pl.BlockSpec((tm, k2), lambda i: (i, 0)),
            pl.BlockSpec((k1 + k2, d_model), lambda i: (0, 0), **resident),
            pl.BlockSpec((tm, d_model), lambda i: (i, 0)),
            vec, vec,
            pl.BlockSpec((d_model, d_ff), lambda i: (0, 0), **resident),
            pl.BlockSpec((d_model, d_ff), lambda i: (0, 0), **resident),
            pl.BlockSpec((d_ff, d_model), lambda i: (0, 0), **resident),
            vec, vec,
        ],
        out_specs=(pl.BlockSpec((tm, d_model), lambda i: (i, 0)),
                   pl.BlockSpec((tm, d_model), lambda i: (i, 0))),
        compiler_params=_params("arbitrary"),
        name="mixer_ffn_ln",
    )(a1, a2, wo, res, g1.reshape(1, -1), b1.reshape(1, -1), wg, wu, wd, g2.reshape(1, -1), b2.reshape(1, -1))


def _dft_constants(s_len):
    half = s_len // 2
    n = np.arange(half, dtype=np.int64)
    ang_s = 2.0 * np.pi * ((n[:, None] * n[None, :]) % s_len) / s_len
    c = np.arange(HEAD_DIM, dtype=np.int64)
    ang_c = 2.0 * np.pi * ((c[:, None] * c[None, :]) % HEAD_DIM) / HEAD_DIM
    norm = 1.0 / np.sqrt(float(s_len * HEAD_DIM))
    mirror = (n[:, None] + n[None, :] == half).astype(np.float64)
    return tuple(jnp.asarray(a, dtype=F32).astype(BF16)
                 for a in (np.cos(ang_s) * norm, np.sin(ang_s) * norm, mirror, np.cos(ang_c), np.sin(ang_c)))


def _fourier_kernel(u_ref, c1_ref, s1_ref, mirror_ref, cc_ref, sc_ref, o_ref, ae_scr, bo_scr, *, norm):
    n_grp, s_len, _ = u_ref.shape
    half = s_len // 2
    cc = cc_ref[...]
    sc = sc_ref[...]
    lanes = lambda parts: jnp.concatenate(parts, axis=1)

    u_low = lanes([u_ref[g, 0:half, :] for g in range(n_grp)])
    u_high = lanes([u_ref[g, half:, :] for g in range(n_grp)])
    u_mirror = jnp.dot(mirror_ref[...], u_high, preferred_element_type=F32)
    even = (u_low.astype(F32) + u_mirror).astype(BF16)
    odd = (u_low.astype(F32) - u_mirror).astype(BF16)

    row_sign = 1.0 - 2.0 * lax.bitwise_and(lax.broadcasted_iota(I32, (half, 1), 0), 1).astype(F32)
    pad_row = lax.broadcasted_iota(I32, (SUBLANES, 1), 0)
    mid_rows, alt_rows = [], []
    for g in range(n_grp):
        cols = slice(g * HEAD_DIM, (g + 1) * HEAD_DIM)
        ae_scr[:, cols] = jnp.dot(even[:, cols], cc, preferred_element_type=F32).astype(BF16)
        bo_scr[:, cols] = jnp.dot(odd[:, cols], sc, preferred_element_type=F32).astype(BF16)
        u_g = u_ref[g].astype(F32)
        alt = jnp.sum((u_g[0:half, :] + u_g[half:, :]) * row_sign, axis=0, keepdims=True)
        mid = u_g[half:half + 1, :]
        pad = jnp.where(pad_row == 0, mid, jnp.where(pad_row == 1, alt, 0.0)).astype(BF16)
        both = jnp.dot(pad, cc, preferred_element_type=F32) * norm
        mid_rows.append(both[0:1, :])
        alt_rows.append(both[1:2, :])
    mid_term = row_sign * lanes(mid_rows)
    p = jnp.dot(c1_ref[...], ae_scr[...], preferred_element_type=F32) + mid_term
    q = jnp.dot(s1_ref[...], bo_scr[...], preferred_element_type=F32)
    o_ref[0:half, :] = (p - q).astype(o_ref.dtype)
    upper = jnp.dot(mirror_ref[...], (p + q).astype(BF16), preferred_element_type=F32)
    is_row0 = lax.broadcasted_iota(I32, (half, 1), 0) == 0
    o_ref[half:, :] = jnp.where(is_row0, lanes(alt_rows), upper).astype(o_ref.dtype)


def _fourier_mix(u_heads):
    b_sz, _, _, s_len, _ = u_heads.shape
    n_grp = HEADS_PER_BLOCK
    half = s_len // 2
    consts = _dft_constants(s_len)
    const_spec = lambda n: pl.BlockSpec((n, n), lambda b: (0, 0), pipeline_mode=pl.Buffered(1))
    return pl.pallas_call(
        functools.partial(_fourier_kernel, norm=1.0 / np.sqrt(float(s_len * HEAD_DIM))),
        out_shape=jax.ShapeDtypeStruct((b_sz, s_len, n_grp * HEAD_DIM), BF16),
        grid=(b_sz,),
        in_specs=[pl.BlockSpec((None, n_grp, None, s_len, HEAD_DIM), lambda b: (b, 0, 0, 0, 0)),
                  const_spec(half), const_spec(half), const_spec(half), const_spec(HEAD_DIM), const_spec(HEAD_DIM)],
        out_specs=pl.BlockSpec((None, s_len, n_grp * HEAD_DIM), lambda b: (b, 0, 0)),
        scratch_shapes=[pltpu.VMEM((half, n_grp * HEAD_DIM), BF16), pltpu.VMEM((half, n_grp * HEAD_DIM), BF16)],
        compiler_params=_params("arbitrary"),
        name="fourier_mix",
    )(u_heads, *consts)


def _split_bf16(a):
    hi = a.astype(BF16)
    lo = (a - hi.astype(F32)).astype(BF16)
    return hi, lo


def _router_kernel(x_ref, w_ref, b_ref, route_ref, route_t_ref, cnt_ref, carry_ref):
    tq = route_ref.shape[0]

    @pl.when(pl.program_id(0) == 0)
    def _():
        carry_ref[...] = jnp.zeros_like(carry_ref)

    x_hi, x_lo = _split_bf16(_load_tiled_rows(x_ref))
    w_hi, w_lo = _split_bf16(w_ref[...])
    hi_terms = jnp.dot(x_hi, jnp.concatenate([w_hi, w_lo], axis=1), preferred_element_type=F32)
    logits = (hi_terms[:, 0:LANES] + hi_terms[:, LANES:]
              + jnp.dot(x_lo, w_hi, preferred_element_type=F32)) + b_ref[...]
    lane = lax.broadcasted_iota(I32, (tq, LANES), 1).astype(F32)
    logits = jnp.where(lane < N_EXPERTS, logits, NEG_INF)

    def top1(vals):
        m = jnp.max(vals, axis=-1, keepdims=True)
        idx = jnp.min(jnp.where(vals == m, lane, float(LANES)), axis=-1, keepdims=True)
        return m, idx

    m1, i1 = top1(logits)
    oh1 = lane == i1
    m2, i2 = top1(jnp.where(oh1, NEG_INF, logits))
    oh2 = lane == i2
    e2 = jnp.exp(m2 - m1)
    gate1 = 1.0 / (1.0 + e2)
    gate2 = e2 / (1.0 + e2)

    oh1f = oh1.astype(F32)
    oh2f = oh2.astype(F32)
    both = oh1f + oh2f
    tri = (lax.broadcasted_iota(I32, (tq, tq), 0) > lax.broadcasted_iota(I32, (tq, tq), 1)).astype(BF16)
    before = jnp.dot(tri, both.astype(BF16), preferred_element_type=F32) + carry_ref[0:1, :]
    rank1 = jnp.sum(oh1f * before, axis=-1, keepdims=True)
    rank2 = jnp.sum(oh2f * before, axis=-1, keepdims=True)
    carry_ref[...] = carry_ref[...] + jnp.sum(both, axis=0, keepdims=True)

    rec = jnp.zeros((tq, LANES), F32)
    for lane_idx, val in ((ROUTE_E, i1), (ROUTE_E + 1, i2),
                          (ROUTE_RANK, rank1), (ROUTE_RANK + 1, rank2),
                          (ROUTE_GATE, gate1), (ROUTE_GATE + 1, gate2)):
        rec = jnp.where(lane == lane_idx, val, rec)
    route_ref[...] = rec
    route_t_ref[...] = rec.T[0:route_t_ref.shape[0], :]
    cnt_ref[...] = carry_ref[...]


def _router(x_tiled, w, b, *, tq=512):
    t_len = x_tiled.shape[0] // SUBLANES
    d_model = SUBLANES * LANES
    w_pad = jnp.zeros((d_model, LANES), F32).at[:, :N_EXPERTS].set(w)
    b_pad = jnp.zeros((1, LANES), F32).at[0, :N_EXPERTS].set(b)
    return pl.pallas_call(
        _router_kernel,
        out_shape=(jax.ShapeDtypeStruct((t_len, LANES), F32), jax.ShapeDtypeStruct((SUBLANES, t_len), F32),
                   jax.ShapeDtypeStruct((8, LANES), F32)),
        grid=(t_len // tq,),
        in_specs=[
            pl.BlockSpec((tq * SUBLANES, LANES), lambda i: (i, 0)),
            pl.BlockSpec((d_model, LANES), lambda i: (0, 0)),
            pl.BlockSpec((1, LANES), lambda i: (0, 0)),
        ],
        out_specs=(pl.BlockSpec((tq, LANES), lambda i: (i, 0)), pl.BlockSpec((SUBLANES, tq), lambda i: (0, i)),
                   pl.BlockSpec((8, LANES), lambda i: (0, 0))),
        scratch_shapes=[pltpu.VMEM((8, LANES), F32)],
        compiler_params=_params("arbitrary"),
        name="router",
    )(x_tiled, w_pad, b_pad)


def _store_tiled_rows(dst_ref, val):
    n_rows, width = val.shape
    for j in range(width // LANES):
        dst_ref[pl.ds(j, n_rows, stride=width // LANES), :] = val[:, j * LANES:(j + 1) * LANES]


def _load_tiled_rows(src_ref):
    n_rows = src_ref.shape[0] // SUBLANES
    return jnp.concatenate([src_ref[pl.ds(j, n_rows, stride=SUBLANES), :] for j in range(SUBLANES)], axis=1)


def _tiled_row(ref, row):
    return ref.at[pl.ds(pl.multiple_of(row * SUBLANES, SUBLANES), SUBLANES)]


def _dispatch_kernel(fill_ref, pos0_ref, pos1_ref, x_ref, xs_hbm, zero_scr, sem, *, tm):
    tq = x_ref.shape[0] // SUBLANES
    pos_refs = (pos0_ref, pos1_ref)

    @pl.when(pl.program_id(0) == 0)
    def _():
        zero_scr[...] = jnp.zeros_like(zero_scr)

        def fill_copies(tile):
            copies = []
            for piece in range(tm // tq):
                row0 = pl.multiple_of((tile * tm + piece * tq) * SUBLANES, SUBLANES)
                copies.append(pltpu.make_async_copy(zero_scr, xs_hbm.at[pl.ds(row0, tq * SUBLANES)], sem))
            return copies

        def start_fill(tile, carry):
            @pl.when(fill_ref[tile] == 1)
            def _():
                for copy in fill_copies(tile):
                    copy.start()
            return carry

        def wait_fill(tile, carry):
            @pl.when(fill_ref[tile] == 1)
            def _():
                for copy in fill_copies(tile):
                    copy.wait()
            return carry

        lax.fori_loop(0, fill_ref.shape[0], start_fill, 0)
        lax.fori_loop(0, fill_ref.shape[0], wait_fill, 0)

    def start(t, carry):
        for k in range(TOP_K):
            pltpu.make_async_copy(_tiled_row(x_ref, t), _tiled_row(xs_hbm, pos_refs[k][t]), sem).start(priority=k)
        return carry

    lax.fori_loop(0, tq, start, 0, unroll=8)
    for k in range(TOP_K):
        pltpu.make_async_copy(x_ref, xs_hbm.at[pl.ds(0, tq * SUBLANES)], sem).wait()


def _dispatch(x_tiled, pos, fill_tiles, *, tm, tq=512):
    t_len = x_tiled.shape[0] // SUBLANES
    assert tm % tq == 0
    grid_spec = pltpu.PrefetchScalarGridSpec(
        num_scalar_prefetch=1,
        grid=(t_len // tq,),
        in_specs=[
            pl.BlockSpec((tq,), lambda i, fill: (i,), memory_space=pltpu.SMEM),
            pl.BlockSpec((tq,), lambda i, fill: (i,), memory_space=pltpu.SMEM),
            pl.BlockSpec((tq * SUBLANES, LANES), lambda i, fill: (i, 0)),
        ],
        out_specs=pl.BlockSpec(memory_space=pl.ANY),
        scratch_shapes=[pltpu.VMEM((tq * SUBLANES, LANES), x_tiled.dtype), pltpu.SemaphoreType.DMA],
    )
    return pl.pallas_call(
        functools.partial(_dispatch_kernel, tm=tm),
        out_shape=jax.ShapeDtypeStruct((fill_tiles.shape[0] * tm * SUBLANES, LANES), x_tiled.dtype),
        grid_spec=grid_spec,
        compiler_params=_params("arbitrary"),
        name="moe_dispatch",
    )(fill_tiles, pos[0], pos[1], x_tiled)


def _expert_kernel(tile_e_ref, nt_ref, nv_ref, x_ref, wg_ref, wu_ref, wd_ref, o_ref, x16_scr, acc_ref):
    del tile_e_ref, nt_ref
    c = pl.program_id(1)
    n_valid = nv_ref[pl.program_id(0)]
    part = x16_scr.shape[0] // EXPERT_ROW_PARTS

    @pl.when(c == 0)
    def _():
        acc_ref[...] = jnp.zeros_like(acc_ref)
        x16_scr[...] = _load_tiled_rows(x_ref).astype(BF16)

    def accumulate(rows):
        acc_ref[rows, :] += _swiglu_chunk(x16_scr[rows, :], wg_ref[...].astype(BF16), wu_ref[...].astype(BF16),
                                          wd_ref[...].astype(BF16))

    for n_parts in range(1, EXPERT_ROW_PARTS + 1):
        @pl.when(jnp.logical_and(n_valid > (n_parts - 1) * part, n_valid <= n_parts * part))
        def _(n_parts=n_parts):
            accumulate(pl.ds(0, n_parts * part))

    @pl.when(c == pl.num_programs(1) - 1)
    def _():
        _store_tiled_rows(o_ref, acc_ref[...])


def _expert_ffn(xs, tile_e, n_tiles, n_valid, wg, wu, wd, *, tm, tf):
    d_model = wg.shape[1]
    d_ff = wg.shape[2]
    n_chunks = d_ff // tf
    last = n_chunks - 1

    def row_map(i, c, te, nt, nv):
        return (jnp.minimum(i, nt[0] - 1), 0)

    def chunk_of(i, c, nt):
        return jnp.where(i < nt[0], c, last)

    grid_spec = pltpu.PrefetchScalarGridSpec(
        num_scalar_prefetch=3,
        grid=(xs.shape[0] // (tm * SUBLANES), n_chunks),
        in_specs=[
            pl.BlockSpec((tm * SUBLANES, LANES), row_map),
            pl.BlockSpec((None, d_model, tf), lambda i, c, te, nt, nv: (te[i], 0, chunk_of(i, c, nt))),
            pl.BlockSpec((None, d_model, tf), lambda i, c, te, nt, nv: (te[i], 0, chunk_of(i, c, nt))),
            pl.BlockSpec((None, tf, d_model), lambda i, c, te, nt, nv: (te[i], chunk_of(i, c, nt), 0)),
        ],
        out_specs=pl.BlockSpec((tm * SUBLANES, LANES), lambda i, c, te, nt, nv: (i, 0)),
        scratch_shapes=[pltpu.VMEM((tm, d_model), BF16), pltpu.VMEM((tm, d_model), F32)],
    )
    return pl.pallas_call(
        _expert_kernel,
        out_shape=jax.ShapeDtypeStruct(xs.shape, F32),
        grid_spec=grid_spec,
        compiler_params=_params("arbitrary", "arbitrary"),
        name="expert_ffn",
    )(tile_e, n_tiles, n_valid, xs, wg, wu, wd)


def _combine_kernel(pos0_ref, pos1_ref, pos0_next_ref, pos1_next_ref, res_ref, route_ref, g_ref, b_ref, ys_hbm,
                    o_ref, buf_ref, sem):
    i = pl.program_id(0)
    tq = o_ref.shape[0]
    slot = lax.bitwise_and(i, 1)

    def gather(positions, dst_slot):
        def start(t, carry):
            for k in range(TOP_K):
                pltpu.make_async_copy(_tiled_row(ys_hbm, positions[k][t]),
                                      _tiled_row(buf_ref.at[dst_slot, k], t), sem.at[dst_slot]).start(priority=k)
            return carry
        lax.fori_loop(0, tq, start, 0, unroll=8)

    @pl.when(i == 0)
    def _():
        gather((pos0_ref, pos1_ref), 0)

    @pl.when(i + 1 < pl.num_programs(0))
    def _():
        gather((pos0_next_ref, pos1_next_ref), 1 - slot)

    for k in range(TOP_K):
        pltpu.make_async_copy(ys_hbm.at[pl.ds(0, tq * SUBLANES)], buf_ref.at[slot, k], sem.at[slot]).wait()

    route = route_ref[...]
    y = (route[:, ROUTE_GATE:ROUTE_GATE + 1] * _load_tiled_rows(buf_ref.at[slot, 0])
         + route[:, ROUTE_GATE + 1:ROUTE_GATE + 2] * _load_tiled_rows(buf_ref.at[slot, 1]))
    o_ref[...] = _layer_norm(ALPHA * _load_tiled_rows(res_ref) + y, g_ref[...], b_ref[...])


def _combine_ln(ys, pos, res_tiled, route, gain, bias, *, tq=256):
    t_len = res_tiled.shape[0] // SUBLANES
    d_model = SUBLANES * LANES
    n_steps = t_len // tq
    this_tile = pl.BlockSpec((tq,), lambda i: (i,), memory_space=pltpu.SMEM)
    next_tile = pl.BlockSpec((tq,), lambda i: (jnp.minimum(i + 1, n_steps - 1),), memory_space=pltpu.SMEM)
    return pl.pallas_call(
        _combine_kernel,
        out_shape=jax.ShapeDtypeStruct((t_len, d_model), F32),
        grid=(n_steps,),
        in_specs=[
            this_tile, this_tile, next_tile, next_tile,
            pl.BlockSpec((tq * SUBLANES, LANES), lambda i: (i, 0)),
            pl.BlockSpec((tq, LANES), lambda i: (i, 0)),
            pl.BlockSpec((1, d_model), lambda i: (0, 0)),
            pl.BlockSpec((1, d_model), lambda i: (0, 0)),
            pl.BlockSpec(memory_space=pl.ANY),
        ],
        out_specs=pl.BlockSpec((tq, d_model), lambda i: (i, 0)),
        scratch_shapes=[pltpu.VMEM((2, TOP_K, tq * SUBLANES, LANES), F32), pltpu.SemaphoreType.DMA((2,))],
        compiler_params=_params("arbitrary"),
        name="moe_combine_ln",
    )(pos[0], pos[1], pos[0], pos[1], res_tiled, route, gain.reshape(1, -1), bias.reshape(1, -1), ys)


def _moe_ffn_ln(x_tiled, router_w, router_b, wg, wu, wd, gain, bias, *, tm=MOE_ROW_TILE, tf=MOE_FF_CHUNK):
    t_len = x_tiled.shape[0] // SUBLANES
    route, route_t, counts = _router(x_tiled, router_w, router_b)
    experts = route_t[ROUTE_E:ROUTE_E + TOP_K].astype(I32)
    ranks = route_t[ROUTE_RANK:ROUTE_RANK + TOP_K].astype(I32)
    counts = counts[0, :N_EXPERTS].astype(I32)
    tiles_per_expert = (counts + tm - 1) // tm
    tile_end = jnp.cumsum(tiles_per_expert)
    row_start = (tile_end - tiles_per_expert) * tm
    pos = ranks + sum(jnp.where(experts == e, row_start[e], 0) for e in range(N_EXPERTS))
    n_tiles_max = (TOP_K * t_len) // tm + N_EXPERTS
    n_tiles = tile_end[-1:]
    all_tiles = jnp.arange(n_tiles_max, dtype=I32)
    tile_ids = jnp.minimum(all_tiles, n_tiles[0] - 1)
    tile_e = jnp.sum((tile_ids[:, None] >= tile_end[None, :-1]).astype(I32), axis=1)
    is_last = jnp.any((all_tiles[:, None] == tile_end[None, :] - 1) & (tiles_per_expert[None, :] > 0), axis=1)
    fill_tiles = (is_last | (all_tiles >= n_tiles[0])).astype(I32)

    first_tile = (tile_end - tiles_per_expert)[tile_e]
    n_valid = jnp.where(all_tiles < n_tiles[0], jnp.clip(counts[tile_e] - (all_tiles - first_tile) * tm, 0, tm), 0)

    xs = _dispatch(x_tiled, pos, fill_tiles, tm=tm, tq=MOE_DISPATCH_TILE)
    ys = _expert_ffn(xs, tile_e, n_tiles.astype(I32), n_valid.astype(I32), wg, wu, wd, tm=tm, tf=tf)
    return _combine_ln(ys, pos, x_tiled, route, gain, bias)


def _alibi_slopes():
    exps = np.arange(1, N_ATTN_HEADS + 1, dtype=np.float32) * np.float32(8.0 / N_ATTN_HEADS)
    return jnp.asarray(np.exp2(-exps).astype(np.float32))


def kernel(x, mem, a_w_in, a_w_mem_kv, a_w_out, a_ln1_g, a_ln1_b, a_ffn_gate, a_ffn_up, a_ffn_down, a_ln2_g, a_ln2_b, b_w_in, b_fourier_g, b_w_mem_kv, b_w_out, b_ln1_g, b_ln1_b, b_router_w, b_router_b, b_moe_gate, b_moe_up, b_moe_down, b_ln2_g, b_ln2_b):
    b_sz, s_len, d_model = x.shape
    t_len = b_sz * s_len
    x32 = x.reshape(t_len, d_model)
    a_w_in, a_w_mem_kv, a_w_out, a_ffn_gate, a_ffn_up, a_ffn_down, b_w_in, b_w_mem_kv, b_w_out = (
        w[0].astype(BF16) for w in (a_w_in, a_w_mem_kv, a_w_out, a_ffn_gate, a_ffn_up, a_ffn_down,
                                    b_w_in, b_w_mem_kv, b_w_out))

    n_grp = len(DILATIONS)
    qkv0, x16 = _proj_heads(x, a_w_in, (0, n_grp, 2 * n_grp, 3 * n_grp))
    qkv = [qkv0] + [_proj_heads(x16, a_w_in, (g, n_grp + g, 2 * n_grp + g), dil=DILATIONS[g])
                    for g in range(1, n_grp)]
    mem_kv, mem16 = _proj_heads(mem, a_w_mem_kv, (0, 1))
    attn = _dilated_attention(_alibi_slopes(), qkv).reshape(t_len, -1)
    memo = _memory_attention(qkv[0], 3 * HEADS_PER_BLOCK, mem_kv).reshape(t_len, -1)
    x32, x16 = _mixer_ffn_ln(attn, memo, a_w_out, x32, a_ln1_g[0], a_ln1_b[0],
                             a_ffn_gate, a_ffn_up, a_ffn_down, a_ln2_g[0], a_ln2_b[0])

    x16_b = x16.reshape(b_sz, s_len, d_model)
    u_q = _proj_heads(x16_b, b_w_in, (0, 1), gain=b_fourier_g[0], ln_blocks=(True, False))
    mem_kv = _proj_heads(mem16, b_w_mem_kv, (0, 1))
    four = _fourier_mix(u_q).reshape(t_len, -1)
    memo = _memory_attention(u_q, HEADS_PER_BLOCK, mem_kv).reshape(t_len, -1)
    x_tiled = _out_proj_ln(four, memo, b_w_out, x32, b_ln1_g[0], b_ln1_b[0])
    out = _moe_ffn_ln(x_tiled, b_router_w[0], b_router_b[0], b_moe_gate[0], b_moe_up[0], b_moe_down[0],
                      b_ln2_g[0], b_ln2_b[0])
    return out.reshape(b_sz, s_len, d_model)
```

```python
import functools

import numpy as np
import jax
import jax.numpy as jnp
from jax import lax
from jax.experimental import pallas as pl
from jax.experimental.pallas import tpu as pltpu

F32 = jnp.float32
BF16 = jnp.bfloat16
I32 = jnp.int32

LANES = 128
HEAD_DIM = 128
HEADS_PER_BLOCK = 4
DILATIONS = (1, 4, 16)
N_SIDE = 64
ATTN_Q_BLOCK = 64
ATTN_K_WINDOW = 256
ATTN_BATCH = 32
PROJ_MAX_ROW_STRIDE = 4
N_ATTN_HEADS = 12
N_EXPERTS = 8
TOP_K = 2
MOE_ROW_TILE = 1024
MOE_FF_CHUNK = 512
MOE_DISPATCH_TILE = 1024
EXPERT_ROW_PARTS = 4
SUBLANES = 8
ALPHA = (2.0 * 2) ** 0.25
LN_EPS = 1e-5
NEG_INF = -1e30
VMEM_LIMIT_BYTES = 56 * 1024 * 1024

ROUTE_E, ROUTE_RANK, ROUTE_GATE = 0, 2, 4


def _params(*sem):
    return pltpu.CompilerParams(dimension_semantics=sem, vmem_limit_bytes=VMEM_LIMIT_BYTES)


def _layer_norm(z, g, b=None):
    mu = jnp.mean(z, axis=-1, keepdims=True)
    zc = z - mu
    var = jnp.mean(zc * zc, axis=-1, keepdims=True)
    y = zc * lax.rsqrt(var + LN_EPS) * g
    return y if b is None else y + b


def _proj_kernel(*refs, dil, row_chunk, ln_blocks, cast_x):
    n_blocks = len(ln_blocks)
    x_ref, w_refs = refs[0], refs[1:1 + n_blocks]
    rest = refs[1 + n_blocks:]
    g_ref = None
    if any(ln_blocks):
        g_ref, rest = rest[0], rest[1:]
    o_ref, scratch = rest[0], rest[1:]
    if cast_x:
        x16_ref, scratch = scratch[0], scratch[1:]
        x16_ref[...] = x_ref[...].astype(BF16)
        x_ref = x16_ref
    s_len = x_ref.shape[0]
    for blk in range(n_blocks):
        w = w_refs[blk][...]
        acc_ref = scratch[blk % 2] if dil > 1 else None
        for rc in range(s_len // row_chunk):
            rows = pl.ds(rc * row_chunk, row_chunk)
            r = jnp.dot(x_ref[rows, :], w, preferred_element_type=F32)
            for hh in range(HEADS_PER_BLOCK):
                rh = r[:, hh * HEAD_DIM:(hh + 1) * HEAD_DIM]
                if ln_blocks[blk]:
                    rh = _layer_norm(rh, g_ref[hh:hh + 1, :])
                if dil == 1:
                    o_ref[blk * HEADS_PER_BLOCK + hh, 0, rows, :] = rh.astype(o_ref.dtype)
                else:
                    acc_ref[hh, rows, :] = rh
        if dil == 1:
            continue
        for hh in range(HEADS_PER_BLOCK):
            head = blk * HEADS_PER_BLOCK + hh
            if dil <= PROJ_MAX_ROW_STRIDE:
                for r_ in range(dil):
                    o_ref[head, r_, :, :] = acc_ref.at[hh][pl.ds(r_, s_len // dil, stride=dil), :].astype(o_ref.dtype)
            else:
                s1, s2 = PROJ_MAX_ROW_STRIDE, dil // PROJ_MAX_ROW_STRIDE
                tmp_ref = scratch[2]
                for a in range(s1):
                    tmp_ref[a] = acc_ref.at[hh][pl.ds(a, s_len // s1, stride=s1), :]
                for a in range(s1):
                    for b in range(s2):
                        o_ref[head, s1 * b + a, :, :] = tmp_ref.at[a][pl.ds(b, s_len // dil, stride=s2), :].astype(
                            o_ref.dtype)


def _proj_heads(x, w, col_blocks, *, dil=1, gain=None, ln_blocks=None):
    b_sz, s_len, d_model = x.shape
    cast_x = x.dtype != BF16
    blk_cols = HEADS_PER_BLOCK * HEAD_DIM
    n_blocks = len(col_blocks)
    ln_blocks = tuple(ln_blocks) if ln_blocks is not None else (False,) * n_blocks
    in_specs = [pl.BlockSpec((None, s_len, d_model), lambda b: (b, 0, 0))]
    in_specs += [pl.BlockSpec((d_model, blk_cols), lambda b, cb=cb: (0, cb), pipeline_mode=pl.Buffered(1))
                 for cb in col_blocks]
    args = [x] + [w] * n_blocks
    if any(ln_blocks):
        in_specs.append(pl.BlockSpec((HEADS_PER_BLOCK, HEAD_DIM), lambda b: (0, 0)))
        args.append(gain)
    scratch = []
    if dil > 1:
        scratch += [pltpu.VMEM((HEADS_PER_BLOCK, s_len, HEAD_DIM), F32)] * 2
    if dil > PROJ_MAX_ROW_STRIDE:
        scratch.append(pltpu.VMEM((PROJ_MAX_ROW_STRIDE, s_len // PROJ_MAX_ROW_STRIDE, HEAD_DIM), F32))
    n_heads = n_blocks * HEADS_PER_BLOCK
    out_shape = jax.ShapeDtypeStruct((b_sz, n_heads, dil, s_len // dil, HEAD_DIM), BF16)
    out_specs = pl.BlockSpec((None, n_heads, dil, s_len // dil, HEAD_DIM), lambda b: (b, 0, 0, 0, 0))
    if cast_x:
        out_shape = (out_shape, jax.ShapeDtypeStruct(x.shape, BF16))
        out_specs = (out_specs, pl.BlockSpec((None, s_len, d_model), lambda b: (b, 0, 0)))
    return pl.pallas_call(
        functools.partial(_proj_kernel, dil=dil, row_chunk=min(512, s_len), ln_blocks=ln_blocks, cast_x=cast_x),
        out_shape=out_shape,
        grid=(b_sz,),
        in_specs=in_specs,
        out_specs=out_specs,
        scratch_shapes=scratch,
        compiler_params=_params("arbitrary"),
        name=f"proj_heads_d{dil}",
    )(*args)


def _dil_attn_kernel(slopes_ref, q0, k0, v0, q1, k1, v1, q2, k2, v2, o_ref, o_scr, l_scr, bias_scr):
    s_len = o_ref.shape[0]
    h = pl.program_id(1)
    scale = HEAD_DIM ** -0.5
    qb = ATTN_Q_BLOCK
    groups = ((q0, k0, v0), (q1, k1, v1), (q2, k2, v2))
    for g, (qr, kr, vr) in enumerate(groups):
        dil = DILATIONS[g]
        cls_len = s_len // dil
        kw = min(ATTN_K_WINDOW, cls_len)
        nb = cls_len // qb
        nb_shift = nb.bit_length() - 1

        neg_slope = jnp.full((1, kw), slopes_ref[g * HEADS_PER_BLOCK + h], F32) * (-float(dil))
        base_delta = (lax.broadcasted_iota(I32, (qb, kw), 1) - lax.broadcasted_iota(I32, (qb, kw), 0))
        for case in range(kw // N_SIDE):
            dist = jnp.abs(base_delta - case * N_SIDE)
            bias_scr[g, case, :, 0:kw] = jnp.where(dist <= N_SIDE, dist.astype(F32) * neg_slope, NEG_INF)

        def body(it, carry, qr=qr, kr=kr, vr=vr, dil=dil, cls_len=cls_len, kw=kw, nb=nb, g=g,
                 nb_shift=nb_shift):
            blocks = []
            for j in range(ATTN_BATCH):
                idx = it * ATTN_BATCH + j
                r = lax.shift_right_logical(idx, nb_shift)
                i = lax.bitwise_and(idx, nb - 1)
                qs = pl.multiple_of(i * qb, qb)
                ks = pl.multiple_of(jnp.clip(qs - N_SIDE, 0, cls_len - kw), N_SIDE)
                blocks.append((r, qs, ks))
            scores = []
            for r, qs, ks in blocks:
                q = qr[r, pl.ds(qs, qb), :]
                k = kr[r, pl.ds(ks, kw), :]
                scores.append(lax.dot_general(q, k, (((1,), (1,)), ((), ())), preferred_element_type=F32))
            probs = []
            for (r, qs, ks), s in zip(blocks, scores):
                bias = bias_scr[g, lax.shift_right_logical(qs - ks, N_SIDE.bit_length() - 1), :, 0:kw]
                s = s * scale + bias
                m = jnp.max(s, axis=-1, keepdims=True)
                p = jnp.exp(s - m)
                l = jnp.sum(p, axis=-1, keepdims=True)
                probs.append((p.astype(BF16), m, l))
            for (r, qs, ks), (p, m, l) in zip(blocks, probs):
                v = vr[r, pl.ds(ks, kw), :]
                o = jnp.dot(p, v, preferred_element_type=F32) / l
                lse = jnp.broadcast_to(m + jnp.log(l), (qb, LANES))
                row0 = qs * dil + r
                if dil == 1:
                    o_scr[g, pl.ds(row0, qb), :] = o
                    l_scr[g, pl.ds(row0, qb), :] = lse
                else:
                    o_scr.at[g][pl.ds(row0, qb, stride=dil), :] = o
                    l_scr.at[g][pl.ds(row0, qb, stride=dil), :] = lse
            return carry

        lax.fori_loop(0, dil * nb // ATTN_BATCH, body, 0)

    chunk = 256
    for c in range(s_len // chunk):
        rows = pl.ds(c * chunk, chunk)
        l0, l1, l2 = l_scr[0, rows, :], l_scr[1, rows, :], l_scr[2, rows, :]
        m = jnp.maximum(jnp.maximum(l0, l1), l2)
        e0, e1, e2 = jnp.exp(l0 - m), jnp.exp(l1 - m), jnp.exp(l2 - m)
        num = e0 * o_scr[0, rows, :] + e1 * o_scr[1, rows, :] + e2 * o_scr[2, rows, :]
        o_ref[rows, :] = (num / (e0 + e1 + e2)).astype(o_ref.dtype)


def _dilated_attention(slopes, qkv):
    b_sz = qkv[0].shape[0]
    s_len = qkv[0].shape[2] * qkv[0].shape[3]
    in_specs = [pl.BlockSpec(memory_space=pltpu.SMEM)]
    args = [slopes]
    for g, arr in enumerate(qkv):
        dil = DILATIONS[g]
        for which in range(3):
            in_specs.append(pl.BlockSpec((None, None, dil, s_len // dil, HEAD_DIM),
                                         lambda b, h, which=which: (b, which * HEADS_PER_BLOCK + h, 0, 0, 0)))
            args.append(arr)
    return pl.pallas_call(
        _dil_attn_kernel,
        out_shape=jax.ShapeDtypeStruct((b_sz, s_len, HEADS_PER_BLOCK * HEAD_DIM), BF16),
        grid=(b_sz, HEADS_PER_BLOCK),
        in_specs=in_specs,
        out_specs=pl.BlockSpec((None, s_len, HEAD_DIM), lambda b, h: (b, 0, h)),
        scratch_shapes=[pltpu.VMEM((3, s_len, HEAD_DIM), F32), pltpu.VMEM((3, s_len, LANES), F32),
                        pltpu.VMEM((3, ATTN_K_WINDOW // N_SIDE, ATTN_Q_BLOCK, ATTN_K_WINDOW), F32)],
        compiler_params=_params("arbitrary", "arbitrary"),
        name="dilated_attention",
    )(*args)


def _mem_attn_kernel(q_ref, kv_ref, o_ref):
    n_heads, s_len, _ = q_ref.shape
    scale = HEAD_DIM ** -0.5
    chunk = 256
    row_chunks = [pl.ds(c * chunk, chunk) for c in range(s_len // chunk)]
    for h in range(n_heads):
        k = kv_ref[h]
        v = kv_ref[n_heads + h]
        scores = [lax.dot_general(q_ref[h, rows, :], k, (((1,), (1,)), ((), ())), preferred_element_type=F32)
                  for rows in row_chunks]
        probs = []
        for s in scores:
            s = s * scale
            m = jnp.max(s, axis=-1, keepdims=True)
            p = jnp.exp(s - m)
            probs.append((p.astype(BF16), jnp.sum(p, axis=-1, keepdims=True)))
        for rows, (p, l) in zip(row_chunks, probs):
            o_ref[rows, h * HEAD_DIM:(h + 1) * HEAD_DIM] = (
                jnp.dot(p, v, preferred_element_type=F32) / l).astype(o_ref.dtype)


def _memory_attention(q_heads, q_head0, kv_heads):
    b_sz, _, _, s_len, _ = q_heads.shape
    n_mem = kv_heads.shape[3]
    assert q_head0 % HEADS_PER_BLOCK == 0
    return pl.pallas_call(
        _mem_attn_kernel,
        out_shape=jax.ShapeDtypeStruct((b_sz, s_len, HEADS_PER_BLOCK * HEAD_DIM), BF16),
        grid=(b_sz,),
        in_specs=[
            pl.BlockSpec((None, HEADS_PER_BLOCK, None, s_len, HEAD_DIM),
                         lambda b: (b, q_head0 // HEADS_PER_BLOCK, 0, 0, 0)),
            pl.BlockSpec((None, 2 * HEADS_PER_BLOCK, None, n_mem, HEAD_DIM), lambda b: (b, 0, 0, 0, 0)),
        ],
        out_specs=pl.BlockSpec((None, s_len, HEADS_PER_BLOCK * HEAD_DIM), lambda b: (b, 0, 0)),
        compiler_params=_params("arbitrary"),
        name="memory_attention",
    )(q_heads, kv_heads)


def _out_ln_kernel(a1_ref, a2_ref, w_ref, res_ref, g_ref, b_ref, o_ref):
    k1 = a1_ref.shape[1]
    half = res_ref.shape[0] // 2
    halves = [pl.ds(hf * half, half) for hf in range(2)]
    ys = [jnp.dot(a1_ref[rows, :], w_ref[0:k1, :], preferred_element_type=F32)
          + jnp.dot(a2_ref[rows, :], w_ref[k1:, :], preferred_element_type=F32) for rows in halves]
    for hf, (rows, y) in enumerate(zip(halves, ys)):
        z = _layer_norm(ALPHA * res_ref[rows, :] + y, g_ref[...], b_ref[...])
        _store_tiled_rows(o_ref.at[pl.ds(hf * half * SUBLANES, half * SUBLANES)], z)


def _out_proj_ln(a1, a2, w, res, gain, bias, *, tm=1024):
    t_len, d_model = res.shape
    assert d_model == SUBLANES * LANES
    k1, k2 = a1.shape[1], a2.shape[1]
    return pl.pallas_call(
        _out_ln_kernel,
        out_shape=jax.ShapeDtypeStruct((t_len * SUBLANES, LANES), F32),
        grid=(t_len // tm,),
        in_specs=[
            pl.BlockSpec((tm, k1), lambda i: (i, 0)),
            pl.BlockSpec((tm, k2), lambda i: (i, 0)),
            pl.BlockSpec((k1 + k2, d_model), lambda i: (0, 0), pipeline_mode=pl.Buffered(1)),
            pl.BlockSpec((tm, d_model), lambda i: (i, 0)),
            pl.BlockSpec((1, d_model), lambda i: (0, 0)),
            pl.BlockSpec((1, d_model), lambda i: (0, 0)),
        ],
        out_specs=pl.BlockSpec((tm * SUBLANES, LANES), lambda i: (i, 0)),
        compiler_params=_params("arbitrary"),
        name="out_proj_ln",
    )(a1, a2, w, res, gain.reshape(1, -1), bias.reshape(1, -1))


def _swiglu_chunk(x, wg, wu, wd):
    hg = jnp.dot(x, wg, preferred_element_type=F32)
    hu = jnp.dot(x, wu, preferred_element_type=F32)
    h = hg * (1.0 / (1.0 + jnp.exp(-hg))) * hu
    return jnp.dot(h.astype(BF16), wd, preferred_element_type=F32)


def _mixer_ffn_kernel(a1_ref, a2_ref, wo_ref, res_ref, g1_ref, b1_ref, wg_ref, wu_ref, wd_ref, g2_ref, b2_ref,
                      o32_ref, o16_ref, *, tf):
    k1 = a1_ref.shape[1]
    tm = res_ref.shape[0]
    halves = [pl.ds(hf * (tm // 2), tm // 2) for hf in range(2)]
    ys = [jnp.dot(a1_ref[rows, :], wo_ref[0:k1, :], preferred_element_type=F32)
          + jnp.dot(a2_ref[rows, :], wo_ref[k1:, :], preferred_element_type=F32) for rows in halves]
    x1s = [_layer_norm(ALPHA * res_ref[rows, :] + y, g1_ref[...], b1_ref[...]) for rows, y in zip(halves, ys)]
    x16s = [x1.astype(BF16) for x1 in x1s]
    accs = [None, None]
    for f in range(wg_ref.shape[1] // tf):
        cols = pl.ds(f * tf, tf)
        for hf in range(2):
            part = _swiglu_chunk(x16s[hf], wg_ref[:, cols], wu_ref[:, cols], wd_ref[cols, :])
            accs[hf] = part if accs[hf] is None else accs[hf] + part
    for rows, x1, acc in zip(halves, x1s, accs):
        z = _layer_norm(ALPHA * x1 + acc, g2_ref[...], b2_ref[...])
        o32_ref[rows, :] = z
        o16_ref[rows, :] = z.astype(BF16)


def _mixer_ffn_ln(a1, a2, wo, res, g1, b1, wg, wu, wd, g2, b2, *, tm=1024, tf=256):
    t_len, d_model = res.shape
    k1, k2 = a1.shape[1], a2.shape[1]
    d_ff = wg.shape[1]
    resident = dict(pipeline_mode=pl.Buffered(1))
    vec = pl.BlockSpec((1, d_model), lambda i: (0, 0))
    return pl.pallas_call(
        functools.partial(_mixer_ffn_kernel, tf=tf),
        out_shape=(jax.ShapeDtypeStruct((t_len, d_model), F32), jax.ShapeDtypeStruct((t_len, d_model), BF16)),
        grid=(t_len // tm,),
        in_specs=[
            pl.BlockSpec((tm, k1), lambda i: (i, 0)),
            pl.BlockSpec((tm, k2), lambda i: (i, 0)),
            pl.BlockSpec((k1 + k2, d_model), lambda i: (0, 0), **resident),
            pl.BlockSpec((tm, d_model), lambda i: (i, 0)),
            vec, vec,
            pl.BlockSpec((d_model, d_ff), lambda i: (0, 0), **resident),
            pl.BlockSpec((d_model, d_ff), lambda i: (0, 0), **resident),
            pl.BlockSpec((d_ff, d_model), lambda i: (0, 0), **resident),
            vec, vec,
        ],
        out_specs=(pl.BlockSpec((tm, d_model), lambda i: (i, 0)),
                   pl.BlockSpec((tm, d_model), lambda i: (i, 0))),
        compiler_params=_params("arbitrary"),
        name="mixer_ffn_ln",
    )(a1, a2, wo, res, g1.reshape(1, -1), b1.reshape(1, -1), wg, wu, wd, g2.reshape(1, -1), b2.reshape(1, -1))


def _dft_constants(s_len):
    half = s_len // 2
    n = np.arange(half, dtype=np.int64)
    ang_s = 2.0 * np.pi * ((n[:, None] * n[None, :]) % s_len) / s_len
    c = np.arange(HEAD_DIM, dtype=np.int64)
    ang_c = 2.0 * np.pi * ((c[:, None] * c[None, :]) % HEAD_DIM) / HEAD_DIM
    norm = 1.0 / np.sqrt(float(s_len * HEAD_DIM))
    mirror = (n[:, None] + n[None, :] == half).astype(np.float64)
    return tuple(jnp.asarray(a, dtype=F32).astype(BF16)
                 for a in (np.cos(ang_s) * norm, np.sin(ang_s) * norm, mirror, np.cos(ang_c), np.sin(ang_c)))


def _fourier_kernel(u_ref, c1_ref, s1_ref, mirror_ref, cc_ref, sc_ref, o_ref, ae_scr, bo_scr, *, norm):
    n_grp, s_len, _ = u_ref.shape
    half = s_len // 2
    cc = cc_ref[...]
    sc = sc_ref[...]
    lanes = lambda parts: jnp.concatenate(parts, axis=1)

    u_low = lanes([u_ref[g, 0:half, :] for g in range(n_grp)])
    u_high = lanes([u_ref[g, half:, :] for g in range(n_grp)])
    u_mirror = jnp.dot(mirror_ref[...], u_high, preferred_element_type=F32)
    even = (u_low.astype(F32) + u_mirror).astype(BF16)
    odd = (u_low.astype(F32) - u_mirror).astype(BF16)

    row_sign = 1.0 - 2.0 * lax.bitwise_and(lax.broadcasted_iota(I32, (half, 1), 0), 1).astype(F32)
    pad_row = lax.broadcasted_iota(I32, (SUBLANES, 1), 0)
    mid_rows, alt_rows = [], []
    for g in range(n_grp):
        cols = slice(g * HEAD_DIM, (g + 1) * HEAD_DIM)
        ae_scr[:, cols] = jnp.dot(even[:, cols], cc, preferred_element_type=F32).astype(BF16)
        bo_scr[:, cols] = jnp.dot(odd[:, cols], sc, preferred_element_type=F32).astype(BF16)
        u_g = u_ref[g].astype(F32)
        alt = jnp.sum((u_g[0:half, :] + u_g[half:, :]) * row_sign, axis=0, keepdims=True)
        mid = u_g[half:half + 1, :]
        pad = jnp.where(pad_row == 0, mid, jnp.where(pad_row == 1, alt, 0.0)).astype(BF16)
        both = jnp.dot(pad, cc, preferred_element_type=F32) * norm
        mid_rows.append(both[0:1, :])
        alt_rows.append(both[1:2, :])
    mid_term = row_sign * lanes(mid_rows)
    p = jnp.dot(c1_ref[...], ae_scr[...], preferred_element_type=F32) + mid_term
    q = jnp.dot(s1_ref[...], bo_scr[...], preferred_element_type=F32)
    o_ref[0:half, :] = (p - q).astype(o_ref.dtype)
    upper = jnp.dot(mirror_ref[...], (p + q).astype(BF16), preferred_element_type=F32)
    is_row0 = lax.broadcasted_iota(I32, (half, 1), 0) == 0
    o_ref[half:, :] = jnp.where(is_row0, lanes(alt_rows), upper).astype(o_ref.dtype)


def _fourier_mix(u_heads):
    b_sz, _, _, s_len, _ = u_heads.shape
    n_grp = HEADS_PER_BLOCK
    half = s_len // 2
    consts = _dft_constants(s_len)
    const_spec = lambda n: pl.BlockSpec((n, n), lambda b: (0, 0), pipeline_mode=pl.Buffered(1))
    return pl.pallas_call(
        functools.partial(_fourier_kernel, norm=1.0 / np.sqrt(float(s_len * HEAD_DIM))),
        out_shape=jax.ShapeDtypeStruct((b_sz, s_len, n_grp * HEAD_DIM), BF16),
        grid=(b_sz,),
        in_specs=[pl.BlockSpec((None, n_grp, None, s_len, HEAD_DIM), lambda b: (b, 0, 0, 0, 0)),
                  const_spec(half), const_spec(half), const_spec(half), const_spec(HEAD_DIM), const_spec(HEAD_DIM)],
        out_specs=pl.BlockSpec((None, s_len, n_grp * HEAD_DIM), lambda b: (b, 0, 0)),
        scratch_shapes=[pltpu.VMEM((half, n_grp * HEAD_DIM), BF16), pltpu.VMEM((half, n_grp * HEAD_DIM), BF16)],
        compiler_params=_params("arbitrary"),
        name="fourier_mix",
    )(u_heads, *consts)


def _split_bf16(a):
    hi = a.astype(BF16)
    lo = (a - hi.astype(F32)).astype(BF16)
    return hi, lo


def _router_kernel(x_ref, w_ref, b_ref, route_ref, route_t_ref, cnt_ref, carry_ref):
    tq = route_ref.shape[0]

    @pl.when(pl.program_id(0) == 0)
    def _():
        carry_ref[...] = jnp.zeros_like(carry_ref)

    x_hi, x_lo = _split_bf16(_load_tiled_rows(x_ref))
    w_hi, w_lo = _split_bf16(w_ref[...])
    hi_terms = jnp.dot(x_hi, jnp.concatenate([w_hi, w_lo], axis=1), preferred_element_type=F32)
    logits = (hi_terms[:, 0:LANES] + hi_terms[:, LANES:]
              + jnp.dot(x_lo, w_hi, preferred_element_type=F32)) + b_ref[...]
    lane = lax.broadcasted_iota(I32, (tq, LANES), 1).astype(F32)
    logits = jnp.where(lane < N_EXPERTS, logits, NEG_INF)

    def top1(vals):
        m = jnp.max(vals, axis=-1, keepdims=True)
        idx = jnp.min(jnp.where(vals == m, lane, float(LANES)), axis=-1, keepdims=True)
        return m, idx

    m1, i1 = top1(logits)
    oh1 = lane == i1
    m2, i2 = top1(jnp.where(oh1, NEG_INF, logits))
    oh2 = lane == i2
    e2 = jnp.exp(m2 - m1)
    gate1 = 1.0 / (1.0 + e2)
    gate2 = e2 / (1.0 + e2)

    oh1f = oh1.astype(F32)
    oh2f = oh2.astype(F32)
    both = oh1f + oh2f
    tri = (lax.broadcasted_iota(I32, (tq, tq), 0) > lax.broadcasted_iota(I32, (tq, tq), 1)).astype(BF16)
    before = jnp.dot(tri, both.astype(BF16), preferred_element_type=F32) + carry_ref[0:1, :]
    rank1 = jnp.sum(oh1f * before, axis=-1, keepdims=True)
    rank2 = jnp.sum(oh2f * before, axis=-1, keepdims=True)
    carry_ref[...] = carry_ref[...] + jnp.sum(both, axis=0, keepdims=True)

    rec = jnp.zeros((tq, LANES), F32)
    for lane_idx, val in ((ROUTE_E, i1), (ROUTE_E + 1, i2),
                          (ROUTE_RANK, rank1), (ROUTE_RANK + 1, rank2),
                          (ROUTE_GATE, gate1), (ROUTE_GATE + 1, gate2)):
        rec = jnp.where(lane == lane_idx, val, rec)
    route_ref[...] = rec
    route_t_ref[...] = rec.T[0:route_t_ref.shape[0], :]
    cnt_ref[...] = carry_ref[...]


def _router(x_tiled, w, b, *, tq=512):
    t_len = x_tiled.shape[0] // SUBLANES
    d_model = SUBLANES * LANES
    w_pad = jnp.zeros((d_model, LANES), F32).at[:, :N_EXPERTS].set(w)
    b_pad = jnp.zeros((1, LANES), F32).at[0, :N_EXPERTS].set(b)
    return pl.pallas_call(
        _router_kernel,
        out_shape=(jax.ShapeDtypeStruct((t_len, LANES), F32), jax.ShapeDtypeStruct((SUBLANES, t_len), F32),
                   jax.ShapeDtypeStruct((8, LANES), F32)),
        grid=(t_len // tq,),
        in_specs=[
            pl.BlockSpec((tq * SUBLANES, LANES), lambda i: (i, 0)),
            pl.BlockSpec((d_model, LANES), lambda i: (0, 0)),
            pl.BlockSpec((1, LANES), lambda i: (0, 0)),
        ],
        out_specs=(pl.BlockSpec((tq, LANES), lambda i: (i, 0)), pl.BlockSpec((SUBLANES, tq), lambda i: (0, i)),
                   pl.BlockSpec((8, LANES), lambda i: (0, 0))),
        scratch_shapes=[pltpu.VMEM((8, LANES), F32)],
        compiler_params=_params("arbitrary"),
        name="router",
    )(x_tiled, w_pad, b_pad)


def _store_tiled_rows(dst_ref, val):
    n_rows, width = val.shape
    for j in range(width // LANES):
        dst_ref[pl.ds(j, n_rows, stride=width // LANES), :] = val[:, j * LANES:(j + 1) * LANES]


def _load_tiled_rows(src_ref):
    n_rows = src_ref.shape[0] // SUBLANES
    return jnp.concatenate([src_ref[pl.ds(j, n_rows, stride=SUBLANES), :] for j in range(SUBLANES)], axis=1)


def _tiled_row(ref, row):
    return ref.at[pl.ds(pl.multiple_of(row * SUBLANES, SUBLANES), SUBLANES)]


def _dispatch_kernel(fill_ref, pos0_ref, pos1_ref, x_ref, xs_hbm, zero_scr, sem, *, tm):
    tq = x_ref.shape[0] // SUBLANES
    pos_refs = (pos0_ref, pos1_ref)

    @pl.when(pl.program_id(0) == 0)
    def _():
        zero_scr[...] = jnp.zeros_like(zero_scr)

        def fill_copies(tile):
            copies = []
            for piece in range(tm // tq):
                row0 = pl.multiple_of((tile * tm + piece * tq) * SUBLANES, SUBLANES)
                copies.append(pltpu.make_async_copy(zero_scr, xs_hbm.at[pl.ds(row0, tq * SUBLANES)], sem))
            return copies

        def start_fill(tile, carry):
            @pl.when(fill_ref[tile] == 1)
            def _():
                for copy in fill_copies(tile):
                    copy.start()
            return carry

        def wait_fill(tile, carry):
            @pl.when(fill_ref[tile] == 1)
            def _():
                for copy in fill_copies(tile):
                    copy.wait()
            return carry

        lax.fori_loop(0, fill_ref.shape[0], start_fill, 0)
        lax.fori_loop(0, fill_ref.shape[0], wait_fill, 0)

    def start(t, carry):
        for k in range(TOP_K):
            pltpu.make_async_copy(_tiled_row(x_ref, t), _tiled_row(xs_hbm, pos_refs[k][t]), sem).start(priority=k)
        return carry

    lax.fori_loop(0, tq, start, 0, unroll=8)
    for k in range(TOP_K):
        pltpu.make_async_copy(x_ref, xs_hbm.at[pl.ds(0, tq * SUBLANES)], sem).wait()


def _dispatch(x_tiled, pos, fill_tiles, *, tm, tq=512):
    t_len = x_tiled.shape[0] // SUBLANES
    assert tm % tq == 0
    grid_spec = pltpu.PrefetchScalarGridSpec(
        num_scalar_prefetch=1,
        grid=(t_len // tq,),
        in_specs=[
            pl.BlockSpec((tq,), lambda i, fill: (i,), memory_space=pltpu.SMEM),
            pl.BlockSpec((tq,), lambda i, fill: (i,), memory_space=pltpu.SMEM),
            pl.BlockSpec((tq * SUBLANES, LANES), lambda i, fill: (i, 0)),
        ],
        out_specs=pl.BlockSpec(memory_space=pl.ANY),
        scratch_shapes=[pltpu.VMEM((tq * SUBLANES, LANES), x_tiled.dtype), pltpu.SemaphoreType.DMA],
    )
    return pl.pallas_call(
        functools.partial(_dispatch_kernel, tm=tm),
        out_shape=jax.ShapeDtypeStruct((fill_tiles.shape[0] * tm * SUBLANES, LANES), x_tiled.dtype),
        grid_spec=grid_spec,
        compiler_params=_params("arbitrary"),
        name="moe_dispatch",
    )(fill_tiles, pos[0], pos[1], x_tiled)


def _expert_kernel(tile_e_ref, nt_ref, nv_ref, x_ref, wg_ref, wu_ref, wd_ref, o_ref, x16_scr, acc_ref):
    del tile_e_ref, nt_ref
    c = pl.program_id(1)
    n_valid = nv_ref[pl.program_id(0)]
    part = x16_scr.shape[0] // EXPERT_ROW_PARTS
    last = pl.num_programs(1) - 1
    full = n_valid > (EXPERT_ROW_PARTS - 1) * part

    def weights():
        return wg_ref[...].astype(BF16), wu_ref[...].astype(BF16), wd_ref[...].astype(BF16)

    @pl.when(jnp.logical_and(full, c == 0))
    def _():
        wg, wu, wd = weights()
        for p in range(EXPERT_ROW_PARTS):
            rows = pl.ds(p * part, part)
            x16 = _load_tiled_rows(x_ref.at[pl.ds(p * part * SUBLANES, part * SUBLANES)]).astype(BF16)
            x16_scr[rows, :] = x16
            acc_ref[rows, :] = _swiglu_chunk(x16, wg, wu, wd)

    @pl.when(jnp.logical_and(full, jnp.logical_and(c > 0, c < last)))
    def _():
        acc_ref[...] += _swiglu_chunk(x16_scr[...], *weights())

    @pl.when(jnp.logical_and(full, c == last))
    def _():
        _store_tiled_rows(o_ref, acc_ref[...] + _swiglu_chunk(x16_scr[...], *weights()))

    partial = jnp.logical_not(full)

    @pl.when(jnp.logical_and(partial, c == 0))
    def _():
        acc_ref[...] = jnp.zeros_like(acc_ref)
        x16_scr[...] = _load_tiled_rows(x_ref).astype(BF16)

    for n_parts in range(1, EXPERT_ROW_PARTS):
        @pl.when(jnp.logical_and(n_valid > (n_parts - 1) * part, n_valid <= n_parts * part))
        def _(n_parts=n_parts):
            rows = pl.ds(0, n_parts * part)
            acc_ref[rows, :] += _swiglu_chunk(x16_scr[rows, :], *weights())

    @pl.when(jnp.logical_and(partial, c == last))
    def _():
        _store_tiled_rows(o_ref, acc_ref[...])


def _expert_ffn(xs, tile_e, n_tiles, n_valid, wg, wu, wd, *, tm, tf):
    d_model = wg.shape[1]
    d_ff = wg.shape[2]
    n_chunks = d_ff // tf
    last = n_chunks - 1

    def row_map(i, c, te, nt, nv):
        return (jnp.minimum(i, nt[0] - 1), 0)

    def chunk_of(i, c, nt):
        return jnp.where(i < nt[0], c, last)

    grid_spec = pltpu.PrefetchScalarGridSpec(
        num_scalar_prefetch=3,
        grid=(xs.shape[0] // (tm * SUBLANES), n_chunks),
        in_specs=[
            pl.BlockSpec((tm * SUBLANES, LANES), row_map),
            pl.BlockSpec((None, d_model, tf), lambda i, c, te, nt, nv: (te[i], 0, chunk_of(i, c, nt))),
            pl.BlockSpec((None, d_model, tf), lambda i, c, te, nt, nv: (te[i], 0, chunk_of(i, c, nt))),
            pl.BlockSpec((None, tf, d_model), lambda i, c, te, nt, nv: (te[i], chunk_of(i, c, nt), 0)),
        ],
        out_specs=pl.BlockSpec((tm * SUBLANES, LANES), lambda i, c, te, nt, nv: (i, 0)),
        scratch_shapes=[pltpu.VMEM((tm, d_model), BF16), pltpu.VMEM((tm, d_model), F32)],
    )
    return pl.pallas_call(
        _expert_kernel,
        out_shape=jax.ShapeDtypeStruct(xs.shape, F32),
        grid_spec=grid_spec,
        compiler_params=_params("arbitrary", "arbitrary"),
        name="expert_ffn",
    )(tile_e, n_tiles, n_valid, xs, wg, wu, wd)


def _combine_kernel(pos0_ref, pos1_ref, pos0_next_ref, pos1_next_ref, res_ref, route_ref, g_ref, b_ref, ys_hbm,
                    o_ref, buf_ref, sem):
    i = pl.program_id(0)
    tq = o_ref.shape[0]
    slot = lax.bitwise_and(i, 1)

    def gather(positions, dst_slot):
        def start(t, carry):
            for k in range(TOP_K):
                pltpu.make_async_copy(_tiled_row(ys_hbm, positions[k][t]),
                                      _tiled_row(buf_ref.at[dst_slot, k], t), sem.at[dst_slot]).start(priority=k)
            return carry
        lax.fori_loop(0, tq, start, 0, unroll=8)

    @pl.when(i == 0)
    def _():
        gather((pos0_ref, pos1_ref), 0)

    @pl.when(i + 1 < pl.num_programs(0))
    def _():
        gather((pos0_next_ref, pos1_next_ref), 1 - slot)

    for k in range(TOP_K):
        pltpu.make_async_copy(ys_hbm.at[pl.ds(0, tq * SUBLANES)], buf_ref.at[slot, k], sem.at[slot]).wait()

    route = route_ref[...]
    y = (route[:, ROUTE_GATE:ROUTE_GATE + 1] * _load_tiled_rows(buf_ref.at[slot, 0])
         + route[:, ROUTE_GATE + 1:ROUTE_GATE + 2] * _load_tiled_rows(buf_ref.at[slot, 1]))
    o_ref[...] = _layer_norm(ALPHA * _load_tiled_rows(res_ref) + y, g_ref[...], b_ref[...])


def _combine_ln(ys, pos, res_tiled, route, gain, bias, *, tq=256):
    t_len = res_tiled.shape[0] // SUBLANES
    d_model = SUBLANES * LANES
    n_steps = t_len // tq
    this_tile = pl.BlockSpec((tq,), lambda i: (i,), memory_space=pltpu.SMEM)
    next_tile = pl.BlockSpec((tq,), lambda i: (jnp.minimum(i + 1, n_steps - 1),), memory_space=pltpu.SMEM)
    return pl.pallas_call(
        _combine_kernel,
        out_shape=jax.ShapeDtypeStruct((t_len, d_model), F32),
        grid=(n_steps,),
        in_specs=[
            this_tile, this_tile, next_tile, next_tile,
            pl.BlockSpec((tq * SUBLANES, LANES), lambda i: (i, 0)),
            pl.BlockSpec((tq, LANES), lambda i: (i, 0)),
            pl.BlockSpec((1, d_model), lambda i: (0, 0)),
            pl.BlockSpec((1, d_model), lambda i: (0, 0)),
            pl.BlockSpec(memory_space=pl.ANY),
        ],
        out_specs=pl.BlockSpec((tq, d_model), lambda i: (i, 0)),
        scratch_shapes=[pltpu.VMEM((2, TOP_K, tq * SUBLANES, LANES), F32), pltpu.SemaphoreType.DMA((2,))],
        compiler_params=_params("arbitrary"),
        name="moe_combine_ln",
    )(pos[0], pos[1], pos[0], pos[1], res_tiled, route, gain.reshape(1, -1), bias.reshape(1, -1), ys)


def _moe_ffn_ln(x_tiled, router_w, router_b, wg, wu, wd, gain, bias, *, tm=MOE_ROW_TILE, tf=MOE_FF_CHUNK):
    t_len = x_tiled.shape[0] // SUBLANES
    route, route_t, counts = _router(x_tiled, router_w, router_b)
    experts = route_t[ROUTE_E:ROUTE_E + TOP_K].astype(I32)
    ranks = route_t[ROUTE_RANK:ROUTE_RANK + TOP_K].astype(I32)
    counts = counts[0, :N_EXPERTS].astype(I32)
    tiles_per_expert = (counts + tm - 1) // tm
    tile_end = jnp.cumsum(tiles_per_expert)
    row_start = (tile_end - tiles_per_expert) * tm
    pos = ranks + sum(jnp.where(experts == e, row_start[e], 0) for e in range(N_EXPERTS))
    n_tiles_max = (TOP_K * t_len) // tm + N_EXPERTS
    n_tiles = tile_end[-1:]
    all_tiles = jnp.arange(n_tiles_max, dtype=I32)
    tile_ids = jnp.minimum(all_tiles, n_tiles[0] - 1)
    tile_e = jnp.sum((tile_ids[:, None] >= tile_end[None, :-1]).astype(I32), axis=1)
    is_last = jnp.any((all_tiles[:, None] == tile_end[None, :] - 1) & (tiles_per_expert[None, :] > 0), axis=1)
    fill_tiles = (is_last | (all_tiles >= n_tiles[0])).astype(I32)

    first_tile = (tile_end - tiles_per_expert)[tile_e]
    n_valid = jnp.where(all_tiles < n_tiles[0], jnp.clip(counts[tile_e] - (all_tiles - first_tile) * tm, 0, tm), 0)

    xs = _dispatch(x_tiled, pos, fill_tiles, tm=tm, tq=MOE_DISPATCH_TILE)
    ys = _expert_ffn(xs, tile_e, n_tiles.astype(I32), n_valid.astype(I32), wg, wu, wd, tm=tm, tf=tf)
    return _combine_ln(ys, pos, x_tiled, route, gain, bias)


def _alibi_slopes():
    exps = np.arange(1, N_ATTN_HEADS + 1, dtype=np.float32) * np.float32(8.0 / N_ATTN_HEADS)
    return jnp.asarray(np.exp2(-exps).astype(np.float32))


def kernel(x, mem, a_w_in, a_w_mem_kv, a_w_out, a_ln1_g, a_ln1_b, a_ffn_gate, a_ffn_up, a_ffn_down, a_ln2_g, a_ln2_b, b_w_in, b_fourier_g, b_w_mem_kv, b_w_out, b_ln1_g, b_ln1_b, b_router_w, b_router_b, b_moe_gate, b_moe_up, b_moe_down, b_ln2_g, b_ln2_b):
    b_sz, s_len, d_model = x.shape
    t_len = b_sz * s_len
    x32 = x.reshape(t_len, d_model)
    a_w_in, a_w_mem_kv, a_w_out, a_ffn_gate, a_ffn_up, a_ffn_down, b_w_in, b_w_mem_kv, b_w_out = (
        w[0].astype(BF16) for w in (a_w_in, a_w_mem_kv, a_w_out, a_ffn_gate, a_ffn_up, a_ffn_down,
                                    b_w_in, b_w_mem_kv, b_w_out))

    n_grp = len(DILATIONS)
    qkv0, x16 = _proj_heads(x, a_w_in, (0, n_grp, 2 * n_grp, 3 * n_grp))
    qkv = [qkv0] + [_proj_heads(x16, a_w_in, (g, n_grp + g, 2 * n_grp + g), dil=DILATIONS[g])
                    for g in range(1, n_grp)]
    mem_kv, mem16 = _proj_heads(mem, a_w_mem_kv, (0, 1))
    attn = _dilated_attention(_alibi_slopes(), qkv).reshape(t_len, -1)
    memo = _memory_attention(qkv[0], 3 * HEADS_PER_BLOCK, mem_kv).reshape(t_len, -1)
    x32, x16 = _mixer_ffn_ln(attn, memo, a_w_out, x32, a_ln1_g[0], a_ln1_b[0],
                             a_ffn_gate, a_ffn_up, a_ffn_down, a_ln2_g[0], a_ln2_b[0])

    x16_b = x16.reshape(b_sz, s_len, d_model)
    u_q = _proj_heads(x16_b, b_w_in, (0, 1), gain=b_fourier_g[0], ln_blocks=(True, False))
    mem_kv = _proj_heads(mem16, b_w_mem_kv, (0, 1))
    four = _fourier_mix(u_q).reshape(t_len, -1)
    memo = _memory_attention(u_q, HEADS_PER_BLOCK, mem_kv).reshape(t_len, -1)
    x_tiled = _out_proj_ln(four, memo, b_w_out, x32, b_ln1_g[0], b_ln1_b[0])
    out = _moe_ffn_ln(x_tiled, b_router_w[0], b_router_b[0], b_moe_gate[0], b_moe_up[0], b_moe_down[0],
                      b_ln2_g[0], b_ln2_b[0])
    return out.reshape(b_sz, s_len, d_model)
```

```python
import functools

import numpy as np
import jax
import jax.numpy as jnp
from jax import lax
from jax.experimental import pallas as pl
from jax.experimental.pallas import tpu as pltpu

F32 = jnp.float32
BF16 = jnp.bfloat16
I32 = jnp.int32

LANES = 128
HEAD_DIM = 128
HEADS_PER_BLOCK = 4
DILATIONS = (1, 4, 16)
N_SIDE = 64
ATTN_Q_BLOCK = 64
ATTN_K_WINDOW = 256
ATTN_BATCH = 32
PROJ_MAX_ROW_STRIDE = 4
N_ATTN_HEADS = 12
N_EXPERTS = 8
TOP_K = 2
MOE_ROW_TILE = 1024
MOE_FF_CHUNK = 512
MOE_DISPATCH_TILE = 1024
EXPERT_ROW_PARTS = 4
SUBLANES = 8
ALPHA = (2.0 * 2) ** 0.25
LN_EPS = 1e-5
NEG_INF = -1e30
VMEM_LIMIT_BYTES = 56 * 1024 * 1024

ROUTE_E, ROUTE_RANK, ROUTE_GATE = 0, 2, 4


def _params(*sem):
    return pltpu.CompilerParams(dimension_semantics=sem, vmem_limit_bytes=VMEM_LIMIT_BYTES)


def _layer_norm(z, g, b=None):
    mu = jnp.mean(z, axis=-1, keepdims=True)
    zc = z - mu
    var = jnp.mean(zc * zc, axis=-1, keepdims=True)
    y = zc * lax.rsqrt(var + LN_EPS) * g
    return y if b is None else y + b


def _proj_kernel(*refs, dil, row_chunk, ln_blocks, cast_x):
    n_blocks = len(ln_blocks)
    x_ref, w_refs = refs[0], refs[1:1 + n_blocks]
    rest = refs[1 + n_blocks:]
    g_ref = None
    if any(ln_blocks):
        g_ref, rest = rest[0], rest[1:]
    o_ref, scratch = rest[0], rest[1:]
    if cast_x:
        x16_ref, scratch = scratch[0], scratch[1:]
        x16_ref[...] = x_ref[...].astype(BF16)
        x_ref = x16_ref
    s_len = x_ref.shape[0]
    for blk in range(n_blocks):
        w = w_refs[blk][...]
        acc_ref = scratch[blk % 2] if dil > 1 else None
        for rc in range(s_len // row_chunk):
            rows = pl.ds(rc * row_chunk, row_chunk)
            r = jnp.dot(x_ref[rows, :], w, preferred_element_type=F32)
            for hh in range(HEADS_PER_BLOCK):
                rh = r[:, hh * HEAD_DIM:(hh + 1) * HEAD_DIM]
                if ln_blocks[blk]:
                    rh = _layer_norm(rh, g_ref[hh:hh + 1, :])
                if dil == 1:
                    o_ref[blk * HEADS_PER_BLOCK + hh, 0, rows, :] = rh.astype(o_ref.dtype)
                else:
                    acc_ref[hh, rows, :] = rh
        if dil == 1:
            continue
        for hh in range(HEADS_PER_BLOCK):
            head = blk * HEADS_PER_BLOCK + hh
            if dil <= PROJ_MAX_ROW_STRIDE:
                for r_ in range(dil):
                    o_ref[head, r_, :, :] = acc_ref.at[hh][pl.ds(r_, s_len // dil, stride=dil), :].astype(o_ref.dtype)
            else:
                s1, s2 = PROJ_MAX_ROW_STRIDE, dil // PROJ_MAX_ROW_STRIDE
                tmp_ref = scratch[2]
                for a in range(s1):
                    tmp_ref[a] = acc_ref.at[hh][pl.ds(a, s_len // s1, stride=s1), :]
                for a in range(s1):
                    for b in range(s2):
                        o_ref[head, s1 * b + a, :, :] = tmp_ref.at[a][pl.ds(b, s_len // dil, stride=s2), :].astype(
                            o_ref.dtype)


def _proj_heads(x, w, col_blocks, *, dil=1, gain=None, ln_blocks=None):
    b_sz, s_len, d_model = x.shape
    cast_x = x.dtype != BF16
    blk_cols = HEADS_PER_BLOCK * HEAD_DIM
    n_blocks = len(col_blocks)
    ln_blocks = tuple(ln_blocks) if ln_blocks is not None else (False,) * n_blocks
    in_specs = [pl.BlockSpec((None, s_len, d_model), lambda b: (b, 0, 0))]
    in_specs += [pl.BlockSpec((d_model, blk_cols), lambda b, cb=cb: (0, cb), pipeline_mode=pl.Buffered(1))
                 for cb in col_blocks]
    args = [x] + [w] * n_blocks
    if any(ln_blocks):
        in_specs.append(pl.BlockSpec((HEADS_PER_BLOCK, HEAD_DIM), lambda b: (0, 0)))
        args.append(gain)
    scratch = []
    if dil > 1:
        scratch += [pltpu.VMEM((HEADS_PER_BLOCK, s_len, HEAD_DIM), F32)] * 2
    if dil > PROJ_MAX_ROW_STRIDE:
        scratch.append(pltpu.VMEM((PROJ_MAX_ROW_STRIDE, s_len // PROJ_MAX_ROW_STRIDE, HEAD_DIM), F32))
    n_heads = n_blocks * HEADS_PER_BLOCK
    out_shape = jax.ShapeDtypeStruct((b_sz, n_heads, dil, s_len // dil, HEAD_DIM), BF16)
    out_specs = pl.BlockSpec((None, n_heads, dil, s_len // dil, HEAD_DIM), lambda b: (b, 0, 0, 0, 0))
    if cast_x:
        out_shape = (out_shape, jax.ShapeDtypeStruct(x.shape, BF16))
        out_specs = (out_specs, pl.BlockSpec((None, s_len, d_model), lambda b: (b, 0, 0)))
    return pl.pallas_call(
        functools.partial(_proj_kernel, dil=dil, row_chunk=min(512, s_len), ln_blocks=ln_blocks, cast_x=cast_x),
        out_shape=out_shape,
        grid=(b_sz,),
        in_specs=in_specs,
        out_specs=out_specs,
        scratch_shapes=scratch,
        compiler_params=_params("arbitrary"),
        name=f"proj_heads_d{dil}",
    )(*args)


def _dil_attn_kernel(slopes_ref, q0, k0, v0, q1, k1, v1, q2, k2, v2, o_ref, o_scr, l_scr, bias_scr):
    s_len = o_ref.shape[0]
    h = pl.program_id(1)
    scale = HEAD_DIM ** -0.5
    qb = ATTN_Q_BLOCK
    groups = ((q0, k0, v0), (q1, k1, v1), (q2, k2, v2))
    for g, (qr, kr, vr) in enumerate(groups):
        dil = DILATIONS[g]
        cls_len = s_len // dil
        kw = min(ATTN_K_WINDOW, cls_len)
        nb = cls_len // qb
        nb_shift = nb.bit_length() - 1

        neg_slope = jnp.full((1, kw), slopes_ref[g * HEADS_PER_BLOCK + h], F32) * (-float(dil))
        base_delta = (lax.broadcasted_iota(I32, (qb, kw), 1) - lax.broadcasted_iota(I32, (qb, kw), 0))
        for case in range(kw // N_SIDE):
            dist = jnp.abs(base_delta - case * N_SIDE)
            bias_scr[g, case, :, 0:kw] = jnp.where(dist <= N_SIDE, dist.astype(F32) * neg_slope, NEG_INF)

        def body(it, carry, qr=qr, kr=kr, vr=vr, dil=dil, cls_len=cls_len, kw=kw, nb=nb, g=g,
                 nb_shift=nb_shift):
            blocks = []
            for j in range(ATTN_BATCH):
                idx = it * ATTN_BATCH + j
                r = lax.shift_right_logical(idx, nb_shift)
                i = lax.bitwise_and(idx, nb - 1)
                qs = pl.multiple_of(i * qb, qb)
                ks = pl.multiple_of(jnp.clip(qs - N_SIDE, 0, cls_len - kw), N_SIDE)
                blocks.append((r, qs, ks))
            scores = []
            for r, qs, ks in blocks:
                q = qr[r, pl.ds(qs, qb), :]
                k = kr[r, pl.ds(ks, kw), :]
                scores.append(lax.dot_general(q, k, (((1,), (1,)), ((), ())), preferred_element_type=F32))
            probs = []
            for (r, qs, ks), s in zip(blocks, scores):
                bias = bias_scr[g, lax.shift_right_logical(qs - ks, N_SIDE.bit_length() - 1), :, 0:kw]
                s = s * scale + bias
                m = jnp.max(s, axis=-1, keepdims=True)
                p = jnp.exp(s - m)
                l = jnp.sum(p, axis=-1, keepdims=True)
                probs.append((p.astype(BF16), m, l))
            for (r, qs, ks), (p, m, l) in zip(blocks, probs):
                v = vr[r, pl.ds(ks, kw), :]
                o = jnp.dot(p, v, preferred_element_type=F32) / l
                lse = jnp.broadcast_to(m + jnp.log(l), (qb, LANES))
                row0 = qs * dil + r
                if dil == 1:
                    o_scr[g, pl.ds(row0, qb), :] = o
                    l_scr[g, pl.ds(row0, qb), :] = lse
                else:
                    o_scr.at[g][pl.ds(row0, qb, stride=dil), :] = o
                    l_scr.at[g][pl.ds(row0, qb, stride=dil), :] = lse
            return carry

        lax.fori_loop(0, dil * nb // ATTN_BATCH, body, 0)

    chunk = 256
    for c in range(s_len // chunk):
        rows = pl.ds(c * chunk, chunk)
        l0, l1, l2 = l_scr[0, rows, :], l_scr[1, rows, :], l_scr[2, rows, :]
        m = jnp.maximum(jnp.maximum(l0, l1), l2)
        e0, e1, e2 = jnp.exp(l0 - m), jnp.exp(l1 - m), jnp.exp(l2 - m)
        num = e0 * o_scr[0, rows, :] + e1 * o_scr[1, rows, :] + e2 * o_scr[2, rows, :]
        o_ref[rows, :] = (num / (e0 + e1 + e2)).astype(o_ref.dtype)


def _dilated_attention(slopes, qkv):
    b_sz = qkv[0].shape[0]
    s_len = qkv[0].shape[2] * qkv[0].shape[3]
    in_specs = [pl.BlockSpec(memory_space=pltpu.SMEM)]
    args = [slopes]
    for g, arr in enumerate(qkv):
        dil = DILATIONS[g]
        for which in range(3):
            in_specs.append(pl.BlockSpec((None, None, dil, s_len // dil, HEAD_DIM),
                                         lambda b, h, which=which: (b, which * HEADS_PER_BLOCK + h, 0, 0, 0)))
            args.append(arr)
    return pl.pallas_call(
        _dil_attn_kernel,
        out_shape=jax.ShapeDtypeStruct((b_sz, s_len, HEADS_PER_BLOCK * HEAD_DIM), BF16),
        grid=(b_sz, HEADS_PER_BLOCK),
        in_specs=in_specs,
        out_specs=pl.BlockSpec((None, s_len, HEAD_DIM), lambda b, h: (b, 0, h)),
        scratch_shapes=[pltpu.VMEM((3, s_len, HEAD_DIM), F32), pltpu.VMEM((3, s_len, LANES), F32),
                        pltpu.VMEM((3, ATTN_K_WINDOW // N_SIDE, ATTN_Q_BLOCK, ATTN_K_WINDOW), F32)],
        compiler_params=_params("arbitrary", "arbitrary"),
        name="dilated_attention",
    )(*args)


def _mem_attn_kernel(q_ref, kv_ref, o_ref):
    n_heads, s_len, _ = q_ref.shape
    scale = HEAD_DIM ** -0.5
    chunk = 256
    row_chunks = [pl.ds(c * chunk, chunk) for c in range(s_len // chunk)]
    for h in range(n_heads):
        k = kv_ref[h]
        v = kv_ref[n_heads + h]
        scores = [lax.dot_general(q_ref[h, rows, :], k, (((1,), (1,)), ((), ())), preferred_element_type=F32)
                  for rows in row_chunks]
        probs = []
        for s in scores:
            s = s * scale
            m = jnp.max(s, axis=-1, keepdims=True)
            p = jnp.exp(s - m)
            probs.append((p.astype(BF16), jnp.sum(p, axis=-1, keepdims=True)))
        for rows, (p, l) in zip(row_chunks, probs):
            o_ref[rows, h * HEAD_DIM:(h + 1) * HEAD_DIM] = (
                jnp.dot(p, v, preferred_element_type=F32) / l).astype(o_ref.dtype)


def _memory_attention(q_heads, q_head0, kv_heads):
    b_sz, _, _, s_len, _ = q_heads.shape
    n_mem = kv_heads.shape[3]
    assert q_head0 % HEADS_PER_BLOCK == 0
    return pl.pallas_call(
        _mem_attn_kernel,
        out_shape=jax.ShapeDtypeStruct((b_sz, s_len, HEADS_PER_BLOCK * HEAD_DIM), BF16),
        grid=(b_sz,),
        in_specs=[
            pl.BlockSpec((None, HEADS_PER_BLOCK, None, s_len, HEAD_DIM),
                         lambda b: (b, q_head0 // HEADS_PER_BLOCK, 0, 0, 0)),
            pl.BlockSpec((None, 2 * HEADS_PER_BLOCK, None, n_mem, HEAD_DIM), lambda b: (b, 0, 0, 0, 0)),
        ],
        out_specs=pl.BlockSpec((None, s_len, HEADS_PER_BLOCK * HEAD_DIM), lambda b: (b, 0, 0)),
        compiler_params=_params("arbitrary"),
        name="memory_attention",
    )(q_heads, kv_heads)


def _out_ln_route_kernel(a1_ref, a2_ref, w_ref, res_ref, g_ref, b_ref, rw_ref, rb_ref,
                         o_ref, route_ref, route_t_ref, cnt_ref, carry_ref):
    @pl.when(pl.program_id(0) == 0)
    def _():
        carry_ref[...] = jnp.zeros_like(carry_ref)

    k1 = a1_ref.shape[1]
    half = res_ref.shape[0] // 2
    halves = [pl.ds(hf * half, half) for hf in range(2)]
    ys = [jnp.dot(a1_ref[rows, :], w_ref[0:k1, :], preferred_element_type=F32)
          + jnp.dot(a2_ref[rows, :], w_ref[k1:, :], preferred_element_type=F32) for rows in halves]
    for hf, (rows, y) in enumerate(zip(halves, ys)):
        z = _layer_norm(ALPHA * res_ref[rows, :] + y, g_ref[...], b_ref[...])
        _store_tiled_rows(o_ref.at[pl.ds(hf * half * SUBLANES, half * SUBLANES)], z)
        rec = _route_tokens(z, rw_ref, rb_ref, carry_ref)
        route_ref[rows, :] = rec
        route_t_ref[:, rows] = rec.T[0:route_t_ref.shape[0], :]
    cnt_ref[...] = carry_ref[...]


def _out_proj_ln_route(a1, a2, w, res, gain, bias, router_w, router_b, *, tm=1024):
    t_len, d_model = res.shape
    assert d_model == SUBLANES * LANES
    k1, k2 = a1.shape[1], a2.shape[1]
    w_pad = jnp.zeros((d_model, LANES), F32).at[:, :N_EXPERTS].set(router_w)
    b_pad = jnp.zeros((1, LANES), F32).at[0, :N_EXPERTS].set(router_b)
    return pl.pallas_call(
        _out_ln_route_kernel,
        out_shape=(jax.ShapeDtypeStruct((t_len * SUBLANES, LANES), F32), jax.ShapeDtypeStruct((t_len, LANES), F32),
                   jax.ShapeDtypeStruct((SUBLANES, t_len), F32), jax.ShapeDtypeStruct((8, LANES), F32)),
        grid=(t_len // tm,),
        in_specs=[
            pl.BlockSpec((tm, k1), lambda i: (i, 0)),
            pl.BlockSpec((tm, k2), lambda i: (i, 0)),
            pl.BlockSpec((k1 + k2, d_model), lambda i: (0, 0), pipeline_mode=pl.Buffered(1)),
            pl.BlockSpec((tm, d_model), lambda i: (i, 0)),
            pl.BlockSpec((1, d_model), lambda i: (0, 0)),
            pl.BlockSpec((1, d_model), lambda i: (0, 0)),
            pl.BlockSpec((d_model, LANES), lambda i: (0, 0)),
            pl.BlockSpec((1, LANES), lambda i: (0, 0)),
        ],
        out_specs=(pl.BlockSpec((tm * SUBLANES, LANES), lambda i: (i, 0)),
                   pl.BlockSpec((tm, LANES), lambda i: (i, 0)),
                   pl.BlockSpec((SUBLANES, tm), lambda i: (0, i)),
                   pl.BlockSpec((8, LANES), lambda i: (0, 0))),
        scratch_shapes=[pltpu.VMEM((8, LANES), F32)],
        compiler_params=_params("arbitrary"),
        name="out_proj_ln_route",
    )(a1, a2, w, res, gain.reshape(1, -1), bias.reshape(1, -1), w_pad, b_pad)


def _swiglu_chunk(x, wg, wu, wd):
    hg = jnp.dot(x, wg, preferred_element_type=F32)
    hu = jnp.dot(x, wu, preferred_element_type=F32)
    h = hg * (1.0 / (1.0 + jnp.exp(-hg))) * hu
    return jnp.dot(h.astype(BF16), wd, preferred_element_type=F32)


def _mixer_ffn_kernel(a1_ref, a2_ref, wo_ref, res_ref, g1_ref, b1_ref, wg_ref, wu_ref, wd_ref, g2_ref, b2_ref,
                      o32_ref, o16_ref, *, tf):
    k1 = a1_ref.shape[1]
    tm = res_ref.shape[0]
    halves = [pl.ds(hf * (tm // 2), tm // 2) for hf in range(2)]
    ys = [jnp.dot(a1_ref[rows, :], wo_ref[0:k1, :], preferred_element_type=F32)
          + jnp.dot(a2_ref[rows, :], wo_ref[k1:, :], preferred_element_type=F32) for rows in halves]
    x1s = [_layer_norm(ALPHA * res_ref[rows, :] + y, g1_ref[...], b1_ref[...]) for rows, y in zip(halves, ys)]
    x16s = [x1.astype(BF16) for x1 in x1s]
    accs = [None, None]
    for f in range(wg_ref.shape[1] // tf):
        cols = pl.ds(f * tf, tf)
        for hf in range(2):
            part = _swiglu_chunk(x16s[hf], wg_ref[:, cols], wu_ref[:, cols], wd_ref[cols, :])
            accs[hf] = part if accs[hf] is None else accs[hf] + part
    for rows, x1, acc in zip(halves, x1s, accs):
        z = _layer_norm(ALPHA * x1 + acc, g2_ref[...], b2_ref[...])
        o32_ref[rows, :] = z
        o16_ref[rows, :] = z.astype(BF16)


def _mixer_ffn_ln(a1, a2, wo, res, g1, b1, wg, wu, wd, g2, b2, *, tm=1024, tf=256):
    t_len, d_model = res.shape
    k1, k2 = a1.shape[1], a2.shape[1]
    d_ff = wg.shape[1]
    resident = dict(pipeline_mode=pl.Buffered(1))
    vec = pl.BlockSpec((1, d_model), lambda i: (0, 0))
    return pl.pallas_call(
        functools.partial(_mixer_ffn_kernel, tf=tf),
        out_shape=(jax.ShapeDtypeStruct((t_len, d_model), F32), jax.ShapeDtypeStruct((t_len, d_model), BF16)),
        grid=(t_len // tm,),
        in_specs=[
            pl.BlockSpec((tm, k1), lambda i: (i, 0)),
            pl.BlockSpec((tm, k2), lambda i: (i, 0)),
            pl.BlockSpec((k1 + k2, d_model), lambda i: (0, 0), **resident),
            pl.BlockSpec((tm, d_model), lambda i: (i, 0)),
            vec, vec,
            pl.BlockSpec((d_model, d_ff), lambda i: (0, 0), **resident),
            pl.BlockSpec((d_model, d_ff), lambda i: (0, 0), **resident),
            pl.BlockSpec((d_ff, d_model), lambda i: (0, 0), **resident),
            vec, vec,
        ],
        out_specs=(pl.BlockSpec((tm, d_model), lambda i: (i, 0)),
                   pl.BlockSpec((tm, d_model), lambda i: (i, 0))),
        compiler_params=_params("arbitrary"),
        name="mixer_ffn_ln",
    )(a1, a2, wo, res, g1.reshape(1, -1), b1.reshape(1, -1), wg, wu, wd, g2.reshape(1, -1), b2.reshape(1, -1))


def _dft_constants(s_len):
    half = s_len // 2
    n = np.arange(half, dtype=np.int64)
    ang_s = 2.0 * np.pi * ((n[:, None] * n[None, :]) % s_len) / s_len
    c = np.arange(HEAD_DIM, dtype=np.int64)
    ang_c = 2.0 * np.pi * ((c[:, None] * c[None, :]) % HEAD_DIM) / HEAD_DIM
    norm = 1.0 / np.sqrt(float(s_len * HEAD_DIM))
    mirror = (n[:, None] + n[None, :] == half).astype(np.float64)
    return tuple(jnp.asarray(a, dtype=F32).astype(BF16)
                 for a in (np.cos(ang_s) * norm, np.sin(ang_s) * norm, mirror, np.cos(ang_c), np.sin(ang_c)))


def _fourier_kernel(u_ref, c1_ref, s1_ref, mirror_ref, cc_ref, sc_ref, o_ref, ae_scr, bo_scr, *, norm):
    n_grp, s_len, _ = u_ref.shape
    half = s_len // 2
    cc = cc_ref[...]
    sc = sc_ref[...]
    lanes = lambda parts: jnp.concatenate(parts, axis=1)

    u_low = lanes([u_ref[g, 0:half, :] for g in range(n_grp)])
    u_high = lanes([u_ref[g, half:, :] for g in range(n_grp)])
    u_mirror = jnp.dot(mirror_ref[...], u_high, preferred_element_type=F32)
    even = (u_low.astype(F32) + u_mirror).astype(BF16)
    odd = (u_low.astype(F32) - u_mirror).astype(BF16)

    row_sign = 1.0 - 2.0 * lax.bitwise_and(lax.broadcasted_iota(I32, (half, 1), 0), 1).astype(F32)
    pad_row = lax.broadcasted_iota(I32, (SUBLANES, 1), 0)
    mid_rows, alt_rows = [], []
    for g in range(n_grp):
        cols = slice(g * HEAD_DIM, (g + 1) * HEAD_DIM)
        ae_scr[:, cols] = jnp.dot(even[:, cols], cc, preferred_element_type=F32).astype(BF16)
        bo_scr[:, cols] = jnp.dot(odd[:, cols], sc, preferred_element_type=F32).astype(BF16)
        u_g = u_ref[g].astype(F32)
        alt = jnp.sum((u_g[0:half, :] + u_g[half:, :]) * row_sign, axis=0, keepdims=True)
        mid = u_g[half:half + 1, :]
        pad = jnp.where(pad_row == 0, mid, jnp.where(pad_row == 1, alt, 0.0)).astype(BF16)
        both = jnp.dot(pad, cc, preferred_element_type=F32) * norm
        mid_rows.append(both[0:1, :])
        alt_rows.append(both[1:2, :])
    mid_term = row_sign * lanes(mid_rows)
    p = jnp.dot(c1_ref[...], ae_scr[...], preferred_element_type=F32) + mid_term
    q = jnp.dot(s1_ref[...], bo_scr[...], preferred_element_type=F32)
    o_ref[0:half, :] = (p - q).astype(o_ref.dtype)
    upper = jnp.dot(mirror_ref[...], (p + q).astype(BF16), preferred_element_type=F32)
    is_row0 = lax.broadcasted_iota(I32, (half, 1), 0) == 0
    o_ref[half:, :] = jnp.where(is_row0, lanes(alt_rows), upper).astype(o_ref.dtype)


def _fourier_mix(u_heads):
    b_sz, _, _, s_len, _ = u_heads.shape
    n_grp = HEADS_PER_BLOCK
    half = s_len // 2
    consts = _dft_constants(s_len)
    const_spec = lambda n: pl.BlockSpec((n, n), lambda b: (0, 0), pipeline_mode=pl.Buffered(1))
    return pl.pallas_call(
        functools.partial(_fourier_kernel, norm=1.0 / np.sqrt(float(s_len * HEAD_DIM))),
        out_shape=jax.ShapeDtypeStruct((b_sz, s_len, n_grp * HEAD_DIM), BF16),
        grid=(b_sz,),
        in_specs=[pl.BlockSpec((None, n_grp, None, s_len, HEAD_DIM), lambda b: (b, 0, 0, 0, 0)),
                  const_spec(half), const_spec(half), const_spec(half), const_spec(HEAD_DIM), const_spec(HEAD_DIM)],
        out_specs=pl.BlockSpec((None, s_len, n_grp * HEAD_DIM), lambda b: (b, 0, 0)),
        scratch_shapes=[pltpu.VMEM((half, n_grp * HEAD_DIM), BF16), pltpu.VMEM((half, n_grp * HEAD_DIM), BF16)],
        compiler_params=_params("arbitrary"),
        name="fourier_mix",
    )(u_heads, *consts)


def _split_bf16(a):
    hi = a.astype(BF16)
    lo = (a - hi.astype(F32)).astype(BF16)
    return hi, lo


def _route_tokens(x, w_ref, b_ref, carry_ref):
    tq = x.shape[0]
    x_hi, x_lo = _split_bf16(x)
    w_hi, w_lo = _split_bf16(w_ref[...])
    hi_terms = jnp.dot(x_hi, jnp.concatenate([w_hi, w_lo], axis=1), preferred_element_type=F32)
    logits = (hi_terms[:, 0:LANES] + hi_terms[:, LANES:]
              + jnp.dot(x_lo, w_hi, preferred_element_type=F32)) + b_ref[...]
    lane = lax.broadcasted_iota(I32, (tq, LANES), 1).astype(F32)
    logits = jnp.where(lane < N_EXPERTS, logits, NEG_INF)

    def top1(vals):
        m = jnp.max(vals, axis=-1, keepdims=True)
        idx = jnp.min(jnp.where(vals == m, lane, float(LANES)), axis=-1, keepdims=True)
        return m, idx

    m1, i1 = top1(logits)
    oh1 = lane == i1
    m2, i2 = top1(jnp.where(oh1, NEG_INF, logits))
    oh2 = lane == i2
    e2 = jnp.exp(m2 - m1)
    gate1 = 1.0 / (1.0 + e2)
    gate2 = e2 / (1.0 + e2)

    oh1f = oh1.astype(F32)
    oh2f = oh2.astype(F32)
    both = oh1f + oh2f
    tri = (lax.broadcasted_iota(I32, (tq, tq), 0) > lax.broadcasted_iota(I32, (tq, tq), 1)).astype(BF16)
    before = jnp.dot(tri, both.astype(BF16), preferred_element_type=F32) + carry_ref[0:1, :]
    rank1 = jnp.sum(oh1f * before, axis=-1, keepdims=True)
    rank2 = jnp.sum(oh2f * before, axis=-1, keepdims=True)
    carry_ref[...] = carry_ref[...] + jnp.sum(both, axis=0, keepdims=True)

    rec = jnp.zeros((tq, LANES), F32)
    for lane_idx, val in ((ROUTE_E, i1), (ROUTE_E + 1, i2),
                          (ROUTE_RANK, rank1), (ROUTE_RANK + 1, rank2),
                          (ROUTE_GATE, gate1), (ROUTE_GATE + 1, gate2)):
        rec = jnp.where(lane == lane_idx, val, rec)
    return rec


def _store_tiled_rows(dst_ref, val):
    n_rows, width = val.shape
    for j in range(width // LANES):
        dst_ref[pl.ds(j, n_rows, stride=width // LANES), :] = val[:, j * LANES:(j + 1) * LANES]


def _load_tiled_rows(src_ref):
    n_rows = src_ref.shape[0] // SUBLANES
    return jnp.concatenate([src_ref[pl.ds(j, n_rows, stride=SUBLANES), :] for j in range(SUBLANES)], axis=1)


def _tiled_row(ref, row):
    return ref.at[pl.ds(pl.multiple_of(row * SUBLANES, SUBLANES), SUBLANES)]


def _dispatch_kernel(fill_ref, pos0_ref, pos1_ref, x_ref, xs_hbm, zero_scr, sem, *, tm):
    tq = x_ref.shape[0] // SUBLANES
    pos_refs = (pos0_ref, pos1_ref)

    @pl.when(pl.program_id(0) == 0)
    def _():
        zero_scr[...] = jnp.zeros_like(zero_scr)

        def fill_copies(tile):
            copies = []
            for piece in range(tm // tq):
                row0 = pl.multiple_of((tile * tm + piece * tq) * SUBLANES, SUBLANES)
                copies.append(pltpu.make_async_copy(zero_scr, xs_hbm.at[pl.ds(row0, tq * SUBLANES)], sem))
            return copies

        def start_fill(tile, carry):
            @pl.when(fill_ref[tile] == 1)
            def _():
                for copy in fill_copies(tile):
                    copy.start()
            return carry

        def wait_fill(tile, carry):
            @pl.when(fill_ref[tile] == 1)
            def _():
                for copy in fill_copies(tile):
                    copy.wait()
            return carry

        lax.fori_loop(0, fill_ref.shape[0], start_fill, 0)
        lax.fori_loop(0, fill_ref.shape[0], wait_fill, 0)

    def start(t, carry):
        for k in range(TOP_K):
            pltpu.make_async_copy(_tiled_row(x_ref, t), _tiled_row(xs_hbm, pos_refs[k][t]), sem).start(priority=k)
        return carry

    lax.fori_loop(0, tq, start, 0, unroll=8)
    for k in range(TOP_K):
        pltpu.make_async_copy(x_ref, xs_hbm.at[pl.ds(0, tq * SUBLANES)], sem).wait()


def _dispatch(x_tiled, pos, fill_tiles, *, tm, tq=512):
    t_len = x_tiled.shape[0] // SUBLANES
    assert tm % tq == 0
    grid_spec = pltpu.PrefetchScalarGridSpec(
        num_scalar_prefetch=1,
        grid=(t_len // tq,),
        in_specs=[
            pl.BlockSpec((tq,), lambda i, fill: (i,), memory_space=pltpu.SMEM),
            pl.BlockSpec((tq,), lambda i, fill: (i,), memory_space=pltpu.SMEM),
            pl.BlockSpec((tq * SUBLANES, LANES), lambda i, fill: (i, 0)),
        ],
        out_specs=pl.BlockSpec(memory_space=pl.ANY),
        scratch_shapes=[pltpu.VMEM((tq * SUBLANES, LANES), x_tiled.dtype), pltpu.SemaphoreType.DMA],
    )
    return pl.pallas_call(
        functools.partial(_dispatch_kernel, tm=tm),
        out_shape=jax.ShapeDtypeStruct((fill_tiles.shape[0] * tm * SUBLANES, LANES), x_tiled.dtype),
        grid_spec=grid_spec,
        compiler_params=_params("arbitrary"),
        name="moe_dispatch",
    )(fill_tiles, pos[0], pos[1], x_tiled)


def _expert_kernel(tile_e_ref, nt_ref, nv_ref, x_ref, wg_ref, wu_ref, wd_ref, o_ref, x16_scr, acc_ref):
    del tile_e_ref, nt_ref
    c = pl.program_id(1)
    n_valid = nv_ref[pl.program_id(0)]
    part = x16_scr.shape[0] // EXPERT_ROW_PARTS
    last = pl.num_programs(1) - 1
    full = n_valid > (EXPERT_ROW_PARTS - 1) * part

    def weights():
        return wg_ref[...].astype(BF16), wu_ref[...].astype(BF16), wd_ref[...].astype(BF16)

    @pl.when(jnp.logical_and(full, c == 0))
    def _():
        wg, wu, wd = weights()
        for p in range(EXPERT_ROW_PARTS):
            rows = pl.ds(p * part, part)
            x16 = _load_tiled_rows(x_ref.at[pl.ds(p * part * SUBLANES, part * SUBLANES)]).astype(BF16)
            x16_scr[rows, :] = x16
            acc_ref[rows, :] = _swiglu_chunk(x16, wg, wu, wd)

    @pl.when(jnp.logical_and(full, jnp.logical_and(c > 0, c < last)))
    def _():
        acc_ref[...] += _swiglu_chunk(x16_scr[...], *weights())

    @pl.when(jnp.logical_and(full, c == last))
    def _():
        _store_tiled_rows(o_ref, acc_ref[...] + _swiglu_chunk(x16_scr[...], *weights()))

    partial = jnp.logical_not(full)

    @pl.when(jnp.logical_and(partial, c == 0))
    def _():
        acc_ref[...] = jnp.zeros_like(acc_ref)
        x16_scr[...] = _load_tiled_rows(x_ref).astype(BF16)

    for n_parts in range(1, EXPERT_ROW_PARTS):
        @pl.when(jnp.logical_and(n_valid > (n_parts - 1) * part, n_valid <= n_parts * part))
        def _(n_parts=n_parts):
            rows = pl.ds(0, n_parts * part)
            acc_ref[rows, :] += _swiglu_chunk(x16_scr[rows, :], *weights())

    @pl.when(jnp.logical_and(partial, c == last))
    def _():
        _store_tiled_rows(o_ref, acc_ref[...])


def _expert_ffn(xs, tile_e, n_tiles, n_valid, wg, wu, wd, *, tm, tf):
    d_model = wg.shape[1]
    d_ff = wg.shape[2]
    n_chunks = d_ff // tf
    last = n_chunks - 1

    def row_map(i, c, te, nt, nv):
        return (jnp.minimum(i, nt[0] - 1), 0)

    def chunk_of(i, c, nt):
        return jnp.where(i < nt[0], c, last)

    grid_spec = pltpu.PrefetchScalarGridSpec(
        num_scalar_prefetch=3,
        grid=(xs.shape[0] // (tm * SUBLANES), n_chunks),
        in_specs=[
            pl.BlockSpec((tm * SUBLANES, LANES), row_map),
            pl.BlockSpec((None, d_model, tf), lambda i, c, te, nt, nv: (te[i], 0, chunk_of(i, c, nt))),
            pl.BlockSpec((None, d_model, tf), lambda i, c, te, nt, nv: (te[i], 0, chunk_of(i, c, nt))),
            pl.BlockSpec((None, tf, d_model), lambda i, c, te, nt, nv: (te[i], chunk_of(i, c, nt), 0)),
        ],
        out_specs=pl.BlockSpec((tm * SUBLANES, LANES), lambda i, c, te, nt, nv: (i, 0)),
        scratch_shapes=[pltpu.VMEM((tm, d_model), BF16), pltpu.VMEM((tm, d_model), F32)],
    )
    return pl.pallas_call(
        _expert_kernel,
        out_shape=jax.ShapeDtypeStruct(xs.shape, F32),
        grid_spec=grid_spec,
        compiler_params=_params("arbitrary", "arbitrary"),
        name="expert_ffn",
    )(tile_e, n_tiles, n_valid, xs, wg, wu, wd)


def _combine_kernel(pos0_ref, pos1_ref, pos0_next_ref, pos1_next_ref, res_ref, route_ref, g_ref, b_ref, ys_hbm,
                    o_ref, buf_ref, sem):
    i = pl.program_id(0)
    tq = o_ref.shape[0]
    slot = lax.bitwise_and(i, 1)

    def gather(positions, dst_slot):
        def start(t, carry):
            for k in range(TOP_K):
                pltpu.make_async_copy(_tiled_row(ys_hbm, positions[k][t]),
                                      _tiled_row(buf_ref.at[dst_slot, k], t), sem.at[dst_slot]).start(priority=k)
            return carry
        lax.fori_loop(0, tq, start, 0, unroll=8)

    @pl.when(i == 0)
    def _():
        gather((pos0_ref, pos1_ref), 0)

    @pl.when(i + 1 < pl.num_programs(0))
    def _():
        gather((pos0_next_ref, pos1_next_ref), 1 - slot)

    for k in range(TOP_K):
        pltpu.make_async_copy(ys_hbm.at[pl.ds(0, tq * SUBLANES)], buf_ref.at[slot, k], sem.at[slot]).wait()

    route = route_ref[...]
    y = (route[:, ROUTE_GATE:ROUTE_GATE + 1] * _load_tiled_rows(buf_ref.at[slot, 0])
         + route[:, ROUTE_GATE + 1:ROUTE_GATE + 2] * _load_tiled_rows(buf_ref.at[slot, 1]))
    o_ref[...] = _layer_norm(ALPHA * _load_tiled_rows(res_ref) + y, g_ref[...], b_ref[...])


def _combine_ln(ys, pos, res_tiled, route, gain, bias, *, tq=256):
    t_len = res_tiled.shape[0] // SUBLANES
    d_model = SUBLANES * LANES
    n_steps = t_len // tq
    this_tile = pl.BlockSpec((tq,), lambda i: (i,), memory_space=pltpu.SMEM)
    next_tile = pl.BlockSpec((tq,), lambda i: (jnp.minimum(i + 1, n_steps - 1),), memory_space=pltpu.SMEM)
    return pl.pallas_call(
        _combine_kernel,
        out_shape=jax.ShapeDtypeStruct((t_len, d_model), F32),
        grid=(n_steps,),
        in_specs=[
            this_tile, this_tile, next_tile, next_tile,
            pl.BlockSpec((tq * SUBLANES, LANES), lambda i: (i, 0)),
            pl.BlockSpec((tq, LANES), lambda i: (i, 0)),
            pl.BlockSpec((1, d_model), lambda i: (0, 0)),
            pl.BlockSpec((1, d_model), lambda i: (0, 0)),
            pl.BlockSpec(memory_space=pl.ANY),
        ],
        out_specs=pl.BlockSpec((tq, d_model), lambda i: (i, 0)),
        scratch_shapes=[pltpu.VMEM((2, TOP_K, tq * SUBLANES, LANES), F32), pltpu.SemaphoreType.DMA((2,))],
        compiler_params=_params("arbitrary"),
        name="moe_combine_ln",
    )(pos[0], pos[1], pos[0], pos[1], res_tiled, route, gain.reshape(1, -1), bias.reshape(1, -1), ys)


def _moe_ffn_ln(x_tiled, route, route_t, counts, wg, wu, wd, gain, bias, *, tm=MOE_ROW_TILE, tf=MOE_FF_CHUNK):
    t_len = x_tiled.shape[0] // SUBLANES
    experts = route_t[ROUTE_E:ROUTE_E + TOP_K].astype(I32)
    ranks = route_t[ROUTE_RANK:ROUTE_RANK + TOP_K].astype(I32)
    counts = counts[0, :N_EXPERTS].astype(I32)
    tiles_per_expert = (counts + tm - 1) // tm
    tile_end = jnp.cumsum(tiles_per_expert)
    row_start = (tile_end - tiles_per_expert) * tm
    pos = ranks + sum(jnp.where(experts == e, row_start[e], 0) for e in range(N_EXPERTS))
    n_tiles_max = (TOP_K * t_len) // tm + N_EXPERTS
    n_tiles = tile_end[-1:]
    all_tiles = jnp.arange(n_tiles_max, dtype=I32)
    tile_ids = jnp.minimum(all_tiles, n_tiles[0] - 1)
    tile_e = jnp.sum((tile_ids[:, None] >= tile_end[None, :-1]).astype(I32), axis=1)
    is_last = jnp.any((all_tiles[:, None] == tile_end[None, :] - 1) & (tiles_per_expert[None, :] > 0), axis=1)
    fill_tiles = (is_last | (all_tiles >= n_tiles[0])).astype(I32)

    first_tile = (tile_end - tiles_per_expert)[tile_e]
    n_valid = jnp.where(all_tiles < n_tiles[0], jnp.clip(counts[tile_e] - (all_tiles - first_tile) * tm, 0, tm), 0)

    xs = _dispatch(x_tiled, pos, fill_tiles, tm=tm, tq=MOE_DISPATCH_TILE)
    ys = _expert_ffn(xs, tile_e, n_tiles.astype(I32), n_valid.astype(I32), wg, wu, wd, tm=tm, tf=tf)
    return _combine_ln(ys, pos, x_tiled, route, gain, bias)


def _alibi_slopes():
    exps = np.arange(1, N_ATTN_HEADS + 1, dtype=np.float32) * np.float32(8.0 / N_ATTN_HEADS)
    return jnp.asarray(np.exp2(-exps).astype(np.float32))


def kernel(x, mem, a_w_in, a_w_mem_kv, a_w_out, a_ln1_g, a_ln1_b, a_ffn_gate, a_ffn_up, a_ffn_down, a_ln2_g, a_ln2_b, b_w_in, b_fourier_g, b_w_mem_kv, b_w_out, b_ln1_g, b_ln1_b, b_router_w, b_router_b, b_moe_gate, b_moe_up, b_moe_down, b_ln2_g, b_ln2_b):
    b_sz, s_len, d_model = x.shape
    t_len = b_sz * s_len
    x32 = x.reshape(t_len, d_model)
    a_w_in, a_w_mem_kv, a_w_out, a_ffn_gate, a_ffn_up, a_ffn_down, b_w_in, b_w_mem_kv, b_w_out = (
        w[0].astype(BF16) for w in (a_w_in, a_w_mem_kv, a_w_out, a_ffn_gate, a_ffn_up, a_ffn_down,
                                    b_w_in, b_w_mem_kv, b_w_out))

    n_grp = len(DILATIONS)
    qkv0, x16 = _proj_heads(x, a_w_in, (0, n_grp, 2 * n_grp, 3 * n_grp))
    qkv = [qkv0] + [_proj_heads(x16, a_w_in, (g, n_grp + g, 2 * n_grp + g), dil=DILATIONS[g])
                    for g in range(1, n_grp)]
    mem_kv, mem16 = _proj_heads(mem, a_w_mem_kv, (0, 1))
    attn = _dilated_attention(_alibi_slopes(), qkv).reshape(t_len, -1)
    memo = _memory_attention(qkv[0], 3 * HEADS_PER_BLOCK, mem_kv).reshape(t_len, -1)
    x32, x16 = _mixer_ffn_ln(attn, memo, a_w_out, x32, a_ln1_g[0], a_ln1_b[0],
                             a_ffn_gate, a_ffn_up, a_ffn_down, a_ln2_g[0], a_ln2_b[0])

    x16_b = x16.reshape(b_sz, s_len, d_model)
    u_q = _proj_heads(x16_b, b_w_in, (0, 1), gain=b_fourier_g[0], ln_blocks=(True, False))
    mem_kv = _proj_heads(mem16, b_w_mem_kv, (0, 1))
    four = _fourier_mix(u_q).reshape(t_len, -1)
    memo = _memory_attention(u_q, HEADS_PER_BLOCK, mem_kv).reshape(t_len, -1)
    x_tiled, route, route_t, counts = _out_proj_ln_route(four, memo, b_w_out, x32, b_ln1_g[0], b_ln1_b[0],
                                                         b_router_w[0], b_router_b[0])
    out = _moe_ffn_ln(x_tiled, route, route_t, counts, b_moe_gate[0], b_moe_up[0], b_moe_down[0],
                      b_ln2_g[0], b_ln2_b[0])
    return out.reshape(b_sz, s_len, d_model)
```

```python
import functools

import numpy as np
import jax
import jax.numpy as jnp
from jax import lax
from jax.experimental import pallas as pl
from jax.experimental.pallas import tpu as pltpu

F32 = jnp.float32
BF16 = jnp.bfloat16
I32 = jnp.int32

LANES = 128
HEAD_DIM = 128
HEADS_PER_BLOCK = 4
DILATIONS = (1, 4, 16)
N_SIDE = 64
ATTN_Q_BLOCK = 64
ATTN_K_WINDOW = 256
ATTN_BATCH = 32
PROJ_MAX_ROW_STRIDE = 4
N_ATTN_HEADS = 12
N_EXPERTS = 8
TOP_K = 2
MOE_ROW_TILE = 1024
MOE_FF_CHUNK = 512
MOE_DISPATCH_TILE = 1024
EXPERT_WEIGHT_RING = 3
EXPERT_ROW_PARTS = 4
SUBLANES = 8
ALPHA = (2.0 * 2) ** 0.25
LN_EPS = 1e-5
NEG_INF = -1e30
VMEM_LIMIT_BYTES = 56 * 1024 * 1024

ROUTE_E, ROUTE_RANK, ROUTE_GATE = 0, 2, 4


def _params(*sem):
    return pltpu.CompilerParams(dimension_semantics=sem, vmem_limit_bytes=VMEM_LIMIT_BYTES)


def _layer_norm(z, g, b=None):
    mu = jnp.mean(z, axis=-1, keepdims=True)
    zc = z - mu
    var = jnp.mean(zc * zc, axis=-1, keepdims=True)
    y = zc * lax.rsqrt(var + LN_EPS) * g
    return y if b is None else y + b


def _proj_kernel(*refs, dil, row_chunk, ln_blocks, cast_x):
    n_blocks = len(ln_blocks)
    x_ref, w_refs = refs[0], refs[1:1 + n_blocks]
    rest = refs[1 + n_blocks:]
    g_ref = None
    if any(ln_blocks):
        g_ref, rest = rest[0], rest[1:]
    o_ref, scratch = rest[0], rest[1:]
    if cast_x:
        x16_ref, scratch = scratch[0], scratch[1:]
        x16_ref[...] = x_ref[...].astype(BF16)
        x_ref = x16_ref
    s_len = x_ref.shape[0]
    for blk in range(n_blocks):
        w = w_refs[blk][...]
        acc_ref = scratch[blk % 2] if dil > 1 else None
        for rc in range(s_len // row_chunk):
            rows = pl.ds(rc * row_chunk, row_chunk)
            r = jnp.dot(x_ref[rows, :], w, preferred_element_type=F32)
            for hh in range(HEADS_PER_BLOCK):
                rh = r[:, hh * HEAD_DIM:(hh + 1) * HEAD_DIM]
                if ln_blocks[blk]:
                    rh = _layer_norm(rh, g_ref[hh:hh + 1, :])
                if dil == 1:
                    o_ref[blk * HEADS_PER_BLOCK + hh, 0, rows, :] = rh.astype(o_ref.dtype)
                else:
                    acc_ref[hh, rows, :] = rh
        if dil == 1:
            continue
        for hh in range(HEADS_PER_BLOCK):
            head = blk * HEADS_PER_BLOCK + hh
            if dil <= PROJ_MAX_ROW_STRIDE:
                for r_ in range(dil):
                    o_ref[head, r_, :, :] = acc_ref.at[hh][pl.ds(r_, s_len // dil, stride=dil), :].astype(o_ref.dtype)
            else:
                s1, s2 = PROJ_MAX_ROW_STRIDE, dil // PROJ_MAX_ROW_STRIDE
                tmp_ref = scratch[2]
                for a in range(s1):
                    tmp_ref[a] = acc_ref.at[hh][pl.ds(a, s_len // s1, stride=s1), :]
                for a in range(s1):
                    for b in range(s2):
                        o_ref[head, s1 * b + a, :, :] = tmp_ref.at[a][pl.ds(b, s_len // dil, stride=s2), :].astype(
                            o_ref.dtype)


def _proj_heads(x, w, col_blocks, *, dil=1, gain=None, ln_blocks=None):
    b_sz, s_len, d_model = x.shape
    cast_x = x.dtype != BF16
    blk_cols = HEADS_PER_BLOCK * HEAD_DIM
    n_blocks = len(col_blocks)
    ln_blocks = tuple(ln_blocks) if ln_blocks is not None else (False,) * n_blocks
    in_specs = [pl.BlockSpec((None, s_len, d_model), lambda b: (b, 0, 0))]
    in_specs += [pl.BlockSpec((d_model, blk_cols), lambda b, cb=cb: (0, cb), pipeline_mode=pl.Buffered(1))
                 for cb in col_blocks]
    args = [x] + [w] * n_blocks
    if any(ln_blocks):
        in_specs.append(pl.BlockSpec((HEADS_PER_BLOCK, HEAD_DIM), lambda b: (0, 0)))
        args.append(gain)
    scratch = []
    if dil > 1:
        scratch += [pltpu.VMEM((HEADS_PER_BLOCK, s_len, HEAD_DIM), F32)] * 2
    if dil > PROJ_MAX_ROW_STRIDE:
        scratch.append(pltpu.VMEM((PROJ_MAX_ROW_STRIDE, s_len // PROJ_MAX_ROW_STRIDE, HEAD_DIM), F32))
    n_heads = n_blocks * HEADS_PER_BLOCK
    out_shape = jax.ShapeDtypeStruct((b_sz, n_heads, dil, s_len // dil, HEAD_DIM), BF16)
    out_specs = pl.BlockSpec((None, n_heads, dil, s_len // dil, HEAD_DIM), lambda b: (b, 0, 0, 0, 0))
    if cast_x:
        out_shape = (out_shape, jax.ShapeDtypeStruct(x.shape, BF16))
        out_specs = (out_specs, pl.BlockSpec((None, s_len, d_model), lambda b: (b, 0, 0)))
    return pl.pallas_call(
        functools.partial(_proj_kernel, dil=dil, row_chunk=min(512, s_len), ln_blocks=ln_blocks, cast_x=cast_x),
        out_shape=out_shape,
        grid=(b_sz,),
        in_specs=in_specs,
        out_specs=out_specs,
        scratch_shapes=scratch,
        compiler_params=_params("arbitrary"),
        name=f"proj_heads_d{dil}",
    )(*args)


def _dil_attn_kernel(slopes_ref, q0, k0, v0, q1, k1, v1, q2, k2, v2, o_ref, o_scr, l_scr, bias_scr):
    s_len = o_ref.shape[0]
    h = pl.program_id(1)
    scale = HEAD_DIM ** -0.5
    qb = ATTN_Q_BLOCK
    groups = ((q0, k0, v0), (q1, k1, v1), (q2, k2, v2))
    for g, (qr, kr, vr) in enumerate(groups):
        dil = DILATIONS[g]
        cls_len = s_len // dil
        kw = min(ATTN_K_WINDOW, cls_len)
        nb = cls_len // qb
        nb_shift = nb.bit_length() - 1

        neg_slope = jnp.full((1, kw), slopes_ref[g * HEADS_PER_BLOCK + h], F32) * (-float(dil))
        base_delta = (lax.broadcasted_iota(I32, (qb, kw), 1) - lax.broadcasted_iota(I32, (qb, kw), 0))
        for case in range(kw // N_SIDE):
            dist = jnp.abs(base_delta - case * N_SIDE)
            bias_scr[g, case, :, 0:kw] = jnp.where(dist <= N_SIDE, dist.astype(F32) * neg_slope, NEG_INF)

        def body(it, carry, qr=qr, kr=kr, vr=vr, dil=dil, cls_len=cls_len, kw=kw, nb=nb, g=g,
                 nb_shift=nb_shift):
            blocks = []
            for j in range(ATTN_BATCH):
                idx = it * ATTN_BATCH + j
                r = lax.shift_right_logical(idx, nb_shift)
                i = lax.bitwise_and(idx, nb - 1)
                qs = pl.multiple_of(i * qb, qb)
                ks = pl.multiple_of(jnp.clip(qs - N_SIDE, 0, cls_len - kw), N_SIDE)
                blocks.append((r, qs, ks))
            scores = []
            for r, qs, ks in blocks:
                q = qr[r, pl.ds(qs, qb), :]
                k = kr[r, pl.ds(ks, kw), :]
                scores.append(lax.dot_general(q, k, (((1,), (1,)), ((), ())), preferred_element_type=F32))
            probs = []
            for (r, qs, ks), s in zip(blocks, scores):
                bias = bias_scr[g, lax.shift_right_logical(qs - ks, N_SIDE.bit_length() - 1), :, 0:kw]
                s = s * scale + bias
                m = jnp.max(s, axis=-1, keepdims=True)
                p = jnp.exp(s - m)
                l = jnp.sum(p, axis=-1, keepdims=True)
                probs.append((p.astype(BF16), m, l))
            for (r, qs, ks), (p, m, l) in zip(blocks, probs):
                v = vr[r, pl.ds(ks, kw), :]
                o = jnp.dot(p, v, preferred_element_type=F32) / l
                lse = jnp.broadcast_to(m + jnp.log(l), (qb, LANES))
                row0 = qs * dil + r
                if dil == 1:
                    o_scr[g, pl.ds(row0, qb), :] = o
                    l_scr[g, pl.ds(row0, qb), :] = lse
                else:
                    o_scr.at[g][pl.ds(row0, qb, stride=dil), :] = o
                    l_scr.at[g][pl.ds(row0, qb, stride=dil), :] = lse
            return carry

        lax.fori_loop(0, dil * nb // ATTN_BATCH, body, 0)

    chunk = 256
    for c in range(s_len // chunk):
        rows = pl.ds(c * chunk, chunk)
        l0, l1, l2 = l_scr[0, rows, :], l_scr[1, rows, :], l_scr[2, rows, :]
        m = jnp.maximum(jnp.maximum(l0, l1), l2)
        e0, e1, e2 = jnp.exp(l0 - m), jnp.exp(l1 - m), jnp.exp(l2 - m)
        num = e0 * o_scr[0, rows, :] + e1 * o_scr[1, rows, :] + e2 * o_scr[2, rows, :]
        o_ref[rows, :] = (num / (e0 + e1 + e2)).astype(o_ref.dtype)


def _dilated_attention(slopes, qkv):
    b_sz = qkv[0].shape[0]
    s_len = qkv[0].shape[2] * qkv[0].shape[3]
    in_specs = [pl.BlockSpec(memory_space=pltpu.SMEM)]
    args = [slopes]
    for g, arr in enumerate(qkv):
        dil = DILATIONS[g]
        for which in range(3):
            in_specs.append(pl.BlockSpec((None, None, dil, s_len // dil, HEAD_DIM),
                                         lambda b, h, which=which: (b, which * HEADS_PER_BLOCK + h, 0, 0, 0)))
            args.append(arr)
    return pl.pallas_call(
        _dil_attn_kernel,
        out_shape=jax.ShapeDtypeStruct((b_sz, s_len, HEADS_PER_BLOCK * HEAD_DIM), BF16),
        grid=(b_sz, HEADS_PER_BLOCK),
        in_specs=in_specs,
        out_specs=pl.BlockSpec((None, s_len, HEAD_DIM), lambda b, h: (b, 0, h)),
        scratch_shapes=[pltpu.VMEM((3, s_len, HEAD_DIM), F32), pltpu.VMEM((3, s_len, LANES), F32),
                        pltpu.VMEM((3, ATTN_K_WINDOW // N_SIDE, ATTN_Q_BLOCK, ATTN_K_WINDOW), F32)],
        compiler_params=_params("arbitrary", "arbitrary"),
        name="dilated_attention",
    )(*args)


def _mem_attn_kernel(q_ref, kv_ref, o_ref):
    n_heads, s_len, _ = q_ref.shape
    scale = HEAD_DIM ** -0.5
    chunk = 256
    row_chunks = [pl.ds(c * chunk, chunk) for c in range(s_len // chunk)]
    for h in range(n_heads):
        k = kv_ref[h]
        v = kv_ref[n_heads + h]
        scores = [lax.dot_general(q_ref[h, rows, :], k, (((1,), (1,)), ((), ())), preferred_element_type=F32)
                  for rows in row_chunks]
        probs = []
        for s in scores:
            s = s * scale
            m = jnp.max(s, axis=-1, keepdims=True)
            p = jnp.exp(s - m)
            probs.append((p.astype(BF16), jnp.sum(p, axis=-1, keepdims=True)))
        for rows, (p, l) in zip(row_chunks, probs):
            o_ref[rows, h * HEAD_DIM:(h + 1) * HEAD_DIM] = (
                jnp.dot(p, v, preferred_element_type=F32) / l).astype(o_ref.dtype)


def _memory_attention(q_heads, q_head0, kv_heads):
    b_sz, _, _, s_len, _ = q_heads.shape
    n_mem = kv_heads.shape[3]
    assert q_head0 % HEADS_PER_BLOCK == 0
    return pl.pallas_call(
        _mem_attn_kernel,
        out_shape=jax.ShapeDtypeStruct((b_sz, s_len, HEADS_PER_BLOCK * HEAD_DIM), BF16),
        grid=(b_sz,),
        in_specs=[
            pl.BlockSpec((None, HEADS_PER_BLOCK, None, s_len, HEAD_DIM),
                         lambda b: (b, q_head0 // HEADS_PER_BLOCK, 0, 0, 0)),
            pl.BlockSpec((None, 2 * HEADS_PER_BLOCK, None, n_mem, HEAD_DIM), lambda b: (b, 0, 0, 0, 0)),
        ],
        out_specs=pl.BlockSpec((None, s_len, HEADS_PER_BLOCK * HEAD_DIM), lambda b: (b, 0, 0)),
        compiler_params=_params("arbitrary"),
        name="memory_attention",
    )(q_heads, kv_heads)


def _out_ln_route_kernel(a1_ref, a2_ref, w_ref, res_ref, g_ref, b_ref, rw_ref, rb_ref,
                         o_ref, route_ref, route_t_ref, cnt_ref, carry_ref):
    @pl.when(pl.program_id(0) == 0)
    def _():
        carry_ref[...] = jnp.zeros_like(carry_ref)

    k1 = a1_ref.shape[1]
    half = res_ref.shape[0] // 2
    halves = [pl.ds(hf * half, half) for hf in range(2)]
    ys = [jnp.dot(a1_ref[rows, :], w_ref[0:k1, :], preferred_element_type=F32)
          + jnp.dot(a2_ref[rows, :], w_ref[k1:, :], preferred_element_type=F32) for rows in halves]
    for hf, (rows, y) in enumerate(zip(halves, ys)):
        z = _layer_norm(ALPHA * res_ref[rows, :] + y, g_ref[...], b_ref[...])
        _store_tiled_rows(o_ref.at[pl.ds(hf * half * SUBLANES, half * SUBLANES)], z)
        rec = _route_tokens(z, rw_ref, rb_ref, carry_ref)
        route_ref[rows, :] = rec
        route_t_ref[:, rows] = rec.T[0:route_t_ref.shape[0], :]
    cnt_ref[...] = carry_ref[...]


def _out_proj_ln_route(a1, a2, w, res, gain, bias, router_w, router_b, *, tm=1024):
    t_len, d_model = res.shape
    assert d_model == SUBLANES * LANES
    k1, k2 = a1.shape[1], a2.shape[1]
    w_pad = jnp.zeros((d_model, LANES), F32).at[:, :N_EXPERTS].set(router_w)
    b_pad = jnp.zeros((1, LANES), F32).at[0, :N_EXPERTS].set(router_b)
    return pl.pallas_call(
        _out_ln_route_kernel,
        out_shape=(jax.ShapeDtypeStruct((t_len * SUBLANES, LANES), F32), jax.ShapeDtypeStruct((t_len, LANES), F32),
                   jax.ShapeDtypeStruct((SUBLANES, t_len), F32), jax.ShapeDtypeStruct((8, LANES), F32)),
        grid=(t_len // tm,),
        in_specs=[
            pl.BlockSpec((tm, k1), lambda i: (i, 0)),
            pl.BlockSpec((tm, k2), lambda i: (i, 0)),
            pl.BlockSpec((k1 + k2, d_model), lambda i: (0, 0), pipeline_mode=pl.Buffered(1)),
            pl.BlockSpec((tm, d_model), lambda i: (i, 0)),
            pl.BlockSpec((1, d_model), lambda i: (0, 0)),
            pl.BlockSpec((1, d_model), lambda i: (0, 0)),
            pl.BlockSpec((d_model, LANES), lambda i: (0, 0)),
            pl.BlockSpec((1, LANES), lambda i: (0, 0)),
        ],
        out_specs=(pl.BlockSpec((tm * SUBLANES, LANES), lambda i: (i, 0)),
                   pl.BlockSpec((tm, LANES), lambda i: (i, 0)),
                   pl.BlockSpec((SUBLANES, tm), lambda i: (0, i)),
                   pl.BlockSpec((8, LANES), lambda i: (0, 0))),
        scratch_shapes=[pltpu.VMEM((8, LANES), F32)],
        compiler_params=_params("arbitrary"),
        name="out_proj_ln_route",
    )(a1, a2, w, res, gain.reshape(1, -1), bias.reshape(1, -1), w_pad, b_pad)


def _swiglu_chunk(x, wg, wu, wd):
    hg = jnp.dot(x, wg, preferred_element_type=F32)
    hu = jnp.dot(x, wu, preferred_element_type=F32)
    h = hg * (1.0 / (1.0 + jnp.exp(-hg))) * hu
    return jnp.dot(h.astype(BF16), wd, preferred_element_type=F32)


def _mixer_ffn_kernel(a1_ref, a2_ref, wo_ref, res_ref, g1_ref, b1_ref, wg_ref, wu_ref, wd_ref, g2_ref, b2_ref,
                      o32_ref, o16_ref, *, tf):
    k1 = a1_ref.shape[1]
    tm = res_ref.shape[0]
    halves = [pl.ds(hf * (tm // 2), tm // 2) for hf in range(2)]
    ys = [jnp.dot(a1_ref[rows, :], wo_ref[0:k1, :], preferred_element_type=F32)
          + jnp.dot(a2_ref[rows, :], wo_ref[k1:, :], preferred_element_type=F32) for rows in halves]
    x1s = [_layer_norm(ALPHA * res_ref[rows, :] + y, g1_ref[...], b1_ref[...]) for rows, y in zip(halves, ys)]
    x16s = [x1.astype(BF16) for x1 in x1s]
    accs = [None, None]
    for f in range(wg_ref.shape[1] // tf):
        cols = pl.ds(f * tf, tf)
        for hf in range(2):
            part = _swiglu_chunk(x16s[hf], wg_ref[:, cols], wu_ref[:, cols], wd_ref[cols, :])
            accs[hf] = part if accs[hf] is None else accs[hf] + part
    for rows, x1, acc in zip(halves, x1s, accs):
        z = _layer_norm(ALPHA * x1 + acc, g2_ref[...], b2_ref[...])
        o32_ref[rows, :] = z
        o16_ref[rows, :] = z.astype(BF16)


def _mixer_ffn_ln(a1, a2, wo, res, g1, b1, wg, wu, wd, g2, b2, *, tm=1024, tf=256):
    t_len, d_model = res.shape
    k1, k2 = a1.shape[1], a2.shape[1]
    d_ff = wg.shape[1]
    resident = dict(pipeline_mode=pl.Buffered(1))
    vec = pl.BlockSpec((1, d_model), lambda i: (0, 0))
    return pl.pallas_call(
        functools.partial(_mixer_ffn_kernel, tf=tf),
        out_shape=(jax.ShapeDtypeStruct((t_len, d_model), F32), jax.ShapeDtypeStruct((t_len, d_model), BF16)),
        grid=(t_len // tm,),
        in_specs=[
            pl.BlockSpec((tm, k1), lambda i: (i, 0)),
            pl.BlockSpec((tm, k2), lambda i: (i, 0)),
            pl.BlockSpec((k1 + k2, d_model), lambda i: (0, 0), **resident),
            pl.BlockSpec((tm, d_model), lambda i: (i, 0)),
            vec, vec,
            pl.BlockSpec((d_model, d_ff), lambda i: (0, 0), **resident),
            pl.BlockSpec((d_model, d_ff), lambda i: (0, 0), **resident),
            pl.BlockSpec((d_ff, d_model), lambda i: (0, 0), **resident),
            vec, vec,
        ],
        out_specs=(pl.BlockSpec((tm, d_model), lambda i: (i, 0)),
                   pl.BlockSpec((tm, d_model), lambda i: (i, 0))),
        compiler_params=_params("arbitrary"),
        name="mixer_ffn_ln",
    )(a1, a2, wo, res, g1.reshape(1, -1), b1.reshape(1, -1), wg, wu, wd, g2.reshape(1, -1), b2.reshape(1, -1))


def _dft_constants(s_len):
    half = s_len // 2
    n = np.arange(half, dtype=np.int64)
    ang_s = 2.0 * np.pi * ((n[:, None] * n[None, :]) % s_len) / s_len
    c = np.arange(HEAD_DIM, dtype=np.int64)
    ang_c = 2.0 * np.pi * ((c[:, None] * c[None, :]) % HEAD_DIM) / HEAD_DIM
    norm = 1.0 / np.sqrt(float(s_len * HEAD_DIM))
    mirror = (n[:, None] + n[None, :] == half).astype(np.float64)
    return tuple(jnp.asarray(a, dtype=F32).astype(BF16)
                 for a in (np.cos(ang_s) * norm, np.sin(ang_s) * norm, mirror, np.cos(ang_c), np.sin(ang_c)))


def _fourier_kernel(u_ref, c1_ref, s1_ref, mirror_ref, cc_ref, sc_ref, o_ref, ae_scr, bo_scr, *, norm):
    n_grp, s_len, _ = u_ref.shape
    half = s_len // 2
    cc = cc_ref[...]
    sc = sc_ref[...]
    lanes = lambda parts: jnp.concatenate(parts, axis=1)

    u_low = lanes([u_ref[g, 0:half, :] for g in range(n_grp)])
    u_high = lanes([u_ref[g, half:, :] for g in range(n_grp)])
    u_mirror = jnp.dot(mirror_ref[...], u_high, preferred_element_type=F32)
    even = (u_low.astype(F32) + u_mirror).astype(BF16)
    odd = (u_low.astype(F32) - u_mirror).astype(BF16)

    row_sign = 1.0 - 2.0 * lax.bitwise_and(lax.broadcasted_iota(I32, (half, 1), 0), 1).astype(F32)
    pad_row = lax.broadcasted_iota(I32, (SUBLANES, 1), 0)
    mid_rows, alt_rows = [], []
    for g in range(n_grp):
        cols = slice(g * HEAD_DIM, (g + 1) * HEAD_DIM)
        ae_scr[:, cols] = jnp.dot(even[:, cols], cc, preferred_element_type=F32).astype(BF16)
        bo_scr[:, cols] = jnp.dot(odd[:, cols], sc, preferred_element_type=F32).astype(BF16)
        u_g = u_ref[g].astype(F32)
        alt = jnp.sum((u_g[0:half, :] + u_g[half:, :]) * row_sign, axis=0, keepdims=True)
        mid = u_g[half:half + 1, :]
        pad = jnp.where(pad_row == 0, mid, jnp.where(pad_row == 1, alt, 0.0)).astype(BF16)
        both = jnp.dot(pad, cc, preferred_element_type=F32) * norm
        mid_rows.append(both[0:1, :])
        alt_rows.append(both[1:2, :])
    mid_term = row_sign * lanes(mid_rows)
    p = jnp.dot(c1_ref[...], ae_scr[...], preferred_element_type=F32) + mid_term
    q = jnp.dot(s1_ref[...], bo_scr[...], preferred_element_type=F32)
    o_ref[0:half, :] = (p - q).astype(o_ref.dtype)
    upper = jnp.dot(mirror_ref[...], (p + q).astype(BF16), preferred_element_type=F32)
    is_row0 = lax.broadcasted_iota(I32, (half, 1), 0) == 0
    o_ref[half:, :] = jnp.where(is_row0, lanes(alt_rows), upper).astype(o_ref.dtype)


def _fourier_mix(u_heads):
    b_sz, _, _, s_len, _ = u_heads.shape
    n_grp = HEADS_PER_BLOCK
    half = s_len // 2
    consts = _dft_constants(s_len)
    const_spec = lambda n: pl.BlockSpec((n, n), lambda b: (0, 0), pipeline_mode=pl.Buffered(1))
    return pl.pallas_call(
        functools.partial(_fourier_kernel, norm=1.0 / np.sqrt(float(s_len * HEAD_DIM))),
        out_shape=jax.ShapeDtypeStruct((b_sz, s_len, n_grp * HEAD_DIM), BF16),
        grid=(b_sz,),
        in_specs=[pl.BlockSpec((None, n_grp, None, s_len, HEAD_DIM), lambda b: (b, 0, 0, 0, 0)),
                  const_spec(half), const_spec(half), const_spec(half), const_spec(HEAD_DIM), const_spec(HEAD_DIM)],
        out_specs=pl.BlockSpec((None, s_len, n_grp * HEAD_DIM), lambda b: (b, 0, 0)),
        scratch_shapes=[pltpu.VMEM((half, n_grp * HEAD_DIM), BF16), pltpu.VMEM((half, n_grp * HEAD_DIM), BF16)],
        compiler_params=_params("arbitrary"),
        name="fourier_mix",
    )(u_heads, *consts)


def _split_bf16(a):
    hi = a.astype(BF16)
    lo = (a - hi.astype(F32)).astype(BF16)
    return hi, lo


def _route_tokens(x, w_ref, b_ref, carry_ref):
    tq = x.shape[0]
    x_hi, x_lo = _split_bf16(x)
    w_hi, w_lo = _split_bf16(w_ref[...])
    hi_terms = jnp.dot(x_hi, jnp.concatenate([w_hi, w_lo], axis=1), preferred_element_type=F32)
    logits = (hi_terms[:, 0:LANES] + hi_terms[:, LANES:]
              + jnp.dot(x_lo, w_hi, preferred_element_type=F32)) + b_ref[...]
    lane = lax.broadcasted_iota(I32, (tq, LANES), 1).astype(F32)
    logits = jnp.where(lane < N_EXPERTS, logits, NEG_INF)

    def top1(vals):
        m = jnp.max(vals, axis=-1, keepdims=True)
        idx = jnp.min(jnp.where(vals == m, lane, float(LANES)), axis=-1, keepdims=True)
        return m, idx

    m1, i1 = top1(logits)
    oh1 = lane == i1
    m2, i2 = top1(jnp.where(oh1, NEG_INF, logits))
    oh2 = lane == i2
    e2 = jnp.exp(m2 - m1)
    gate1 = 1.0 / (1.0 + e2)
    gate2 = e2 / (1.0 + e2)

    oh1f = oh1.astype(F32)
    oh2f = oh2.astype(F32)
    both = oh1f + oh2f
    tri = (lax.broadcasted_iota(I32, (tq, tq), 0) > lax.broadcasted_iota(I32, (tq, tq), 1)).astype(BF16)
    before = jnp.dot(tri, both.astype(BF16), preferred_element_type=F32) + carry_ref[0:1, :]
    rank1 = jnp.sum(oh1f * before, axis=-1, keepdims=True)
    rank2 = jnp.sum(oh2f * before, axis=-1, keepdims=True)
    carry_ref[...] = carry_ref[...] + jnp.sum(both, axis=0, keepdims=True)

    rec = jnp.zeros((tq, LANES), F32)
    for lane_idx, val in ((ROUTE_E, i1), (ROUTE_E + 1, i2),
                          (ROUTE_RANK, rank1), (ROUTE_RANK + 1, rank2),
                          (ROUTE_GATE, gate1), (ROUTE_GATE + 1, gate2)):
        rec = jnp.where(lane == lane_idx, val, rec)
    return rec


def _store_tiled_rows(dst_ref, val):
    n_rows, width = val.shape
    for j in range(width // LANES):
        dst_ref[pl.ds(j, n_rows, stride=width // LANES), :] = val[:, j * LANES:(j + 1) * LANES]


def _load_tiled_rows(src_ref):
    n_rows = src_ref.shape[0] // SUBLANES
    return jnp.concatenate([src_ref[pl.ds(j, n_rows, stride=SUBLANES), :] for j in range(SUBLANES)], axis=1)


def _tiled_row(ref, row):
    return ref.at[pl.ds(pl.multiple_of(row * SUBLANES, SUBLANES), SUBLANES)]


def _dispatch_kernel(fill_ref, pos0_ref, pos1_ref, x_ref, xs_hbm, zero_scr, sem, *, tm):
    tq = x_ref.shape[0] // SUBLANES
    pos_refs = (pos0_ref, pos1_ref)

    @pl.when(pl.program_id(0) == 0)
    def _():
        zero_scr[...] = jnp.zeros_like(zero_scr)

        def fill_copies(tile):
            copies = []
            for piece in range(tm // tq):
                row0 = pl.multiple_of((tile * tm + piece * tq) * SUBLANES, SUBLANES)
                copies.append(pltpu.make_async_copy(zero_scr, xs_hbm.at[pl.ds(row0, tq * SUBLANES)], sem))
            return copies

        def start_fill(tile, carry):
            @pl.when(fill_ref[tile] == 1)
            def _():
                for copy in fill_copies(tile):
                    copy.start()
            return carry

        def wait_fill(tile, carry):
            @pl.when(fill_ref[tile] == 1)
            def _():
                for copy in fill_copies(tile):
                    copy.wait()
            return carry

        lax.fori_loop(0, fill_ref.shape[0], start_fill, 0)
        lax.fori_loop(0, fill_ref.shape[0], wait_fill, 0)

    def start(t, carry):
        for k in range(TOP_K):
            pltpu.make_async_copy(_tiled_row(x_ref, t), _tiled_row(xs_hbm, pos_refs[k][t]), sem).start(priority=k)
        return carry

    lax.fori_loop(0, tq, start, 0, unroll=8)
    for k in range(TOP_K):
        pltpu.make_async_copy(x_ref, xs_hbm.at[pl.ds(0, tq * SUBLANES)], sem).wait()


def _dispatch(x_tiled, pos, fill_tiles, *, tm, tq=512):
    t_len = x_tiled.shape[0] // SUBLANES
    assert tm % tq == 0
    grid_spec = pltpu.PrefetchScalarGridSpec(
        num_scalar_prefetch=1,
        grid=(t_len // tq,),
        in_specs=[
            pl.BlockSpec((tq,), lambda i, fill: (i,), memory_space=pltpu.SMEM),
            pl.BlockSpec((tq,), lambda i, fill: (i,), memory_space=pltpu.SMEM),
            pl.BlockSpec((tq * SUBLANES, LANES), lambda i, fill: (i, 0)),
        ],
        out_specs=pl.BlockSpec(memory_space=pl.ANY),
        scratch_shapes=[pltpu.VMEM((tq * SUBLANES, LANES), x_tiled.dtype), pltpu.SemaphoreType.DMA],
    )
    return pl.pallas_call(
        functools.partial(_dispatch_kernel, tm=tm),
        out_shape=jax.ShapeDtypeStruct((fill_tiles.shape[0] * tm * SUBLANES, LANES), x_tiled.dtype),
        grid_spec=grid_spec,
        compiler_params=_params("arbitrary"),
        name="moe_dispatch",
    )(fill_tiles, pos[0], pos[1], x_tiled)


def _expert_kernel(tile_e_ref, nt_ref, nv_ref, x_ref, wg_hbm, wu_hbm, wd_hbm, o_ref, x16_scr, acc_ref,
                   wg_buf, wu_buf, wd_buf, sem, *, n_chunks):
    del nt_ref
    i = pl.program_id(0)
    c = pl.program_id(1)
    n_tiles_max = pl.num_programs(0)
    n_valid = nv_ref[i]
    part = x16_scr.shape[0] // EXPERT_ROW_PARTS
    last = n_chunks - 1
    full = n_valid > (EXPERT_ROW_PARTS - 1) * part
    tf = wg_buf.shape[2]
    step = i * n_chunks + c
    slot = lax.rem(step, EXPERT_WEIGHT_RING)

    def weight_copies(tile, chunk, ring_slot):
        e = tile_e_ref[tile]
        cols = pl.ds(pl.multiple_of(chunk * tf, tf), tf)
        return (pltpu.make_async_copy(wg_hbm.at[e, :, cols], wg_buf.at[ring_slot], sem.at[ring_slot, 0]),
                pltpu.make_async_copy(wu_hbm.at[e, :, cols], wu_buf.at[ring_slot], sem.at[ring_slot, 1]),
                pltpu.make_async_copy(wd_hbm.at[e, cols, :], wd_buf.at[ring_slot], sem.at[ring_slot, 2]))

    def fetch(ahead):
        later = step + ahead
        tile = lax.div(later, n_chunks)
        chunk = later - tile * n_chunks

        @pl.when(jnp.logical_and(tile < n_tiles_max, nv_ref[jnp.minimum(tile, n_tiles_max - 1)] > 0))
        def _():
            for copy in weight_copies(tile, chunk, lax.rem(later, EXPERT_WEIGHT_RING)):
                copy.start()

    @pl.when(step == 0)
    def _():
        for ahead in range(EXPERT_WEIGHT_RING - 1):
            fetch(ahead)

    fetch(EXPERT_WEIGHT_RING - 1)

    @pl.when(n_valid > 0)
    def _():
        for copy in weight_copies(i, c, slot):
            copy.wait()

    def weights():
        return wg_buf[slot].astype(BF16), wu_buf[slot].astype(BF16), wd_buf[slot].astype(BF16)

    @pl.when(jnp.logical_and(full, c == 0))
    def _():
        wg, wu, wd = weights()
        for p in range(EXPERT_ROW_PARTS):
            rows = pl.ds(p * part, part)
            x16 = _load_tiled_rows(x_ref.at[pl.ds(p * part * SUBLANES, part * SUBLANES)]).astype(BF16)
            x16_scr[rows, :] = x16
            acc_ref[rows, :] = _swiglu_chunk(x16, wg, wu, wd)

    @pl.when(jnp.logical_and(full, jnp.logical_and(c > 0, c < last)))
    def _():
        acc_ref[...] += _swiglu_chunk(x16_scr[...], *weights())

    @pl.when(jnp.logical_and(full, c == last))
    def _():
        _store_tiled_rows(o_ref, acc_ref[...] + _swiglu_chunk(x16_scr[...], *weights()))

    partial = jnp.logical_not(full)

    @pl.when(jnp.logical_and(partial, c == 0))
    def _():
        acc_ref[...] = jnp.zeros_like(acc_ref)
        x16_scr[...] = _load_tiled_rows(x_ref).astype(BF16)

    for n_parts in range(1, EXPERT_ROW_PARTS):
        @pl.when(jnp.logical_and(n_valid > (n_parts - 1) * part, n_valid <= n_parts * part))
        def _(n_parts=n_parts):
            rows = pl.ds(0, n_parts * part)
            acc_ref[rows, :] += _swiglu_chunk(x16_scr[rows, :], *weights())

    @pl.when(jnp.logical_and(partial, c == last))
    def _():
        _store_tiled_rows(o_ref, acc_ref[...])


def _expert_ffn(xs, tile_e, n_tiles, n_valid, wg, wu, wd, *, tm, tf):
    d_model = wg.shape[1]
    d_ff = wg.shape[2]
    n_chunks = d_ff // tf
    last = n_chunks - 1

    def row_map(i, c, te, nt, nv):
        return (jnp.minimum(i, nt[0] - 1), 0)

    ring = EXPERT_WEIGHT_RING
    grid_spec = pltpu.PrefetchScalarGridSpec(
        num_scalar_prefetch=3,
        grid=(xs.shape[0] // (tm * SUBLANES), n_chunks),
        in_specs=[
            pl.BlockSpec((tm * SUBLANES, LANES), row_map),
            pl.BlockSpec(memory_space=pl.ANY),
            pl.BlockSpec(memory_space=pl.ANY),
            pl.BlockSpec(memory_space=pl.ANY),
        ],
        out_specs=pl.BlockSpec((tm * SUBLANES, LANES), lambda i, c, te, nt, nv: (i, 0)),
        scratch_shapes=[pltpu.VMEM((tm, d_model), BF16), pltpu.VMEM((tm, d_model), F32),
                        pltpu.VMEM((ring, d_model, tf), wg.dtype), pltpu.VMEM((ring, d_model, tf), wu.dtype),
                        pltpu.VMEM((ring, tf, d_model), wd.dtype), pltpu.SemaphoreType.DMA((ring, 3))],
    )
    return pl.pallas_call(
        functools.partial(_expert_kernel, n_chunks=n_chunks),
        out_shape=jax.ShapeDtypeStruct(xs.shape, F32),
        grid_spec=grid_spec,
        compiler_params=_params("arbitrary", "arbitrary"),
        name="expert_ffn",
    )(tile_e, n_tiles, n_valid, xs, wg, wu, wd)


def _combine_kernel(pos0_ref, pos1_ref, pos0_next_ref, pos1_next_ref, res_ref, route_ref, g_ref, b_ref, ys_hbm,
                    o_ref, buf_ref, sem):
    i = pl.program_id(0)
    tq = o_ref.shape[0]
    slot = lax.bitwise_and(i, 1)

    def gather(positions, dst_slot):
        def start(t, carry):
            for k in range(TOP_K):
                pltpu.make_async_copy(_tiled_row(ys_hbm, positions[k][t]),
                                      _tiled_row(buf_ref.at[dst_slot, k], t), sem.at[dst_slot]).start(priority=k)
            return carry
        lax.fori_loop(0, tq, start, 0, unroll=8)

    @pl.when(i == 0)
    def _():
        gather((pos0_ref, pos1_ref), 0)

    @pl.when(i + 1 < pl.num_programs(0))
    def _():
        gather((pos0_next_ref, pos1_next_ref), 1 - slot)

    for k in range(TOP_K):
        pltpu.make_async_copy(ys_hbm.at[pl.ds(0, tq * SUBLANES)], buf_ref.at[slot, k], sem.at[slot]).wait()

    route = route_ref[...]
    y = (route[:, ROUTE_GATE:ROUTE_GATE + 1] * _load_tiled_rows(buf_ref.at[slot, 0])
         + route[:, ROUTE_GATE + 1:ROUTE_GATE + 2] * _load_tiled_rows(buf_ref.at[slot, 1]))
    o_ref[...] = _layer_norm(ALPHA * _load_tiled_rows(res_ref) + y, g_ref[...], b_ref[...])


def _combine_ln(ys, pos, res_tiled, route, gain, bias, *, tq=256):
    t_len = res_tiled.shape[0] // SUBLANES
    d_model = SUBLANES * LANES
    n_steps = t_len // tq
    this_tile = pl.BlockSpec((tq,), lambda i: (i,), memory_space=pltpu.SMEM)
    next_tile = pl.BlockSpec((tq,), lambda i: (jnp.minimum(i + 1, n_steps - 1),), memory_space=pltpu.SMEM)
    return pl.pallas_call(
        _combine_kernel,
        out_shape=jax.ShapeDtypeStruct((t_len, d_model), F32),
        grid=(n_steps,),
        in_specs=[
            this_tile, this_tile, next_tile, next_tile,
            pl.BlockSpec((tq * SUBLANES, LANES), lambda i: (i, 0)),
            pl.BlockSpec((tq, LANES), lambda i: (i, 0)),
            pl.BlockSpec((1, d_model), lambda i: (0, 0)),
            pl.BlockSpec((1, d_model), lambda i: (0, 0)),
            pl.BlockSpec(memory_space=pl.ANY),
        ],
        out_specs=pl.BlockSpec((tq, d_model), lambda i: (i, 0)),
        scratch_shapes=[pltpu.VMEM((2, TOP_K, tq * SUBLANES, LANES), F32), pltpu.SemaphoreType.DMA((2,))],
        compiler_params=_params("arbitrary"),
        name="moe_combine_ln",
    )(pos[0], pos[1], pos[0], pos[1], res_tiled, route, gain.reshape(1, -1), bias.reshape(1, -1), ys)


def _moe_ffn_ln(x_tiled, route, route_t, counts, wg, wu, wd, gain, bias, *, tm=MOE_ROW_TILE, tf=MOE_FF_CHUNK):
    t_len = x_tiled.shape[0] // SUBLANES
    experts = route_t[ROUTE_E:ROUTE_E + TOP_K].astype(I32)
    ranks = route_t[ROUTE_RANK:ROUTE_RANK + TOP_K].astype(I32)
    counts = counts[0, :N_EXPERTS].astype(I32)
    tiles_per_expert = (counts + tm - 1) // tm
    tile_end = jnp.cumsum(tiles_per_expert)
    row_start = (tile_end - tiles_per_expert) * tm
    pos = ranks + sum(jnp.where(experts == e, row_start[e], 0) for e in range(N_EXPERTS))
    n_tiles_max = (TOP_K * t_len) // tm + N_EXPERTS
    n_tiles = tile_end[-1:]
    all_tiles = jnp.arange(n_tiles_max, dtype=I32)
    tile_ids = jnp.minimum(all_tiles, n_tiles[0] - 1)
    tile_e = jnp.sum((tile_ids[:, None] >= tile_end[None, :-1]).astype(I32), axis=1)
    is_last = jnp.any((all_tiles[:, None] == tile_end[None, :] - 1) & (tiles_per_expert[None, :] > 0), axis=1)
    fill_tiles = (is_last | (all_tiles >= n_tiles[0])).astype(I32)

    first_tile = (tile_end - tiles_per_expert)[tile_e]
    n_valid = jnp.where(all_tiles < n_tiles[0], jnp.clip(counts[tile_e] - (all_tiles - first_tile) * tm, 0, tm), 0)

    xs = _dispatch(x_tiled, pos, fill_tiles, tm=tm, tq=MOE_DISPATCH_TILE)
    ys = _expert_ffn(xs, tile_e, n_tiles.astype(I32), n_valid.astype(I32), wg, wu, wd, tm=tm, tf=tf)
    return _combine_ln(ys, pos, x_tiled, route, gain, bias)


def _alibi_slopes():
    exps = np.arange(1, N_ATTN_HEADS + 1, dtype=np.float32) * np.float32(8.0 / N_ATTN_HEADS)
    return jnp.asarray(np.exp2(-exps).astype(np.float32))


def kernel(x, mem, a_w_in, a_w_mem_kv, a_w_out, a_ln1_g, a_ln1_b, a_ffn_gate, a_ffn_up, a_ffn_down, a_ln2_g, a_ln2_b, b_w_in, b_fourier_g, b_w_mem_kv, b_w_out, b_ln1_g, b_ln1_b, b_router_w, b_router_b, b_moe_gate, b_moe_up, b_moe_down, b_ln2_g, b_ln2_b):
    b_sz, s_len, d_model = x.shape
    t_len = b_sz * s_len
    x32 = x.reshape(t_len, d_model)
    a_w_in, a_w_mem_kv, a_w_out, a_ffn_gate, a_ffn_up, a_ffn_down, b_w_in, b_w_mem_kv, b_w_out = (
        w[0].astype(BF16) for w in (a_w_in, a_w_mem_kv, a_w_out, a_ffn_gate, a_ffn_up, a_ffn_down,
                                    b_w_in, b_w_mem_kv, b_w_out))

    n_grp = len(DILATIONS)
    qkv0, x16 = _proj_heads(x, a_w_in, (0, n_grp, 2 * n_grp, 3 * n_grp))
    qkv = [qkv0] + [_proj_heads(x16, a_w_in, (g, n_grp + g, 2 * n_grp + g), dil=DILATIONS[g])
                    for g in range(1, n_grp)]
    mem_kv, mem16 = _proj_heads(mem, a_w_mem_kv, (0, 1))
    attn = _dilated_attention(_alibi_slopes(), qkv).reshape(t_len, -1)
    memo = _memory_attention(qkv[0], 3 * HEADS_PER_BLOCK, mem_kv).reshape(t_len, -1)
    x32, x16 = _mixer_ffn_ln(attn, memo, a_w_out, x32, a_ln1_g[0], a_ln1_b[0],
                             a_ffn_gate, a_ffn_up, a_ffn_down, a_ln2_g[0], a_ln2_b[0])

    x16_b = x16.reshape(b_sz, s_len, d_model)
    u_q = _proj_heads(x16_b, b_w_in, (0, 1), gain=b_fourier_g[0], ln_blocks=(True, False))
    mem_kv = _proj_heads(mem16, b_w_mem_kv, (0, 1))
    four = _fourier_mix(u_q).reshape(t_len, -1)
    memo = _memory_attention(u_q, HEADS_PER_BLOCK, mem_kv).reshape(t_len, -1)
    x_tiled, route, route_t, counts = _out_proj_ln_route(four, memo, b_w_out, x32, b_ln1_g[0], b_ln1_b[0],
                                                         b_router_w[0], b_router_b[0])
    out = _moe_ffn_ln(x_tiled, route, route_t, counts, b_moe_gate[0], b_moe_up[0], b_moe_down[0],
                      b_ln2_g[0], b_ln2_b[0])
    return out.reshape(b_sz, s_len, d_model)
```

```python
import functools

import numpy as np
import jax
import jax.numpy as jnp
from jax import lax
from jax.experimental import pallas as pl
from jax.experimental.pallas import tpu as pltpu

F32 = jnp.float32
BF16 = jnp.bfloat16
I32 = jnp.int32

LANES = 128
HEAD_DIM = 128
HEADS_PER_BLOCK = 4
DILATIONS = (1, 4, 16)
N_SIDE = 64
ATTN_Q_BLOCK = 64
ATTN_K_WINDOW = 256
ATTN_BATCH = 32
PROJ_MAX_ROW_STRIDE = 4
N_ATTN_HEADS = 12
N_EXPERTS = 8
TOP_K = 2
MOE_ROW_TILE = 1024
MOE_FF_CHUNK = 512
MOE_DISPATCH_TILE = 1024
EXPERT_WEIGHT_RING = 3
EXPERT_ROW_PARTS = 4
SUBLANES = 8
ALPHA = (2.0 * 2) ** 0.25
LN_EPS = 1e-5
NEG_INF = -1e30
VMEM_LIMIT_BYTES = 56 * 1024 * 1024

ROUTE_E, ROUTE_RANK, ROUTE_GATE = 0, 2, 4


def _params(*sem):
    return pltpu.CompilerParams(dimension_semantics=sem, vmem_limit_bytes=VMEM_LIMIT_BYTES)


def _layer_norm(z, g, b=None):
    mu = jnp.mean(z, axis=-1, keepdims=True)
    zc = z - mu
    var = jnp.mean(zc * zc, axis=-1, keepdims=True)
    y = zc * lax.rsqrt(var + LN_EPS) * g
    return y if b is None else y + b


def _proj_kernel(*refs, dil, row_chunk, ln_blocks, cast_x):
    n_blocks = len(ln_blocks)
    x_ref, w_refs = refs[0], refs[1:1 + n_blocks]
    rest = refs[1 + n_blocks:]
    g_ref = None
    if any(ln_blocks):
        g_ref, rest = rest[0], rest[1:]
    o_ref, scratch = rest[0], rest[1:]
    if cast_x:
        x16_ref, scratch = scratch[0], scratch[1:]
        x16_ref[...] = x_ref[...].astype(BF16)
        x_ref = x16_ref
    s_len = x_ref.shape[0]
    for blk in range(n_blocks):
        w = w_refs[blk][...]
        acc_ref = scratch[blk % 2] if dil > 1 else None
        for rc in range(s_len // row_chunk):
            rows = pl.ds(rc * row_chunk, row_chunk)
            r = jnp.dot(x_ref[rows, :], w, preferred_element_type=F32)
            for hh in range(HEADS_PER_BLOCK):
                rh = r[:, hh * HEAD_DIM:(hh + 1) * HEAD_DIM]
                if ln_blocks[blk]:
                    rh = _layer_norm(rh, g_ref[hh:hh + 1, :])
                if dil == 1:
                    o_ref[blk * HEADS_PER_BLOCK + hh, 0, rows, :] = rh.astype(o_ref.dtype)
                else:
                    acc_ref[hh, rows, :] = rh
        if dil == 1:
            continue
        for hh in range(HEADS_PER_BLOCK):
            head = blk * HEADS_PER_BLOCK + hh
            if dil <= PROJ_MAX_ROW_STRIDE:
                for r_ in range(dil):
                    o_ref[head, r_, :, :] = acc_ref.at[hh][pl.ds(r_, s_len // dil, stride=dil), :].astype(o_ref.dtype)
            else:
                s1, s2 = PROJ_MAX_ROW_STRIDE, dil // PROJ_MAX_ROW_STRIDE
                tmp_ref = scratch[2]
                for a in range(s1):
                    tmp_ref[a] = acc_ref.at[hh][pl.ds(a, s_len // s1, stride=s1), :]
                for a in range(s1):
                    for b in range(s2):
                        o_ref[head, s1 * b + a, :, :] = tmp_ref.at[a][pl.ds(b, s_len // dil, stride=s2), :].astype(
                            o_ref.dtype)


def _proj_heads(x, w, col_blocks, *, dil=1, gain=None, ln_blocks=None):
    b_sz, s_len, d_model = x.shape
    cast_x = x.dtype != BF16
    blk_cols = HEADS_PER_BLOCK * HEAD_DIM
    n_blocks = len(col_blocks)
    ln_blocks = tuple(ln_blocks) if ln_blocks is not None else (False,) * n_blocks
    in_specs = [pl.BlockSpec((None, s_len, d_model), lambda b: (b, 0, 0))]
    in_specs += [pl.BlockSpec((d_model, blk_cols), lambda b, cb=cb: (0, cb), pipeline_mode=pl.Buffered(1))
                 for cb in col_blocks]
    args = [x] + [w] * n_blocks
    if any(ln_blocks):
        in_specs.append(pl.BlockSpec((HEADS_PER_BLOCK, HEAD_DIM), lambda b: (0, 0)))
        args.append(gain)
    scratch = []
    if dil > 1:
        scratch += [pltpu.VMEM((HEADS_PER_BLOCK, s_len, HEAD_DIM), F32)] * 2
    if dil > PROJ_MAX_ROW_STRIDE:
        scratch.append(pltpu.VMEM((PROJ_MAX_ROW_STRIDE, s_len // PROJ_MAX_ROW_STRIDE, HEAD_DIM), F32))
    n_heads = n_blocks * HEADS_PER_BLOCK
    out_shape = jax.ShapeDtypeStruct((b_sz, n_heads, dil, s_len // dil, HEAD_DIM), BF16)
    out_specs = pl.BlockSpec((None, n_heads, dil, s_len // dil, HEAD_DIM), lambda b: (b, 0, 0, 0, 0))
    if cast_x:
        out_shape = (out_shape, jax.ShapeDtypeStruct(x.shape, BF16))
        out_specs = (out_specs, pl.BlockSpec((None, s_len, d_model), lambda b: (b, 0, 0)))
    return pl.pallas_call(
        functools.partial(_proj_kernel, dil=dil, row_chunk=min(512, s_len), ln_blocks=ln_blocks, cast_x=cast_x),
        out_shape=out_shape,
        grid=(b_sz,),
        in_specs=in_specs,
        out_specs=out_specs,
        scratch_shapes=scratch,
        compiler_params=_params("arbitrary"),
        name=f"proj_heads_d{dil}",
    )(*args)


def _dil_attn_kernel(slopes_ref, q0, k0, v0, q1, k1, v1, q2, k2, v2, o_ref, o_scr, l_scr, bias_scr):
    s_len = o_ref.shape[0]
    h = pl.program_id(1)
    scale = HEAD_DIM ** -0.5
    qb = ATTN_Q_BLOCK
    groups = ((q0, k0, v0), (q1, k1, v1), (q2, k2, v2))
    for g, (qr, kr, vr) in enumerate(groups):
        dil = DILATIONS[g]
        cls_len = s_len // dil
        kw = min(ATTN_K_WINDOW, cls_len)
        nb = cls_len // qb
        nb_shift = nb.bit_length() - 1

        neg_slope = jnp.full((1, kw), slopes_ref[g * HEADS_PER_BLOCK + h], F32) * (-float(dil))
        base_delta = (lax.broadcasted_iota(I32, (qb, kw), 1) - lax.broadcasted_iota(I32, (qb, kw), 0))
        for case in range(kw // N_SIDE):
            dist = jnp.abs(base_delta - case * N_SIDE)
            bias_scr[g, case, :, 0:kw] = jnp.where(dist <= N_SIDE, dist.astype(F32) * neg_slope, NEG_INF)

        def body(it, carry, qr=qr, kr=kr, vr=vr, dil=dil, cls_len=cls_len, kw=kw, nb=nb, g=g,
                 nb_shift=nb_shift):
            blocks = []
            for j in range(ATTN_BATCH):
                idx = it * ATTN_BATCH + j
                r = lax.shift_right_logical(idx, nb_shift)
                i = lax.bitwise_and(idx, nb - 1)
                qs = pl.multiple_of(i * qb, qb)
                ks = pl.multiple_of(jnp.clip(qs - N_SIDE, 0, cls_len - kw), N_SIDE)
                blocks.append((r, qs, ks))
            scores = []
            for r, qs, ks in blocks:
                q = qr[r, pl.ds(qs, qb), :]
                k = kr[r, pl.ds(ks, kw), :]
                scores.append(lax.dot_general(q, k, (((1,), (1,)), ((), ())), preferred_element_type=F32))
            probs = []
            for (r, qs, ks), s in zip(blocks, scores):
                bias = bias_scr[g, lax.shift_right_logical(qs - ks, N_SIDE.bit_length() - 1), :, 0:kw]
                s = s * scale + bias
                m = jnp.max(s, axis=-1, keepdims=True)
                p = jnp.exp(s - m)
                l = jnp.sum(p, axis=-1, keepdims=True)
                probs.append((p.astype(BF16), m, l))
            for (r, qs, ks), (p, m, l) in zip(blocks, probs):
                v = vr[r, pl.ds(ks, kw), :]
                o = jnp.dot(p, v, preferred_element_type=F32) / l
                lse = jnp.broadcast_to(m + jnp.log(l), (qb, LANES))
                row0 = qs * dil + r
                if dil == 1:
                    o_scr[g, pl.ds(row0, qb), :] = o
                    l_scr[g, pl.ds(row0, qb), :] = lse
                else:
                    o_scr.at[g][pl.ds(row0, qb, stride=dil), :] = o
                    l_scr.at[g][pl.ds(row0, qb, stride=dil), :] = lse
            return carry

        lax.fori_loop(0, dil * nb // ATTN_BATCH, body, 0)

    chunk = 256
    for c in range(s_len // chunk):
        rows = pl.ds(c * chunk, chunk)
        l0, l1, l2 = l_scr[0, rows, :], l_scr[1, rows, :], l_scr[2, rows, :]
        m = jnp.maximum(jnp.maximum(l0, l1), l2)
        e0, e1, e2 = jnp.exp(l0 - m), jnp.exp(l1 - m), jnp.exp(l2 - m)
        num = e0 * o_scr[0, rows, :] + e1 * o_scr[1, rows, :] + e2 * o_scr[2, rows, :]
        o_ref[rows, :] = (num / (e0 + e1 + e2)).astype(o_ref.dtype)


def _dilated_attention(slopes, qkv):
    b_sz = qkv[0].shape[0]
    s_len = qkv[0].shape[2] * qkv[0].shape[3]
    in_specs = [pl.BlockSpec(memory_space=pltpu.SMEM)]
    args = [slopes]
    for g, arr in enumerate(qkv):
        dil = DILATIONS[g]
        for which in range(3):
            in_specs.append(pl.BlockSpec((None, None, dil, s_len // dil, HEAD_DIM),
                                         lambda b, h, which=which: (b, which * HEADS_PER_BLOCK + h, 0, 0, 0)))
            args.append(arr)
    return pl.pallas_call(
        _dil_attn_kernel,
        out_shape=jax.ShapeDtypeStruct((b_sz, s_len, HEADS_PER_BLOCK * HEAD_DIM), BF16),
        grid=(b_sz, HEADS_PER_BLOCK),
        in_specs=in_specs,
        out_specs=pl.BlockSpec((None, s_len, HEAD_DIM), lambda b, h: (b, 0, h)),
        scratch_shapes=[pltpu.VMEM((3, s_len, HEAD_DIM), F32), pltpu.VMEM((3, s_len, LANES), F32),
                        pltpu.VMEM((3, ATTN_K_WINDOW // N_SIDE, ATTN_Q_BLOCK, ATTN_K_WINDOW), F32)],
        compiler_params=_params("arbitrary", "arbitrary"),
        name="dilated_attention",
    )(*args)


def _mem_attn_kernel(q_ref, kv_ref, o_ref):
    n_heads, s_len, _ = q_ref.shape
    scale = HEAD_DIM ** -0.5
    chunk = 256
    row_chunks = [pl.ds(c * chunk, chunk) for c in range(s_len // chunk)]
    for h in range(n_heads):
        k = kv_ref[h]
        v = kv_ref[n_heads + h]
        scores = [lax.dot_general(q_ref[h, rows, :], k, (((1,), (1,)), ((), ())), preferred_element_type=F32)
                  for rows in row_chunks]
        probs = []
        for s in scores:
            s = s * scale
            m = jnp.max(s, axis=-1, keepdims=True)
            p = jnp.exp(s - m)
            probs.append((p.astype(BF16), jnp.sum(p, axis=-1, keepdims=True)))
        for rows, (p, l) in zip(row_chunks, probs):
            o_ref[rows, h * HEAD_DIM:(h + 1) * HEAD_DIM] = (
                jnp.dot(p, v, preferred_element_type=F32) / l).astype(o_ref.dtype)


def _memory_attention(q_heads, q_head0, kv_heads):
    b_sz, _, _, s_len, _ = q_heads.shape
    n_mem = kv_heads.shape[3]
    assert q_head0 % HEADS_PER_BLOCK == 0
    return pl.pallas_call(
        _mem_attn_kernel,
        out_shape=jax.ShapeDtypeStruct((b_sz, s_len, HEADS_PER_BLOCK * HEAD_DIM), BF16),
        grid=(b_sz,),
        in_specs=[
            pl.BlockSpec((None, HEADS_PER_BLOCK, None, s_len, HEAD_DIM),
                         lambda b: (b, q_head0 // HEADS_PER_BLOCK, 0, 0, 0)),
            pl.BlockSpec((None, 2 * HEADS_PER_BLOCK, None, n_mem, HEAD_DIM), lambda b: (b, 0, 0, 0, 0)),
        ],
        out_specs=pl.BlockSpec((None, s_len, HEADS_PER_BLOCK * HEAD_DIM), lambda b: (b, 0, 0)),
        compiler_params=_params("arbitrary"),
        name="memory_attention",
    )(q_heads, kv_heads)


def _out_ln_route_kernel(a1_ref, a2_ref, w_ref, res_ref, g_ref, b_ref, rw_ref, rb_ref,
                         o_ref, route_ref, route_t_ref, cnt_ref, carry_ref):
    @pl.when(pl.program_id(0) == 0)
    def _():
        carry_ref[...] = jnp.zeros_like(carry_ref)

    k1 = a1_ref.shape[1]
    half = res_ref.shape[0] // 2
    halves = [pl.ds(hf * half, half) for hf in range(2)]
    ys = [jnp.dot(a1_ref[rows, :], w_ref[0:k1, :], preferred_element_type=F32)
          + jnp.dot(a2_ref[rows, :], w_ref[k1:, :], preferred_element_type=F32) for rows in halves]
    for hf, (rows, y) in enumerate(zip(halves, ys)):
        z = _layer_norm(ALPHA * res_ref[rows, :] + y, g_ref[...], b_ref[...])
        _store_tiled_rows(o_ref.at[pl.ds(hf * half * SUBLANES, half * SUBLANES)], z)
        rec = _route_tokens(z, rw_ref, rb_ref, carry_ref)
        route_ref[rows, :] = rec
        route_t_ref[:, rows] = rec.T[0:route_t_ref.shape[0], :]
    cnt_ref[...] = carry_ref[...]


def _out_proj_ln_route(a1, a2, w, res, gain, bias, router_w, router_b, *, tm=1024):
    t_len, d_model = res.shape
    assert d_model == SUBLANES * LANES
    k1, k2 = a1.shape[1], a2.shape[1]
    w_pad = jnp.zeros((d_model, LANES), F32).at[:, :N_EXPERTS].set(router_w)
    b_pad = jnp.zeros((1, LANES), F32).at[0, :N_EXPERTS].set(router_b)
    return pl.pallas_call(
        _out_ln_route_kernel,
        out_shape=(jax.ShapeDtypeStruct((t_len * SUBLANES, LANES), F32), jax.ShapeDtypeStruct((t_len, LANES), F32),
                   jax.ShapeDtypeStruct((SUBLANES, t_len), F32), jax.ShapeDtypeStruct((8, LANES), F32)),
        grid=(t_len // tm,),
        in_specs=[
            pl.BlockSpec((tm, k1), lambda i: (i, 0)),
            pl.BlockSpec((tm, k2), lambda i: (i, 0)),
            pl.BlockSpec((k1 + k2, d_model), lambda i: (0, 0), pipeline_mode=pl.Buffered(1)),
            pl.BlockSpec((tm, d_model), lambda i: (i, 0)),
            pl.BlockSpec((1, d_model), lambda i: (0, 0)),
            pl.BlockSpec((1, d_model), lambda i: (0, 0)),
            pl.BlockSpec((d_model, LANES), lambda i: (0, 0)),
            pl.BlockSpec((1, LANES), lambda i: (0, 0)),
        ],
        out_specs=(pl.BlockSpec((tm * SUBLANES, LANES), lambda i: (i, 0)),
                   pl.BlockSpec((tm, LANES), lambda i: (i, 0)),
                   pl.BlockSpec((SUBLANES, tm), lambda i: (0, i)),
                   pl.BlockSpec((8, LANES), lambda i: (0, 0))),
        scratch_shapes=[pltpu.VMEM((8, LANES), F32)],
        compiler_params=_params("arbitrary"),
        name="out_proj_ln_route",
    )(a1, a2, w, res, gain.reshape(1, -1), bias.reshape(1, -1), w_pad, b_pad)


def _swiglu_chunk(x, wg, wu, wd):
    hg = jnp.dot(x, wg, preferred_element_type=F32)
    hu = jnp.dot(x, wu, preferred_element_type=F32)
    h = hg * (1.0 / (1.0 + jnp.exp(-hg))) * hu
    return jnp.dot(h.astype(BF16), wd, preferred_element_type=F32)


def _mixer_ffn_kernel(a1_ref, a2_ref, wo_ref, res_ref, g1_ref, b1_ref, wg_ref, wu_ref, wd_ref, g2_ref, b2_ref,
                      o32_ref, o16_ref, *, tf):
    k1 = a1_ref.shape[1]
    tm = res_ref.shape[0]
    halves = [pl.ds(hf * (tm // 2), tm // 2) for hf in range(2)]
    ys = [jnp.dot(a1_ref[rows, :], wo_ref[0:k1, :], preferred_element_type=F32)
          + jnp.dot(a2_ref[rows, :], wo_ref[k1:, :], preferred_element_type=F32) for rows in halves]
    x1s = [_layer_norm(ALPHA * res_ref[rows, :] + y, g1_ref[...], b1_ref[...]) for rows, y in zip(halves, ys)]
    x16s = [x1.astype(BF16) for x1 in x1s]
    accs = [None, None]
    for f in range(wg_ref.shape[1] // tf):
        cols = pl.ds(f * tf, tf)
        for hf in range(2):
            part = _swiglu_chunk(x16s[hf], wg_ref[:, cols], wu_ref[:, cols], wd_ref[cols, :])
            accs[hf] = part if accs[hf] is None else accs[hf] + part
    for rows, x1, acc in zip(halves, x1s, accs):
        z = _layer_norm(ALPHA * x1 + acc, g2_ref[...], b2_ref[...])
        o32_ref[rows, :] = z
        o16_ref[rows, :] = z.astype(BF16)


def _mixer_ffn_ln(a1, a2, wo, res, g1, b1, wg, wu, wd, g2, b2, *, tm=1024, tf=256):
    t_len, d_model = res.shape
    k1, k2 = a1.shape[1], a2.shape[1]
    d_ff = wg.shape[1]
    resident = dict(pipeline_mode=pl.Buffered(1))
    vec = pl.BlockSpec((1, d_model), lambda i: (0, 0))
    return pl.pallas_call(
        functools.partial(_mixer_ffn_kernel, tf=tf),
        out_shape=(jax.ShapeDtypeStruct((t_len, d_model), F32), jax.ShapeDtypeStruct((t_len, d_model), BF16)),
        grid=(t_len // tm,),
        in_specs=[
            pl.BlockSpec((tm, k1), lambda i: (i, 0)),
            pl.BlockSpec((tm, k2), lambda i: (i, 0)),
            pl.BlockSpec((k1 + k2, d_model), lambda i: (0, 0), **resident),
            pl.BlockSpec((tm, d_model), lambda i: (i, 0)),
            vec, vec,
            pl.BlockSpec((d_model, d_ff), lambda i: (0, 0), **resident),
            pl.BlockSpec((d_model, d_ff), lambda i: (0, 0), **resident),
            pl.BlockSpec((d_ff, d_model), lambda i: (0, 0), **resident),
            vec, vec,
        ],
        out_specs=(pl.BlockSpec((tm, d_model), lambda i: (i, 0)),
                   pl.BlockSpec((tm, d_model), lambda i: (i, 0))),
        compiler_params=_params("arbitrary"),
        name="mixer_ffn_ln",
    )(a1, a2, wo, res, g1.reshape(1, -1), b1.reshape(1, -1), wg, wu, wd, g2.reshape(1, -1), b2.reshape(1, -1))


def _dft_constants(s_len):
    half = s_len // 2
    n = np.arange(half, dtype=np.int64)
    ang_s = 2.0 * np.pi * ((n[:, None] * n[None, :]) % s_len) / s_len
    c = np.arange(HEAD_DIM, dtype=np.int64)
    ang_c = 2.0 * np.pi * ((c[:, None] * c[None, :]) % HEAD_DIM) / HEAD_DIM
    norm = 1.0 / np.sqrt(float(s_len * HEAD_DIM))
    mirror = (n[:, None] + n[None, :] == half).astype(np.float64)
    return tuple(jnp.asarray(a, dtype=F32).astype(BF16)
                 for a in (np.cos(ang_s) * norm, np.sin(ang_s) * norm, mirror, np.cos(ang_c), np.sin(ang_c)))


def _fourier_kernel(u_ref, c1_ref, s1_ref, mirror_ref, cc_ref, sc_ref, o_ref, ae_scr, bo_scr, *, norm):
    n_grp, s_len, _ = u_ref.shape
    half = s_len // 2
    cc = cc_ref[...]
    sc = sc_ref[...]
    lanes = lambda parts: jnp.concatenate(parts, axis=1)

    u_low = lanes([u_ref[g, 0:half, :] for g in range(n_grp)])
    u_high = lanes([u_ref[g, half:, :] for g in range(n_grp)])
    u_mirror = jnp.dot(mirror_ref[...], u_high, preferred_element_type=F32)
    even = (u_low.astype(F32) + u_mirror).astype(BF16)
    odd = (u_low.astype(F32) - u_mirror).astype(BF16)

    row_sign = 1.0 - 2.0 * lax.bitwise_and(lax.broadcasted_iota(I32, (half, 1), 0), 1).astype(F32)
    pad_row = lax.broadcasted_iota(I32, (SUBLANES, 1), 0)
    mid_rows, alt_rows = [], []
    for g in range(n_grp):
        cols = slice(g * HEAD_DIM, (g + 1) * HEAD_DIM)
        ae_scr[:, cols] = jnp.dot(even[:, cols], cc, preferred_element_type=F32).astype(BF16)
        bo_scr[:, cols] = jnp.dot(odd[:, cols], sc, preferred_element_type=F32).astype(BF16)
        u_g = u_ref[g].astype(F32)
        alt = jnp.sum((u_g[0:half, :] + u_g[half:, :]) * row_sign, axis=0, keepdims=True)
        mid = u_g[half:half + 1, :]
        pad = jnp.where(pad_row == 0, mid, jnp.where(pad_row == 1, alt, 0.0)).astype(BF16)
        both = jnp.dot(pad, cc, preferred_element_type=F32) * norm
        mid_rows.append(both[0:1, :])
        alt_rows.append(both[1:2, :])
    mid_term = row_sign * lanes(mid_rows)
    p = jnp.dot(c1_ref[...], ae_scr[...], preferred_element_type=F32) + mid_term
    q = jnp.dot(s1_ref[...], bo_scr[...], preferred_element_type=F32)
    o_ref[0:half, :] = (p - q).astype(o_ref.dtype)
    upper = jnp.dot(mirror_ref[...], (p + q).astype(BF16), preferred_element_type=F32)
    is_row0 = lax.broadcasted_iota(I32, (half, 1), 0) == 0
    o_ref[half:, :] = jnp.where(is_row0, lanes(alt_rows), upper).astype(o_ref.dtype)


def _fourier_mix(u_heads):
    b_sz, _, _, s_len, _ = u_heads.shape
    n_grp = HEADS_PER_BLOCK
    half = s_len // 2
    consts = _dft_constants(s_len)
    const_spec = lambda n: pl.BlockSpec((n, n), lambda b: (0, 0), pipeline_mode=pl.Buffered(1))
    return pl.pallas_call(
        functools.partial(_fourier_kernel, norm=1.0 / np.sqrt(float(s_len * HEAD_DIM))),
        out_shape=jax.ShapeDtypeStruct((b_sz, s_len, n_grp * HEAD_DIM), BF16),
        grid=(b_sz,),
        in_specs=[pl.BlockSpec((None, n_grp, None, s_len, HEAD_DIM), lambda b: (b, 0, 0, 0, 0)),
                  const_spec(half), const_spec(half), const_spec(half), const_spec(HEAD_DIM), const_spec(HEAD_DIM)],
        out_specs=pl.BlockSpec((None, s_len, n_grp * HEAD_DIM), lambda b: (b, 0, 0)),
        scratch_shapes=[pltpu.VMEM((half, n_grp * HEAD_DIM), BF16), pltpu.VMEM((half, n_grp * HEAD_DIM), BF16)],
        compiler_params=_params("arbitrary"),
        name="fourier_mix",
    )(u_heads, *consts)


def _split_bf16(a):
    hi = a.astype(BF16)
    lo = (a - hi.astype(F32)).astype(BF16)
    return hi, lo


def _route_tokens(x, w_ref, b_ref, carry_ref):
    tq = x.shape[0]
    x_hi, x_lo = _split_bf16(x)
    w_hi, w_lo = _split_bf16(w_ref[...])
    hi_terms = jnp.dot(x_hi, jnp.concatenate([w_hi, w_lo], axis=1), preferred_element_type=F32)
    logits = (hi_terms[:, 0:LANES] + hi_terms[:, LANES:]
              + jnp.dot(x_lo, w_hi, preferred_element_type=F32)) + b_ref[...]
    lane = lax.broadcasted_iota(I32, (tq, LANES), 1).astype(F32)
    logits = jnp.where(lane < N_EXPERTS, logits, NEG_INF)

    def top1(vals):
        m = jnp.max(vals, axis=-1, keepdims=True)
        idx = jnp.min(jnp.where(vals == m, lane, float(LANES)), axis=-1, keepdims=True)
        return m, idx

    m1, i1 = top1(logits)
    oh1 = lane == i1
    m2, i2 = top1(jnp.where(oh1, NEG_INF, logits))
    oh2 = lane == i2
    e2 = jnp.exp(m2 - m1)
    gate1 = 1.0 / (1.0 + e2)
    gate2 = e2 / (1.0 + e2)

    oh1f = oh1.astype(F32)
    oh2f = oh2.astype(F32)
    both = oh1f + oh2f
    tri = (lax.broadcasted_iota(I32, (tq, tq), 0) > lax.broadcasted_iota(I32, (tq, tq), 1)).astype(BF16)
    before = jnp.dot(tri, both.astype(BF16), preferred_element_type=F32) + carry_ref[0:1, :]
    rank1 = jnp.sum(oh1f * before, axis=-1, keepdims=True)
    rank2 = jnp.sum(oh2f * before, axis=-1, keepdims=True)
    carry_ref[...] = carry_ref[...] + jnp.sum(both, axis=0, keepdims=True)

    rec = jnp.zeros((tq, LANES), F32)
    for lane_idx, val in ((ROUTE_E, i1), (ROUTE_E + 1, i2),
                          (ROUTE_RANK, rank1), (ROUTE_RANK + 1, rank2),
                          (ROUTE_GATE, gate1), (ROUTE_GATE + 1, gate2)):
        rec = jnp.where(lane == lane_idx, val, rec)
    return rec


def _store_tiled_rows(dst_ref, val):
    n_rows, width = val.shape
    for j in range(width // LANES):
        dst_ref[pl.ds(j, n_rows, stride=width // LANES), :] = val[:, j * LANES:(j + 1) * LANES]


def _load_tiled_rows(src_ref):
    n_rows = src_ref.shape[0] // SUBLANES
    return jnp.concatenate([src_ref[pl.ds(j, n_rows, stride=SUBLANES), :] for j in range(SUBLANES)], axis=1)


def _tiled_row(ref, row):
    return ref.at[pl.ds(pl.multiple_of(row * SUBLANES, SUBLANES), SUBLANES)]


def _dispatch_kernel(fill_ref, pos0_ref, pos1_ref, x_ref, xs_hbm, zero_scr, sem, *, tm):
    tq = x_ref.shape[0] // SUBLANES
    pos_refs = (pos0_ref, pos1_ref)

    @pl.when(pl.program_id(0) == 0)
    def _():
        zero_scr[...] = jnp.zeros_like(zero_scr)

        def fill_copies(tile):
            copies = []
            for piece in range(tm // tq):
                row0 = pl.multiple_of((tile * tm + piece * tq) * SUBLANES, SUBLANES)
                copies.append(pltpu.make_async_copy(zero_scr, xs_hbm.at[pl.ds(row0, tq * SUBLANES)], sem))
            return copies

        def start_fill(tile, carry):
            @pl.when(fill_ref[tile] == 1)
            def _():
                for copy in fill_copies(tile):
                    copy.start()
            return carry

        def wait_fill(tile, carry):
            @pl.when(fill_ref[tile] == 1)
            def _():
                for copy in fill_copies(tile):
                    copy.wait()
            return carry

        lax.fori_loop(0, fill_ref.shape[0], start_fill, 0)
        lax.fori_loop(0, fill_ref.shape[0], wait_fill, 0)

    def start(t, carry):
        for k in range(TOP_K):
            pltpu.make_async_copy(_tiled_row(x_ref, t), _tiled_row(xs_hbm, pos_refs[k][t]), sem).start(priority=k)
        return carry

    lax.fori_loop(0, tq, start, 0, unroll=8)
    for k in range(TOP_K):
        pltpu.make_async_copy(x_ref, xs_hbm.at[pl.ds(0, tq * SUBLANES)], sem).wait()


def _dispatch(x_tiled, pos, fill_tiles, *, tm, tq=512):
    t_len = x_tiled.shape[0] // SUBLANES
    assert tm % tq == 0
    grid_spec = pltpu.PrefetchScalarGridSpec(
        num_scalar_prefetch=1,
        grid=(t_len // tq,),
        in_specs=[
            pl.BlockSpec((tq,), lambda i, fill: (i,), memory_space=pltpu.SMEM),
            pl.BlockSpec((tq,), lambda i, fill: (i,), memory_space=pltpu.SMEM),
            pl.BlockSpec((tq * SUBLANES, LANES), lambda i, fill: (i, 0)),
        ],
        out_specs=pl.BlockSpec(memory_space=pl.ANY),
        scratch_shapes=[pltpu.VMEM((tq * SUBLANES, LANES), x_tiled.dtype), pltpu.SemaphoreType.DMA],
    )
    return pl.pallas_call(
        functools.partial(_dispatch_kernel, tm=tm),
        out_shape=jax.ShapeDtypeStruct((fill_tiles.shape[0] * tm * SUBLANES, LANES), x_tiled.dtype),
        grid_spec=grid_spec,
        compiler_params=_params("arbitrary"),
        name="moe_dispatch",
    )(fill_tiles, pos[0], pos[1], x_tiled)


def _expert_kernel(tile_e_ref, nt_ref, nv_ref, x_ref, wg_hbm, wu_hbm, wd_hbm, o_ref, x16_scr, acc_ref,
                   wg_buf, wu_buf, wd_buf, sem, *, n_chunks):
    del nt_ref
    i = pl.program_id(0)
    c = pl.program_id(1)
    n_tiles_max = pl.num_programs(0)
    n_valid = nv_ref[i]
    part = x16_scr.shape[0] // EXPERT_ROW_PARTS
    last = n_chunks - 1
    full = n_valid > (EXPERT_ROW_PARTS - 1) * part
    tf = wg_buf.shape[2]
    step = i * n_chunks + c
    slot = lax.rem(step, EXPERT_WEIGHT_RING)

    def weight_copies(tile, chunk, ring_slot):
        e = tile_e_ref[tile]
        cols = pl.ds(pl.multiple_of(chunk * tf, tf), tf)
        return (pltpu.make_async_copy(wg_hbm.at[e, :, cols], wg_buf.at[ring_slot], sem.at[ring_slot, 0]),
                pltpu.make_async_copy(wu_hbm.at[e, :, cols], wu_buf.at[ring_slot], sem.at[ring_slot, 1]),
                pltpu.make_async_copy(wd_hbm.at[e, cols, :], wd_buf.at[ring_slot], sem.at[ring_slot, 2]))

    def fetch(ahead):
        later = step + ahead
        tile = lax.div(later, n_chunks)
        chunk = later - tile * n_chunks

        @pl.when(jnp.logical_and(tile < n_tiles_max, nv_ref[jnp.minimum(tile, n_tiles_max - 1)] > 0))
        def _():
            for copy in weight_copies(tile, chunk, lax.rem(later, EXPERT_WEIGHT_RING)):
                copy.start(priority=1)

    @pl.when(step == 0)
    def _():
        for ahead in range(EXPERT_WEIGHT_RING - 1):
            fetch(ahead)

    fetch(EXPERT_WEIGHT_RING - 1)

    @pl.when(n_valid > 0)
    def _():
        for copy in weight_copies(i, c, slot):
            copy.wait()

    def weights():
        return wg_buf[slot].astype(BF16), wu_buf[slot].astype(BF16), wd_buf[slot].astype(BF16)

    @pl.when(jnp.logical_and(full, c == 0))
    def _():
        wg, wu, wd = weights()
        for p in range(EXPERT_ROW_PARTS):
            rows = pl.ds(p * part, part)
            x16 = _load_tiled_rows(x_ref.at[pl.ds(p * part * SUBLANES, part * SUBLANES)]).astype(BF16)
            x16_scr[rows, :] = x16
            acc_ref[rows, :] = _swiglu_chunk(x16, wg, wu, wd)

    @pl.when(jnp.logical_and(full, jnp.logical_and(c > 0, c < last)))
    def _():
        acc_ref[...] += _swiglu_chunk(x16_scr[...], *weights())

    @pl.when(jnp.logical_and(full, c == last))
    def _():
        _store_tiled_rows(o_ref, acc_ref[...] + _swiglu_chunk(x16_scr[...], *weights()))

    partial = jnp.logical_not(full)

    @pl.when(jnp.logical_and(partial, c == 0))
    def _():
        acc_ref[...] = jnp.zeros_like(acc_ref)
        x16_scr[...] = _load_tiled_rows(x_ref).astype(BF16)

    for n_parts in range(1, EXPERT_ROW_PARTS):
        @pl.when(jnp.logical_and(n_valid > (n_parts - 1) * part, n_valid <= n_parts * part))
        def _(n_parts=n_parts):
            rows = pl.ds(0, n_parts * part)
            acc_ref[rows, :] += _swiglu_chunk(x16_scr[rows, :], *weights())

    @pl.when(jnp.logical_and(partial, c == last))
    def _():
        _store_tiled_rows(o_ref, acc_ref[...])


def _expert_ffn(xs, tile_e, n_tiles, n_valid, wg, wu, wd, *, tm, tf):
    d_model = wg.shape[1]
    d_ff = wg.shape[2]
    n_chunks = d_ff // tf

    def row_map(i, c, te, nt, nv):
        return (jnp.minimum(i, nt[0] - 1), 0)

    ring = EXPERT_WEIGHT_RING
    grid_spec = pltpu.PrefetchScalarGridSpec(
        num_scalar_prefetch=3,
        grid=(xs.shape[0] // (tm * SUBLANES), n_chunks),
        in_specs=[
            pl.BlockSpec((tm * SUBLANES, LANES), row_map),
            pl.BlockSpec(memory_space=pl.ANY),
            pl.BlockSpec(memory_space=pl.ANY),
            pl.BlockSpec(memory_space=pl.ANY),
        ],
        out_specs=pl.BlockSpec((tm * SUBLANES, LANES), lambda i, c, te, nt, nv: (i, 0)),
        scratch_shapes=[pltpu.VMEM((tm, d_model), BF16), pltpu.VMEM((tm, d_model), F32),
                        pltpu.VMEM((ring, d_model, tf), wg.dtype), pltpu.VMEM((ring, d_model, tf), wu.dtype),
                        pltpu.VMEM((ring, tf, d_model), wd.dtype), pltpu.SemaphoreType.DMA((ring, 3))],
    )
    return pl.pallas_call(
        functools.partial(_expert_kernel, n_chunks=n_chunks),
        out_shape=jax.ShapeDtypeStruct(xs.shape, F32),
        grid_spec=grid_spec,
        compiler_params=_params("arbitrary", "arbitrary"),
        name="expert_ffn",
    )(tile_e, n_tiles, n_valid, xs, wg, wu, wd)


def _combine_kernel(pos0_ref, pos1_ref, pos0_next_ref, pos1_next_ref, res_ref, route_ref, g_ref, b_ref, ys_hbm,
                    o_ref, buf_ref, sem):
    i = pl.program_id(0)
    tq = o_ref.shape[0]
    slot = lax.bitwise_and(i, 1)

    def gather(positions, dst_slot):
        def start(t, carry):
            for k in range(TOP_K):
                pltpu.make_async_copy(_tiled_row(ys_hbm, positions[k][t]),
                                      _tiled_row(buf_ref.at[dst_slot, k], t), sem.at[dst_slot]).start(priority=k)
            return carry
        lax.fori_loop(0, tq, start, 0, unroll=8)

    @pl.when(i == 0)
    def _():
        gather((pos0_ref, pos1_ref), 0)

    @pl.when(i + 1 < pl.num_programs(0))
    def _():
        gather((pos0_next_ref, pos1_next_ref), 1 - slot)

    for k in range(TOP_K):
        pltpu.make_async_copy(ys_hbm.at[pl.ds(0, tq * SUBLANES)], buf_ref.at[slot, k], sem.at[slot]).wait()

    route = route_ref[...]
    y = (route[:, ROUTE_GATE:ROUTE_GATE + 1] * _load_tiled_rows(buf_ref.at[slot, 0])
         + route[:, ROUTE_GATE + 1:ROUTE_GATE + 2] * _load_tiled_rows(buf_ref.at[slot, 1]))
    o_ref[...] = _layer_norm(ALPHA * _load_tiled_rows(res_ref) + y, g_ref[...], b_ref[...])


def _combine_ln(ys, pos, res_tiled, route, gain, bias, *, tq=256):
    t_len = res_tiled.shape[0] // SUBLANES
    d_model = SUBLANES * LANES
    n_steps = t_len // tq
    this_tile = pl.BlockSpec((tq,), lambda i: (i,), memory_space=pltpu.SMEM)
    next_tile = pl.BlockSpec((tq,), lambda i: (jnp.minimum(i + 1, n_steps - 1),), memory_space=pltpu.SMEM)
    return pl.pallas_call(
        _combine_kernel,
        out_shape=jax.ShapeDtypeStruct((t_len, d_model), F32),
        grid=(n_steps,),
        in_specs=[
            this_tile, this_tile, next_tile, next_tile,
            pl.BlockSpec((tq * SUBLANES, LANES), lambda i: (i, 0)),
            pl.BlockSpec((tq, LANES), lambda i: (i, 0)),
            pl.BlockSpec((1, d_model), lambda i: (0, 0)),
            pl.BlockSpec((1, d_model), lambda i: (0, 0)),
            pl.BlockSpec(memory_space=pl.ANY),
        ],
        out_specs=pl.BlockSpec((tq, d_model), lambda i: (i, 0)),
        scratch_shapes=[pltpu.VMEM((2, TOP_K, tq * SUBLANES, LANES), F32), pltpu.SemaphoreType.DMA((2,))],
        compiler_params=_params("arbitrary"),
        name="moe_combine_ln",
    )(pos[0], pos[1], pos[0], pos[1], res_tiled, route, gain.reshape(1, -1), bias.reshape(1, -1), ys)


def _moe_ffn_ln(x_tiled, route, route_t, counts, wg, wu, wd, gain, bias, *, tm=MOE_ROW_TILE, tf=MOE_FF_CHUNK):
    t_len = x_tiled.shape[0] // SUBLANES
    experts = route_t[ROUTE_E:ROUTE_E + TOP_K].astype(I32)
    ranks = route_t[ROUTE_RANK:ROUTE_RANK + TOP_K].astype(I32)
    counts = counts[0, :N_EXPERTS].astype(I32)
    tiles_per_expert = (counts + tm - 1) // tm
    tile_end = jnp.cumsum(tiles_per_expert)
    row_start = (tile_end - tiles_per_expert) * tm
    pos = ranks + sum(jnp.where(experts == e, row_start[e], 0) for e in range(N_EXPERTS))
    n_tiles_max = (TOP_K * t_len) // tm + N_EXPERTS
    n_tiles = tile_end[-1:]
    all_tiles = jnp.arange(n_tiles_max, dtype=I32)
    tile_ids = jnp.minimum(all_tiles, n_tiles[0] - 1)
    tile_e = jnp.sum((tile_ids[:, None] >= tile_end[None, :-1]).astype(I32), axis=1)
    is_last = jnp.any((all_tiles[:, None] == tile_end[None, :] - 1) & (tiles_per_expert[None, :] > 0), axis=1)
    fill_tiles = (is_last | (all_tiles >= n_tiles[0])).astype(I32)

    first_tile = (tile_end - tiles_per_expert)[tile_e]
    n_valid = jnp.where(all_tiles < n_tiles[0], jnp.clip(counts[tile_e] - (all_tiles - first_tile) * tm, 0, tm), 0)

    xs = _dispatch(x_tiled, pos, fill_tiles, tm=tm, tq=MOE_DISPATCH_TILE)
    ys = _expert_ffn(xs, tile_e, n_tiles.astype(I32), n_valid.astype(I32), wg, wu, wd, tm=tm, tf=tf)
    return _combine_ln(ys, pos, x_tiled, route, gain, bias)


def _alibi_slopes():
    exps = np.arange(1, N_ATTN_HEADS + 1, dtype=np.float32) * np.float32(8.0 / N_ATTN_HEADS)
    return jnp.asarray(np.exp2(-exps).astype(np.float32))


def kernel(x, mem, a_w_in, a_w_mem_kv, a_w_out, a_ln1_g, a_ln1_b, a_ffn_gate, a_ffn_up, a_ffn_down, a_ln2_g, a_ln2_b, b_w_in, b_fourier_g, b_w_mem_kv, b_w_out, b_ln1_g, b_ln1_b, b_router_w, b_router_b, b_moe_gate, b_moe_up, b_moe_down, b_ln2_g, b_ln2_b):
    b_sz, s_len, d_model = x.shape
    t_len = b_sz * s_len
    x32 = x.reshape(t_len, d_model)
    a_w_in, a_w_mem_kv, a_w_out, a_ffn_gate, a_ffn_up, a_ffn_down, b_w_in, b_w_mem_kv, b_w_out = (
        w[0].astype(BF16) for w in (a_w_in, a_w_mem_kv, a_w_out, a_ffn_gate, a_ffn_up, a_ffn_down,
                                    b_w_in, b_w_mem_kv, b_w_out))

    n_grp = len(DILATIONS)
    qkv0, x16 = _proj_heads(x, a_w_in, (0, n_grp, 2 * n_grp, 3 * n_grp))
    qkv = [qkv0] + [_proj_heads(x16, a_w_in, (g, n_grp + g, 2 * n_grp + g), dil=DILATIONS[g])
                    for g in range(1, n_grp)]
    mem_kv, mem16 = _proj_heads(mem, a_w_mem_kv, (0, 1))
    attn = _dilated_attention(_alibi_slopes(), qkv).reshape(t_len, -1)
    memo = _memory_attention(qkv[0], 3 * HEADS_PER_BLOCK, mem_kv).reshape(t_len, -1)
    x32, x16 = _mixer_ffn_ln(attn, memo, a_w_out, x32, a_ln1_g[0], a_ln1_b[0],
                             a_ffn_gate, a_ffn_up, a_ffn_down, a_ln2_g[0], a_ln2_b[0])

    x16_b = x16.reshape(b_sz, s_len, d_model)
    u_q = _proj_heads(x16_b, b_w_in, (0, 1), gain=b_fourier_g[0], ln_blocks=(True, False))
    mem_kv = _proj_heads(mem16, b_w_mem_kv, (0, 1))
    four = _fourier_mix(u_q).reshape(t_len, -1)
    memo = _memory_attention(u_q, HEADS_PER_BLOCK, mem_kv).reshape(t_len, -1)
    x_tiled, route, route_t, counts = _out_proj_ln_route(four, memo, b_w_out, x32, b_ln1_g[0], b_ln1_b[0],
                                                         b_router_w[0], b_router_b[0])
    out = _moe_ffn_ln(x_tiled, route, route_t, counts, b_moe_gate[0], b_moe_up[0], b_moe_down[0],
                      b_ln2_g[0], b_ln2_b[0])
    return out.reshape(b_sz, s_len, d_model)
```

```python
import functools

import numpy as np
import jax
import jax.numpy as jnp
from jax import lax
from jax.experimental import pallas as pl
from jax.experimental.pallas import tpu as pltpu

F32 = jnp.float32
BF16 = jnp.bfloat16
I32 = jnp.int32

LANES = 128
HEAD_DIM = 128
HEADS_PER_BLOCK = 4
DILATIONS = (1, 4, 16)
N_SIDE = 64
ATTN_Q_BLOCK = 64
ATTN_K_WINDOW = 256
ATTN_BATCH = 32
PROJ_MAX_ROW_STRIDE = 4
N_ATTN_HEADS = 12
N_EXPERTS = 8
TOP_K = 2
MOE_ROW_TILE = 1024
MOE_FF_CHUNK = 512
MOE_DISPATCH_TILE = 1024
EXPERT_WEIGHT_RING = 3
EXPERT_ROW_PARTS = 4
SUBLANES = 8
ALPHA = (2.0 * 2) ** 0.25
LN_EPS = 1e-5
NEG_INF = -1e30
VMEM_LIMIT_BYTES = 56 * 1024 * 1024

ROUTE_E, ROUTE_RANK, ROUTE_GATE = 0, 2, 4


def _params(*sem):
    return pltpu.CompilerParams(dimension_semantics=sem, vmem_limit_bytes=VMEM_LIMIT_BYTES)


def _layer_norm(z, g, b=None):
    mu = jnp.mean(z, axis=-1, keepdims=True)
    zc = z - mu
    var = jnp.mean(zc * zc, axis=-1, keepdims=True)
    y = zc * lax.rsqrt(var + LN_EPS) * g
    return y if b is None else y + b


def _proj_kernel(*refs, dil, row_chunk, ln_blocks, cast_x):
    n_blocks = len(ln_blocks)
    x_ref, w_refs = refs[0], refs[1:1 + n_blocks]
    rest = refs[1 + n_blocks:]
    g_ref = None
    if any(ln_blocks):
        g_ref, rest = rest[0], rest[1:]
    o_ref, scratch = rest[0], rest[1:]
    if cast_x:
        x16_ref, scratch = scratch[0], scratch[1:]
        x16_ref[...] = x_ref[...].astype(BF16)
        x_ref = x16_ref
    s_len = x_ref.shape[0]
    for blk in range(n_blocks):
        w = w_refs[blk][...]
        acc_ref = scratch[blk % 2] if dil > 1 else None
        for rc in range(s_len // row_chunk):
            rows = pl.ds(rc * row_chunk, row_chunk)
            r = jnp.dot(x_ref[rows, :], w, preferred_element_type=F32)
            for hh in range(HEADS_PER_BLOCK):
                rh = r[:, hh * HEAD_DIM:(hh + 1) * HEAD_DIM]
                if ln_blocks[blk]:
                    rh = _layer_norm(rh, g_ref[hh:hh + 1, :])
                if dil == 1:
                    o_ref[blk * HEADS_PER_BLOCK + hh, 0, rows, :] = rh.astype(o_ref.dtype)
                else:
                    acc_ref[hh, rows, :] = rh
        if dil == 1:
            continue
        for hh in range(HEADS_PER_BLOCK):
            head = blk * HEADS_PER_BLOCK + hh
            if dil <= PROJ_MAX_ROW_STRIDE:
                for r_ in range(dil):
                    o_ref[head, r_, :, :] = acc_ref.at[hh][pl.ds(r_, s_len // dil, stride=dil), :].astype(o_ref.dtype)
            else:
                s1, s2 = PROJ_MAX_ROW_STRIDE, dil // PROJ_MAX_ROW_STRIDE
                tmp_ref = scratch[2]
                for a in range(s1):
                    tmp_ref[a] = acc_ref.at[hh][pl.ds(a, s_len // s1, stride=s1), :]
                for a in range(s1):
                    for b in range(s2):
                        o_ref[head, s1 * b + a, :, :] = tmp_ref.at[a][pl.ds(b, s_len // dil, stride=s2), :].astype(
                            o_ref.dtype)


def _proj_heads(x, w, col_blocks, *, dil=1, gain=None, ln_blocks=None):
    b_sz, s_len, d_model = x.shape
    cast_x = x.dtype != BF16
    blk_cols = HEADS_PER_BLOCK * HEAD_DIM
    n_blocks = len(col_blocks)
    ln_blocks = tuple(ln_blocks) if ln_blocks is not None else (False,) * n_blocks
    in_specs = [pl.BlockSpec((None, s_len, d_model), lambda b: (b, 0, 0))]
    in_specs += [pl.BlockSpec((d_model, blk_cols), lambda b, cb=cb: (0, cb), pipeline_mode=pl.Buffered(1))
                 for cb in col_blocks]
    args = [x] + [w] * n_blocks
    if any(ln_blocks):
        in_specs.append(pl.BlockSpec((HEADS_PER_BLOCK, HEAD_DIM), lambda b: (0, 0)))
        args.append(gain)
    scratch = []
    if dil > 1:
        scratch += [pltpu.VMEM((HEADS_PER_BLOCK, s_len, HEAD_DIM), F32)] * 2
    if dil > PROJ_MAX_ROW_STRIDE:
        scratch.append(pltpu.VMEM((PROJ_MAX_ROW_STRIDE, s_len // PROJ_MAX_ROW_STRIDE, HEAD_DIM), F32))
    n_heads = n_blocks * HEADS_PER_BLOCK
    out_shape = jax.ShapeDtypeStruct((b_sz, n_heads, dil, s_len // dil, HEAD_DIM), BF16)
    out_specs = pl.BlockSpec((None, n_heads, dil, s_len // dil, HEAD_DIM), lambda b: (b, 0, 0, 0, 0))
    if cast_x:
        out_shape = (out_shape, jax.ShapeDtypeStruct(x.shape, BF16))
        out_specs = (out_specs, pl.BlockSpec((None, s_len, d_model), lambda b: (b, 0, 0)))
    return pl.pallas_call(
        functools.partial(_proj_kernel, dil=dil, row_chunk=min(512, s_len), ln_blocks=ln_blocks, cast_x=cast_x),
        out_shape=out_shape,
        grid=(b_sz,),
        in_specs=in_specs,
        out_specs=out_specs,
        scratch_shapes=scratch,
        compiler_params=_params("arbitrary"),
        name=f"proj_heads_d{dil}",
    )(*args)


def _dil_attn_kernel(slopes_ref, q0, k0, v0, q1, k1, v1, q2, k2, v2, o_ref, o_scr, l_scr, bias_scr):
    s_len = o_ref.shape[0]
    h = pl.program_id(1)
    scale = HEAD_DIM ** -0.5
    qb = ATTN_Q_BLOCK
    groups = ((q0, k0, v0), (q1, k1, v1), (q2, k2, v2))
    for g, (qr, kr, vr) in enumerate(groups):
        dil = DILATIONS[g]
        cls_len = s_len // dil
        kw = min(ATTN_K_WINDOW, cls_len)
        nb = cls_len // qb
        nb_shift = nb.bit_length() - 1

        neg_slope = jnp.full((1, kw), slopes_ref[g * HEADS_PER_BLOCK + h], F32) * (-float(dil))
        base_delta = (lax.broadcasted_iota(I32, (qb, kw), 1) - lax.broadcasted_iota(I32, (qb, kw), 0))
        for case in range(kw // N_SIDE):
            dist = jnp.abs(base_delta - case * N_SIDE)
            bias_scr[g, case, :, 0:kw] = jnp.where(dist <= N_SIDE, dist.astype(F32) * neg_slope, NEG_INF)

        def body(it, carry, qr=qr, kr=kr, vr=vr, dil=dil, cls_len=cls_len, kw=kw, nb=nb, g=g,
                 nb_shift=nb_shift):
            blocks = []
            for j in range(ATTN_BATCH):
                idx = it * ATTN_BATCH + j
                r = lax.shift_right_logical(idx, nb_shift)
                i = lax.bitwise_and(idx, nb - 1)
                qs = pl.multiple_of(i * qb, qb)
                ks = pl.multiple_of(jnp.clip(qs - N_SIDE, 0, cls_len - kw), N_SIDE)
                blocks.append((r, qs, ks))
            scores = []
            for r, qs, ks in blocks:
                q = qr[r, pl.ds(qs, qb), :]
                k = kr[r, pl.ds(ks, kw), :]
                scores.append(lax.dot_general(q, k, (((1,), (1,)), ((), ())), preferred_element_type=F32))
            probs = []
            for (r, qs, ks), s in zip(blocks, scores):
                bias = bias_scr[g, lax.shift_right_logical(qs - ks, N_SIDE.bit_length() - 1), :, 0:kw]
                s = s * scale + bias
                m = jnp.max(s, axis=-1, keepdims=True)
                p = jnp.exp(s - m)
                l = jnp.sum(p, axis=-1, keepdims=True)
                probs.append((p.astype(BF16), m, l))
            for (r, qs, ks), (p, m, l) in zip(blocks, probs):
                v = vr[r, pl.ds(ks, kw), :]
                o = jnp.dot(p, v, preferred_element_type=F32) / l
                lse = jnp.broadcast_to(m + jnp.log(l), (qb, LANES))
                row0 = qs * dil + r
                if dil == 1:
                    o_scr[g, pl.ds(row0, qb), :] = o
                    l_scr[g, pl.ds(row0, qb), :] = lse
                else:
                    o_scr.at[g][pl.ds(row0, qb, stride=dil), :] = o
                    l_scr.at[g][pl.ds(row0, qb, stride=dil), :] = lse
            return carry

        lax.fori_loop(0, dil * nb // ATTN_BATCH, body, 0)

    chunk = 256
    for c in range(s_len // chunk):
        rows = pl.ds(c * chunk, chunk)
        l0, l1, l2 = l_scr[0, rows, :], l_scr[1, rows, :], l_scr[2, rows, :]
        m = jnp.maximum(jnp.maximum(l0, l1), l2)
        e0, e1, e2 = jnp.exp(l0 - m), jnp.exp(l1 - m), jnp.exp(l2 - m)
        num = e0 * o_scr[0, rows, :] + e1 * o_scr[1, rows, :] + e2 * o_scr[2, rows, :]
        o_ref[rows, :] = (num / (e0 + e1 + e2)).astype(o_ref.dtype)


def _dilated_attention(slopes, qkv):
    b_sz = qkv[0].shape[0]
    s_len = qkv[0].shape[2] * qkv[0].shape[3]
    in_specs = [pl.BlockSpec(memory_space=pltpu.SMEM)]
    args = [slopes]
    for g, arr in enumerate(qkv):
        dil = DILATIONS[g]
        for which in range(3):
            in_specs.append(pl.BlockSpec((None, None, dil, s_len // dil, HEAD_DIM),
                                         lambda b, h, which=which: (b, which * HEADS_PER_BLOCK + h, 0, 0, 0)))
            args.append(arr)
    return pl.pallas_call(
        _dil_attn_kernel,
        out_shape=jax.ShapeDtypeStruct((b_sz, s_len, HEADS_PER_BLOCK * HEAD_DIM), BF16),
        grid=(b_sz, HEADS_PER_BLOCK),
        in_specs=in_specs,
        out_specs=pl.BlockSpec((None, s_len, HEAD_DIM), lambda b, h: (b, 0, h)),
        scratch_shapes=[pltpu.VMEM((3, s_len, HEAD_DIM), F32), pltpu.VMEM((3, s_len, LANES), F32),
                        pltpu.VMEM((3, ATTN_K_WINDOW // N_SIDE, ATTN_Q_BLOCK, ATTN_K_WINDOW), F32)],
        compiler_params=_params("arbitrary", "arbitrary"),
        name="dilated_attention",
    )(*args)


def _mem_attn_kernel(q_ref, kv_ref, o_ref):
    n_heads, s_len, _ = q_ref.shape
    scale = HEAD_DIM ** -0.5
    chunk = 256
    row_chunks = [pl.ds(c * chunk, chunk) for c in range(s_len // chunk)]
    for h in range(n_heads):
        k = kv_ref[h]
        v = kv_ref[n_heads + h]
        scores = [lax.dot_general(q_ref[h, rows, :], k, (((1,), (1,)), ((), ())), preferred_element_type=F32)
                  for rows in row_chunks]
        probs = []
        for s in scores:
            s = s * scale
            m = jnp.max(s, axis=-1, keepdims=True)
            p = jnp.exp(s - m)
            probs.append((p.astype(BF16), jnp.sum(p, axis=-1, keepdims=True)))
        for rows, (p, l) in zip(row_chunks, probs):
            o_ref[rows, h * HEAD_DIM:(h + 1) * HEAD_DIM] = (
                jnp.dot(p, v, preferred_element_type=F32) / l).astype(o_ref.dtype)


def _memory_attention(q_heads, q_head0, kv_heads):
    b_sz, _, _, s_len, _ = q_heads.shape
    n_mem = kv_heads.shape[3]
    assert q_head0 % HEADS_PER_BLOCK == 0
    return pl.pallas_call(
        _mem_attn_kernel,
        out_shape=jax.ShapeDtypeStruct((b_sz, s_len, HEADS_PER_BLOCK * HEAD_DIM), BF16),
        grid=(b_sz,),
        in_specs=[
            pl.BlockSpec((None, HEADS_PER_BLOCK, None, s_len, HEAD_DIM),
                         lambda b: (b, q_head0 // HEADS_PER_BLOCK, 0, 0, 0)),
            pl.BlockSpec((None, 2 * HEADS_PER_BLOCK, None, n_mem, HEAD_DIM), lambda b: (b, 0, 0, 0, 0)),
        ],
        out_specs=pl.BlockSpec((None, s_len, HEADS_PER_BLOCK * HEAD_DIM), lambda b: (b, 0, 0)),
        compiler_params=_params("arbitrary"),
        name="memory_attention",
    )(q_heads, kv_heads)


def _out_ln_route_kernel(a1_ref, a2_ref, w_ref, res_ref, g_ref, b_ref, rw_ref, rb_ref,
                         o_ref, route_ref, route_t_ref, cnt_ref, carry_ref):
    @pl.when(pl.program_id(0) == 0)
    def _():
        carry_ref[...] = jnp.zeros_like(carry_ref)

    k1 = a1_ref.shape[1]
    half = res_ref.shape[0] // 2
    halves = [pl.ds(hf * half, half) for hf in range(2)]
    ys = [jnp.dot(a1_ref[rows, :], w_ref[0:k1, :], preferred_element_type=F32)
          + jnp.dot(a2_ref[rows, :], w_ref[k1:, :], preferred_element_type=F32) for rows in halves]
    for hf, (rows, y) in enumerate(zip(halves, ys)):
        z = _layer_norm(ALPHA * res_ref[rows, :] + y, g_ref[...], b_ref[...])
        _store_tiled_rows(o_ref.at[pl.ds(hf * half * SUBLANES, half * SUBLANES)], z)
        rec = _route_tokens(z, rw_ref, rb_ref, carry_ref)
        route_ref[rows, :] = rec
        route_t_ref[:, rows] = rec.T[0:route_t_ref.shape[0], :]
    cnt_ref[...] = carry_ref[...]


def _out_proj_ln_route(a1, a2, w, res, gain, bias, router_w, router_b, *, tm=1024):
    t_len, d_model = res.shape
    assert d_model == SUBLANES * LANES
    k1, k2 = a1.shape[1], a2.shape[1]
    w_pad = jnp.zeros((d_model, LANES), F32).at[:, :N_EXPERTS].set(router_w)
    b_pad = jnp.zeros((1, LANES), F32).at[0, :N_EXPERTS].set(router_b)
    return pl.pallas_call(
        _out_ln_route_kernel,
        out_shape=(jax.ShapeDtypeStruct((t_len * SUBLANES, LANES), F32), jax.ShapeDtypeStruct((t_len, LANES), F32),
                   jax.ShapeDtypeStruct((SUBLANES, t_len), F32), jax.ShapeDtypeStruct((8, LANES), F32)),
        grid=(t_len // tm,),
        in_specs=[
            pl.BlockSpec((tm, k1), lambda i: (i, 0)),
            pl.BlockSpec((tm, k2), lambda i: (i, 0)),
            pl.BlockSpec((k1 + k2, d_model), lambda i: (0, 0), pipeline_mode=pl.Buffered(1)),
            pl.BlockSpec((tm, d_model), lambda i: (i, 0)),
            pl.BlockSpec((1, d_model), lambda i: (0, 0)),
            pl.BlockSpec((1, d_model), lambda i: (0, 0)),
            pl.BlockSpec((d_model, LANES), lambda i: (0, 0)),
            pl.BlockSpec((1, LANES), lambda i: (0, 0)),
        ],
        out_specs=(pl.BlockSpec((tm * SUBLANES, LANES), lambda i: (i, 0)),
                   pl.BlockSpec((tm, LANES), lambda i: (i, 0)),
                   pl.BlockSpec((SUBLANES, tm), lambda i: (0, i)),
                   pl.BlockSpec((8, LANES), lambda i: (0, 0))),
        scratch_shapes=[pltpu.VMEM((8, LANES), F32)],
        compiler_params=_params("arbitrary"),
        name="out_proj_ln_route",
    )(a1, a2, w, res, gain.reshape(1, -1), bias.reshape(1, -1), w_pad, b_pad)


def _swiglu_chunk(x, wg, wu, wd):
    hg = jnp.dot(x, wg, preferred_element_type=F32)
    hu = jnp.dot(x, wu, preferred_element_type=F32)
    h = hg * (1.0 / (1.0 + jnp.exp(-hg))) * hu
    return jnp.dot(h.astype(BF16), wd, preferred_element_type=F32)


def _mixer_ffn_kernel(a1_ref, a2_ref, wo_ref, res_ref, g1_ref, b1_ref, wg_ref, wu_ref, wd_ref, g2_ref, b2_ref,
                      o32_ref, o16_ref, *, tf):
    k1 = a1_ref.shape[1]
    tm = res_ref.shape[0]
    halves = [pl.ds(hf * (tm // 2), tm // 2) for hf in range(2)]
    ys = [jnp.dot(a1_ref[rows, :], wo_ref[0:k1, :], preferred_element_type=F32)
          + jnp.dot(a2_ref[rows, :], wo_ref[k1:, :], preferred_element_type=F32) for rows in halves]
    x1s = [_layer_norm(ALPHA * res_ref[rows, :] + y, g1_ref[...], b1_ref[...]) for rows, y in zip(halves, ys)]
    x16s = [x1.astype(BF16) for x1 in x1s]
    accs = [None, None]
    for f in range(wg_ref.shape[1] // tf):
        cols = pl.ds(f * tf, tf)
        for hf in range(2):
            part = _swiglu_chunk(x16s[hf], wg_ref[:, cols], wu_ref[:, cols], wd_ref[cols, :])
            accs[hf] = part if accs[hf] is None else accs[hf] + part
    for rows, x1, acc in zip(halves, x1s, accs):
        z = _layer_norm(ALPHA * x1 + acc, g2_ref[...], b2_ref[...])
        o32_ref[rows, :] = z
        o16_ref[rows, :] = z.astype(BF16)


def _mixer_ffn_ln(a1, a2, wo, res, g1, b1, wg, wu, wd, g2, b2, *, tm=1024, tf=256):
    t_len, d_model = res.shape
    k1, k2 = a1.shape[1], a2.shape[1]
    d_ff = wg.shape[1]
    resident = dict(pipeline_mode=pl.Buffered(1))
    vec = pl.BlockSpec((1, d_model), lambda i: (0, 0))
    return pl.pallas_call(
        functools.partial(_mixer_ffn_kernel, tf=tf),
        out_shape=(jax.ShapeDtypeStruct((t_len, d_model), F32), jax.ShapeDtypeStruct((t_len, d_model), BF16)),
        grid=(t_len // tm,),
        in_specs=[
            pl.BlockSpec((tm, k1), lambda i: (i, 0)),
            pl.BlockSpec((tm, k2), lambda i: (i, 0)),
            pl.BlockSpec((k1 + k2, d_model), lambda i: (0, 0), **resident),
            pl.BlockSpec((tm, d_model), lambda i: (i, 0)),
            vec, vec,
            pl.BlockSpec((d_model, d_ff), lambda i: (0, 0), **resident),
            pl.BlockSpec((d_model, d_ff), lambda i: (0, 0), **resident),
            pl.BlockSpec((d_ff, d_model), lambda i: (0, 0), **resident),
            vec, vec,
        ],
        out_specs=(pl.BlockSpec((tm, d_model), lambda i: (i, 0)),
                   pl.BlockSpec((tm, d_model), lambda i: (i, 0))),
        compiler_params=_params("arbitrary"),
        name="mixer_ffn_ln",
    )(a1, a2, wo, res, g1.reshape(1, -1), b1.reshape(1, -1), wg, wu, wd, g2.reshape(1, -1), b2.reshape(1, -1))


def _dft_constants(s_len):
    half = s_len // 2
    n = np.arange(half, dtype=np.int64)
    ang_s = 2.0 * np.pi * ((n[:, None] * n[None, :]) % s_len) / s_len
    c = np.arange(HEAD_DIM, dtype=np.int64)
    ang_c = 2.0 * np.pi * ((c[:, None] * c[None, :]) % HEAD_DIM) / HEAD_DIM
    norm = 1.0 / np.sqrt(float(s_len * HEAD_DIM))
    mirror = (n[:, None] + n[None, :] == half).astype(np.float64)
    return tuple(jnp.asarray(a, dtype=F32).astype(BF16)
                 for a in (np.cos(ang_s) * norm, np.sin(ang_s) * norm, mirror, np.cos(ang_c), np.sin(ang_c)))


def _fourier_kernel(u_ref, c1_ref, s1_ref, mirror_ref, cc_ref, sc_ref, o_ref, ae_scr, bo_scr, *, norm):
    n_grp, s_len, _ = u_ref.shape
    half = s_len // 2
    cc = cc_ref[...]
    sc = sc_ref[...]
    lanes = lambda parts: jnp.concatenate(parts, axis=1)

    u_low = lanes([u_ref[g, 0:half, :] for g in range(n_grp)])
    u_high = lanes([u_ref[g, half:, :] for g in range(n_grp)])
    u_mirror = jnp.dot(mirror_ref[...], u_high, preferred_element_type=F32)
    even = (u_low.astype(F32) + u_mirror).astype(BF16)
    odd = (u_low.astype(F32) - u_mirror).astype(BF16)

    row_sign = 1.0 - 2.0 * lax.bitwise_and(lax.broadcasted_iota(I32, (half, 1), 0), 1).astype(F32)
    pad_row = lax.broadcasted_iota(I32, (SUBLANES, 1), 0)
    mid_rows, alt_rows = [], []
    for g in range(n_grp):
        cols = slice(g * HEAD_DIM, (g + 1) * HEAD_DIM)
        ae_scr[:, cols] = jnp.dot(even[:, cols], cc, preferred_element_type=F32).astype(BF16)
        bo_scr[:, cols] = jnp.dot(odd[:, cols], sc, preferred_element_type=F32).astype(BF16)
        u_g = u_ref[g].astype(F32)
        alt = jnp.sum((u_g[0:half, :] + u_g[half:, :]) * row_sign, axis=0, keepdims=True)
        mid = u_g[half:half + 1, :]
        pad = jnp.where(pad_row == 0, mid, jnp.where(pad_row == 1, alt, 0.0)).astype(BF16)
        both = jnp.dot(pad, cc, preferred_element_type=F32) * norm
        mid_rows.append(both[0:1, :])
        alt_rows.append(both[1:2, :])
    mid_term = row_sign * lanes(mid_rows)
    p = jnp.dot(c1_ref[...], ae_scr[...], preferred_element_type=F32) + mid_term
    q = jnp.dot(s1_ref[...], bo_scr[...], preferred_element_type=F32)
    o_ref[0:half, :] = (p - q).astype(o_ref.dtype)
    upper = jnp.dot(mirror_ref[...], (p + q).astype(BF16), preferred_element_type=F32)
    is_row0 = lax.broadcasted_iota(I32, (half, 1), 0) == 0
    o_ref[half:, :] = jnp.where(is_row0, lanes(alt_rows), upper).astype(o_ref.dtype)


def _fourier_mix(u_heads):
    b_sz, _, _, s_len, _ = u_heads.shape
    n_grp = HEADS_PER_BLOCK
    half = s_len // 2
    consts = _dft_constants(s_len)
    const_spec = lambda n: pl.BlockSpec((n, n), lambda b: (0, 0), pipeline_mode=pl.Buffered(1))
    return pl.pallas_call(
        functools.partial(_fourier_kernel, norm=1.0 / np.sqrt(float(s_len * HEAD_DIM))),
        out_shape=jax.ShapeDtypeStruct((b_sz, s_len, n_grp * HEAD_DIM), BF16),
        grid=(b_sz,),
        in_specs=[pl.BlockSpec((None, n_grp, None, s_len, HEAD_DIM), lambda b: (b, 0, 0, 0, 0)),
                  const_spec(half), const_spec(half), const_spec(half), const_spec(HEAD_DIM), const_spec(HEAD_DIM)],
        out_specs=pl.BlockSpec((None, s_len, n_grp * HEAD_DIM), lambda b: (b, 0, 0)),
        scratch_shapes=[pltpu.VMEM((half, n_grp * HEAD_DIM), BF16), pltpu.VMEM((half, n_grp * HEAD_DIM), BF16)],
        compiler_params=_params("arbitrary"),
        name="fourier_mix",
    )(u_heads, *consts)


def _split_bf16(a):
    hi = a.astype(BF16)
    lo = (a - hi.astype(F32)).astype(BF16)
    return hi, lo


def _route_tokens(x, w_ref, b_ref, carry_ref):
    tq = x.shape[0]
    x_hi, x_lo = _split_bf16(x)
    w_hi, w_lo = _split_bf16(w_ref[...])
    hi_terms = jnp.dot(x_hi, jnp.concatenate([w_hi, w_lo], axis=1), preferred_element_type=F32)
    logits = (hi_terms[:, 0:LANES] + hi_terms[:, LANES:]
              + jnp.dot(x_lo, w_hi, preferred_element_type=F32)) + b_ref[...]
    lane = lax.broadcasted_iota(I32, (tq, LANES), 1).astype(F32)
    logits = jnp.where(lane < N_EXPERTS, logits, NEG_INF)

    def top1(vals):
        m = jnp.max(vals, axis=-1, keepdims=True)
        idx = jnp.min(jnp.where(vals == m, lane, float(LANES)), axis=-1, keepdims=True)
        return m, idx

    m1, i1 = top1(logits)
    oh1 = lane == i1
    m2, i2 = top1(jnp.where(oh1, NEG_INF, logits))
    oh2 = lane == i2
    e2 = jnp.exp(m2 - m1)
    gate1 = 1.0 / (1.0 + e2)
    gate2 = e2 / (1.0 + e2)

    oh1f = oh1.astype(F32)
    oh2f = oh2.astype(F32)
    both = oh1f + oh2f
    tri = (lax.broadcasted_iota(I32, (tq, tq), 0) > lax.broadcasted_iota(I32, (tq, tq), 1)).astype(BF16)
    before = jnp.dot(tri, both.astype(BF16), preferred_element_type=F32) + carry_ref[0:1, :]
    rank1 = jnp.sum(oh1f * before, axis=-1, keepdims=True)
    rank2 = jnp.sum(oh2f * before, axis=-1, keepdims=True)
    carry_ref[...] = carry_ref[...] + jnp.sum(both, axis=0, keepdims=True)

    rec = jnp.zeros((tq, LANES), F32)
    for lane_idx, val in ((ROUTE_E, i1), (ROUTE_E + 1, i2),
                          (ROUTE_RANK, rank1), (ROUTE_RANK + 1, rank2),
                          (ROUTE_GATE, gate1), (ROUTE_GATE + 1, gate2)):
        rec = jnp.where(lane == lane_idx, val, rec)
    return rec


def _store_tiled_rows(dst_ref, val):
    n_rows, width = val.shape
    for j in range(width // LANES):
        dst_ref[pl.ds(j, n_rows, stride=width // LANES), :] = val[:, j * LANES:(j + 1) * LANES]


def _load_tiled_rows(src_ref):
    n_rows = src_ref.shape[0] // SUBLANES
    return jnp.concatenate([src_ref[pl.ds(j, n_rows, stride=SUBLANES), :] for j in range(SUBLANES)], axis=1)


def _tiled_row(ref, row):
    return ref.at[pl.ds(pl.multiple_of(row * SUBLANES, SUBLANES), SUBLANES)]


def _dispatch_kernel(fill_ref, pos0_ref, pos1_ref, x_ref, xs_hbm, zero_scr, sem, *, tm):
    tq = x_ref.shape[0] // SUBLANES
    pos_refs = (pos0_ref, pos1_ref)

    @pl.when(pl.program_id(0) == 0)
    def _():
        zero_scr[...] = jnp.zeros_like(zero_scr)

        def fill_copies(tile):
            copies = []
            for piece in range(tm // tq):
                row0 = pl.multiple_of((tile * tm + piece * tq) * SUBLANES, SUBLANES)
                copies.append(pltpu.make_async_copy(zero_scr, xs_hbm.at[pl.ds(row0, tq * SUBLANES)], sem))
            return copies

        def start_fill(tile, carry):
            @pl.when(fill_ref[tile] == 1)
            def _():
                for copy in fill_copies(tile):
                    copy.start()
            return carry

        def wait_fill(tile, carry):
            @pl.when(fill_ref[tile] == 1)
            def _():
                for copy in fill_copies(tile):
                    copy.wait()
            return carry

        lax.fori_loop(0, fill_ref.shape[0], start_fill, 0)
        lax.fori_loop(0, fill_ref.shape[0], wait_fill, 0)

    def start(t, carry):
        for k in range(TOP_K):
            pltpu.make_async_copy(_tiled_row(x_ref, t), _tiled_row(xs_hbm, pos_refs[k][t]), sem).start(priority=k)
        return carry

    lax.fori_loop(0, tq, start, 0, unroll=8)
    for k in range(TOP_K):
        pltpu.make_async_copy(x_ref, xs_hbm.at[pl.ds(0, tq * SUBLANES)], sem).wait()


def _dispatch(x_tiled, pos, fill_tiles, *, tm, tq=512):
    t_len = x_tiled.shape[0] // SUBLANES
    assert tm % tq == 0
    grid_spec = pltpu.PrefetchScalarGridSpec(
        num_scalar_prefetch=1,
        grid=(t_len // tq,),
        in_specs=[
            pl.BlockSpec((tq,), lambda i, fill: (i,), memory_space=pltpu.SMEM),
            pl.BlockSpec((tq,), lambda i, fill: (i,), memory_space=pltpu.SMEM),
            pl.BlockSpec((tq * SUBLANES, LANES), lambda i, fill: (i, 0)),
        ],
        out_specs=pl.BlockSpec(memory_space=pl.ANY),
        scratch_shapes=[pltpu.VMEM((tq * SUBLANES, LANES), x_tiled.dtype), pltpu.SemaphoreType.DMA],
    )
    return pl.pallas_call(
        functools.partial(_dispatch_kernel, tm=tm),
        out_shape=jax.ShapeDtypeStruct((fill_tiles.shape[0] * tm * SUBLANES, LANES), x_tiled.dtype),
        grid_spec=grid_spec,
        compiler_params=_params("arbitrary"),
        name="moe_dispatch",
    )(fill_tiles, pos[0], pos[1], x_tiled)


def _expert_kernel(tile_e_ref, nt_ref, nv_ref, x_ref, wg_hbm, wu_hbm, wd_hbm, o_ref, x16_scr, acc_ref,
                   wg_buf, wu_buf, wd_buf, sem, *, n_chunks):
    del nt_ref
    i = pl.program_id(0)
    c = pl.program_id(1)
    n_tiles_max = pl.num_programs(0)
    n_valid = nv_ref[i]
    part = x16_scr.shape[0] // EXPERT_ROW_PARTS
    last = n_chunks - 1
    full = n_valid > (EXPERT_ROW_PARTS - 1) * part
    tf = wg_buf.shape[2]
    step = i * n_chunks + c
    slot = lax.rem(step, EXPERT_WEIGHT_RING)

    def weight_copies(tile, chunk, ring_slot):
        e = tile_e_ref[tile]
        cols = pl.ds(pl.multiple_of(chunk * tf, tf), tf)
        return (pltpu.make_async_copy(wg_hbm.at[e, :, cols], wg_buf.at[ring_slot], sem.at[ring_slot, 0]),
                pltpu.make_async_copy(wu_hbm.at[e, :, cols], wu_buf.at[ring_slot], sem.at[ring_slot, 1]),
                pltpu.make_async_copy(wd_hbm.at[e, cols, :], wd_buf.at[ring_slot], sem.at[ring_slot, 2]))

    def fetch(ahead):
        later = step + ahead
        tile = lax.div(later, n_chunks)
        chunk = later - tile * n_chunks

        @pl.when(jnp.logical_and(tile < n_tiles_max, nv_ref[jnp.minimum(tile, n_tiles_max - 1)] > 0))
        def _():
            for copy in weight_copies(tile, chunk, lax.rem(later, EXPERT_WEIGHT_RING)):
                copy.start()

    @pl.when(step == 0)
    def _():
        for ahead in range(EXPERT_WEIGHT_RING - 1):
            fetch(ahead)

    @pl.when(n_valid > 0)
    def _():
        for copy in weight_copies(i, c, slot):
            copy.wait()

    def weights():
        return wg_buf[slot].astype(BF16), wu_buf[slot].astype(BF16), wd_buf[slot].astype(BF16)

    @pl.when(jnp.logical_and(full, c == 0))
    def _():
        wg, wu, wd = weights()
        for p in range(EXPERT_ROW_PARTS):
            rows = pl.ds(p * part, part)
            x16 = _load_tiled_rows(x_ref.at[pl.ds(p * part * SUBLANES, part * SUBLANES)]).astype(BF16)
            x16_scr[rows, :] = x16
            acc_ref[rows, :] = _swiglu_chunk(x16, wg, wu, wd)

    @pl.when(jnp.logical_and(full, jnp.logical_and(c > 0, c < last)))
    def _():
        acc_ref[...] += _swiglu_chunk(x16_scr[...], *weights())

    @pl.when(jnp.logical_and(full, c == last))
    def _():
        _store_tiled_rows(o_ref, acc_ref[...] + _swiglu_chunk(x16_scr[...], *weights()))

    partial = jnp.logical_not(full)

    @pl.when(jnp.logical_and(partial, c == 0))
    def _():
        acc_ref[...] = jnp.zeros_like(acc_ref)
        x16_scr[...] = _load_tiled_rows(x_ref).astype(BF16)

    for n_parts in range(1, EXPERT_ROW_PARTS):
        @pl.when(jnp.logical_and(n_valid > (n_parts - 1) * part, n_valid <= n_parts * part))
        def _(n_parts=n_parts):
            rows = pl.ds(0, n_parts * part)
            acc_ref[rows, :] += _swiglu_chunk(x16_scr[rows, :], *weights())

    @pl.when(jnp.logical_and(partial, c == last))
    def _():
        _store_tiled_rows(o_ref, acc_ref[...])

    fetch(EXPERT_WEIGHT_RING - 1)


def _expert_ffn(xs, tile_e, n_tiles, n_valid, wg, wu, wd, *, tm, tf):
    d_model = wg.shape[1]
    d_ff = wg.shape[2]
    n_chunks = d_ff // tf
    last = n_chunks - 1

    def row_map(i, c, te, nt, nv):
        return (jnp.minimum(i, nt[0] - 1), 0)

    ring = EXPERT_WEIGHT_RING
    grid_spec = pltpu.PrefetchScalarGridSpec(
        num_scalar_prefetch=3,
        grid=(xs.shape[0] // (tm * SUBLANES), n_chunks),
        in_specs=[
            pl.BlockSpec((tm * SUBLANES, LANES), row_map),
            pl.BlockSpec(memory_space=pl.ANY),
            pl.BlockSpec(memory_space=pl.ANY),
            pl.BlockSpec(memory_space=pl.ANY),
        ],
        out_specs=pl.BlockSpec((tm * SUBLANES, LANES), lambda i, c, te, nt, nv: (i, 0)),
        scratch_shapes=[pltpu.VMEM((tm, d_model), BF16), pltpu.VMEM((tm, d_model), F32),
                        pltpu.VMEM((ring, d_model, tf), wg.dtype), pltpu.VMEM((ring, d_model, tf), wu.dtype),
                        pltpu.VMEM((ring, tf, d_model), wd.dtype), pltpu.SemaphoreType.DMA((ring, 3))],
    )
    return pl.pallas_call(
        functools.partial(_expert_kernel, n_chunks=n_chunks),
        out_shape=jax.ShapeDtypeStruct(xs.shape, F32),
        grid_spec=grid_spec,
        compiler_params=_params("arbitrary", "arbitrary"),
        name="expert_ffn",
    )(tile_e, n_tiles, n_valid, xs, wg, wu, wd)


def _combine_kernel(pos0_ref, pos1_ref, pos0_next_ref, pos1_next_ref, res_ref, route_ref, g_ref, b_ref, ys_hbm,
                    o_ref, buf_ref, sem):
    i = pl.program_id(0)
    tq = o_ref.shape[0]
    slot = lax.bitwise_and(i, 1)

    def gather(positions, dst_slot):
        def start(t, carry):
            for k in range(TOP_K):
                pltpu.make_async_copy(_tiled_row(ys_hbm, positions[k][t]),
                                      _tiled_row(buf_ref.at[dst_slot, k], t), sem.at[dst_slot]).start(priority=k)
            return carry
        lax.fori_loop(0, tq, start, 0, unroll=8)

    @pl.when(i == 0)
    def _():
        gather((pos0_ref, pos1_ref), 0)

    @pl.when(i + 1 < pl.num_programs(0))
    def _():
        gather((pos0_next_ref, pos1_next_ref), 1 - slot)

    for k in range(TOP_K):
        pltpu.make_async_copy(ys_hbm.at[pl.ds(0, tq * SUBLANES)], buf_ref.at[slot, k], sem.at[slot]).wait()

    route = route_ref[...]
    y = (route[:, ROUTE_GATE:ROUTE_GATE + 1] * _load_tiled_rows(buf_ref.at[slot, 0])
         + route[:, ROUTE_GATE + 1:ROUTE_GATE + 2] * _load_tiled_rows(buf_ref.at[slot, 1]))
    o_ref[...] = _layer_norm(ALPHA * _load_tiled_rows(res_ref) + y, g_ref[...], b_ref[...])


def _combine_ln(ys, pos, res_tiled, route, gain, bias, *, tq=256):
    t_len = res_tiled.shape[0] // SUBLANES
    d_model = SUBLANES * LANES
    n_steps = t_len // tq
    this_tile = pl.BlockSpec((tq,), lambda i: (i,), memory_space=pltpu.SMEM)
    next_tile = pl.BlockSpec((tq,), lambda i: (jnp.minimum(i + 1, n_steps - 1),), memory_space=pltpu.SMEM)
    return pl.pallas_call(
        _combine_kernel,
        out_shape=jax.ShapeDtypeStruct((t_len, d_model), F32),
        grid=(n_steps,),
        in_specs=[
            this_tile, this_tile, next_tile, next_tile,
            pl.BlockSpec((tq * SUBLANES, LANES), lambda i: (i, 0)),
            pl.BlockSpec((tq, LANES), lambda i: (i, 0)),
            pl.BlockSpec((1, d_model), lambda i: (0, 0)),
            pl.BlockSpec((1, d_model), lambda i: (0, 0)),
            pl.BlockSpec(memory_space=pl.ANY),
        ],
        out_specs=pl.BlockSpec((tq, d_model), lambda i: (i, 0)),
        scratch_shapes=[pltpu.VMEM((2, TOP_K, tq * SUBLANES, LANES), F32), pltpu.SemaphoreType.DMA((2,))],
        compiler_params=_params("arbitrary"),
        name="moe_combine_ln",
    )(pos[0], pos[1], pos[0], pos[1], res_tiled, route, gain.reshape(1, -1), bias.reshape(1, -1), ys)


def _moe_ffn_ln(x_tiled, route, route_t, counts, wg, wu, wd, gain, bias, *, tm=MOE_ROW_TILE, tf=MOE_FF_CHUNK):
    t_len = x_tiled.shape[0] // SUBLANES
    experts = route_t[ROUTE_E:ROUTE_E + TOP_K].astype(I32)
    ranks = route_t[ROUTE_RANK:ROUTE_RANK + TOP_K].astype(I32)
    counts = counts[0, :N_EXPERTS].astype(I32)
    tiles_per_expert = (counts + tm - 1) // tm
    tile_end = jnp.cumsum(tiles_per_expert)
    row_start = (tile_end - tiles_per_expert) * tm
    pos = ranks + sum(jnp.where(experts == e, row_start[e], 0) for e in range(N_EXPERTS))
    n_tiles_max = (TOP_K * t_len) // tm + N_EXPERTS
    n_tiles = tile_end[-1:]
    all_tiles = jnp.arange(n_tiles_max, dtype=I32)
    tile_ids = jnp.minimum(all_tiles, n_tiles[0] - 1)
    tile_e = jnp.sum((tile_ids[:, None] >= tile_end[None, :-1]).astype(I32), axis=1)
    is_last = jnp.any((all_tiles[:, None] == tile_end[None, :] - 1) & (tiles_per_expert[None, :] > 0), axis=1)
    fill_tiles = (is_last | (all_tiles >= n_tiles[0])).astype(I32)

    first_tile = (tile_end - tiles_per_expert)[tile_e]
    n_valid = jnp.where(all_tiles < n_tiles[0], jnp.clip(counts[tile_e] - (all_tiles - first_tile) * tm, 0, tm), 0)

    xs = _dispatch(x_tiled, pos, fill_tiles, tm=tm, tq=MOE_DISPATCH_TILE)
    ys = _expert_ffn(xs, tile_e, n_tiles.astype(I32), n_valid.astype(I32), wg, wu, wd, tm=tm, tf=tf)
    return _combine_ln(ys, pos, x_tiled, route, gain, bias)


def _alibi_slopes():
    exps = np.arange(1, N_ATTN_HEADS + 1, dtype=np.float32) * np.float32(8.0 / N_ATTN_HEADS)
    return jnp.asarray(np.exp2(-exps).astype(np.float32))


def kernel(x, mem, a_w_in, a_w_mem_kv, a_w_out, a_ln1_g, a_ln1_b, a_ffn_gate, a_ffn_up, a_ffn_down, a_ln2_g, a_ln2_b, b_w_in, b_fourier_g, b_w_mem_kv, b_w_out, b_ln1_g, b_ln1_b, b_router_w, b_router_b, b_moe_gate, b_moe_up, b_moe_down, b_ln2_g, b_ln2_b):
    b_sz, s_len, d_model = x.shape
    t_len = b_sz * s_len
    x32 = x.reshape(t_len, d_model)
    a_w_in, a_w_mem_kv, a_w_out, a_ffn_gate, a_ffn_up, a_ffn_down, b_w_in, b_w_mem_kv, b_w_out = (
        w[0].astype(BF16) for w in (a_w_in, a_w_mem_kv, a_w_out, a_ffn_gate, a_ffn_up, a_ffn_down,
                                    b_w_in, b_w_mem_kv, b_w_out))

    n_grp = len(DILATIONS)
    qkv0, x16 = _proj_heads(x, a_w_in, (0, n_grp, 2 * n_grp, 3 * n_grp))
    qkv = [qkv0] + [_proj_heads(x16, a_w_in, (g, n_grp + g, 2 * n_grp + g), dil=DILATIONS[g])
                    for g in range(1, n_grp)]
    mem_kv, mem16 = _proj_heads(mem, a_w_mem_kv, (0, 1))
    attn = _dilated_attention(_alibi_slopes(), qkv).reshape(t_len, -1)
    memo = _memory_attention(qkv[0], 3 * HEADS_PER_BLOCK, mem_kv).reshape(t_len, -1)
    x32, x16 = _mixer_ffn_ln(attn, memo, a_w_out, x32, a_ln1_g[0], a_ln1_b[0],
                             a_ffn_gate, a_ffn_up, a_ffn_down, a_ln2_g[0], a_ln2_b[0])

    x16_b = x16.reshape(b_sz, s_len, d_model)
    u_q = _proj_heads(x16_b, b_w_in, (0, 1), gain=b_fourier_g[0], ln_blocks=(True, False))
    mem_kv = _proj_heads(mem16, b_w_mem_kv, (0, 1))
    four = _fourier_mix(u_q).reshape(t_len, -1)
    memo = _memory_attention(u_q, HEADS_PER_BLOCK, mem_kv).reshape(t_len, -1)
    x_tiled, route, route_t, counts = _out_proj_ln_route(four, memo, b_w_out, x32, b_ln1_g[0], b_ln1_b[0],
                                                         b_router_w[0], b_router_b[0])
    out = _moe_ffn_ln(x_tiled, route, route_t, counts, b_moe_gate[0], b_moe_up[0], b_moe_down[0],
                      b_ln2_g[0], b_ln2_b[0])
    return out.reshape(b_sz, s_len, d_model)
```
